```python
import math
import jax, jax.numpy as jnp
from jax import lax
import numpy as np

D_MODEL = 2048
BATCH = 4
SEQ = 4096
DEPTH = 2

CHUNK = 64
N_LEFT_CHUNKS = 8
BAND = (N_LEFT_CHUNKS + 1) * CHUNK

A_HEADS = 16
A_HEAD_DIM = 64
A_WIDTH = A_HEADS * A_HEAD_DIM
MAX_REL = 128
N_REL = 2 * MAX_REL + 1

S5_WIDTH = D_MODEL // 2
S5_GROUP = 16
S5_GROUPS = S5_WIDTH // S5_GROUP
S5_STATE = 64
DT_MIN = 0.001
DT_MAX = 0.1

C_HEADS = 16
C_HEAD_DIM = 128
C_WIDTH = C_HEADS * C_HEAD_DIM
Q_BLOCK = 128

EVEN_MIX = A_WIDTH + S5_WIDTH
W_IN_EVEN = 3 * A_WIDTH + S5_WIDTH + A_WIDTH + S5_WIDTH
W_IN_ODD = 3 * C_WIDTH + C_WIDTH + C_HEADS
N_EVEN = (DEPTH + 1) // 2
N_ODD = DEPTH // 2
EPS = 1e-6

kernel_name = "hybrid_chunked_attn_s5_fox_encoder"


def rmsnorm(x, g):
    xf = x.astype(jnp.float32)
    y = xf * lax.rsqrt(jnp.mean(xf * xf, axis=-1, keepdims=True) + EPS)
    return (y * g.astype(jnp.float32)).astype(x.dtype)


def chunked_relpos_attention(q, k, v, rel_bias):
    b, l, h, dh = q.shape
    nc = l // CHUNK
    qc = q.reshape(b, nc, CHUNK, h, dh)
    pad = ((0, 0), (N_LEFT_CHUNKS, 0), (0, 0), (0, 0), (0, 0))
    kc = jnp.pad(k.reshape(b, nc, CHUNK, h, dh), pad)
    vc = jnp.pad(v.reshape(b, nc, CHUNK, h, dh), pad)
    band_idx = jnp.arange(nc)[:, None] + jnp.arange(N_LEFT_CHUNKS + 1)[None, :]
    kb = jnp.take(kc, band_idx, axis=1).reshape(b, nc, BAND, h, dh)
    vb = jnp.take(vc, band_idx, axis=1).reshape(b, nc, BAND, h, dh)
    scale = 1.0 / math.sqrt(dh)
    s = jnp.einsum('bcqhd,bckhd->bhcqk', qc, kb).astype(jnp.float32) * scale
    rel = jnp.arange(CHUNK)[:, None] + N_LEFT_CHUNKS * CHUNK - jnp.arange(BAND)[None, :]
    rel_idx = jnp.clip(rel, -MAX_REL, MAX_REL) + MAX_REL
    bias = rel_bias.astype(jnp.float32)[:, rel_idx]
    key_chunk = jnp.arange(nc)[:, None] - N_LEFT_CHUNKS + jnp.arange(BAND)[None, :] // CHUNK
    valid = key_chunk >= 0
    s = s + bias[None, :, None]
    s = jnp.where(valid[None, None, :, None, :], s, -jnp.inf)
    p = jax.nn.softmax(s, axis=-1).astype(v.dtype)
    o = jnp.einsum('bhcqk,bckhd->bcqhd', p, vb)
    return o.reshape(b, l, h * dh)


def s5_ssm(u, lam_re, lam_im, log_dt, b_re, b_im, c_re, c_im, d_skip):
    bsz, l, _ = u.shape
    uf = u.astype(jnp.float32).reshape(bsz, l, S5_GROUPS, S5_GROUP)
    lr = lam_re.astype(jnp.float32)
    li = lam_im.astype(jnp.float32)
    dt = jnp.exp(log_dt.astype(jnp.float32))[:, None]
    mag = jnp.exp(lr * dt)
    ang = li * dt
    a_re = mag * jnp.cos(ang)
    a_im = mag * jnp.sin(ang)
    den = lr * lr + li * li
    nr = a_re - 1.0
    ni = a_im
    coef_re = (nr * lr + ni * li) / den
    coef_im = (ni * lr - nr * li) / den
    br = b_re.astype(jnp.float32)
    bi = b_im.astype(jnp.float32)
    bb_re = coef_re[..., None] * br - coef_im[..., None] * bi
    bb_im = coef_re[..., None] * bi + coef_im[..., None] * br
    bu_re = jnp.einsum('blgc,gpc->blgp', uf, bb_re)
    bu_im = jnp.einsum('blgc,gpc->blgp', uf, bb_im)
    a_re_t = jnp.broadcast_to(a_re, (1, l, S5_GROUPS, S5_STATE))
    a_im_t = jnp.broadcast_to(a_im, (1, l, S5_GROUPS, S5_STATE))

    def combine(e1, e2):
        ar1, ai1, xr1, xi1 = e1
        ar2, ai2, xr2, xi2 = e2
        return (ar2 * ar1 - ai2 * ai1,
                ar2 * ai1 + ai2 * ar1,
                ar2 * xr1 - ai2 * xi1 + xr2,
                ar2 * xi1 + ai2 * xr1 + xi2)

    _, _, xr, xi = lax.associative_scan(combine, (a_re_t, a_im_t, bu_re, bu_im), axis=1)
    y = (jnp.einsum('blgp,gcp->blgc', xr, c_re.astype(jnp.float32))
         - jnp.einsum('blgp,gcp->blgc', xi, c_im.astype(jnp.float32))
         + d_skip.astype(jnp.float32) * uf)
    return y.reshape(bsz, l, S5_WIDTH)


def forgetting_attention(q, k, v, log_f):
    b, l, h, dh = q.shape
    nqb = l // Q_BLOCK
    cum = jnp.cumsum(log_f, axis=1).transpose(0, 2, 1)
    qb = q.reshape(b, nqb, Q_BLOCK, h, dh).transpose(1, 0, 3, 2, 4)
    cq = cum.reshape(b, h, nqb, Q_BLOCK).transpose(2, 0, 1, 3)
    kpos = jnp.arange(l)
    scale = 1.0 / math.sqrt(dh)

    def block(args):
        qi, cqi, bi = args
        s = jnp.einsum('bhqd,bkhd->bhqk', qi, k).astype(jnp.float32) * scale
        s = s + cqi[..., None] - cum[:, :, None, :]
        qpos = bi * Q_BLOCK + jnp.arange(Q_BLOCK)
        s = jnp.where(kpos[None, :] <= qpos[:, None], s, -jnp.inf)
        p = jax.nn.softmax(s, axis=-1).astype(v.dtype)
        return jnp.einsum('bhqk,bkhd->bqhd', p, v)

    o = lax.map(block, (qb, cq, jnp.arange(nqb)))
    return o.transpose(1, 0, 2, 3, 4).reshape(b, l, h * dh)


def even_layer(x, norm_g, w_in, rel_bias, lam_re, lam_im, log_dt, b_re, b_im,
               c_re, c_im, d_skip, w_glu, b_glu, w_out):
    bsz, l, _ = x.shape
    hn = rmsnorm(x, norm_g)
    proj = hn @ w_in
    cuts = [A_WIDTH, 2 * A_WIDTH, 3 * A_WIDTH, 3 * A_WIDTH + S5_WIDTH,
            4 * A_WIDTH + S5_WIDTH]
    q, k, v, u, z_a, z_b = jnp.split(proj, cuts, axis=-1)
    shp = (bsz, l, A_HEADS, A_HEAD_DIM)
    o_a = chunked_relpos_attention(q.reshape(shp), k.reshape(shp), v.reshape(shp), rel_bias)
    o_a = o_a * jax.nn.silu(z_a)
    y = jax.nn.gelu(s5_ssm(u, lam_re, lam_im, log_dt, b_re, b_im, c_re, c_im, d_skip).astype(x.dtype))
    o_b = y * jax.nn.sigmoid(y @ w_glu + b_glu)
    o_b = o_b * jax.nn.silu(z_b)
    return x + jnp.concatenate([o_a, o_b], axis=-1) @ w_out


def odd_layer(x, norm_g, w_in, b_forget, w_out):
    bsz, l, _ = x.shape
    hn = rmsnorm(x, norm_g)
    proj = hn @ w_in
    cuts = [C_WIDTH, 2 * C_WIDTH, 3 * C_WIDTH, 4 * C_WIDTH]
    q, k, v, z, f_logit = jnp.split(proj, cuts, axis=-1)
    log_f = jax.nn.log_sigmoid((f_logit + b_forget).astype(jnp.float32))
    shp = (bsz, l, C_HEADS, C_HEAD_DIM)
    o = forgetting_attention(q.reshape(shp), k.reshape(shp), v.reshape(shp), log_f)
    o = o * jax.nn.silu(z)
    return x + o @ w_out


def setup_inputs(seed: int = 0) -> dict:
    key = jax.random.key(seed)
    ks = jax.random.split(key, 20)
    f32 = jnp.float32
    x = jax.random.normal(ks[0], (BATCH, SEQ, D_MODEL), f32)
    norm_even_g = 1.0 + 0.05 * jax.random.normal(ks[1], (N_EVEN, D_MODEL), f32)
    w_in_even = jax.random.normal(ks[2], (N_EVEN, D_MODEL, W_IN_EVEN), f32) * D_MODEL ** -0.5
    rel_bias = 0.2 * jax.random.normal(ks[3], (N_EVEN, A_HEADS, N_REL), f32)
    n = jnp.arange(S5_STATE, dtype=f32)
    s5_lambda_re = -0.5 + 0.01 * jax.random.normal(ks[4], (N_EVEN, S5_GROUPS, S5_STATE), f32)
    s5_lambda_im = math.pi * n + 0.01 * jax.random.normal(ks[5], (N_EVEN, S5_GROUPS, S5_STATE), f32)
    s5_log_dt = jax.random.uniform(ks[6], (N_EVEN, S5_GROUPS), f32,
                                   math.log(DT_MIN), math.log(DT_MAX))
    s5_b_re = jax.random.normal(ks[7], (N_EVEN, S5_GROUPS, S5_STATE, S5_GROUP), f32) * (2 * S5_GROUP) ** -0.5
    s5_b_im = jax.random.normal(ks[8], (N_EVEN, S5_GROUPS, S5_STATE, S5_GROUP), f32) * (2 * S5_GROUP) ** -0.5
    s5_c_re = jax.random.normal(ks[9], (N_EVEN, S5_GROUPS, S5_GROUP, S5_STATE), f32) * (2 * S5_STATE) ** -0.5
    s5_c_im = jax.random.normal(ks[10], (N_EVEN, S5_GROUPS, S5_GROUP, S5_STATE), f32) * (2 * S5_STATE) ** -0.5
    s5_d = 0.5 * jax.random.normal(ks[11], (N_EVEN, S5_GROUPS, S5_GROUP), f32)
    w_glu = jax.random.normal(ks[12], (N_EVEN, S5_WIDTH, S5_WIDTH), f32) * S5_WIDTH ** -0.5
    b_glu = 0.02 * jax.random.normal(ks[13], (N_EVEN, S5_WIDTH), f32)
    w_out_even = jax.random.normal(ks[14], (N_EVEN, EVEN_MIX, D_MODEL), f32) * EVEN_MIX ** -0.5
    norm_odd_g = 1.0 + 0.05 * jax.random.normal(ks[15], (N_ODD, D_MODEL), f32)
    w_in_odd = jax.random.normal(ks[16], (N_ODD, D_MODEL, W_IN_ODD), f32) * D_MODEL ** -0.5
    b_forget = jax.random.uniform(ks[17], (N_ODD, C_HEADS), f32, 1.0, 4.0)
    w_out_odd = jax.random.normal(ks[18], (N_ODD, C_WIDTH, D_MODEL), f32) * C_WIDTH ** -0.5
    final_norm_g = 1.0 + 0.05 * jax.random.normal(ks[19], (D_MODEL,), f32)
    return {"x": x, "norm_even_g": norm_even_g, "w_in_even": w_in_even, "rel_bias": rel_bias,
            "s5_lambda_re": s5_lambda_re, "s5_lambda_im": s5_lambda_im, "s5_log_dt": s5_log_dt,
            "s5_b_re": s5_b_re, "s5_b_im": s5_b_im, "s5_c_re": s5_c_re, "s5_c_im": s5_c_im,
            "s5_d": s5_d, "w_glu": w_glu, "b_glu": b_glu, "w_out_even": w_out_even,
            "norm_odd_g": norm_odd_g, "w_in_odd": w_in_odd, "b_forget": b_forget,
            "w_out_odd": w_out_odd, "final_norm_g": final_norm_g}


def reference(x, norm_even_g, w_in_even, rel_bias, s5_lambda_re, s5_lambda_im, s5_log_dt,
              s5_b_re, s5_b_im, s5_c_re, s5_c_im, s5_d, w_glu, b_glu, w_out_even,
              norm_odd_g, w_in_odd, b_forget, w_out_odd, final_norm_g):
    for layer in range(DEPTH):
        i = layer // 2
        if layer % 2 == 0:
            x = even_layer(x, norm_even_g[i], w_in_even[i], rel_bias[i], s5_lambda_re[i],
                           s5_lambda_im[i], s5_log_dt[i], s5_b_re[i], s5_b_im[i], s5_c_re[i],
                           s5_c_im[i], s5_d[i], w_glu[i], b_glu[i], w_out_even[i])
        else:
            x = odd_layer(x, norm_odd_g[i], w_in_odd[i], b_forget[i], w_out_odd[i])
    return rmsnorm(x, final_norm_g)
```

```python
import functools
import math

import jax
import jax.numpy as jnp
from jax import lax
from jax.experimental import pallas as pl
from jax.experimental.pallas import tpu as pltpu

F32 = jnp.float32
BF16 = jnp.bfloat16

EPS = 1e-6
NEG = -1e30

CHUNK = 64
N_LEFT = 8
A_HEADS = 16
A_DH = 64
MAX_REL = 128
S5_GROUP = 16
S5_STATE = 64
C_HEADS = 16
C_DH = 128

LANES = 128
VMEM_LIMIT = 56 * 1024 * 1024

TM_PROJ = 1024
TN_PROJ = 1024
TM_OUT = 512
A_QB = 2 * CHUNK
A_KB = A_QB + N_LEFT * CHUNK
A_TW = A_KB + N_LEFT * CHUNK
S5_TC = 16
FOX_TQ = 256
FOX_TK = 256


def _cparams(sem):
    return pltpu.CompilerParams(dimension_semantics=sem, vmem_limit_bytes=VMEM_LIMIT)


def _rms(x, g):
    ms = jnp.mean(x * x, axis=-1, keepdims=True)
    return x * lax.rsqrt(ms + EPS) * g


def _silu(x):
    return x * jax.nn.sigmoid(x)


def _gelu_tanh(x):
    c = math.sqrt(2.0 / math.pi)
    return 0.5 * x * (1.0 + jnp.tanh(c * (x + 0.044715 * (x * x * x))))


def _norm_proj_kernel(x_ref, g_ref, w_ref, o_ref, hn_ref):
    @pl.when(pl.program_id(1) == 0)
    def _():
        hn_ref[...] = _rms(x_ref[...], g_ref[...]).astype(BF16)

    o_ref[...] = jnp.dot(hn_ref[...], w_ref[...],
                         preferred_element_type=F32).astype(o_ref.dtype)


def _norm_proj(x2, g, w):
    m, d = x2.shape
    n = w.shape[1]
    tm, tn = min(TM_PROJ, m), TN_PROJ
    return pl.pallas_call(
        _norm_proj_kernel,
        grid=(m // tm, n // tn),
        in_specs=[pl.BlockSpec((tm, d), lambda i, j: (i, 0)),
                  pl.BlockSpec((1, d), lambda i, j: (0, 0)),
                  pl.BlockSpec((d, tn), lambda i, j: (0, j))],
        out_specs=pl.BlockSpec((tm, tn), lambda i, j: (i, j)),
        out_shape=jax.ShapeDtypeStruct((m, n), BF16),
        scratch_shapes=[pltpu.VMEM((tm, d), BF16)],
        compiler_params=_cparams(("arbitrary", "arbitrary")),
    )(x2, g.reshape(1, d), w)


def _proj_forget_kernel(h_ref, w_ref, wf_ref, o_ref, f_ref):
    @pl.when(pl.program_id(1) == 0)
    def _():
        f_ref[...] = lax.dot_general(wf_ref[...], h_ref[...],
                                     (((1,), (1,)), ((), ())),
                                     preferred_element_type=F32)

    o_ref[...] = jnp.dot(h_ref[...], w_ref[...],
                         preferred_element_type=F32).astype(o_ref.dtype)


def _proj_forget(hn, w, wf_t):
    m, d = hn.shape
    n = w.shape[1]
    nh = wf_t.shape[0]
    tm, tn = min(TM_PROJ, m), TN_PROJ
    return pl.pallas_call(
        _proj_forget_kernel,
        grid=(m // tm, n // tn),
        in_specs=[pl.BlockSpec((tm, d), lambda i, j: (i, 0)),
                  pl.BlockSpec((d, tn), lambda i, j: (0, j)),
                  pl.BlockSpec((nh, d), lambda i, j: (0, 0))],
        out_specs=[pl.BlockSpec((tm, tn), lambda i, j: (i, j)),
                   pl.BlockSpec((nh, tm), lambda i, j: (0, i))],
        out_shape=[jax.ShapeDtypeStruct((m, n), BF16),
                   jax.ShapeDtypeStruct((nh, m), F32)],
        compiler_params=_cparams(("arbitrary", "arbitrary")),
    )(hn, w, wf_t)


def _attn_a_kernel(q_ref, k_ref, v_ref, z_ref, t_ref, o_ref):
    qb = pl.program_id(2)
    start = qb * A_QB - N_LEFT * CHUNK
    start_c = jnp.maximum(start, 0)
    off = pl.multiple_of(start_c - start, LANES)
    start_c = pl.multiple_of(start_c, LANES)
    q = q_ref[0]
    kb = k_ref[0, pl.ds(start_c, A_KB), :]
    vb = v_ref[0, pl.ds(start_c, A_KB), :]
    scale = 1.0 / math.sqrt(A_DH)
    outs = []
    for h in range(2):
        sl = slice(h * A_DH, (h + 1) * A_DH)
        s = lax.dot_general(q[:, sl], kb[:, sl], (((1,), (1,)), ((), ())),
                            preferred_element_type=F32)
        s = s * scale + t_ref[h, :, pl.ds(off, A_KB)]
        m = jnp.max(s, axis=1, keepdims=True)
        p = jnp.exp(s - m)
        l = jnp.sum(p, axis=1, keepdims=True)
        o = jnp.dot(p.astype(BF16), vb[:, sl], preferred_element_type=F32)
        outs.append(o / l)
    o = jnp.concatenate(outs, axis=1)
    o_ref[0] = (o * _silu(z_ref[0].astype(F32))).astype(o_ref.dtype)


def _attn_a(proj3, table):
    b, l, _ = proj3.shape
    hp = A_HEADS // 2
    width = 2 * A_DH
    ncol = A_HEADS * A_DH // width
    zcol = (3 * A_HEADS * A_DH + (A_HEADS * A_DH)) // width
    return pl.pallas_call(
        _attn_a_kernel,
        grid=(hp, b, l // A_QB),
        in_specs=[pl.BlockSpec((1, A_QB, width), lambda h, bi, qi: (bi, qi, h)),
                  pl.BlockSpec((1, l, width), lambda h, bi, qi: (bi, 0, ncol + h)),
                  pl.BlockSpec((1, l, width), lambda h, bi, qi: (bi, 0, 2 * ncol + h)),
                  pl.BlockSpec((1, A_QB, width), lambda h, bi, qi: (bi, qi, zcol + h)),
                  pl.BlockSpec((2, A_QB, A_TW), lambda h, bi, qi: (h, 0, 0))],
        out_specs=pl.BlockSpec((1, A_QB, width), lambda h, bi, qi: (bi, qi, h)),
        out_shape=jax.ShapeDtypeStruct((b, l, A_HEADS * A_DH), BF16),
        compiler_params=_cparams(("arbitrary", "arbitrary", "arbitrary")),
    )(proj3, proj3, proj3, proj3, table)


def _rel_bias_table(rel_bias):
    i = jnp.arange(A_QB)[:, None]
    m = jnp.arange(A_TW)[None, :]
    rel = jnp.clip(N_LEFT * CHUNK + i - m, -MAX_REL, MAX_REL) + MAX_REL
    dchunk = m // CHUNK - i // CHUNK
    ok = (dchunk >= 0) & (dchunk <= N_LEFT)
    return jnp.where(ok[None], rel_bias.astype(F32)[:, rel], NEG)


def _s5_params(lam_re, lam_im, log_dt, b_re, b_im, c_re, c_im):
    hi = lax.Precision.HIGHEST
    lr = lam_re.astype(F32)
    li = lam_im.astype(F32)
    dt = jnp.exp(log_dt.astype(F32))[:, None]
    n = jnp.arange(S5_TC + 1, dtype=F32)[:, None, None]
    mag = jnp.exp(n * (lr * dt)[None])
    ang = n * (li * dt)[None]
    pw_re = mag * jnp.cos(ang)
    pw_im = mag * jnp.sin(ang)
    a_re, a_im = pw_re[1], pw_im[1]
    den = lr * lr + li * li
    nr = a_re - 1.0
    ni = a_im
    coef_re = (nr * lr + ni * li) / den
    coef_im = (ni * lr - nr * li) / den
    br = b_re.astype(F32)
    bi = b_im.astype(F32)
    bb_re = coef_re[..., None] * br - coef_im[..., None] * bi
    bb_im = coef_re[..., None] * bi + coef_im[..., None] * br
    cr = c_re.astype(F32)
    ci = c_im.astype(F32)
    g = lr.shape[0]
    rp_re = pw_re[S5_TC - 1::-1][:S5_TC]
    rp_im = pw_im[S5_TC - 1::-1][:S5_TC]
    r_re = rp_re[..., None] * bb_re[None] - rp_im[..., None] * bb_im[None]
    r_im = rp_re[..., None] * bb_im[None] + rp_im[..., None] * bb_re[None]
    r_re = r_re.transpose(1, 0, 3, 2).reshape(g, S5_TC * S5_GROUP, S5_STATE)
    r_im = r_im.transpose(1, 0, 3, 2).reshape(g, S5_TC * S5_GROUP, S5_STATE)
    rt = jnp.concatenate([r_re, r_im, r_im, r_re], axis=-1)
    ca_re = cr[None] * pw_re[:, :, None, :] - ci[None] * pw_im[:, :, None, :]
    ca_im = cr[None] * pw_im[:, :, None, :] + ci[None] * pw_re[:, :, None, :]
    o_re = ca_re[1:].transpose(1, 3, 0, 2).reshape(g, S5_STATE, S5_TC * S5_GROUP)
    o_im = ca_im[1:].transpose(1, 3, 0, 2).reshape(g, S5_STATE, S5_TC * S5_GROUP)
    ot = jnp.concatenate([o_re, -o_im], axis=1)
    kern = (jnp.einsum('ngcp,gpd->ngcd', ca_re[:S5_TC], bb_re, precision=hi)
            - jnp.einsum('ngcp,gpd->ngcd', ca_im[:S5_TC], bb_im, precision=hi))
    t = jnp.arange(S5_TC)
    lag = t[None, :] - t[:, None]
    mt = jnp.where((lag >= 0)[:, :, None, None, None],
                   kern[jnp.clip(lag, 0, S5_TC - 1)], 0.0)
    mt = mt.transpose(2, 0, 4, 1, 3).reshape(g, S5_TC * S5_GROUP, S5_TC * S5_GROUP)
    p_row = jnp.concatenate([pw_re[S5_TC], pw_re[S5_TC]], axis=-1)
    q_row = jnp.concatenate([-pw_im[S5_TC], pw_im[S5_TC]], axis=-1)
    return rt, mt, ot, p_row, q_row


def _s5_state_in_kernel(u_ref, rt_ref, s_ref):
    nb, _, nk, w = u_ref.shape
    u = u_ref[...].reshape(nb * nk, w)
    s = jnp.dot(u, rt_ref[0], preferred_element_type=F32)
    s_ref[...] = s.reshape(nb, 1, nk, s.shape[-1])


def _s5_state_in(u4, rt):
    b, g, nk, w = u4.shape
    n = rt.shape[-1]
    return pl.pallas_call(
        _s5_state_in_kernel,
        grid=(g,),
        in_specs=[pl.BlockSpec((b, 1, nk, w), lambda gi: (0, gi, 0, 0)),
                  pl.BlockSpec((1, w, n), lambda gi: (gi, 0, 0))],
        out_specs=pl.BlockSpec((b, 1, nk, n), lambda gi: (0, gi, 0, 0)),
        out_shape=jax.ShapeDtypeStruct((b, g, nk, n), F32),
        compiler_params=_cparams(("arbitrary",)),
    )(u4, rt)


def _s5_scan_kernel(s_ref, p_ref, q_ref, x_ref):
    nk = s_ref.shape[0]
    half = s_ref.shape[-1] // 2
    p = p_ref[...]
    q = q_ref[...]

    def body(k, carry):
        v, w = carry
        x_ref[k] = v
        s = s_ref[k]
        v2 = p * v + q * w + s[:, :half]
        w2 = p * w - q * v + s[:, half:]
        return v2, w2

    zero = jnp.zeros(p.shape, F32)
    lax.fori_loop(0, nk, body, (zero, zero))


def _s5_scan(s_t, p_rows, q_rows):
    nk, r, n = s_t.shape
    half = n // 2
    tr = min(64, r)
    return pl.pallas_call(
        _s5_scan_kernel,
        grid=(r // tr,),
        in_specs=[pl.BlockSpec((nk, tr, n), lambda i: (0, i, 0)),
                  pl.BlockSpec((tr, half), lambda i: (i, 0)),
                  pl.BlockSpec((tr, half), lambda i: (i, 0))],
        out_specs=pl.BlockSpec((nk, tr, half), lambda i: (0, i, 0)),
        out_shape=jax.ShapeDtypeStruct((nk, r, half), F32),
        compiler_params=_cparams(("arbitrary",)),
    )(s_t, p_rows, q_rows)


def _s5_out_kernel(u_ref, x_ref, mt_ref, ot_ref, d_ref, y_ref):
    nb, _, nk, w = u_ref.shape
    u = u_ref[...].reshape(nb * nk, w)
    x = x_ref[...].reshape(nb * nk, x_ref.shape[-1]).astype(BF16)
    y = jnp.dot(u, mt_ref[0], preferred_element_type=F32)
    y = y + jnp.dot(x, ot_ref[0], preferred_element_type=F32)
    y = y + d_ref[0] * u.astype(F32)
    y_ref[...] = _gelu_tanh(y).reshape(nb, 1, nk, w).astype(y_ref.dtype)


def _s5_out(u4, x4, mt, ot, d_rows):
    b, g, nk, w = u4.shape
    ns = x4.shape[-1]
    return pl.pallas_call(
        _s5_out_kernel,
        grid=(g,),
        in_specs=[pl.BlockSpec((b, 1, nk, w), lambda gi: (0, gi, 0, 0)),
                  pl.BlockSpec((b, 1, nk, ns), lambda gi: (0, gi, 0, 0)),
                  pl.BlockSpec((1, w, w), lambda gi: (gi, 0, 0)),
                  pl.BlockSpec((1, ns, w), lambda gi: (gi, 0, 0)),
                  pl.BlockSpec((1, 1, w), lambda gi: (gi, 0, 0))],
        out_specs=pl.BlockSpec((b, 1, nk, w), lambda gi: (0, gi, 0, 0)),
        out_shape=jax.ShapeDtypeStruct((b, g, nk, w), BF16),
        compiler_params=_cparams(("arbitrary",)),
    )(u4, x4, mt, ot, d_rows)


def _s5_mixer(u3, lam_re, lam_im, log_dt, b_re, b_im, c_re, c_im, d_skip):
    b, l, wd = u3.shape
    g = wd // S5_GROUP
    nk = l // S5_TC
    rt, mt, ot, p_row, q_row = _s5_params(lam_re, lam_im, log_dt, b_re, b_im, c_re, c_im)
    u4 = u3.reshape(b, nk, S5_TC, g, S5_GROUP).transpose(0, 3, 1, 2, 4)
    u4 = u4.reshape(b, g, nk, S5_TC * S5_GROUP)
    s4 = _s5_state_in(u4, rt.astype(BF16))
    s_t = s4.transpose(2, 0, 1, 3).reshape(nk, b * g, s4.shape[-1])
    x_t = _s5_scan(s_t, jnp.tile(p_row, (b, 1)), jnp.tile(q_row, (b, 1)))
    x4 = x_t.reshape(nk, b, g, x_t.shape[-1]).transpose(1, 2, 0, 3)
    d_rows = jnp.tile(d_skip.astype(F32), (1, S5_TC)).reshape(g, 1, S5_TC * S5_GROUP)
    y4 = _s5_out(u4, x4, mt.astype(BF16), ot.astype(BF16), d_rows)
    y = y4.reshape(b, g, nk, S5_TC, S5_GROUP).transpose(0, 2, 3, 1, 4)
    return y.reshape(b, l, wd)


def _even_out_kernel(x_ref, oa_ref, y_ref, zb_ref, wg_ref, bg_ref, wo_ref, g_ref,
                     x1_ref, hn_ref):
    y = y_ref[...]
    t = jnp.dot(y, wg_ref[...], preferred_element_type=F32) + bg_ref[...]
    ob = y.astype(F32) * jax.nn.sigmoid(t) * _silu(zb_ref[...].astype(F32))
    wa = oa_ref.shape[1]
    acc = jnp.dot(oa_ref[...], wo_ref[:wa, :], preferred_element_type=F32)
    acc = acc + jnp.dot(ob.astype(BF16), wo_ref[wa:, :], preferred_element_type=F32)
    x1 = x_ref[...] + acc
    x1_ref[...] = x1
    hn_ref[...] = _rms(x1, g_ref[...]).astype(BF16)


def _even_out(x2, oa, y, proj, w_glu, b_glu, w_out, g_next):
    m, d = x2.shape
    wa = oa.shape[1]
    wb = y.shape[1]
    tm = min(TM_OUT, m)
    zb_col = (proj.shape[1] - wb) // wb
    const = lambda i: (0, 0)
    return pl.pallas_call(
        _even_out_kernel,
        grid=(m // tm,),
        in_specs=[pl.BlockSpec((tm, d), lambda i: (i, 0)),
                  pl.BlockSpec((tm, wa), lambda i: (i, 0)),
                  pl.BlockSpec((tm, wb), lambda i: (i, 0)),
                  pl.BlockSpec((tm, wb), lambda i: (i, zb_col)),
                  pl.BlockSpec((wb, wb), const),
                  pl.BlockSpec((1, wb), const),
                  pl.BlockSpec((wa + wb, d), const),
                  pl.BlockSpec((1, d), const)],
        out_specs=[pl.BlockSpec((tm, d), lambda i: (i, 0)),
                   pl.BlockSpec((tm, d), lambda i: (i, 0))],
        out_shape=[jax.ShapeDtypeStruct((m, d), F32),
                   jax.ShapeDtypeStruct((m, d), BF16)],
        compiler_params=_cparams(("arbitrary",)),
    )(x2, oa, y, proj, w_glu, b_glu.reshape(1, wb), w_out, g_next.reshape(1, d))


def _cum_forget_kernel(f_ref, b_ref, c_ref):
    x = f_ref[...] + b_ref[...]
    v = -(jnp.maximum(-x, 0.0) + jnp.log1p(jnp.exp(-jnp.abs(x))))
    n = v.shape[-1]
    pos = lax.broadcasted_iota(jnp.int32, v.shape, 1)
    d = 1
    while d < n:
        v = v + jnp.where(pos >= d, pltpu.roll(v, d, 1), 0.0)
        d *= 2
    c_ref[...] = v


def _cum_forget(f_t, b_forget, seq):
    nh, m = f_t.shape
    return pl.pallas_call(
        _cum_forget_kernel,
        grid=(m // seq,),
        in_specs=[pl.BlockSpec((nh, seq), lambda i: (0, i)),
                  pl.BlockSpec((nh, 1), lambda i: (0, 0))],
        out_specs=pl.BlockSpec((nh, seq), lambda i: (0, i)),
        out_shape=jax.ShapeDtypeStruct((nh, m), F32),
        compiler_params=_cparams(("arbitrary",)),
    )(f_t, b_forget.astype(F32).reshape(nh, 1))


def _fox_kernel(q_ref, k_ref, v_ref, z_ref, c_ref, o_ref):
    qi = pl.program_id(2)
    tq = q_ref.shape[1]
    tk = FOX_TK
    scale = 1.0 / math.sqrt(C_DH)
    q = (q_ref[0].astype(F32) * scale).astype(BF16)
    qs = pl.multiple_of(qi * tq, tq)
    c_base = jnp.max(c_ref[0, 0, :, pl.ds(qs, tq)], axis=1, keepdims=True)

    def step(j, carry, masked):
        m, l, acc = carry
        ks = pl.multiple_of(j * tk, tk)
        kb = k_ref[0, pl.ds(ks, tk), :]
        vb = v_ref[0, pl.ds(ks, tk), :]
        s = lax.dot_general(q, kb, (((1,), (1,)), ((), ())), preferred_element_type=F32)
        s = s + (c_base - c_ref[0, 0, :, pl.ds(ks, tk)])
        if masked:
            row = lax.broadcasted_iota(jnp.int32, s.shape, 0)
            col = lax.broadcasted_iota(jnp.int32, s.shape, 1)
            s = jnp.where(col <= row, s, NEG)
        m2 = jnp.maximum(m, jnp.max(s, axis=1, keepdims=True))
        alpha = jnp.exp(m - m2)
        p = jnp.exp(s - m2)
        l2 = alpha * l + jnp.sum(p, axis=1, keepdims=True)
        acc2 = alpha * acc + jnp.dot(p.astype(BF16), vb, preferred_element_type=F32)
        return m2, l2, acc2

    init = (jnp.full((tq, 1), NEG, F32), jnp.zeros((tq, 1), F32),
            jnp.zeros((tq, C_DH), F32))
    nfull = qi * (tq // tk)
    carry = lax.fori_loop(0, nfull, functools.partial(step, masked=False), init)
    m, l, acc = step(nfull, carry, True)
    o_ref[0] = (acc / l * _silu(z_ref[0].astype(F32))).astype(o_ref.dtype)


def _fox(proj3, cum4):
    b, l, _ = proj3.shape
    h = C_HEADS
    tq = min(FOX_TQ, l)
    return pl.pallas_call(
        _fox_kernel,
        grid=(b, h, l // tq),
        in_specs=[pl.BlockSpec((1, tq, C_DH), lambda bi, hi, qi: (bi, qi, hi)),
                  pl.BlockSpec((1, l, C_DH), lambda bi, hi, qi: (bi, 0, h + hi)),
                  pl.BlockSpec((1, l, C_DH), lambda bi, hi, qi: (bi, 0, 2 * h + hi)),
                  pl.BlockSpec((1, tq, C_DH), lambda bi, hi, qi: (bi, qi, 3 * h + hi)),
                  pl.BlockSpec((1, 1, 1, l), lambda bi, hi, qi: (bi, hi, 0, 0))],
        out_specs=pl.BlockSpec((1, tq, C_DH), lambda bi, hi, qi: (bi, qi, hi)),
        out_shape=jax.ShapeDtypeStruct((b, l, h * C_DH), BF16),
        compiler_params=_cparams(("arbitrary", "arbitrary", "arbitrary")),
    )(proj3, proj3, proj3, proj3, cum4)


def _odd_out_kernel(x_ref, o_ref, w_ref, g_ref, out_ref):
    x2 = x_ref[...] + jnp.dot(o_ref[...], w_ref[...], preferred_element_type=F32)
    out_ref[...] = _rms(x2, g_ref[...])


def _odd_out(x1, o, w_out, g):
    m, d = x1.shape
    k = o.shape[1]
    tm = min(TM_OUT, m)
    return pl.pallas_call(
        _odd_out_kernel,
        grid=(m // tm,),
        in_specs=[pl.BlockSpec((tm, d), lambda i: (i, 0)),
                  pl.BlockSpec((tm, k), lambda i: (i, 0)),
                  pl.BlockSpec((k, d), lambda i: (0, 0)),
                  pl.BlockSpec((1, d), lambda i: (0, 0))],
        out_specs=pl.BlockSpec((tm, d), lambda i: (i, 0)),
        out_shape=jax.ShapeDtypeStruct((m, d), F32),
        compiler_params=_cparams(("arbitrary",)),
    )(x1, o, w_out, g.reshape(1, d))


def kernel(x, norm_even_g, w_in_even, rel_bias, s5_lambda_re, s5_lambda_im, s5_log_dt,
           s5_b_re, s5_b_im, s5_c_re, s5_c_im, s5_d, w_glu, b_glu, w_out_even,
           norm_odd_g, w_in_odd, b_forget, w_out_odd, final_norm_g):
    b, l, d = x.shape
    assert norm_even_g.shape[0] == 1 and norm_odd_g.shape[0] == 1, "one even + one odd layer"
    assert l % FOX_TQ == 0 and l % A_QB == 0 and l >= A_KB
    m = b * l
    a_w = A_HEADS * A_DH
    s5_w = s5_d.shape[1] * S5_GROUP
    c_w = C_HEADS * C_DH

    x2 = x.reshape(m, d)
    proj = _norm_proj(x2, norm_even_g[0], w_in_even[0].astype(BF16))
    proj3 = proj.reshape(b, l, proj.shape[1])
    o_a = _attn_a(proj3, _rel_bias_table(rel_bias[0]))
    u3 = proj3[:, :, 3 * a_w:3 * a_w + s5_w]
    y = _s5_mixer(u3, s5_lambda_re[0], s5_lambda_im[0], s5_log_dt[0], s5_b_re[0],
                  s5_b_im[0], s5_c_re[0], s5_c_im[0], s5_d[0])
    x1, hn1 = _even_out(x2, o_a.reshape(m, a_w), y.reshape(m, s5_w), proj,
                        w_glu[0].astype(BF16), b_glu[0].astype(F32),
                        w_out_even[0].astype(BF16), norm_odd_g[0])
    w_odd = w_in_odd[0]
    proj1, f_t = _proj_forget(hn1, w_odd[:, :4 * c_w].astype(BF16),
                              w_odd[:, 4 * c_w:].T.astype(BF16))
    cum = _cum_forget(f_t, b_forget[0], l)
    cum4 = cum.reshape(C_HEADS, b, 1, l).transpose(1, 0, 2, 3)
    o_c = _fox(proj1.reshape(b, l, 4 * c_w), cum4)
    out = _odd_out(x1, o_c.reshape(m, c_w), w_out_odd[0].astype(BF16), final_norm_g)
    return out.reshape(b, l, d)
```

```python
import functools
import math

import jax
import jax.numpy as jnp
from jax import lax
from jax.experimental import pallas as pl
from jax.experimental.pallas import tpu as pltpu

F32 = jnp.float32
BF16 = jnp.bfloat16

EPS = 1e-6
NEG = -1e30
LOG2E = math.log2(math.e)

CHUNK = 64
N_LEFT = 8
A_HEADS = 16
A_DH = 64
MAX_REL = 128
S5_GROUP = 16
S5_STATE = 64
C_HEADS = 16
C_DH = 128

LANES = 128
VMEM_LIMIT = 56 * 1024 * 1024

TM_PROJ = 1024
TN_PROJ = 1024
TM_OUT = 512
A_QB = 2 * CHUNK
A_KB = A_QB + N_LEFT * CHUNK
A_TW = A_KB + N_LEFT * CHUNK
S5_TC = 16
FOX_TQ = 512
FOX_HEADS = 4


def _cparams(sem):
    return pltpu.CompilerParams(dimension_semantics=sem, vmem_limit_bytes=VMEM_LIMIT)


def _rms(x, g):
    ms = jnp.mean(x * x, axis=-1, keepdims=True)
    return x * lax.rsqrt(ms + EPS) * g


def _silu(x):
    return x * jax.nn.sigmoid(x)


def _gelu_tanh(x):
    c = math.sqrt(2.0 / math.pi)
    return 0.5 * x * (1.0 + jnp.tanh(c * (x + 0.044715 * (x * x * x))))


def _norm_proj_kernel(x_ref, g_ref, w_ref, o_ref, hn_ref):
    @pl.when(pl.program_id(1) == 0)
    def _():
        hn_ref[...] = _rms(x_ref[...], g_ref[...]).astype(BF16)

    o_ref[...] = jnp.dot(hn_ref[...], w_ref[...],
                         preferred_element_type=F32).astype(o_ref.dtype)


def _norm_proj(x2, g, w):
    m, d = x2.shape
    n = w.shape[1]
    tm, tn = min(TM_PROJ, m), TN_PROJ
    return pl.pallas_call(
        _norm_proj_kernel,
        name="norm_proj",
        grid=(m // tm, n // tn),
        in_specs=[pl.BlockSpec((tm, d), lambda i, j: (i, 0)),
                  pl.BlockSpec((1, d), lambda i, j: (0, 0)),
                  pl.BlockSpec((d, tn), lambda i, j: (0, j))],
        out_specs=pl.BlockSpec((tm, tn), lambda i, j: (i, j)),
        out_shape=jax.ShapeDtypeStruct((m, n), BF16),
        scratch_shapes=[pltpu.VMEM((tm, d), BF16)],
        compiler_params=_cparams(("arbitrary", "arbitrary")),
    )(x2, g.reshape(1, d), w)


def _proj_forget_kernel(h_ref, w_ref, wf_ref, o_ref, f_ref):
    @pl.when(pl.program_id(1) == 0)
    def _():
        f_ref[...] = lax.dot_general(wf_ref[...], h_ref[...],
                                     (((1,), (1,)), ((), ())),
                                     preferred_element_type=F32)

    o_ref[...] = jnp.dot(h_ref[...], w_ref[...],
                         preferred_element_type=F32).astype(o_ref.dtype)


def _proj_forget(hn, w, wf_t):
    m, d = hn.shape
    n = w.shape[1]
    nh = wf_t.shape[0]
    tm, tn = min(TM_PROJ, m), TN_PROJ
    return pl.pallas_call(
        _proj_forget_kernel,
        name="proj_forget",
        grid=(m // tm, n // tn),
        in_specs=[pl.BlockSpec((tm, d), lambda i, j: (i, 0)),
                  pl.BlockSpec((d, tn), lambda i, j: (0, j)),
                  pl.BlockSpec((nh, d), lambda i, j: (0, 0))],
        out_specs=[pl.BlockSpec((tm, tn), lambda i, j: (i, j)),
                   pl.BlockSpec((nh, tm), lambda i, j: (0, i))],
        out_shape=[jax.ShapeDtypeStruct((m, n), BF16),
                   jax.ShapeDtypeStruct((nh, m), F32)],
        compiler_params=_cparams(("arbitrary", "arbitrary")),
    )(hn, w, wf_t)


def _attn_a_kernel(q_ref, k_ref, v_ref, z_ref, t_ref, o_ref):
    qb = pl.program_id(2)
    start = qb * A_QB - N_LEFT * CHUNK
    start_c = jnp.maximum(start, 0)
    off = pl.multiple_of(start_c - start, LANES)
    start_c = pl.multiple_of(start_c, LANES)
    q = q_ref[0]
    kb = k_ref[0, pl.ds(start_c, A_KB), :]
    vb = v_ref[0, pl.ds(start_c, A_KB), :]
    scale = 1.0 / math.sqrt(A_DH)
    outs = []
    for h in range(2):
        sl = slice(h * A_DH, (h + 1) * A_DH)
        s = lax.dot_general(q[:, sl], kb[:, sl], (((1,), (1,)), ((), ())),
                            preferred_element_type=F32)
        s = s * scale + t_ref[h, :, pl.ds(off, A_KB)]
        m = jnp.max(s, axis=1, keepdims=True)
        p = jnp.exp(s - m)
        l = jnp.sum(p, axis=1, keepdims=True)
        o = jnp.dot(p.astype(BF16), vb[:, sl], preferred_element_type=F32)
        outs.append(o / l)
    o = jnp.concatenate(outs, axis=1)
    o_ref[0] = (o * _silu(z_ref[0].astype(F32))).astype(o_ref.dtype)


def _attn_a(proj3, table):
    b, l, _ = proj3.shape
    hp = A_HEADS // 2
    width = 2 * A_DH
    ncol = A_HEADS * A_DH // width
    zcol = (3 * A_HEADS * A_DH + (A_HEADS * A_DH)) // width
    return pl.pallas_call(
        _attn_a_kernel,
        name="attn_a",
        grid=(hp, b, l // A_QB),
        in_specs=[pl.BlockSpec((1, A_QB, width), lambda h, bi, qi: (bi, qi, h)),
                  pl.BlockSpec((1, l, width), lambda h, bi, qi: (bi, 0, ncol + h)),
                  pl.BlockSpec((1, l, width), lambda h, bi, qi: (bi, 0, 2 * ncol + h)),
                  pl.BlockSpec((1, A_QB, width), lambda h, bi, qi: (bi, qi, zcol + h)),
                  pl.BlockSpec((2, A_QB, A_TW), lambda h, bi, qi: (h, 0, 0))],
        out_specs=pl.BlockSpec((1, A_QB, width), lambda h, bi, qi: (bi, qi, h)),
        out_shape=jax.ShapeDtypeStruct((b, l, A_HEADS * A_DH), BF16),
        compiler_params=_cparams(("arbitrary", "arbitrary", "arbitrary")),
    )(proj3, proj3, proj3, proj3, table)


def _rel_bias_table(rel_bias):
    i = jnp.arange(A_QB)[:, None]
    m = jnp.arange(A_TW)[None, :]
    rel = jnp.clip(N_LEFT * CHUNK + i - m, -MAX_REL, MAX_REL) + MAX_REL
    dchunk = m // CHUNK - i // CHUNK
    ok = (dchunk >= 0) & (dchunk <= N_LEFT)
    return jnp.where(ok[None], rel_bias.astype(F32)[:, rel], NEG)


def _s5_params(lam_re, lam_im, log_dt, b_re, b_im, c_re, c_im):
    hi = lax.Precision.HIGHEST
    lr = lam_re.astype(F32)
    li = lam_im.astype(F32)
    dt = jnp.exp(log_dt.astype(F32))[:, None]
    n = jnp.arange(S5_TC + 1, dtype=F32)[:, None, None]
    mag = jnp.exp(n * (lr * dt)[None])
    ang = n * (li * dt)[None]
    pw_re = mag * jnp.cos(ang)
    pw_im = mag * jnp.sin(ang)
    a_re, a_im = pw_re[1], pw_im[1]
    den = lr * lr + li * li
    nr = a_re - 1.0
    ni = a_im
    coef_re = (nr * lr + ni * li) / den
    coef_im = (ni * lr - nr * li) / den
    br = b_re.astype(F32)
    bi = b_im.astype(F32)
    bb_re = coef_re[..., None] * br - coef_im[..., None] * bi
    bb_im = coef_re[..., None] * bi + coef_im[..., None] * br
    cr = c_re.astype(F32)
    ci = c_im.astype(F32)
    g = lr.shape[0]
    rp_re = pw_re[S5_TC - 1::-1][:S5_TC]
    rp_im = pw_im[S5_TC - 1::-1][:S5_TC]
    r_re = rp_re[..., None] * bb_re[None] - rp_im[..., None] * bb_im[None]
    r_im = rp_re[..., None] * bb_im[None] + rp_im[..., None] * bb_re[None]
    r_re = r_re.transpose(1, 0, 3, 2).reshape(g, S5_TC * S5_GROUP, S5_STATE)
    r_im = r_im.transpose(1, 0, 3, 2).reshape(g, S5_TC * S5_GROUP, S5_STATE)
    rt = jnp.concatenate([r_re, r_im, r_im, r_re], axis=-1)
    ca_re = cr[None] * pw_re[:, :, None, :] - ci[None] * pw_im[:, :, None, :]
    ca_im = cr[None] * pw_im[:, :, None, :] + ci[None] * pw_re[:, :, None, :]
    o_re = ca_re[1:].transpose(1, 3, 0, 2).reshape(g, S5_STATE, S5_TC * S5_GROUP)
    o_im = ca_im[1:].transpose(1, 3, 0, 2).reshape(g, S5_STATE, S5_TC * S5_GROUP)
    ot = jnp.concatenate([o_re, -o_im], axis=1)
    kern = (jnp.einsum('ngcp,gpd->ngcd', ca_re[:S5_TC], bb_re, precision=hi)
            - jnp.einsum('ngcp,gpd->ngcd', ca_im[:S5_TC], bb_im, precision=hi))
    t = jnp.arange(S5_TC)
    lag = t[None, :] - t[:, None]
    mt = jnp.where((lag >= 0)[:, :, None, None, None],
                   kern[jnp.clip(lag, 0, S5_TC - 1)], 0.0)
    mt = mt.transpose(2, 0, 4, 1, 3).reshape(g, S5_TC * S5_GROUP, S5_TC * S5_GROUP)
    p_row = jnp.concatenate([pw_re[S5_TC], pw_re[S5_TC]], axis=-1)
    q_row = jnp.concatenate([-pw_im[S5_TC], pw_im[S5_TC]], axis=-1)
    return rt, mt, ot, p_row, q_row


def _s5_state_in_kernel(u_ref, rt_ref, s_ref):
    nb, _, nk, w = u_ref.shape
    u = u_ref[...].reshape(nb * nk, w)
    s = jnp.dot(u, rt_ref[0], preferred_element_type=F32)
    s_ref[...] = s.reshape(nb, 1, nk, s.shape[-1])


def _s5_state_in(u4, rt):
    b, g, nk, w = u4.shape
    n = rt.shape[-1]
    return pl.pallas_call(
        _s5_state_in_kernel,
        name="s5_state_in",
        grid=(g,),
        in_specs=[pl.BlockSpec((b, 1, nk, w), lambda gi: (0, gi, 0, 0)),
                  pl.BlockSpec((1, w, n), lambda gi: (gi, 0, 0))],
        out_specs=pl.BlockSpec((b, 1, nk, n), lambda gi: (0, gi, 0, 0)),
        out_shape=jax.ShapeDtypeStruct((b, g, nk, n), F32),
        compiler_params=_cparams(("arbitrary",)),
    )(u4, rt)


def _s5_scan_kernel(s_ref, p_ref, q_ref, x_ref):
    nk = s_ref.shape[0]
    half = s_ref.shape[-1] // 2
    p = p_ref[...]
    q = q_ref[...]

    def body(k, carry):
        v, w = carry
        x_ref[k] = v
        s = s_ref[k]
        v2 = p * v + q * w + s[:, :half]
        w2 = p * w - q * v + s[:, half:]
        return v2, w2

    zero = jnp.zeros(p.shape, F32)
    lax.fori_loop(0, nk, body, (zero, zero))


def _s5_scan(s_t, p_rows, q_rows):
    nk, r, n = s_t.shape
    half = n // 2
    tr = min(64, r)
    return pl.pallas_call(
        _s5_scan_kernel,
        name="s5_scan",
        grid=(r // tr,),
        in_specs=[pl.BlockSpec((nk, tr, n), lambda i: (0, i, 0)),
                  pl.BlockSpec((tr, half), lambda i: (i, 0)),
                  pl.BlockSpec((tr, half), lambda i: (i, 0))],
        out_specs=pl.BlockSpec((nk, tr, half), lambda i: (0, i, 0)),
        out_shape=jax.ShapeDtypeStruct((nk, r, half), F32),
        compiler_params=_cparams(("arbitrary",)),
    )(s_t, p_rows, q_rows)


def _s5_out_kernel(u_ref, x_ref, mt_ref, ot_ref, d_ref, y_ref):
    nb, _, nk, w = u_ref.shape
    u = u_ref[...].reshape(nb * nk, w)
    x = x_ref[...].reshape(nb * nk, x_ref.shape[-1]).astype(BF16)
    y = jnp.dot(u, mt_ref[0], preferred_element_type=F32)
    y = y + jnp.dot(x, ot_ref[0], preferred_element_type=F32)
    y = y + d_ref[0] * u.astype(F32)
    y_ref[...] = _gelu_tanh(y).reshape(nb, 1, nk, w).astype(y_ref.dtype)


def _s5_out(u4, x4, mt, ot, d_rows):
    b, g, nk, w = u4.shape
    ns = x4.shape[-1]
    return pl.pallas_call(
        _s5_out_kernel,
        name="s5_out",
        grid=(g,),
        in_specs=[pl.BlockSpec((b, 1, nk, w), lambda gi: (0, gi, 0, 0)),
                  pl.BlockSpec((b, 1, nk, ns), lambda gi: (0, gi, 0, 0)),
                  pl.BlockSpec((1, w, w), lambda gi: (gi, 0, 0)),
                  pl.BlockSpec((1, ns, w), lambda gi: (gi, 0, 0)),
                  pl.BlockSpec((1, 1, w), lambda gi: (gi, 0, 0))],
        out_specs=pl.BlockSpec((b, 1, nk, w), lambda gi: (0, gi, 0, 0)),
        out_shape=jax.ShapeDtypeStruct((b, g, nk, w), BF16),
        compiler_params=_cparams(("arbitrary",)),
    )(u4, x4, mt, ot, d_rows)


def _s5_mixer(u3, lam_re, lam_im, log_dt, b_re, b_im, c_re, c_im, d_skip):
    b, l, wd = u3.shape
    g = wd // S5_GROUP
    nk = l // S5_TC
    rt, mt, ot, p_row, q_row = _s5_params(lam_re, lam_im, log_dt, b_re, b_im, c_re, c_im)
    u4 = u3.reshape(b, nk, S5_TC, g, S5_GROUP).transpose(0, 3, 1, 2, 4)
    u4 = u4.reshape(b, g, nk, S5_TC * S5_GROUP)
    s4 = _s5_state_in(u4, rt.astype(BF16))
    s_t = s4.transpose(2, 0, 1, 3).reshape(nk, b * g, s4.shape[-1])
    x_t = _s5_scan(s_t, jnp.tile(p_row, (b, 1)), jnp.tile(q_row, (b, 1)))
    x4 = x_t.reshape(nk, b, g, x_t.shape[-1]).transpose(1, 2, 0, 3)
    d_rows = jnp.tile(d_skip.astype(F32), (1, S5_TC)).reshape(g, 1, S5_TC * S5_GROUP)
    y4 = _s5_out(u4, x4, mt.astype(BF16), ot.astype(BF16), d_rows)
    y = y4.reshape(b, g, nk, S5_TC, S5_GROUP).transpose(0, 2, 3, 1, 4)
    return y.reshape(b, l, wd)


def _even_out_kernel(x_ref, oa_ref, y_ref, zb_ref, wg_ref, bg_ref, wo_ref, g_ref,
                     x1_ref, hn_ref):
    y = y_ref[...]
    t = jnp.dot(y, wg_ref[...], preferred_element_type=F32) + bg_ref[...]
    ob = y.astype(F32) * jax.nn.sigmoid(t) * _silu(zb_ref[...].astype(F32))
    wa = oa_ref.shape[1]
    acc = jnp.dot(oa_ref[...], wo_ref[:wa, :], preferred_element_type=F32)
    acc = acc + jnp.dot(ob.astype(BF16), wo_ref[wa:, :], preferred_element_type=F32)
    x1 = x_ref[...] + acc
    x1_ref[...] = x1
    hn_ref[...] = _rms(x1, g_ref[...]).astype(BF16)


def _even_out(x2, oa, y, proj, w_glu, b_glu, w_out, g_next):
    m, d = x2.shape
    wa = oa.shape[1]
    wb = y.shape[1]
    tm = min(TM_OUT, m)
    zb_col = (proj.shape[1] - wb) // wb
    const = lambda i: (0, 0)
    return pl.pallas_call(
        _even_out_kernel,
        name="even_out",
        grid=(m // tm,),
        in_specs=[pl.BlockSpec((tm, d), lambda i: (i, 0)),
                  pl.BlockSpec((tm, wa), lambda i: (i, 0)),
                  pl.BlockSpec((tm, wb), lambda i: (i, 0)),
                  pl.BlockSpec((tm, wb), lambda i: (i, zb_col)),
                  pl.BlockSpec((wb, wb), const),
                  pl.BlockSpec((1, wb), const),
                  pl.BlockSpec((wa + wb, d), const),
                  pl.BlockSpec((1, d), const)],
        out_specs=[pl.BlockSpec((tm, d), lambda i: (i, 0)),
                   pl.BlockSpec((tm, d), lambda i: (i, 0))],
        out_shape=[jax.ShapeDtypeStruct((m, d), F32),
                   jax.ShapeDtypeStruct((m, d), BF16)],
        compiler_params=_cparams(("arbitrary",)),
    )(x2, oa, y, proj, w_glu, b_glu.reshape(1, wb), w_out, g_next.reshape(1, d))


def _cum_forget_kernel(f_ref, b_ref, c_ref):
    x = f_ref[...] + b_ref[...]
    v = -(jnp.maximum(-x, 0.0) + jnp.log1p(jnp.exp(-jnp.abs(x))))
    n = v.shape[-1]
    pos = lax.broadcasted_iota(jnp.int32, v.shape, 1)
    d = 1
    while d < n:
        v = v + jnp.where(pos >= d, pltpu.roll(v, d, 1), 0.0)
        d *= 2
    c_ref[...] = v


def _cum_forget(f_t, b_forget, seq):
    nh, m = f_t.shape
    return pl.pallas_call(
        _cum_forget_kernel,
        name="cum_forget",
        grid=(m // seq,),
        in_specs=[pl.BlockSpec((nh, seq), lambda i: (0, i)),
                  pl.BlockSpec((nh, 1), lambda i: (0, 0))],
        out_specs=pl.BlockSpec((nh, seq), lambda i: (0, i)),
        out_shape=jax.ShapeDtypeStruct((nh, m), F32),
        compiler_params=_cparams(("arbitrary",)),
    )(f_t, b_forget.astype(F32).reshape(nh, 1))


def _split3(x):
    hi = x.astype(BF16).astype(F32)
    r = x - hi
    mid = r.astype(BF16).astype(F32)
    lo = (r - mid).astype(BF16).astype(F32)
    return hi, mid, lo


def _fox_kernel(q_ref, k_ref, v_ref, z_ref, c_ref, o_ref, ke_ref, vt_ref, s_ref, acc_ref):
    qi = pl.program_id(2)
    tq = q_ref.shape[1]
    tk = tq
    dh = C_DH
    nh = c_ref.shape[1]
    nblk = k_ref.shape[1] // tk
    nt = (((1,), (1,)), ((), ()))

    def cols(h):
        return slice(h * dh, (h + 1) * dh)

    @pl.when(qi == 0)
    def _():
        r = lax.broadcasted_iota(jnp.int32, (dh, dh), 0)
        c = lax.broadcasted_iota(jnp.int32, (dh, dh), 1)
        eye = (r == c).astype(BF16)
        row = lax.broadcasted_iota(jnp.int32, (dh, tk), 0)

        def blk(j, carry):
            ks = pl.multiple_of(j * tk, tk)
            for h in range(nh):
                hi, mid, lo = _split3(c_ref[0, h, :, pl.ds(ks, tk)] * (-LOG2E))
                a = jnp.where(row == 0, hi, jnp.where(row == 1, mid, jnp.where(row == 2, lo,
                              jnp.where(row < 6, 1.0, 0.0))))
                ke_ref[h, pl.ds(ks, tk), :] = a.T.astype(BF16)
                vt = lax.dot_general(eye, v_ref[0, pl.ds(ks, tk), cols(h)], nt,
                                     preferred_element_type=F32)
                vt_ref[h, :, pl.ds(ks, tk)] = vt.astype(BF16)
            return carry

        lax.fori_loop(0, nblk, blk, 0)

    qs = pl.multiple_of(qi * tq, tq)
    lane = lax.broadcasted_iota(jnp.int32, (tq, dh), 1)
    qscale = LOG2E / math.sqrt(dh)
    qa = []
    for h in range(nh):
        c_base = jnp.max(c_ref[0, h, :, pl.ds(qs, tq)], axis=1, keepdims=True) * LOG2E
        bh, bm, bl = _split3(c_base)
        g = jnp.where(lane < 3, 1.0, jnp.where(lane == 3, bh, jnp.where(lane == 4, bm,
                      jnp.where(lane == 5, bl, 0.0))))
        qh = (q_ref[0, :, cols(h)].astype(F32) * qscale).astype(BF16)
        qa.append(jnp.concatenate([qh, g.astype(BF16)], axis=1))

    def scores(h, j):
        ks = pl.multiple_of(j * tk, tk)
        ka = jnp.concatenate([k_ref[0, pl.ds(ks, tk), cols(h)], ke_ref[h, pl.ds(ks, tk), :]],
                             axis=1)
        return lax.dot_general(ka, qa[h], nt, preferred_element_type=F32)

    key = lax.broadcasted_iota(jnp.int32, (tk, tq), 0)
    qry = lax.broadcasted_iota(jnp.int32, (tk, tq), 1)

    def produce(h, j, keep):
        s = scores(h, j)
        if keep is not None:
            s = jnp.where(keep, s, NEG)
        s_ref[h] = s
        return jnp.max(s, axis=0, keepdims=True)

    def consume(h, j, stats):
        m, l, mb = stats
        ks = pl.multiple_of(j * tk, tk)
        m2 = jnp.maximum(m, mb)
        alpha = jnp.exp2(m - m2)
        p = jnp.exp2(s_ref[h] - m2)
        l2 = alpha * l + jnp.sum(p, axis=0, keepdims=True)
        pv = jnp.dot(vt_ref[h, :, pl.ds(ks, tk)], p.astype(BF16), preferred_element_type=F32)
        acc_ref[h] = alpha * acc_ref[h] + pv
        return m2, l2

    def step(j, carry, diagonal_next):
        keep = (key <= qry) if diagonal_next else None
        out = []
        for h, stats in enumerate(carry):
            m2, l2 = consume(h, j, stats)
            out.append((m2, l2, produce(h, j + 1, keep)))
        return tuple(out)

    first_keep = (key <= qry) | (qi > 0)
    carry = []
    for h in range(nh):
        acc_ref[h] = jnp.zeros((dh, tq), F32)
        carry.append((jnp.full((1, tq), NEG, F32), jnp.zeros((1, tq), F32),
                      produce(h, 0, first_keep)))
    carry = lax.fori_loop(0, qi - 1, functools.partial(step, diagonal_next=False), tuple(carry))
    carry = lax.fori_loop(jnp.maximum(qi - 1, 0), qi,
                          functools.partial(step, diagonal_next=True), carry)
    for h, stats in enumerate(carry):
        m, l = consume(h, qi, stats)
        o = (acc_ref[h] / l).T
        o_ref[0, :, cols(h)] = (o * _silu(z_ref[0, :, cols(h)].astype(F32))).astype(o_ref.dtype)


def _fox(proj3, cum4):
    b, l, _ = proj3.shape
    ng = C_HEADS // FOX_HEADS
    w = FOX_HEADS * C_DH
    tq = min(FOX_TQ, l)
    return pl.pallas_call(
        _fox_kernel,
        name="fox",
        grid=(b, ng, l // tq),
        in_specs=[pl.BlockSpec((1, tq, w), lambda bi, gi, qi: (bi, qi, gi)),
                  pl.BlockSpec((1, l, w), lambda bi, gi, qi: (bi, 0, ng + gi)),
                  pl.BlockSpec((1, l, w), lambda bi, gi, qi: (bi, 0, 2 * ng + gi)),
                  pl.BlockSpec((1, tq, w), lambda bi, gi, qi: (bi, qi, 3 * ng + gi)),
                  pl.BlockSpec((1, FOX_HEADS, 1, l), lambda bi, gi, qi: (bi, gi, 0, 0))],
        out_specs=pl.BlockSpec((1, tq, w), lambda bi, gi, qi: (bi, qi, gi)),
        out_shape=jax.ShapeDtypeStruct((b, l, C_HEADS * C_DH), BF16),
        scratch_shapes=[pltpu.VMEM((FOX_HEADS, l, C_DH), BF16),
                        pltpu.VMEM((FOX_HEADS, C_DH, l), BF16),
                        pltpu.VMEM((FOX_HEADS, tq, tq), F32),
                        pltpu.VMEM((FOX_HEADS, C_DH, tq), F32)],
        compiler_params=_cparams(("arbitrary", "arbitrary", "arbitrary")),
    )(proj3, proj3, proj3, proj3, cum4)


def _odd_out_kernel(x_ref, o_ref, w_ref, g_ref, out_ref):
    x2 = x_ref[...] + jnp.dot(o_ref[...], w_ref[...], preferred_element_type=F32)
    out_ref[...] = _rms(x2, g_ref[...])


def _odd_out(x1, o, w_out, g):
    m, d = x1.shape
    k = o.shape[1]
    tm = min(TM_OUT, m)
    return pl.pallas_call(
        _odd_out_kernel,
        name="odd_out",
        grid=(m // tm,),
        in_specs=[pl.BlockSpec((tm, d), lambda i: (i, 0)),
                  pl.BlockSpec((tm, k), lambda i: (i, 0)),
                  pl.BlockSpec((k, d), lambda i: (0, 0)),
                  pl.BlockSpec((1, d), lambda i: (0, 0))],
        out_specs=pl.BlockSpec((tm, d), lambda i: (i, 0)),
        out_shape=jax.ShapeDtypeStruct((m, d), F32),
        compiler_params=_cparams(("arbitrary",)),
    )(x1, o, w_out, g.reshape(1, d))


def kernel(x, norm_even_g, w_in_even, rel_bias, s5_lambda_re, s5_lambda_im, s5_log_dt,
           s5_b_re, s5_b_im, s5_c_re, s5_c_im, s5_d, w_glu, b_glu, w_out_even,
           norm_odd_g, w_in_odd, b_forget, w_out_odd, final_norm_g):
    b, l, d = x.shape
    assert norm_even_g.shape[0] == 1 and norm_odd_g.shape[0] == 1, "one even + one odd layer"
    assert l % FOX_TQ == 0 and l % A_QB == 0 and l >= A_KB
    m = b * l
    a_w = A_HEADS * A_DH
    s5_w = s5_d.shape[1] * S5_GROUP
    c_w = C_HEADS * C_DH

    x2 = x.reshape(m, d)
    proj = _norm_proj(x2, norm_even_g[0], w_in_even[0].astype(BF16))
    proj3 = proj.reshape(b, l, proj.shape[1])
    o_a = _attn_a(proj3, _rel_bias_table(rel_bias[0]))
    u3 = proj3[:, :, 3 * a_w:3 * a_w + s5_w]
    y = _s5_mixer(u3, s5_lambda_re[0], s5_lambda_im[0], s5_log_dt[0], s5_b_re[0],
                  s5_b_im[0], s5_c_re[0], s5_c_im[0], s5_d[0])
    x1, hn1 = _even_out(x2, o_a.reshape(m, a_w), y.reshape(m, s5_w), proj,
                        w_glu[0].astype(BF16), b_glu[0].astype(F32),
                        w_out_even[0].astype(BF16), norm_odd_g[0])
    w_odd = w_in_odd[0]
    proj1, f_t = _proj_forget(hn1, w_odd[:, :4 * c_w].astype(BF16),
                              w_odd[:, 4 * c_w:].T.astype(BF16))
    cum = _cum_forget(f_t, b_forget[0], l)
    cum4 = cum.reshape(C_HEADS, b, 1, l).transpose(1, 0, 2, 3)
    o_c = _fox(proj1.reshape(b, l, 4 * c_w), cum4)
    out = _odd_out(x1, o_c.reshape(m, c_w), w_out_odd[0].astype(BF16), final_norm_g)
    return out.reshape(b, l, d)
```

```python
import functools
import math

import jax
import jax.numpy as jnp
from jax import lax
from jax.experimental import pallas as pl
from jax.experimental.pallas import tpu as pltpu

F32 = jnp.float32
BF16 = jnp.bfloat16

EPS = 1e-6
NEG = -1e30
LOG2E = math.log2(math.e)

CHUNK = 64
N_LEFT = 8
A_HEADS = 16
A_DH = 64
MAX_REL = 128
S5_GROUP = 16
S5_STATE = 64
C_HEADS = 16
C_DH = 128

LANES = 128
VMEM_LIMIT = 56 * 1024 * 1024

TM_PROJ = 1024
TN_PROJ = 1024
TM_OUT = 512
A_QB = 2 * CHUNK
A_KB = A_QB + N_LEFT * CHUNK
A_TW = A_KB + N_LEFT * CHUNK
A_UNROLL = 4
VT_BLK = 512
S5_TC = 16
FOX_TQ = 512
FOX_HEADS = 4


def _cparams(sem):
    return pltpu.CompilerParams(dimension_semantics=sem, vmem_limit_bytes=VMEM_LIMIT)


def _rms(x, g):
    ms = jnp.mean(x * x, axis=-1, keepdims=True)
    return x * lax.rsqrt(ms + EPS) * g


def _silu(x):
    return x * jax.nn.sigmoid(x)


def _gelu_tanh(x):
    c = math.sqrt(2.0 / math.pi)
    return 0.5 * x * (1.0 + jnp.tanh(c * (x + 0.044715 * (x * x * x))))


def _norm_proj_kernel(x_ref, g_ref, w_ref, o_ref, hn_ref):
    @pl.when(pl.program_id(1) == 0)
    def _():
        hn_ref[...] = _rms(x_ref[...], g_ref[...]).astype(BF16)

    o_ref[...] = jnp.dot(hn_ref[...], w_ref[...],
                         preferred_element_type=F32).astype(o_ref.dtype)


def _norm_proj(x2, g, w):
    m, d = x2.shape
    n = w.shape[1]
    tm, tn = min(TM_PROJ, m), TN_PROJ
    return pl.pallas_call(
        _norm_proj_kernel,
        name="norm_proj",
        grid=(m // tm, n // tn),
        in_specs=[pl.BlockSpec((tm, d), lambda i, j: (i, 0)),
                  pl.BlockSpec((1, d), lambda i, j: (0, 0)),
                  pl.BlockSpec((d, tn), lambda i, j: (0, j))],
        out_specs=pl.BlockSpec((tm, tn), lambda i, j: (i, j)),
        out_shape=jax.ShapeDtypeStruct((m, n), BF16),
        scratch_shapes=[pltpu.VMEM((tm, d), BF16)],
        compiler_params=_cparams(("arbitrary", "arbitrary")),
    )(x2, g.reshape(1, d), w)


def _proj_forget_kernel(h_ref, w_ref, wf_ref, o_ref, f_ref):
    @pl.when(pl.program_id(1) == 0)
    def _():
        f_ref[...] = lax.dot_general(wf_ref[...], h_ref[...],
                                     (((1,), (1,)), ((), ())),
                                     preferred_element_type=F32)

    o_ref[...] = jnp.dot(h_ref[...], w_ref[...],
                         preferred_element_type=F32).astype(o_ref.dtype)


def _proj_forget(hn, w, wf_t):
    m, d = hn.shape
    n = w.shape[1]
    nh = wf_t.shape[0]
    tm, tn = min(TM_PROJ, m), TN_PROJ
    return pl.pallas_call(
        _proj_forget_kernel,
        name="proj_forget",
        grid=(m // tm, n // tn),
        in_specs=[pl.BlockSpec((tm, d), lambda i, j: (i, 0)),
                  pl.BlockSpec((d, tn), lambda i, j: (0, j)),
                  pl.BlockSpec((nh, d), lambda i, j: (0, 0))],
        out_specs=[pl.BlockSpec((tm, tn), lambda i, j: (i, j)),
                   pl.BlockSpec((nh, tm), lambda i, j: (0, i))],
        out_shape=[jax.ShapeDtypeStruct((m, n), BF16),
                   jax.ShapeDtypeStruct((nh, m), F32)],
        compiler_params=_cparams(("arbitrary", "arbitrary")),
    )(hn, w, wf_t)


def _attn_a_kernel(q_ref, k_ref, v_ref, z_ref, e_ref, o_ref, tt_ref, vt_ref):
    l_all = k_ref.shape[1]
    w = q_ref.shape[2]
    nt = (((1,), (1,)), ((), ()))
    band = N_LEFT * CHUNK

    @pl.when(pl.program_id(1) == 0)
    def _():
        i = lax.broadcasted_iota(jnp.int32, (A_QB, A_TW), 0)
        m = lax.broadcasted_iota(jnp.int32, (A_QB, A_TW), 1)
        dchunk = m // CHUNK - i // CHUNK
        ok = (dchunk >= 0) & (dchunk <= N_LEFT)
        for h in range(2):
            rows = jnp.broadcast_to(e_ref[0, h:h + 1, :], (A_QB, A_QB + A_TW))
            toep = pltpu.roll(rows, 0, 1, stride=1, stride_axis=0)[:, A_QB:]
            t = jnp.where(ok, toep * LOG2E, NEG)
            tt_ref[:, h * A_QB:(h + 1) * A_QB] = t.T

    r = lax.broadcasted_iota(jnp.int32, (w, w), 0)
    c = lax.broadcasted_iota(jnp.int32, (w, w), 1)
    eye = (r == c).astype(BF16)

    def vt_blk(j, carry):
        ks = pl.multiple_of(j * VT_BLK, VT_BLK)
        vt = lax.dot_general(eye, v_ref[0, pl.ds(ks, VT_BLK), :], nt, preferred_element_type=F32)
        vt_ref[:, pl.ds(ks, VT_BLK)] = vt.astype(BF16)
        return carry

    lax.fori_loop(0, l_all // VT_BLK, vt_blk, 0)

    lane = lax.broadcasted_iota(jnp.int32, (A_QB, w), 1)
    qscale = LOG2E / math.sqrt(A_DH)

    def block(qb, carry):
        qs = pl.multiple_of(qb * A_QB, A_QB)
        start = qs - band
        start_c = jnp.maximum(start, 0)
        off = pl.multiple_of(start_c - start, LANES)
        start_c = pl.multiple_of(start_c, LANES)
        q = (q_ref[0, pl.ds(qs, A_QB), :].astype(F32) * qscale).astype(BF16)
        zero = jnp.zeros_like(q)
        qbd = jnp.concatenate([jnp.where(lane < A_DH, q, zero), jnp.where(lane >= A_DH, q, zero)],
                              axis=0)
        kb = k_ref[0, pl.ds(start_c, A_KB), :]
        s = lax.dot_general(kb, qbd, nt, preferred_element_type=F32)
        s = s + tt_ref[pl.ds(off, A_KB), :]
        m = jnp.max(s, axis=0, keepdims=True)
        p = jnp.exp2(s - m)
        inv_l = 1.0 / jnp.sum(p, axis=0, keepdims=True)
        pb = p.astype(BF16)
        outs = []
        for h in range(2):
            vt = vt_ref[h * A_DH:(h + 1) * A_DH, pl.ds(start_c, A_KB)]
            oh = jnp.dot(vt, pb[:, h * A_QB:(h + 1) * A_QB], preferred_element_type=F32)
            outs.append(oh * inv_l[:, h * A_QB:(h + 1) * A_QB])
        o = jnp.concatenate(outs, axis=0).T
        gate = _silu(z_ref[0, pl.ds(qs, A_QB), :].astype(F32))
        o_ref[0, pl.ds(qs, A_QB), :] = (o * gate).astype(o_ref.dtype)
        return carry

    lax.fori_loop(0, l_all // A_QB, block, 0, unroll=A_UNROLL)


def _attn_a(proj3, rel_bias):
    b, l, _ = proj3.shape
    hp = A_HEADS // 2
    width = 2 * A_DH
    ncol = A_HEADS * A_DH // width
    zcol = (3 * A_HEADS * A_DH + (A_HEADS * A_DH)) // width
    left = N_LEFT * CHUNK + A_QB - MAX_REL
    right = A_QB + A_TW - left - (2 * MAX_REL + 1)
    ext = jnp.pad(rel_bias.astype(F32)[:, ::-1], ((0, 0), (left, right)), mode="edge")
    ext = ext.reshape(hp, 2, A_QB + A_TW)
    seq = lambda h, bi: (bi, 0, h)
    return pl.pallas_call(
        _attn_a_kernel,
        name="attn_a",
        grid=(hp, b),
        in_specs=[pl.BlockSpec((1, l, width), seq),
                  pl.BlockSpec((1, l, width), lambda h, bi: (bi, 0, ncol + h)),
                  pl.BlockSpec((1, l, width), lambda h, bi: (bi, 0, 2 * ncol + h)),
                  pl.BlockSpec((1, l, width), lambda h, bi: (bi, 0, zcol + h)),
                  pl.BlockSpec((1, 2, A_QB + A_TW), lambda h, bi: (h, 0, 0))],
        out_specs=pl.BlockSpec((1, l, width), seq),
        out_shape=jax.ShapeDtypeStruct((b, l, A_HEADS * A_DH), BF16),
        scratch_shapes=[pltpu.VMEM((A_TW, 2 * A_QB), F32),
                        pltpu.VMEM((width, l), BF16)],
        compiler_params=_cparams(("arbitrary", "arbitrary")),
    )(proj3, proj3, proj3, proj3, ext)


def _s5_params(lam_re, lam_im, log_dt, b_re, b_im, c_re, c_im):
    hi = lax.Precision.HIGHEST
    lr = lam_re.astype(F32)
    li = lam_im.astype(F32)
    dt = jnp.exp(log_dt.astype(F32))[:, None]
    n = jnp.arange(S5_TC + 1, dtype=F32)[:, None, None]
    mag = jnp.exp(n * (lr * dt)[None])
    ang = n * (li * dt)[None]
    pw_re = mag * jnp.cos(ang)
    pw_im = mag * jnp.sin(ang)
    a_re, a_im = pw_re[1], pw_im[1]
    den = lr * lr + li * li
    nr = a_re - 1.0
    ni = a_im
    coef_re = (nr * lr + ni * li) / den
    coef_im = (ni * lr - nr * li) / den
    br = b_re.astype(F32)
    bi = b_im.astype(F32)
    bb_re = coef_re[..., None] * br - coef_im[..., None] * bi
    bb_im = coef_re[..., None] * bi + coef_im[..., None] * br
    cr = c_re.astype(F32)
    ci = c_im.astype(F32)
    g = lr.shape[0]
    rp_re = pw_re[S5_TC - 1::-1][:S5_TC]
    rp_im = pw_im[S5_TC - 1::-1][:S5_TC]
    r_re = rp_re[..., None] * bb_re[None] - rp_im[..., None] * bb_im[None]
    r_im = rp_re[..., None] * bb_im[None] + rp_im[..., None] * bb_re[None]
    r_re = r_re.transpose(1, 0, 3, 2).reshape(g, S5_TC * S5_GROUP, S5_STATE)
    r_im = r_im.transpose(1, 0, 3, 2).reshape(g, S5_TC * S5_GROUP, S5_STATE)
    rt = jnp.concatenate([r_re, r_im, r_im, r_re], axis=-1)
    ca_re = cr[None] * pw_re[:, :, None, :] - ci[None] * pw_im[:, :, None, :]
    ca_im = cr[None] * pw_im[:, :, None, :] + ci[None] * pw_re[:, :, None, :]
    o_re = ca_re[1:].transpose(1, 3, 0, 2).reshape(g, S5_STATE, S5_TC * S5_GROUP)
    o_im = ca_im[1:].transpose(1, 3, 0, 2).reshape(g, S5_STATE, S5_TC * S5_GROUP)
    ot = jnp.concatenate([o_re, -o_im], axis=1)
    kern = (jnp.einsum('ngcp,gpd->ngcd', ca_re[:S5_TC], bb_re, precision=hi)
            - jnp.einsum('ngcp,gpd->ngcd', ca_im[:S5_TC], bb_im, precision=hi))
    t = jnp.arange(S5_TC)
    lag = t[None, :] - t[:, None]
    mt = jnp.where((lag >= 0)[:, :, None, None, None],
                   kern[jnp.clip(lag, 0, S5_TC - 1)], 0.0)
    mt = mt.transpose(2, 0, 4, 1, 3).reshape(g, S5_TC * S5_GROUP, S5_TC * S5_GROUP)
    p_row = jnp.concatenate([pw_re[S5_TC], pw_re[S5_TC]], axis=-1)
    q_row = jnp.concatenate([-pw_im[S5_TC], pw_im[S5_TC]], axis=-1)
    return rt, mt, ot, p_row, q_row


def _s5_state_in_kernel(u_ref, rt_ref, s_ref):
    nb, _, nk, w = u_ref.shape
    u = u_ref[...].reshape(nb * nk, w)
    s = jnp.dot(u, rt_ref[0], preferred_element_type=F32)
    s_ref[...] = s.reshape(nb, 1, nk, s.shape[-1])


def _s5_state_in(u4, rt):
    b, g, nk, w = u4.shape
    n = rt.shape[-1]
    return pl.pallas_call(
        _s5_state_in_kernel,
        name="s5_state_in",
        grid=(g,),
        in_specs=[pl.BlockSpec((b, 1, nk, w), lambda gi: (0, gi, 0, 0)),
                  pl.BlockSpec((1, w, n), lambda gi: (gi, 0, 0))],
        out_specs=pl.BlockSpec((b, 1, nk, n), lambda gi: (0, gi, 0, 0)),
        out_shape=jax.ShapeDtypeStruct((b, g, nk, n), F32),
        compiler_params=_cparams(("arbitrary",)),
    )(u4, rt)


def _s5_scan_kernel(s_ref, p_ref, q_ref, x_ref):
    nk = s_ref.shape[0]
    half = s_ref.shape[-1] // 2
    p = p_ref[...]
    q = q_ref[...]

    def body(k, carry):
        v, w = carry
        x_ref[k] = v
        s = s_ref[k]
        v2 = p * v + q * w + s[:, :half]
        w2 = p * w - q * v + s[:, half:]
        return v2, w2

    zero = jnp.zeros(p.shape, F32)
    lax.fori_loop(0, nk, body, (zero, zero))


def _s5_scan(s_t, p_rows, q_rows):
    nk, r, n = s_t.shape
    half = n // 2
    tr = min(64, r)
    return pl.pallas_call(
        _s5_scan_kernel,
        name="s5_scan",
        grid=(r // tr,),
        in_specs=[pl.BlockSpec((nk, tr, n), lambda i: (0, i, 0)),
                  pl.BlockSpec((tr, half), lambda i: (i, 0)),
                  pl.BlockSpec((tr, half), lambda i: (i, 0))],
        out_specs=pl.BlockSpec((nk, tr, half), lambda i: (0, i, 0)),
        out_shape=jax.ShapeDtypeStruct((nk, r, half), F32),
        compiler_params=_cparams(("arbitrary",)),
    )(s_t, p_rows, q_rows)


def _s5_out_kernel(u_ref, x_ref, mt_ref, ot_ref, d_ref, y_ref):
    nb, _, nk, w = u_ref.shape
    u = u_ref[...].reshape(nb * nk, w)
    x = x_ref[...].reshape(nb * nk, x_ref.shape[-1]).astype(BF16)
    y = jnp.dot(u, mt_ref[0], preferred_element_type=F32)
    y = y + jnp.dot(x, ot_ref[0], preferred_element_type=F32)
    y = y + d_ref[0] * u.astype(F32)
    y_ref[...] = _gelu_tanh(y).reshape(nb, 1, nk, w).astype(y_ref.dtype)


def _s5_out(u4, x4, mt, ot, d_rows):
    b, g, nk, w = u4.shape
    ns = x4.shape[-1]
    return pl.pallas_call(
        _s5_out_kernel,
        name="s5_out",
        grid=(g,),
        in_specs=[pl.BlockSpec((b, 1, nk, w), lambda gi: (0, gi, 0, 0)),
                  pl.BlockSpec((b, 1, nk, ns), lambda gi: (0, gi, 0, 0)),
                  pl.BlockSpec((1, w, w), lambda gi: (gi, 0, 0)),
                  pl.BlockSpec((1, ns, w), lambda gi: (gi, 0, 0)),
                  pl.BlockSpec((1, 1, w), lambda gi: (gi, 0, 0))],
        out_specs=pl.BlockSpec((b, 1, nk, w), lambda gi: (0, gi, 0, 0)),
        out_shape=jax.ShapeDtypeStruct((b, g, nk, w), BF16),
        compiler_params=_cparams(("arbitrary",)),
    )(u4, x4, mt, ot, d_rows)


def _s5_mixer(u3, lam_re, lam_im, log_dt, b_re, b_im, c_re, c_im, d_skip):
    b, l, wd = u3.shape
    g = wd // S5_GROUP
    nk = l // S5_TC
    rt, mt, ot, p_row, q_row = _s5_params(lam_re, lam_im, log_dt, b_re, b_im, c_re, c_im)
    u4 = u3.reshape(b, nk, S5_TC, g, S5_GROUP).transpose(0, 3, 1, 2, 4)
    u4 = u4.reshape(b, g, nk, S5_TC * S5_GROUP)
    s4 = _s5_state_in(u4, rt.astype(BF16))
    s_t = s4.transpose(2, 0, 1, 3).reshape(nk, b * g, s4.shape[-1])
    x_t = _s5_scan(s_t, jnp.tile(p_row, (b, 1)), jnp.tile(q_row, (b, 1)))
    x4 = x_t.reshape(nk, b, g, x_t.shape[-1]).transpose(1, 2, 0, 3)
    d_rows = jnp.tile(d_skip.astype(F32), (1, S5_TC)).reshape(g, 1, S5_TC * S5_GROUP)
    y4 = _s5_out(u4, x4, mt.astype(BF16), ot.astype(BF16), d_rows)
    y = y4.reshape(b, g, nk, S5_TC, S5_GROUP).transpose(0, 2, 3, 1, 4)
    return y.reshape(b, l, wd)


def _even_out_kernel(x_ref, oa_ref, y_ref, zb_ref, wg_ref, bg_ref, wo_ref, g_ref,
                     x1_ref, hn_ref):
    y = y_ref[...]
    t = jnp.dot(y, wg_ref[...], preferred_element_type=F32) + bg_ref[...]
    ob = y.astype(F32) * jax.nn.sigmoid(t) * _silu(zb_ref[...].astype(F32))
    wa = oa_ref.shape[1]
    acc = jnp.dot(oa_ref[...], wo_ref[:wa, :], preferred_element_type=F32)
    acc = acc + jnp.dot(ob.astype(BF16), wo_ref[wa:, :], preferred_element_type=F32)
    x1 = x_ref[...] + acc
    x1_ref[...] = x1
    hn_ref[...] = _rms(x1, g_ref[...]).astype(BF16)


def _even_out(x2, oa, y, proj, w_glu, b_glu, w_out, g_next):
    m, d = x2.shape
    wa = oa.shape[1]
    wb = y.shape[1]
    tm = min(TM_OUT, m)
    zb_col = (proj.shape[1] - wb) // wb
    const = lambda i: (0, 0)
    return pl.pallas_call(
        _even_out_kernel,
        name="even_out",
        grid=(m // tm,),
        in_specs=[pl.BlockSpec((tm, d), lambda i: (i, 0)),
                  pl.BlockSpec((tm, wa), lambda i: (i, 0)),
                  pl.BlockSpec((tm, wb), lambda i: (i, 0)),
                  pl.BlockSpec((tm, wb), lambda i: (i, zb_col)),
                  pl.BlockSpec((wb, wb), const),
                  pl.BlockSpec((1, wb), const),
                  pl.BlockSpec((wa + wb, d), const),
                  pl.BlockSpec((1, d), const)],
        out_specs=[pl.BlockSpec((tm, d), lambda i: (i, 0)),
                   pl.BlockSpec((tm, d), lambda i: (i, 0))],
        out_shape=[jax.ShapeDtypeStruct((m, d), F32),
                   jax.ShapeDtypeStruct((m, d), BF16)],
        compiler_params=_cparams(("arbitrary",)),
    )(x2, oa, y, proj, w_glu, b_glu.reshape(1, wb), w_out, g_next.reshape(1, d))


def _cum_forget_kernel(f_ref, b_ref, c_ref):
    x = f_ref[...] + b_ref[...]
    v = -(jnp.maximum(-x, 0.0) + jnp.log1p(jnp.exp(-jnp.abs(x))))
    n = v.shape[-1]
    pos = lax.broadcasted_iota(jnp.int32, v.shape, 1)
    d = 1
    while d < n:
        v = v + jnp.where(pos >= d, pltpu.roll(v, d, 1), 0.0)
        d *= 2
    c_ref[...] = v


def _cum_forget(f_t, b_forget, seq):
    nh, m = f_t.shape
    return pl.pallas_call(
        _cum_forget_kernel,
        name="cum_forget",
        grid=(m // seq,),
        in_specs=[pl.BlockSpec((nh, seq), lambda i: (0, i)),
                  pl.BlockSpec((nh, 1), lambda i: (0, 0))],
        out_specs=pl.BlockSpec((nh, seq), lambda i: (0, i)),
        out_shape=jax.ShapeDtypeStruct((nh, m), F32),
        compiler_params=_cparams(("arbitrary",)),
    )(f_t, b_forget.astype(F32).reshape(nh, 1))


def _split3(x):
    hi = x.astype(BF16).astype(F32)
    r = x - hi
    mid = r.astype(BF16).astype(F32)
    lo = (r - mid).astype(BF16).astype(F32)
    return hi, mid, lo


def _fox_kernel(q_ref, k_ref, v_ref, z_ref, c_ref, o_ref, ke_ref, vt_ref, s_ref, acc_ref):
    qi = pl.program_id(2)
    tq = q_ref.shape[1]
    tk = tq
    dh = C_DH
    nh = c_ref.shape[1]
    nblk = k_ref.shape[1] // tk
    nt = (((1,), (1,)), ((), ()))

    def cols(h):
        return slice(h * dh, (h + 1) * dh)

    @pl.when(qi == 0)
    def _():
        r = lax.broadcasted_iota(jnp.int32, (dh, dh), 0)
        c = lax.broadcasted_iota(jnp.int32, (dh, dh), 1)
        eye = (r == c).astype(BF16)
        row = lax.broadcasted_iota(jnp.int32, (dh, tk), 0)

        def blk(j, carry):
            ks = pl.multiple_of(j * tk, tk)
            for h in range(nh):
                hi, mid, lo = _split3(c_ref[0, h, :, pl.ds(ks, tk)] * (-LOG2E))
                a = jnp.where(row == 0, hi, jnp.where(row == 1, mid, jnp.where(row == 2, lo,
                              jnp.where(row < 6, 1.0, 0.0))))
                ke_ref[h, pl.ds(ks, tk), :] = a.T.astype(BF16)
                vt = lax.dot_general(eye, v_ref[0, pl.ds(ks, tk), cols(h)], nt,
                                     preferred_element_type=F32)
                vt_ref[h, :, pl.ds(ks, tk)] = vt.astype(BF16)
            return carry

        lax.fori_loop(0, nblk, blk, 0)

    qs = pl.multiple_of(qi * tq, tq)
    lane = lax.broadcasted_iota(jnp.int32, (tq, dh), 1)
    qscale = LOG2E / math.sqrt(dh)
    qa = []
    for h in range(nh):
        c_base = jnp.max(c_ref[0, h, :, pl.ds(qs, tq)], axis=1, keepdims=True) * LOG2E
        bh, bm, bl = _split3(c_base)
        g = jnp.where(lane < 3, 1.0, jnp.where(lane == 3, bh, jnp.where(lane == 4, bm,
                      jnp.where(lane == 5, bl, 0.0))))
        qh = (q_ref[0, :, cols(h)].astype(F32) * qscale).astype(BF16)
        qa.append(jnp.concatenate([qh, g.astype(BF16)], axis=1))

    def scores(h, j):
        ks = pl.multiple_of(j * tk, tk)
        ka = jnp.concatenate([k_ref[0, pl.ds(ks, tk), cols(h)], ke_ref[h, pl.ds(ks, tk), :]],
                             axis=1)
        return lax.dot_general(ka, qa[h], nt, preferred_element_type=F32)

    key = lax.broadcasted_iota(jnp.int32, (tk, tq), 0)
    qry = lax.broadcasted_iota(jnp.int32, (tk, tq), 1)

    def produce(h, j, keep):
        s = scores(h, j)
        if keep is not None:
            s = jnp.where(keep, s, NEG)
        s_ref[h] = s
        return jnp.max(s, axis=0, keepdims=True)

    def consume(h, j, stats):
        m, l, mb = stats
        ks = pl.multiple_of(j * tk, tk)
        m2 = jnp.maximum(m, mb)
        alpha = jnp.exp2(m - m2)
        p = jnp.exp2(s_ref[h] - m2)
        l2 = alpha * l + jnp.sum(p, axis=0, keepdims=True)
        pv = jnp.dot(vt_ref[h, :, pl.ds(ks, tk)], p.astype(BF16), preferred_element_type=F32)
        acc_ref[h] = alpha * acc_ref[h] + pv
        return m2, l2

    def step(j, carry, diagonal_next):
        keep = (key <= qry) if diagonal_next else None
        out = []
        for h, stats in enumerate(carry):
            m2, l2 = consume(h, j, stats)
            out.append((m2, l2, produce(h, j + 1, keep)))
        return tuple(out)

    first_keep = (key <= qry) | (qi > 0)
    carry = []
    for h in range(nh):
        acc_ref[h] = jnp.zeros((dh, tq), F32)
        carry.append((jnp.full((1, tq), NEG, F32), jnp.zeros((1, tq), F32),
                      produce(h, 0, first_keep)))
    carry = lax.fori_loop(0, qi - 1, functools.partial(step, diagonal_next=False), tuple(carry))
    carry = lax.fori_loop(jnp.maximum(qi - 1, 0), qi,
                          functools.partial(step, diagonal_next=True), carry)
    for h, stats in enumerate(carry):
        m, l = consume(h, qi, stats)
        o = (acc_ref[h] / l).T
        o_ref[0, :, cols(h)] = (o * _silu(z_ref[0, :, cols(h)].astype(F32))).astype(o_ref.dtype)


def _fox(proj3, cum4):
    b, l, _ = proj3.shape
    ng = C_HEADS // FOX_HEADS
    w = FOX_HEADS * C_DH
    tq = min(FOX_TQ, l)
    return pl.pallas_call(
        _fox_kernel,
        name="fox",
        grid=(b, ng, l // tq),
        in_specs=[pl.BlockSpec((1, tq, w), lambda bi, gi, qi: (bi, qi, gi)),
                  pl.BlockSpec((1, l, w), lambda bi, gi, qi: (bi, 0, ng + gi)),
                  pl.BlockSpec((1, l, w), lambda bi, gi, qi: (bi, 0, 2 * ng + gi)),
                  pl.BlockSpec((1, tq, w), lambda bi, gi, qi: (bi, qi, 3 * ng + gi)),
                  pl.BlockSpec((1, FOX_HEADS, 1, l), lambda bi, gi, qi: (bi, gi, 0, 0))],
        out_specs=pl.BlockSpec((1, tq, w), lambda bi, gi, qi: (bi, qi, gi)),
        out_shape=jax.ShapeDtypeStruct((b, l, C_HEADS * C_DH), BF16),
        scratch_shapes=[pltpu.VMEM((FOX_HEADS, l, C_DH), BF16),
                        pltpu.VMEM((FOX_HEADS, C_DH, l), BF16),
                        pltpu.VMEM((FOX_HEADS, tq, tq), F32),
                        pltpu.VMEM((FOX_HEADS, C_DH, tq), F32)],
        compiler_params=_cparams(("arbitrary", "arbitrary", "arbitrary")),
    )(proj3, proj3, proj3, proj3, cum4)


def _odd_out_kernel(x_ref, o_ref, w_ref, g_ref, out_ref):
    x2 = x_ref[...] + jnp.dot(o_ref[...], w_ref[...], preferred_element_type=F32)
    out_ref[...] = _rms(x2, g_ref[...])


def _odd_out(x1, o, w_out, g):
    m, d = x1.shape
    k = o.shape[1]
    tm = min(TM_OUT, m)
    return pl.pallas_call(
        _odd_out_kernel,
        name="odd_out",
        grid=(m // tm,),
        in_specs=[pl.BlockSpec((tm, d), lambda i: (i, 0)),
                  pl.BlockSpec((tm, k), lambda i: (i, 0)),
                  pl.BlockSpec((k, d), lambda i: (0, 0)),
                  pl.BlockSpec((1, d), lambda i: (0, 0))],
        out_specs=pl.BlockSpec((tm, d), lambda i: (i, 0)),
        out_shape=jax.ShapeDtypeStruct((m, d), F32),
        compiler_params=_cparams(("arbitrary",)),
    )(x1, o, w_out, g.reshape(1, d))


def kernel(x, norm_even_g, w_in_even, rel_bias, s5_lambda_re, s5_lambda_im, s5_log_dt,
           s5_b_re, s5_b_im, s5_c_re, s5_c_im, s5_d, w_glu, b_glu, w_out_even,
           norm_odd_g, w_in_odd, b_forget, w_out_odd, final_norm_g):
    b, l, d = x.shape
    assert norm_even_g.shape[0] == 1 and norm_odd_g.shape[0] == 1, "one even + one odd layer"
    assert l % FOX_TQ == 0 and l % VT_BLK == 0 and l % (A_QB * A_UNROLL) == 0 and l >= A_KB
    m = b * l
    a_w = A_HEADS * A_DH
    s5_w = s5_d.shape[1] * S5_GROUP
    c_w = C_HEADS * C_DH

    x2 = x.reshape(m, d)
    proj = _norm_proj(x2, norm_even_g[0], w_in_even[0].astype(BF16))
    proj3 = proj.reshape(b, l, proj.shape[1])
    o_a = _attn_a(proj3, rel_bias[0])
    u3 = proj3[:, :, 3 * a_w:3 * a_w + s5_w]
    y = _s5_mixer(u3, s5_lambda_re[0], s5_lambda_im[0], s5_log_dt[0], s5_b_re[0],
                  s5_b_im[0], s5_c_re[0], s5_c_im[0], s5_d[0])
    x1, hn1 = _even_out(x2, o_a.reshape(m, a_w), y.reshape(m, s5_w), proj,
                        w_glu[0].astype(BF16), b_glu[0].astype(F32),
                        w_out_even[0].astype(BF16), norm_odd_g[0])
    w_odd = w_in_odd[0]
    proj1, f_t = _proj_forget(hn1, w_odd[:, :4 * c_w].astype(BF16),
                              w_odd[:, 4 * c_w:].T.astype(BF16))
    cum = _cum_forget(f_t, b_forget[0], l)
    cum4 = cum.reshape(C_HEADS, b, 1, l).transpose(1, 0, 2, 3)
    o_c = _fox(proj1.reshape(b, l, 4 * c_w), cum4)
    out = _odd_out(x1, o_c.reshape(m, c_w), w_out_odd[0].astype(BF16), final_norm_g)
    return out.reshape(b, l, d)
```

```python
import functools
import math

import jax
import jax.numpy as jnp
from jax import lax
from jax.experimental import pallas as pl
from jax.experimental.pallas import tpu as pltpu

F32 = jnp.float32
BF16 = jnp.bfloat16

EPS = 1e-6
NEG = -1e30
LOG2E = math.log2(math.e)

CHUNK = 64
N_LEFT = 8
A_HEADS = 16
A_DH = 64
MAX_REL = 128
S5_GROUP = 16
S5_STATE = 64
C_HEADS = 16
C_DH = 128

LANES = 128
VMEM_LIMIT = 56 * 1024 * 1024

TM_PROJ = 1024
TN_PROJ = 1024
TM_OUT = 512
A_QB = 2 * CHUNK
A_KB = A_QB + N_LEFT * CHUNK
A_TW = A_KB + N_LEFT * CHUNK
A_UNROLL = 4
VT_BLK = 512
S5_TC = 16
S5_GPB = LANES // S5_GROUP
S5_SCAN_SLABS = 4
FOX_TQ = 512
FOX_HEADS = 4


def _cparams(sem):
    return pltpu.CompilerParams(dimension_semantics=sem, vmem_limit_bytes=VMEM_LIMIT)


def _rms(x, g):
    ms = jnp.mean(x * x, axis=-1, keepdims=True)
    return x * lax.rsqrt(ms + EPS) * g


def _silu(x):
    return x * jax.nn.sigmoid(x)


def _gelu_tanh(x):
    c = math.sqrt(2.0 / math.pi)
    return 0.5 * x * (1.0 + jnp.tanh(c * (x + 0.044715 * (x * x * x))))


def _norm_proj_kernel(x_ref, g_ref, w_ref, o_ref, u_ref, hn_ref, *, u_tile):
    j = pl.program_id(1)

    @pl.when(j == 0)
    def _():
        hn_ref[...] = _rms(x_ref[...], g_ref[...]).astype(BF16)

    acc = jnp.dot(hn_ref[...], w_ref[...], preferred_element_type=F32).astype(o_ref.dtype)
    o_ref[...] = acc

    @pl.when(j == u_tile)
    def _():
        for blk in range(u_ref.shape[0]):
            u_ref[blk] = acc[:, blk * LANES:(blk + 1) * LANES]


def _norm_proj(x2, g, w, u_start):
    m, d = x2.shape
    n = w.shape[1]
    tm, tn = min(TM_PROJ, m), TN_PROJ
    assert u_start % tn == 0
    nblk = tn // LANES
    return pl.pallas_call(
        functools.partial(_norm_proj_kernel, u_tile=u_start // tn),
        name="norm_proj",
        grid=(m // tm, n // tn),
        in_specs=[pl.BlockSpec((tm, d), lambda i, j: (i, 0)),
                  pl.BlockSpec((1, d), lambda i, j: (0, 0)),
                  pl.BlockSpec((d, tn), lambda i, j: (0, j))],
        out_specs=[pl.BlockSpec((tm, tn), lambda i, j: (i, j)),
                   pl.BlockSpec((nblk, tm, LANES), lambda i, j: (0, i, 0))],
        out_shape=[jax.ShapeDtypeStruct((m, n), BF16),
                   jax.ShapeDtypeStruct((nblk, m, LANES), BF16)],
        scratch_shapes=[pltpu.VMEM((tm, d), BF16)],
        compiler_params=_cparams(("arbitrary", "arbitrary")),
    )(x2, g.reshape(1, d), w)


def _proj_forget_kernel(h_ref, w_ref, wf_ref, o_ref, f_ref):
    @pl.when(pl.program_id(1) == 0)
    def _():
        f_ref[...] = lax.dot_general(wf_ref[...], h_ref[...],
                                     (((1,), (1,)), ((), ())),
                                     preferred_element_type=F32)

    o_ref[...] = jnp.dot(h_ref[...], w_ref[...],
                         preferred_element_type=F32).astype(o_ref.dtype)


def _proj_forget(hn, w, wf_t):
    m, d = hn.shape
    n = w.shape[1]
    nh = wf_t.shape[0]
    tm, tn = min(TM_PROJ, m), TN_PROJ
    return pl.pallas_call(
        _proj_forget_kernel,
        name="proj_forget",
        grid=(m // tm, n // tn),
        in_specs=[pl.BlockSpec((tm, d), lambda i, j: (i, 0)),
                  pl.BlockSpec((d, tn), lambda i, j: (0, j)),
                  pl.BlockSpec((nh, d), lambda i, j: (0, 0))],
        out_specs=[pl.BlockSpec((tm, tn), lambda i, j: (i, j)),
                   pl.BlockSpec((nh, tm), lambda i, j: (0, i))],
        out_shape=[jax.ShapeDtypeStruct((m, n), BF16),
                   jax.ShapeDtypeStruct((nh, m), F32)],
        compiler_params=_cparams(("arbitrary", "arbitrary")),
    )(hn, w, wf_t)


def _attn_a_kernel(q_ref, k_ref, v_ref, z_ref, e_ref, o_ref, tt_ref, vt_ref):
    l_all = k_ref.shape[1]
    w = q_ref.shape[2]
    nt = (((1,), (1,)), ((), ()))
    band = N_LEFT * CHUNK

    @pl.when(pl.program_id(1) == 0)
    def _():
        i = lax.broadcasted_iota(jnp.int32, (A_QB, A_TW), 0)
        m = lax.broadcasted_iota(jnp.int32, (A_QB, A_TW), 1)
        dchunk = m // CHUNK - i // CHUNK
        ok = (dchunk >= 0) & (dchunk <= N_LEFT)
        for h in range(2):
            rows = jnp.broadcast_to(e_ref[0, h:h + 1, :], (A_QB, A_QB + A_TW))
            toep = pltpu.roll(rows, 0, 1, stride=1, stride_axis=0)[:, A_QB:]
            t = jnp.where(ok, toep * LOG2E, NEG)
            tt_ref[:, h * A_QB:(h + 1) * A_QB] = t.T

    r = lax.broadcasted_iota(jnp.int32, (w, w), 0)
    c = lax.broadcasted_iota(jnp.int32, (w, w), 1)
    eye = (r == c).astype(BF16)

    def vt_blk(j, carry):
        ks = pl.multiple_of(j * VT_BLK, VT_BLK)
        vt = lax.dot_general(eye, v_ref[0, pl.ds(ks, VT_BLK), :], nt, preferred_element_type=F32)
        vt_ref[:, pl.ds(ks, VT_BLK)] = vt.astype(BF16)
        return carry

    lax.fori_loop(0, l_all // VT_BLK, vt_blk, 0)

    lane = lax.broadcasted_iota(jnp.int32, (A_QB, w), 1)
    qscale = LOG2E / math.sqrt(A_DH)

    def block(qb, carry):
        qs = pl.multiple_of(qb * A_QB, A_QB)
        start = qs - band
        start_c = jnp.maximum(start, 0)
        off = pl.multiple_of(start_c - start, LANES)
        start_c = pl.multiple_of(start_c, LANES)
        q = (q_ref[0, pl.ds(qs, A_QB), :].astype(F32) * qscale).astype(BF16)
        zero = jnp.zeros_like(q)
        qbd = jnp.concatenate([jnp.where(lane < A_DH, q, zero), jnp.where(lane >= A_DH, q, zero)],
                              axis=0)
        kb = k_ref[0, pl.ds(start_c, A_KB), :]
        s = lax.dot_general(kb, qbd, nt, preferred_element_type=F32)
        s = s + tt_ref[pl.ds(off, A_KB), :]
        m = jnp.max(s, axis=0, keepdims=True)
        p = jnp.exp2(s - m)
        inv_l = 1.0 / jnp.sum(p, axis=0, keepdims=True)
        pb = p.astype(BF16)
        outs = []
        for h in range(2):
            vt = vt_ref[h * A_DH:(h + 1) * A_DH, pl.ds(start_c, A_KB)]
            oh = jnp.dot(vt, pb[:, h * A_QB:(h + 1) * A_QB], preferred_element_type=F32)
            outs.append(oh * inv_l[:, h * A_QB:(h + 1) * A_QB])
        o = jnp.concatenate(outs, axis=0).T
        gate = _silu(z_ref[0, pl.ds(qs, A_QB), :].astype(F32))
        o_ref[0, pl.ds(qs, A_QB), :] = (o * gate).astype(o_ref.dtype)
        return carry

    lax.fori_loop(0, l_all // A_QB, block, 0, unroll=A_UNROLL)


def _attn_a(proj3, rel_bias):
    b, l, _ = proj3.shape
    hp = A_HEADS // 2
    width = 2 * A_DH
    ncol = A_HEADS * A_DH // width
    zcol = (3 * A_HEADS * A_DH + (A_HEADS * A_DH)) // width
    left = N_LEFT * CHUNK + A_QB - MAX_REL
    right = A_QB + A_TW - left - (2 * MAX_REL + 1)
    ext = jnp.pad(rel_bias.astype(F32)[:, ::-1], ((0, 0), (left, right)), mode="edge")
    ext = ext.reshape(hp, 2, A_QB + A_TW)
    seq = lambda h, bi: (bi, 0, h)
    return pl.pallas_call(
        _attn_a_kernel,
        name="attn_a",
        grid=(hp, b),
        in_specs=[pl.BlockSpec((1, l, width), seq),
                  pl.BlockSpec((1, l, width), lambda h, bi: (bi, 0, ncol + h)),
                  pl.BlockSpec((1, l, width), lambda h, bi: (bi, 0, 2 * ncol + h)),
                  pl.BlockSpec((1, l, width), lambda h, bi: (bi, 0, zcol + h)),
                  pl.BlockSpec((1, 2, A_QB + A_TW), lambda h, bi: (h, 0, 0))],
        out_specs=pl.BlockSpec((1, l, width), seq),
        out_shape=jax.ShapeDtypeStruct((b, l, A_HEADS * A_DH), BF16),
        scratch_shapes=[pltpu.VMEM((A_TW, 2 * A_QB), F32),
                        pltpu.VMEM((width, l), BF16)],
        compiler_params=_cparams(("arbitrary", "arbitrary")),
    )(proj3, proj3, proj3, proj3, ext)


def _cmul(ar, ai, br, bi):
    return ar * br - ai * bi, ar * bi + ai * br


def _s5_discretise(lr, li, log_dt):
    dt = jnp.exp(log_dt)
    mag = jnp.exp(lr * dt)
    ang = li * dt
    a_re = mag * jnp.cos(ang)
    a_im = mag * jnp.sin(ang)
    den = lr * lr + li * li
    nr = a_re - 1.0
    coef_re = (nr * lr + a_im * li) / den
    coef_im = (a_im * lr - nr * li) / den
    return (a_re, a_im), (coef_re, coef_im)


def _s5_weights_kernel(lam_pg_ref, ldt_pg_ref, c_pg_ref, lam_gp_ref, ldt_gp_ref, b_gp_ref,
                       lam_row_ref, ldt_row_ref, b_tl_ref, wm_ref, wr_ref, wo_ref, a_ref):
    tc, ch, ns, gpb = S5_TC, S5_GROUP, S5_STATE, S5_GPB
    lw = gpb * ch
    hi = lax.Precision.HIGHEST

    a_pg, _ = _s5_discretise(lam_pg_ref[0], lam_pg_ref[1], ldt_pg_ref[...])
    pw = (jnp.ones_like(a_pg[0]), jnp.zeros_like(a_pg[0]))
    ca = []
    for n in range(tc + 1):
        ca.append(_cmul(c_pg_ref[0], c_pg_ref[1], *pw))
        pw = _cmul(*pw, *a_pg)

    _, coef_gp = _s5_discretise(lam_gp_ref[0], lam_gp_ref[1], ldt_gp_ref[...])
    bb_gp = _cmul(*coef_gp, b_gp_ref[0], b_gp_ref[1])
    ca_re = jnp.concatenate([ca[n][0] for n in range(tc)], axis=1)
    ca_im = jnp.concatenate([ca[n][1] for n in range(tc)], axis=1)
    z = (jnp.dot(bb_gp[0], ca_re, precision=hi, preferred_element_type=F32)
         - jnp.dot(bb_gp[1], ca_im, precision=hi, preferred_element_type=F32))
    row_g = lax.broadcasted_iota(jnp.int32, z.shape, 0) // ch
    col_g = (lax.broadcasted_iota(jnp.int32, z.shape, 1) % lw) // ch
    z = jnp.where(row_g == col_g, z, 0.0).astype(BF16)
    for t in range(tc):
        parts = [z[:, :(tc - t) * lw]]
        if t:
            parts.insert(0, jnp.zeros((lw, t * lw), BF16))
        wm_ref[0, t * lw:(t + 1) * lw, :] = jnp.concatenate(parts, axis=1)

    rg = lax.broadcasted_iota(jnp.int32, (gpb * ns, lw), 0) // ns
    cg = lax.broadcasted_iota(jnp.int32, (gpb * ns, lw), 1) // ch
    keep = rg == cg
    for t2 in range(tc):
        cr, ci = ca[t2 + 1]
        cr = jnp.where(keep, jnp.concatenate([cr] * gpb, axis=0), 0.0)
        ci = jnp.where(keep, jnp.concatenate([ci] * gpb, axis=0), 0.0)
        wo_ref[0, :gpb * ns, t2 * lw:(t2 + 1) * lw] = cr.astype(BF16)
        wo_ref[0, gpb * ns:, t2 * lw:(t2 + 1) * lw] = (-ci).astype(BF16)

    a_row, coef_row = _s5_discretise(lam_row_ref[0], lam_row_ref[1], ldt_row_ref[...])
    bb_tl = _cmul(*coef_row, b_tl_ref[0], b_tl_ref[1])
    rg = lax.broadcasted_iota(jnp.int32, bb_tl[0].shape, 0) // ch
    cg = lax.broadcasted_iota(jnp.int32, bb_tl[0].shape, 1) // ns
    bb_tl = (jnp.where(rg == cg, bb_tl[0], 0.0), jnp.where(rg == cg, bb_tl[1], 0.0))
    pw = (jnp.ones_like(a_row[0]), jnp.zeros_like(a_row[0]))
    for t in range(tc - 1, -1, -1):
        r_re, r_im = _cmul(*pw, *bb_tl)
        wr_ref[0, t * lw:(t + 1) * lw, :] = jnp.concatenate([r_re, r_im], axis=1).astype(BF16)
        pw = _cmul(*pw, *a_row)
    a_ref[0, 0] = pw[0]
    a_ref[0, 1] = pw[1]


def _s5_weights(lam_re, lam_im, log_dt, b_re, b_im, c_re, c_im):
    g, ns = lam_re.shape
    ch, tc, gpb = S5_GROUP, S5_TC, S5_GPB
    nblk = g // gpb
    lw = gpb * ch
    lam = jnp.stack([lam_re, lam_im]).astype(F32)
    ldt = log_dt.astype(F32)
    bst = jnp.stack([b_re, b_im]).astype(F32)
    cst = jnp.stack([c_re, c_im]).astype(F32)
    lam_pg = jnp.repeat(lam.transpose(0, 2, 1), ch, axis=2)
    ldt_pg = jnp.repeat(ldt, ch)[None]
    c_pg = cst.transpose(0, 3, 1, 2).reshape(2, ns, g * ch)
    lam_gp = jnp.repeat(lam, ch, axis=1)
    ldt_gp = jnp.broadcast_to(jnp.repeat(ldt, ch)[:, None], (g * ch, ns))
    b_gp = bst.transpose(0, 1, 3, 2).reshape(2, g * ch, ns)
    lam_row = lam.reshape(2, 1, g * ns)
    ldt_row = jnp.repeat(ldt, ns)[None]
    b_tl = jnp.tile(b_gp, (1, 1, gpb))
    blk3 = lambda i: (0, 0, i)
    return pl.pallas_call(
        _s5_weights_kernel,
        name="s5_weights",
        grid=(nblk,),
        in_specs=[pl.BlockSpec((2, ns, lw), blk3),
                  pl.BlockSpec((1, lw), lambda i: (0, i)),
                  pl.BlockSpec((2, ns, lw), blk3),
                  pl.BlockSpec((2, lw, ns), lambda i: (0, i, 0)),
                  pl.BlockSpec((lw, ns), lambda i: (i, 0)),
                  pl.BlockSpec((2, lw, ns), lambda i: (0, i, 0)),
                  pl.BlockSpec((2, 1, gpb * ns), blk3),
                  pl.BlockSpec((1, gpb * ns), lambda i: (0, i)),
                  pl.BlockSpec((2, lw, gpb * ns), lambda i: (0, i, 0))],
        out_specs=[pl.BlockSpec((1, tc * lw, tc * lw), lambda i: (i, 0, 0)),
                   pl.BlockSpec((1, tc * lw, 2 * gpb * ns), lambda i: (i, 0, 0)),
                   pl.BlockSpec((1, 2 * gpb * ns, tc * lw), lambda i: (i, 0, 0)),
                   pl.BlockSpec((1, 2, 1, gpb * ns), lambda i: (i, 0, 0, 0))],
        out_shape=[jax.ShapeDtypeStruct((nblk, tc * lw, tc * lw), BF16),
                   jax.ShapeDtypeStruct((nblk, tc * lw, 2 * gpb * ns), BF16),
                   jax.ShapeDtypeStruct((nblk, 2 * gpb * ns, tc * lw), BF16),
                   jax.ShapeDtypeStruct((nblk, 2, 1, gpb * ns), F32)],
        compiler_params=_cparams(("arbitrary",)),
    )(lam_pg, ldt_pg, c_pg, lam_gp, ldt_gp, b_gp, lam_row, ldt_row, b_tl)


def _s5_state_in_kernel(u_ref, wr_ref, s_ref):
    s_ref[...] = jnp.dot(u_ref[0], wr_ref[0], preferred_element_type=F32)


def _s5_state_in(u3, wr, nb):
    nblk, rows, w = u3.shape
    nk = rows // nb
    n = wr.shape[-1]
    return pl.pallas_call(
        _s5_state_in_kernel,
        name="s5_state_in",
        grid=(nblk, nb),
        in_specs=[pl.BlockSpec((1, nk, w), lambda i, b: (i, b, 0)),
                  pl.BlockSpec((1, w, n), lambda i, b: (i, 0, 0))],
        out_specs=pl.BlockSpec((nk, n), lambda i, b: (0, b * nblk + i)),
        out_shape=jax.ShapeDtypeStruct((nk, nb * nblk * n), F32),
        compiler_params=_cparams(("arbitrary", "arbitrary")),
    )(u3, wr)


def _s5_scan_kernel(s_ref, a1_ref, a2_ref, x_ref):
    nk = s_ref.shape[0]
    half = s_ref.shape[2] // 2
    a1 = a1_ref[...]
    a2 = a2_ref[...]

    def body(k, carry):
        v, w = carry
        x_ref[k] = v
        s = s_ref[k]
        s_sw = jnp.concatenate([s[:, half:], s[:, :half]], axis=1)
        return a1 * v + a2 * w + s, a1 * w - a2 * v + s_sw

    zero = jnp.zeros(a1.shape, F32)
    lax.fori_loop(0, nk, body, (zero, zero))


def _s5_scan(s4, a1, a2):
    nk, r = s4.shape[:2]
    tr = min(S5_SCAN_SLABS, r)
    tile = lambda i: (i, 0, 0)
    return pl.pallas_call(
        _s5_scan_kernel,
        name="s5_scan",
        grid=(r // tr,),
        in_specs=[pl.BlockSpec((nk, tr) + s4.shape[2:], lambda i: (0, i, 0, 0)),
                  pl.BlockSpec((tr,) + s4.shape[2:], tile),
                  pl.BlockSpec((tr,) + s4.shape[2:], tile)],
        out_specs=pl.BlockSpec((nk, tr) + s4.shape[2:], lambda i: (0, i, 0, 0)),
        out_shape=jax.ShapeDtypeStruct(s4.shape, F32),
        compiler_params=_cparams(("arbitrary",)),
    )(s4, a1, a2)


def _s5_out_kernel(u_ref, x_ref, wm_ref, wo_ref, d_ref, y_ref):
    u = u_ref[0]
    y = jnp.dot(u, wm_ref[0], preferred_element_type=F32)
    y = y + jnp.dot(x_ref[...].astype(BF16), wo_ref[0], preferred_element_type=F32)
    y = y + d_ref[0] * u.astype(F32)
    y_ref[0] = _gelu_tanh(y).astype(y_ref.dtype)


def _s5_out(u3, x2, wm, wo, d_rows, nb):
    nblk, rows, w = u3.shape
    nk = rows // nb
    ns = wo.shape[1]
    return pl.pallas_call(
        _s5_out_kernel,
        name="s5_out",
        grid=(nblk, nb),
        in_specs=[pl.BlockSpec((1, nk, w), lambda i, b: (i, b, 0)),
                  pl.BlockSpec((nk, ns), lambda i, b: (0, b * nblk + i)),
                  pl.BlockSpec((1, w, w), lambda i, b: (i, 0, 0)),
                  pl.BlockSpec((1, ns, w), lambda i, b: (i, 0, 0)),
                  pl.BlockSpec((1, 1, w), lambda i, b: (i, 0, 0))],
        out_specs=pl.BlockSpec((1, nk, w), lambda i, b: (i, b, 0)),
        out_shape=jax.ShapeDtypeStruct(u3.shape, BF16),
        compiler_params=_cparams(("arbitrary", "arbitrary")),
    )(u3, x2, wm, wo, d_rows)


def _s5_mixer(u_blk, nb, lam_re, lam_im, log_dt, b_re, b_im, c_re, c_im, d_skip):
    nblk, m, lw = u_blk.shape
    tc = S5_TC
    wm, wr, wo, a_pow = _s5_weights(lam_re, lam_im, log_dt, b_re, b_im, c_re, c_im)
    u3 = u_blk.reshape(nblk, m // tc, tc * lw)
    s2 = _s5_state_in(u3, wr, nb)
    nk = s2.shape[0]
    slab = (8, LANES)
    rows = a_pow.shape[-1] // LANES
    a_re = a_pow[:, 0].reshape(nblk, rows, LANES)
    a_im = a_pow[:, 1].reshape(nblk, rows, LANES)
    a1 = jnp.tile(jnp.concatenate([a_re, a_re], axis=1), (nb, 1, 1))
    a2 = jnp.tile(jnp.concatenate([-a_im, a_im], axis=1), (nb, 1, 1))
    x4 = _s5_scan(s2.reshape((nk, nb * nblk) + slab), a1, a2)
    d_rows = jnp.tile(d_skip.astype(F32).reshape(nblk, 1, lw), (1, 1, tc))
    y3 = _s5_out(u3, x4.reshape(s2.shape), wm, wo, d_rows, nb)
    return y3.reshape(nblk, m, lw)


def _even_out_kernel(x_ref, oa_ref, y_ref, zb_ref, wg_ref, bg_ref, wo_ref, g_ref,
                     x1_ref, hn_ref):
    y = jnp.concatenate([y_ref[i] for i in range(y_ref.shape[0])], axis=1)
    t = jnp.dot(y, wg_ref[...], preferred_element_type=F32) + bg_ref[...]
    ob = y.astype(F32) * jax.nn.sigmoid(t) * _silu(zb_ref[...].astype(F32))
    wa = oa_ref.shape[1]
    acc = jnp.dot(oa_ref[...], wo_ref[:wa, :], preferred_element_type=F32)
    acc = acc + jnp.dot(ob.astype(BF16), wo_ref[wa:, :], preferred_element_type=F32)
    x1 = x_ref[...] + acc
    x1_ref[...] = x1
    hn_ref[...] = _rms(x1, g_ref[...]).astype(BF16)


def _even_out(x2, oa, y, proj, w_glu, b_glu, w_out, g_next):
    m, d = x2.shape
    wa = oa.shape[1]
    nblk, _, lw = y.shape
    wb = nblk * lw
    tm = min(TM_OUT, m)
    zb_col = (proj.shape[1] - wb) // wb
    const = lambda i: (0, 0)
    return pl.pallas_call(
        _even_out_kernel,
        name="even_out",
        grid=(m // tm,),
        in_specs=[pl.BlockSpec((tm, d), lambda i: (i, 0)),
                  pl.BlockSpec((tm, wa), lambda i: (i, 0)),
                  pl.BlockSpec((nblk, tm, lw), lambda i: (0, i, 0)),
                  pl.BlockSpec((tm, wb), lambda i: (i, zb_col)),
                  pl.BlockSpec((wb, wb), const),
                  pl.BlockSpec((1, wb), const),
                  pl.BlockSpec((wa + wb, d), const),
                  pl.BlockSpec((1, d), const)],
        out_specs=[pl.BlockSpec((tm, d), lambda i: (i, 0)),
                   pl.BlockSpec((tm, d), lambda i: (i, 0))],
        out_shape=[jax.ShapeDtypeStruct((m, d), F32),
                   jax.ShapeDtypeStruct((m, d), BF16)],
        compiler_params=_cparams(("arbitrary",)),
    )(x2, oa, y, proj, w_glu, b_glu.reshape(1, wb), w_out, g_next.reshape(1, d))


def _cum_forget_kernel(f_ref, b_ref, c_ref):
    x = f_ref[...] + b_ref[...]
    v = -(jnp.maximum(-x, 0.0) + jnp.log1p(jnp.exp(-jnp.abs(x))))
    n = v.shape[-1]
    pos = lax.broadcasted_iota(jnp.int32, v.shape, 1)
    d = 1
    while d < n:
        v = v + jnp.where(pos >= d, pltpu.roll(v, d, 1), 0.0)
        d *= 2
    c_ref[...] = v


def _cum_forget(f_t, b_forget, seq):
    nh, m = f_t.shape
    return pl.pallas_call(
        _cum_forget_kernel,
        name="cum_forget",
        grid=(m // seq,),
        in_specs=[pl.BlockSpec((nh, seq), lambda i: (0, i)),
                  pl.BlockSpec((nh, 1), lambda i: (0, 0))],
        out_specs=pl.BlockSpec((nh, seq), lambda i: (0, i)),
        out_shape=jax.ShapeDtypeStruct((nh, m), F32),
        compiler_params=_cparams(("arbitrary",)),
    )(f_t, b_forget.astype(F32).reshape(nh, 1))


def _split3(x):
    hi = x.astype(BF16).astype(F32)
    r = x - hi
    mid = r.astype(BF16).astype(F32)
    lo = (r - mid).astype(BF16).astype(F32)
    return hi, mid, lo


def _fox_kernel(q_ref, k_ref, v_ref, z_ref, c_ref, o_ref, ke_ref, vt_ref, s_ref, acc_ref):
    qi = pl.program_id(2)
    tq = q_ref.shape[1]
    tk = tq
    dh = C_DH
    nh = c_ref.shape[1]
    nblk = k_ref.shape[1] // tk
    nt = (((1,), (1,)), ((), ()))

    def cols(h):
        return slice(h * dh, (h + 1) * dh)

    @pl.when(qi == 0)
    def _():
        r = lax.broadcasted_iota(jnp.int32, (dh, dh), 0)
        c = lax.broadcasted_iota(jnp.int32, (dh, dh), 1)
        eye = (r == c).astype(BF16)
        row = lax.broadcasted_iota(jnp.int32, (dh, tk), 0)

        def blk(j, carry):
            ks = pl.multiple_of(j * tk, tk)
            for h in range(nh):
                hi, mid, lo = _split3(c_ref[0, h, :, pl.ds(ks, tk)] * (-LOG2E))
                a = jnp.where(row == 0, hi, jnp.where(row == 1, mid, jnp.where(row == 2, lo,
                              jnp.where(row < 6, 1.0, 0.0))))
                ke_ref[h, pl.ds(ks, tk), :] = a.T.astype(BF16)
                vt = lax.dot_general(eye, v_ref[0, pl.ds(ks, tk), cols(h)], nt,
                                     preferred_element_type=F32)
                vt_ref[h, :, pl.ds(ks, tk)] = vt.astype(BF16)
            return carry

        lax.fori_loop(0, nblk, blk, 0)

    qs = pl.multiple_of(qi * tq, tq)
    lane = lax.broadcasted_iota(jnp.int32, (tq, dh), 1)
    qscale = LOG2E / math.sqrt(dh)
    qa = []
    for h in range(nh):
        c_base = jnp.max(c_ref[0, h, :, pl.ds(qs, tq)], axis=1, keepdims=True) * LOG2E
        bh, bm, bl = _split3(c_base)
        g = jnp.where(lane < 3, 1.0, jnp.where(lane == 3, bh, jnp.where(lane == 4, bm,
                      jnp.where(lane == 5, bl, 0.0))))
        qh = (q_ref[0, :, cols(h)].astype(F32) * qscale).astype(BF16)
        qa.append(jnp.concatenate([qh, g.astype(BF16)], axis=1))

    def scores(h, j):
        ks = pl.multiple_of(j * tk, tk)
        ka = jnp.concatenate([k_ref[0, pl.ds(ks, tk), cols(h)], ke_ref[h, pl.ds(ks, tk), :]],
                             axis=1)
        return lax.dot_general(ka, qa[h], nt, preferred_element_type=F32)

    key = lax.broadcasted_iota(jnp.int32, (tk, tq), 0)
    qry = lax.broadcasted_iota(jnp.int32, (tk, tq), 1)

    def produce(h, j, keep):
        s = scores(h, j)
        if keep is not None:
            s = jnp.where(keep, s, NEG)
        s_ref[h] = s
        return jnp.max(s, axis=0, keepdims=True)

    def consume(h, j, stats):
        m, l, mb = stats
        ks = pl.multiple_of(j * tk, tk)
        m2 = jnp.maximum(m, mb)
        alpha = jnp.exp2(m - m2)
        p = jnp.exp2(s_ref[h] - m2)
        l2 = alpha * l + jnp.sum(p, axis=0, keepdims=True)
        pv = jnp.dot(vt_ref[h, :, pl.ds(ks, tk)], p.astype(BF16), preferred_element_type=F32)
        acc_ref[h] = alpha * acc_ref[h] + pv
        return m2, l2

    def step(j, carry, diagonal_next):
        keep = (key <= qry) if diagonal_next else None
        out = []
        for h, stats in enumerate(carry):
            m2, l2 = consume(h, j, stats)
            out.append((m2, l2, produce(h, j + 1, keep)))
        return tuple(out)

    first_keep = (key <= qry) | (qi > 0)
    carry = []
    for h in range(nh):
        acc_ref[h] = jnp.zeros((dh, tq), F32)
        carry.append((jnp.full((1, tq), NEG, F32), jnp.zeros((1, tq), F32),
                      produce(h, 0, first_keep)))
    carry = lax.fori_loop(0, qi - 1, functools.partial(step, diagonal_next=False), tuple(carry))
    carry = lax.fori_loop(jnp.maximum(qi - 1, 0), qi,
                          functools.partial(step, diagonal_next=True), carry)
    for h, stats in enumerate(carry):
        m, l = consume(h, qi, stats)
        o = (acc_ref[h] / l).T
        o_ref[0, :, cols(h)] = (o * _silu(z_ref[0, :, cols(h)].astype(F32))).astype(o_ref.dtype)


def _fox(proj3, cum4):
    b, l, _ = proj3.shape
    ng = C_HEADS // FOX_HEADS
    w = FOX_HEADS * C_DH
    tq = min(FOX_TQ, l)
    return pl.pallas_call(
        _fox_kernel,
        name="fox",
        grid=(b, ng, l // tq),
        in_specs=[pl.BlockSpec((1, tq, w), lambda bi, gi, qi: (bi, qi, gi)),
                  pl.BlockSpec((1, l, w), lambda bi, gi, qi: (bi, 0, ng + gi)),
                  pl.BlockSpec((1, l, w), lambda bi, gi, qi: (bi, 0, 2 * ng + gi)),
                  pl.BlockSpec((1, tq, w), lambda bi, gi, qi: (bi, qi, 3 * ng + gi)),
                  pl.BlockSpec((1, FOX_HEADS, 1, l), lambda bi, gi, qi: (bi, gi, 0, 0))],
        out_specs=pl.BlockSpec((1, tq, w), lambda bi, gi, qi: (bi, qi, gi)),
        out_shape=jax.ShapeDtypeStruct((b, l, C_HEADS * C_DH), BF16),
        scratch_shapes=[pltpu.VMEM((FOX_HEADS, l, C_DH), BF16),
                        pltpu.VMEM((FOX_HEADS, C_DH, l), BF16),
                        pltpu.VMEM((FOX_HEADS, tq, tq), F32),
                        pltpu.VMEM((FOX_HEADS, C_DH, tq), F32)],
        compiler_params=_cparams(("arbitrary", "arbitrary", "arbitrary")),
    )(proj3, proj3, proj3, proj3, cum4)


def _odd_out_kernel(x_ref, o_ref, w_ref, g_ref, out_ref):
    x2 = x_ref[...] + jnp.dot(o_ref[...], w_ref[...], preferred_element_type=F32)
    out_ref[...] = _rms(x2, g_ref[...])


def _odd_out(x1, o, w_out, g):
    m, d = x1.shape
    k = o.shape[1]
    tm = min(TM_OUT, m)
    return pl.pallas_call(
        _odd_out_kernel,
        name="odd_out",
        grid=(m // tm,),
        in_specs=[pl.BlockSpec((tm, d), lambda i: (i, 0)),
                  pl.BlockSpec((tm, k), lambda i: (i, 0)),
                  pl.BlockSpec((k, d), lambda i: (0, 0)),
                  pl.BlockSpec((1, d), lambda i: (0, 0))],
        out_specs=pl.BlockSpec((tm, d), lambda i: (i, 0)),
        out_shape=jax.ShapeDtypeStruct((m, d), F32),
        compiler_params=_cparams(("arbitrary",)),
    )(x1, o, w_out, g.reshape(1, d))


def kernel(x, norm_even_g, w_in_even, rel_bias, s5_lambda_re, s5_lambda_im, s5_log_dt,
           s5_b_re, s5_b_im, s5_c_re, s5_c_im, s5_d, w_glu, b_glu, w_out_even,
           norm_odd_g, w_in_odd, b_forget, w_out_odd, final_norm_g):
    b, l, d = x.shape
    assert norm_even_g.shape[0] == 1 and norm_odd_g.shape[0] == 1, "one even + one odd layer"
    assert l % FOX_TQ == 0 and l % VT_BLK == 0 and l % (A_QB * A_UNROLL) == 0 and l >= A_KB
    m = b * l
    a_w = A_HEADS * A_DH
    s5_w = s5_d.shape[1] * S5_GROUP
    c_w = C_HEADS * C_DH

    x2 = x.reshape(m, d)
    assert s5_w == TN_PROJ and l % S5_TC == 0
    proj, u_blk = _norm_proj(x2, norm_even_g[0], w_in_even[0].astype(BF16), 3 * a_w)
    proj3 = proj.reshape(b, l, proj.shape[1])
    o_a = _attn_a(proj3, rel_bias[0])
    y = _s5_mixer(u_blk, b, s5_lambda_re[0], s5_lambda_im[0], s5_log_dt[0], s5_b_re[0],
                  s5_b_im[0], s5_c_re[0], s5_c_im[0], s5_d[0])
    x1, hn1 = _even_out(x2, o_a.reshape(m, a_w), y, proj,
                        w_glu[0].astype(BF16), b_glu[0].astype(F32),
                        w_out_even[0].astype(BF16), norm_odd_g[0])
    w_odd = w_in_odd[0]
    proj1, f_t = _proj_forget(hn1, w_odd[:, :4 * c_w].astype(BF16),
                              w_odd[:, 4 * c_w:].T.astype(BF16))
    cum = _cum_forget(f_t, b_forget[0], l)
    cum4 = cum.reshape(C_HEADS, b, 1, l).transpose(1, 0, 2, 3)
    o_c = _fox(proj1.reshape(b, l, 4 * c_w), cum4)
    out = _odd_out(x1, o_c.reshape(m, c_w), w_out_odd[0].astype(BF16), final_norm_g)
    return out.reshape(b, l, d)
```

```python
import functools
import math

import jax
import jax.numpy as jnp
from jax import lax
from jax.experimental import pallas as pl
from jax.experimental.pallas import tpu as pltpu

F32 = jnp.float32
BF16 = jnp.bfloat16

EPS = 1e-6
NEG = -1e30
LOG2E = math.log2(math.e)

CHUNK = 64
N_LEFT = 8
A_HEADS = 16
A_DH = 64
MAX_REL = 128
S5_GROUP = 16
S5_STATE = 64
C_HEADS = 16
C_DH = 128

LANES = 128
VMEM_LIMIT = 56 * 1024 * 1024

TM_PROJ = 1024
TN_PROJ = 1024
TM_OUT = 512
A_QB = 2 * CHUNK
A_KB = A_QB + N_LEFT * CHUNK
A_TW = A_KB + N_LEFT * CHUNK
A_UNROLL = 4
VT_BLK = 512
S5_TC = 16
S5_GPB = LANES // S5_GROUP
FOX_TQ = 512
FOX_HEADS = 4


def _cparams(sem):
    return pltpu.CompilerParams(dimension_semantics=sem, vmem_limit_bytes=VMEM_LIMIT)


def _rms(x, g):
    ms = jnp.mean(x * x, axis=-1, keepdims=True)
    return x * lax.rsqrt(ms + EPS) * g


def _silu(x):
    return x * jax.nn.sigmoid(x)


def _gelu_tanh(x):
    c = math.sqrt(2.0 / math.pi)
    return 0.5 * x * (1.0 + jnp.tanh(c * (x + 0.044715 * (x * x * x))))


def _norm_proj_kernel(x_ref, g_ref, w_ref, o_ref, u_ref, hn_ref, us_ref, *, u_tile):
    j = pl.program_id(1)

    @pl.when(j == 0)
    def _():
        hn_ref[...] = _rms(x_ref[...], g_ref[...]).astype(BF16)

    acc = jnp.dot(hn_ref[...], w_ref[...], preferred_element_type=F32)
    o_ref[...] = acc.astype(o_ref.dtype)

    @pl.when(j == u_tile)
    def _():
        nchunk = us_ref.shape[1] // S5_TC
        for blk in range(u_ref.shape[0]):
            us_ref[blk] = acc[:, blk * LANES:(blk + 1) * LANES]
            for t in range(S5_TC):
                rows = us_ref[blk, pl.ds(t, nchunk, stride=S5_TC), :]
                u_ref[blk, :, t * LANES:(t + 1) * LANES] = rows.astype(u_ref.dtype)


def _norm_proj(x2, g, w, u_start):
    m, d = x2.shape
    n = w.shape[1]
    tm, tn = min(TM_PROJ, m), TN_PROJ
    assert u_start % tn == 0 and tm % (16 * S5_TC) == 0
    nblk = tn // LANES
    return pl.pallas_call(
        functools.partial(_norm_proj_kernel, u_tile=u_start // tn),
        name="norm_proj",
        grid=(m // tm, n // tn),
        in_specs=[pl.BlockSpec((tm, d), lambda i, j: (i, 0)),
                  pl.BlockSpec((1, d), lambda i, j: (0, 0)),
                  pl.BlockSpec((d, tn), lambda i, j: (0, j))],
        out_specs=[pl.BlockSpec((tm, tn), lambda i, j: (i, j)),
                   pl.BlockSpec((nblk, tm // S5_TC, S5_TC * LANES), lambda i, j: (0, i, 0))],
        out_shape=[jax.ShapeDtypeStruct((m, n), BF16),
                   jax.ShapeDtypeStruct((nblk, m // S5_TC, S5_TC * LANES), BF16)],
        scratch_shapes=[pltpu.VMEM((tm, d), BF16), pltpu.VMEM((nblk, tm, LANES), F32)],
        compiler_params=_cparams(("arbitrary", "arbitrary")),
    )(x2, g.reshape(1, d), w)


def _proj_forget_kernel(h_ref, w_ref, wf_ref, o_ref, f_ref):
    @pl.when(pl.program_id(1) == 0)
    def _():
        f_ref[...] = lax.dot_general(wf_ref[...], h_ref[...],
                                     (((1,), (1,)), ((), ())),
                                     preferred_element_type=F32)

    o_ref[...] = jnp.dot(h_ref[...], w_ref[...],
                         preferred_element_type=F32).astype(o_ref.dtype)


def _proj_forget(hn, w, wf_t):
    m, d = hn.shape
    n = w.shape[1]
    nh = wf_t.shape[0]
    tm, tn = min(TM_PROJ, m), TN_PROJ
    return pl.pallas_call(
        _proj_forget_kernel,
        name="proj_forget",
        grid=(m // tm, n // tn),
        in_specs=[pl.BlockSpec((tm, d), lambda i, j: (i, 0)),
                  pl.BlockSpec((d, tn), lambda i, j: (0, j)),
                  pl.BlockSpec((nh, d), lambda i, j: (0, 0))],
        out_specs=[pl.BlockSpec((tm, tn), lambda i, j: (i, j)),
                   pl.BlockSpec((nh, tm), lambda i, j: (0, i))],
        out_shape=[jax.ShapeDtypeStruct((m, n), BF16),
                   jax.ShapeDtypeStruct((nh, m), F32)],
        compiler_params=_cparams(("arbitrary", "arbitrary")),
    )(hn, w, wf_t)


def _attn_a_kernel(q_ref, k_ref, v_ref, z_ref, e_ref, o_ref, tt_ref, vt_ref):
    l_all = k_ref.shape[1]
    w = q_ref.shape[2]
    nt = (((1,), (1,)), ((), ()))
    band = N_LEFT * CHUNK

    @pl.when(pl.program_id(1) == 0)
    def _():
        i = lax.broadcasted_iota(jnp.int32, (A_QB, A_TW), 0)
        m = lax.broadcasted_iota(jnp.int32, (A_QB, A_TW), 1)
        dchunk = m // CHUNK - i // CHUNK
        ok = (dchunk >= 0) & (dchunk <= N_LEFT)
        for h in range(2):
            rows = jnp.broadcast_to(e_ref[0, h:h + 1, :], (A_QB, A_QB + A_TW))
            toep = pltpu.roll(rows, 0, 1, stride=1, stride_axis=0)[:, A_QB:]
            t = jnp.where(ok, toep * LOG2E, NEG)
            tt_ref[:, h * A_QB:(h + 1) * A_QB] = t.T

    r = lax.broadcasted_iota(jnp.int32, (w, w), 0)
    c = lax.broadcasted_iota(jnp.int32, (w, w), 1)
    eye = (r == c).astype(BF16)

    def vt_blk(j, carry):
        ks = pl.multiple_of(j * VT_BLK, VT_BLK)
        vt = lax.dot_general(eye, v_ref[0, pl.ds(ks, VT_BLK), :], nt, preferred_element_type=F32)
        vt_ref[:, pl.ds(ks, VT_BLK)] = vt.astype(BF16)
        return carry

    lax.fori_loop(0, l_all // VT_BLK, vt_blk, 0)

    lane = lax.broadcasted_iota(jnp.int32, (A_QB, w), 1)
    qscale = LOG2E / math.sqrt(A_DH)

    def block(qb, carry):
        qs = pl.multiple_of(qb * A_QB, A_QB)
        start = qs - band
        start_c = jnp.maximum(start, 0)
        off = pl.multiple_of(start_c - start, LANES)
        start_c = pl.multiple_of(start_c, LANES)
        q = (q_ref[0, pl.ds(qs, A_QB), :].astype(F32) * qscale).astype(BF16)
        zero = jnp.zeros_like(q)
        qbd = jnp.concatenate([jnp.where(lane < A_DH, q, zero), jnp.where(lane >= A_DH, q, zero)],
                              axis=0)
        kb = k_ref[0, pl.ds(start_c, A_KB), :]
        s = lax.dot_general(kb, qbd, nt, preferred_element_type=F32)
        s = s + tt_ref[pl.ds(off, A_KB), :]
        m = jnp.max(s, axis=0, keepdims=True)
        p = jnp.exp2(s - m)
        inv_l = 1.0 / jnp.sum(p, axis=0, keepdims=True)
        pb = p.astype(BF16)
        outs = []
        for h in range(2):
            vt = vt_ref[h * A_DH:(h + 1) * A_DH, pl.ds(start_c, A_KB)]
            oh = jnp.dot(vt, pb[:, h * A_QB:(h + 1) * A_QB], preferred_element_type=F32)
            outs.append(oh * inv_l[:, h * A_QB:(h + 1) * A_QB])
        o = jnp.concatenate(outs, axis=0).T
        gate = _silu(z_ref[0, pl.ds(qs, A_QB), :].astype(F32))
        o_ref[0, pl.ds(qs, A_QB), :] = (o * gate).astype(o_ref.dtype)
        return carry

    lax.fori_loop(0, l_all // A_QB, block, 0, unroll=A_UNROLL)


def _attn_a(proj3, rel_bias):
    b, l, _ = proj3.shape
    hp = A_HEADS // 2
    width = 2 * A_DH
    ncol = A_HEADS * A_DH // width
    zcol = (3 * A_HEADS * A_DH + (A_HEADS * A_DH)) // width
    left = N_LEFT * CHUNK + A_QB - MAX_REL
    right = A_QB + A_TW - left - (2 * MAX_REL + 1)
    ext = jnp.pad(rel_bias.astype(F32)[:, ::-1], ((0, 0), (left, right)), mode="edge")
    ext = ext.reshape(hp, 2, A_QB + A_TW)
    seq = lambda h, bi: (bi, 0, h)
    return pl.pallas_call(
        _attn_a_kernel,
        name="attn_a",
        grid=(hp, b),
        in_specs=[pl.BlockSpec((1, l, width), seq),
                  pl.BlockSpec((1, l, width), lambda h, bi: (bi, 0, ncol + h)),
                  pl.BlockSpec((1, l, width), lambda h, bi: (bi, 0, 2 * ncol + h)),
                  pl.BlockSpec((1, l, width), lambda h, bi: (bi, 0, zcol + h)),
                  pl.BlockSpec((1, 2, A_QB + A_TW), lambda h, bi: (h, 0, 0))],
        out_specs=pl.BlockSpec((1, l, width), seq),
        out_shape=jax.ShapeDtypeStruct((b, l, A_HEADS * A_DH), BF16),
        scratch_shapes=[pltpu.VMEM((A_TW, 2 * A_QB), F32),
                        pltpu.VMEM((width, l), BF16)],
        compiler_params=_cparams(("arbitrary", "arbitrary")),
    )(proj3, proj3, proj3, proj3, ext)


def _cmul(ar, ai, br, bi):
    return ar * br - ai * bi, ar * bi + ai * br


def _s5_discretise(lr, li, log_dt):
    dt = jnp.exp(log_dt)
    mag = jnp.exp(lr * dt)
    ang = li * dt
    a_re = mag * jnp.cos(ang)
    a_im = mag * jnp.sin(ang)
    den = lr * lr + li * li
    nr = a_re - 1.0
    coef_re = (nr * lr + a_im * li) / den
    coef_im = (a_im * lr - nr * li) / den
    return (a_re, a_im), (coef_re, coef_im)


def _s5_weights_kernel(lam_pg_ref, ldt_pg_ref, c_pg_ref, lam_gp_ref, ldt_gp_ref, b_gp_ref,
                       lam_row_ref, ldt_row_ref, b_tl_ref, wm_ref, wr_ref, wo_ref, a_ref):
    tc, ch, ns, gpb = S5_TC, S5_GROUP, S5_STATE, S5_GPB
    lw = gpb * ch
    hi = lax.Precision.HIGHEST

    a_pg, _ = _s5_discretise(lam_pg_ref[0], lam_pg_ref[1], ldt_pg_ref[...])
    pw = (jnp.ones_like(a_pg[0]), jnp.zeros_like(a_pg[0]))
    ca = []
    for n in range(tc + 1):
        ca.append(_cmul(c_pg_ref[0], c_pg_ref[1], *pw))
        pw = _cmul(*pw, *a_pg)

    _, coef_gp = _s5_discretise(lam_gp_ref[0], lam_gp_ref[1], ldt_gp_ref[...])
    bb_gp = _cmul(*coef_gp, b_gp_ref[0], b_gp_ref[1])
    ca_re = jnp.concatenate([ca[n][0] for n in range(tc)], axis=1)
    ca_im = jnp.concatenate([ca[n][1] for n in range(tc)], axis=1)
    z = (jnp.dot(bb_gp[0], ca_re, precision=hi, preferred_element_type=F32)
         - jnp.dot(bb_gp[1], ca_im, precision=hi, preferred_element_type=F32))
    row_g = lax.broadcasted_iota(jnp.int32, z.shape, 0) // ch
    col_g = (lax.broadcasted_iota(jnp.int32, z.shape, 1) % lw) // ch
    z = jnp.where(row_g == col_g, z, 0.0).astype(BF16)
    for t in range(tc):
        parts = [z[:, :(tc - t) * lw]]
        if t:
            parts.insert(0, jnp.zeros((lw, t * lw), BF16))
        wm_ref[0, t * lw:(t + 1) * lw, :] = jnp.concatenate(parts, axis=1)

    rg = lax.broadcasted_iota(jnp.int32, (gpb * ns, lw), 0) // ns
    cg = lax.broadcasted_iota(jnp.int32, (gpb * ns, lw), 1) // ch
    keep = rg == cg
    for t2 in range(tc):
        cr, ci = ca[t2 + 1]
        cr = jnp.where(keep, jnp.concatenate([cr] * gpb, axis=0), 0.0)
        ci = jnp.where(keep, jnp.concatenate([ci] * gpb, axis=0), 0.0)
        wo_ref[0, :gpb * ns, t2 * lw:(t2 + 1) * lw] = cr.astype(BF16)
        wo_ref[0, gpb * ns:, t2 * lw:(t2 + 1) * lw] = (-ci).astype(BF16)

    a_row, coef_row = _s5_discretise(lam_row_ref[0], lam_row_ref[1], ldt_row_ref[...])
    bb_tl = _cmul(*coef_row, b_tl_ref[0], b_tl_ref[1])
    rg = lax.broadcasted_iota(jnp.int32, bb_tl[0].shape, 0) // ch
    cg = lax.broadcasted_iota(jnp.int32, bb_tl[0].shape, 1) // ns
    bb_tl = (jnp.where(rg == cg, bb_tl[0], 0.0), jnp.where(rg == cg, bb_tl[1], 0.0))
    pw = (jnp.ones_like(a_row[0]), jnp.zeros_like(a_row[0]))
    for t in range(tc - 1, -1, -1):
        r_re, r_im = _cmul(*pw, *bb_tl)
        wr_ref[0, t * lw:(t + 1) * lw, :] = jnp.concatenate([r_re, r_im], axis=1).astype(BF16)
        pw = _cmul(*pw, *a_row)
    a_ref[0, 0] = pw[0]
    a_ref[0, 1] = pw[1]


def _s5_weights(lam_re, lam_im, log_dt, b_re, b_im, c_re, c_im):
    g, ns = lam_re.shape
    ch, tc, gpb = S5_GROUP, S5_TC, S5_GPB
    nblk = g // gpb
    lw = gpb * ch
    lam = jnp.stack([lam_re, lam_im]).astype(F32)
    ldt = log_dt.astype(F32)
    bst = jnp.stack([b_re, b_im]).astype(F32)
    cst = jnp.stack([c_re, c_im]).astype(F32)
    lam_pg = jnp.repeat(lam.transpose(0, 2, 1), ch, axis=2)
    ldt_pg = jnp.repeat(ldt, ch)[None]
    c_pg = cst.transpose(0, 3, 1, 2).reshape(2, ns, g * ch)
    lam_gp = jnp.repeat(lam, ch, axis=1)
    ldt_gp = jnp.broadcast_to(jnp.repeat(ldt, ch)[:, None], (g * ch, ns))
    b_gp = bst.transpose(0, 1, 3, 2).reshape(2, g * ch, ns)
    lam_row = lam.reshape(2, 1, g * ns)
    ldt_row = jnp.repeat(ldt, ns)[None]
    b_tl = jnp.tile(b_gp, (1, 1, gpb))
    blk3 = lambda i: (0, 0, i)
    return pl.pallas_call(
        _s5_weights_kernel,
        name="s5_weights",
        grid=(nblk,),
        in_specs=[pl.BlockSpec((2, ns, lw), blk3),
                  pl.BlockSpec((1, lw), lambda i: (0, i)),
                  pl.BlockSpec((2, ns, lw), blk3),
                  pl.BlockSpec((2, lw, ns), lambda i: (0, i, 0)),
                  pl.BlockSpec((lw, ns), lambda i: (i, 0)),
                  pl.BlockSpec((2, lw, ns), lambda i: (0, i, 0)),
                  pl.BlockSpec((2, 1, gpb * ns), blk3),
                  pl.BlockSpec((1, gpb * ns), lambda i: (0, i)),
                  pl.BlockSpec((2, lw, gpb * ns), lambda i: (0, i, 0))],
        out_specs=[pl.BlockSpec((1, tc * lw, tc * lw), lambda i: (i, 0, 0)),
                   pl.BlockSpec((1, tc * lw, 2 * gpb * ns), lambda i: (i, 0, 0)),
                   pl.BlockSpec((1, 2 * gpb * ns, tc * lw), lambda i: (i, 0, 0)),
                   pl.BlockSpec((1, 2, 1, gpb * ns), lambda i: (i, 0, 0, 0))],
        out_shape=[jax.ShapeDtypeStruct((nblk, tc * lw, tc * lw), BF16),
                   jax.ShapeDtypeStruct((nblk, tc * lw, 2 * gpb * ns), BF16),
                   jax.ShapeDtypeStruct((nblk, 2 * gpb * ns, tc * lw), BF16),
                   jax.ShapeDtypeStruct((nblk, 2, 1, gpb * ns), F32)],
        compiler_params=_cparams(("arbitrary",)),
    )(lam_pg, ldt_pg, c_pg, lam_gp, ldt_gp, b_gp, lam_row, ldt_row, b_tl)


def _s5_mix_kernel(u_ref, wm_ref, wr_ref, wo_ref, a_ref, d_ref, y_ref, s_ref, x_ref, ys_ref):
    nk = u_ref.shape[1]
    half = s_ref.shape[1] // 2
    sub = 8
    u = u_ref[0]
    s_ref[...] = jnp.dot(u, wr_ref[0], preferred_element_type=F32)

    a1 = (a_ref[0, 0], a_ref[0, 1])
    a2 = _cmul(*a1, *a1)
    a4 = _cmul(*a2, *a2)
    a8 = _cmul(*a4, *a4)
    a3 = _cmul(*a2, *a1)
    pows = [(jnp.ones_like(a1[0]), jnp.zeros_like(a1[0])), a1, a2, a3, a4,
            _cmul(*a4, *a1), _cmul(*a4, *a2), _cmul(*a4, *a3)]
    p_re = jnp.concatenate([p[0] for p in pows], axis=0)
    p_im = jnp.concatenate([p[1] for p in pows], axis=0)
    rowid = lax.broadcasted_iota(jnp.int32, (sub, half), 0)
    levels = [(d, jnp.broadcast_to(a[0], (sub, half)), jnp.broadcast_to(a[1], (sub, half)))
              for d, a in ((1, a1), (2, a2), (4, a4))]

    def shift_rows(v, d):
        return jnp.where(rowid >= d, pltpu.roll(v, d, 0), 0.0)

    def tile(j, carry):
        cr, ci = carry
        rows = pl.ds(pl.multiple_of(j * sub, sub), sub)
        zr = s_ref[rows, :half]
        zi = s_ref[rows, half:]
        for d, br, bi in levels:
            sr, si = shift_rows(zr, d), shift_rows(zi, d)
            zr, zi = zr + br * sr - bi * si, zi + br * si + bi * sr
        x_ref[rows, :half] = p_re * cr - p_im * ci + shift_rows(zr, 1)
        x_ref[rows, half:] = p_re * ci + p_im * cr + shift_rows(zi, 1)
        return (a8[0] * cr - a8[1] * ci + zr[sub - 1:], a8[0] * ci + a8[1] * cr + zi[sub - 1:])

    zero = jnp.zeros((1, half), F32)
    lax.fori_loop(0, nk // sub, tile, (zero, zero))

    y = jnp.dot(u, wm_ref[0], preferred_element_type=F32)
    y = y + jnp.dot(x_ref[...].astype(BF16), wo_ref[0], preferred_element_type=F32)
    y = _gelu_tanh(y + d_ref[0] * u.astype(F32))
    for t in range(S5_TC):
        ys_ref[pl.ds(t, nk, stride=S5_TC), :] = y[:, t * LANES:(t + 1) * LANES]
    y_ref[0] = ys_ref[...].astype(y_ref.dtype)


def _s5_mixer(u3, nb, lam_re, lam_im, log_dt, b_re, b_im, c_re, c_im, d_skip):
    nblk, rows, w = u3.shape
    tc = S5_TC
    nk = rows // nb
    assert nk % 8 == 0
    wm, wr, wo, a_pow = _s5_weights(lam_re, lam_im, log_dt, b_re, b_im, c_re, c_im)
    ns = wr.shape[-1]
    d_rows = jnp.tile(d_skip.astype(F32).reshape(nblk, 1, LANES), (1, 1, tc))
    blk = lambda i, b: (i, 0, 0)
    return pl.pallas_call(
        _s5_mix_kernel,
        name="s5_mix",
        grid=(nblk, nb),
        in_specs=[pl.BlockSpec((1, nk, w), lambda i, b: (i, b, 0)),
                  pl.BlockSpec((1, w, w), blk),
                  pl.BlockSpec((1, w, ns), blk),
                  pl.BlockSpec((1, ns, w), blk),
                  pl.BlockSpec((1, 2, 1, ns // 2), lambda i, b: (i, 0, 0, 0)),
                  pl.BlockSpec((1, 1, w), blk)],
        out_specs=pl.BlockSpec((1, nk * tc, LANES), lambda i, b: (i, b, 0)),
        out_shape=jax.ShapeDtypeStruct((nblk, rows * tc, LANES), BF16),
        scratch_shapes=[pltpu.VMEM((nk, ns), F32), pltpu.VMEM((nk, ns), F32),
                        pltpu.VMEM((nk * tc, LANES), F32)],
        compiler_params=_cparams(("arbitrary", "arbitrary")),
    )(u3, wm, wr, wo, a_pow, d_rows)


def _even_out_kernel(x_ref, oa_ref, y_ref, zb_ref, wg_ref, bg_ref, wo_ref, g_ref,
                     x1_ref, hn_ref):
    y = jnp.concatenate([y_ref[i] for i in range(y_ref.shape[0])], axis=1)
    t = jnp.dot(y, wg_ref[...], preferred_element_type=F32) + bg_ref[...]
    ob = y.astype(F32) * jax.nn.sigmoid(t) * _silu(zb_ref[...].astype(F32))
    wa = oa_ref.shape[1]
    acc = jnp.dot(oa_ref[...], wo_ref[:wa, :], preferred_element_type=F32)
    acc = acc + jnp.dot(ob.astype(BF16), wo_ref[wa:, :], preferred_element_type=F32)
    x1 = x_ref[...] + acc
    x1_ref[...] = x1
    hn_ref[...] = _rms(x1, g_ref[...]).astype(BF16)


def _even_out(x2, oa, y, proj, w_glu, b_glu, w_out, g_next):
    m, d = x2.shape
    wa = oa.shape[1]
    nblk, _, lw = y.shape
    wb = nblk * lw
    tm = min(TM_OUT, m)
    zb_col = (proj.shape[1] - wb) // wb
    const = lambda i: (0, 0)
    return pl.pallas_call(
        _even_out_kernel,
        name="even_out",
        grid=(m // tm,),
        in_specs=[pl.BlockSpec((tm, d), lambda i: (i, 0)),
                  pl.BlockSpec((tm, wa), lambda i: (i, 0)),
                  pl.BlockSpec((nblk, tm, lw), lambda i: (0, i, 0)),
                  pl.BlockSpec((tm, wb), lambda i: (i, zb_col)),
                  pl.BlockSpec((wb, wb), const),
                  pl.BlockSpec((1, wb), const),
                  pl.BlockSpec((wa + wb, d), const),
                  pl.BlockSpec((1, d), const)],
        out_specs=[pl.BlockSpec((tm, d), lambda i: (i, 0)),
                   pl.BlockSpec((tm, d), lambda i: (i, 0))],
        out_shape=[jax.ShapeDtypeStruct((m, d), F32),
                   jax.ShapeDtypeStruct((m, d), BF16)],
        compiler_params=_cparams(("arbitrary",)),
    )(x2, oa, y, proj, w_glu, b_glu.reshape(1, wb), w_out, g_next.reshape(1, d))


def _cum_forget_kernel(f_ref, b_ref, c_ref):
    x = f_ref[...] + b_ref[...]
    v = -(jnp.maximum(-x, 0.0) + jnp.log1p(jnp.exp(-jnp.abs(x))))
    n = v.shape[-1]
    pos = lax.broadcasted_iota(jnp.int32, v.shape, 1)
    d = 1
    while d < n:
        v = v + jnp.where(pos >= d, pltpu.roll(v, d, 1), 0.0)
        d *= 2
    c_ref[...] = v


def _cum_forget(f_t, b_forget, seq):
    nh, m = f_t.shape
    return pl.pallas_call(
        _cum_forget_kernel,
        name="cum_forget",
        grid=(m // seq,),
        in_specs=[pl.BlockSpec((nh, seq), lambda i: (0, i)),
                  pl.BlockSpec((nh, 1), lambda i: (0, 0))],
        out_specs=pl.BlockSpec((nh, seq), lambda i: (0, i)),
        out_shape=jax.ShapeDtypeStruct((nh, m), F32),
        compiler_params=_cparams(("arbitrary",)),
    )(f_t, b_forget.astype(F32).reshape(nh, 1))


def _split3(x):
    hi = x.astype(BF16).astype(F32)
    r = x - hi
    mid = r.astype(BF16).astype(F32)
    lo = (r - mid).astype(BF16).astype(F32)
    return hi, mid, lo


def _fox_kernel(q_ref, k_ref, v_ref, z_ref, c_ref, o_ref, ke_ref, vt_ref, s_ref, acc_ref):
    qi = pl.program_id(2)
    tq = q_ref.shape[1]
    tk = tq
    dh = C_DH
    nh = c_ref.shape[1]
    nblk = k_ref.shape[1] // tk
    nt = (((1,), (1,)), ((), ()))

    def cols(h):
        return slice(h * dh, (h + 1) * dh)

    @pl.when(qi == 0)
    def _():
        r = lax.broadcasted_iota(jnp.int32, (dh, dh), 0)
        c = lax.broadcasted_iota(jnp.int32, (dh, dh), 1)
        eye = (r == c).astype(BF16)
        row = lax.broadcasted_iota(jnp.int32, (dh, tk), 0)

        def blk(j, carry):
            ks = pl.multiple_of(j * tk, tk)
            for h in range(nh):
                hi, mid, lo = _split3(c_ref[0, h, :, pl.ds(ks, tk)] * (-LOG2E))
                a = jnp.where(row == 0, hi, jnp.where(row == 1, mid, jnp.where(row == 2, lo,
                              jnp.where(row < 6, 1.0, 0.0))))
                ke_ref[h, pl.ds(ks, tk), :] = a.T.astype(BF16)
                vt = lax.dot_general(eye, v_ref[0, pl.ds(ks, tk), cols(h)], nt,
                                     preferred_element_type=F32)
                vt_ref[h, :, pl.ds(ks, tk)] = vt.astype(BF16)
            return carry

        lax.fori_loop(0, nblk, blk, 0)

    qs = pl.multiple_of(qi * tq, tq)
    lane = lax.broadcasted_iota(jnp.int32, (tq, dh), 1)
    qscale = LOG2E / math.sqrt(dh)
    qa = []
    for h in range(nh):
        c_base = jnp.max(c_ref[0, h, :, pl.ds(qs, tq)], axis=1, keepdims=True) * LOG2E
        bh, bm, bl = _split3(c_base)
        g = jnp.where(lane < 3, 1.0, jnp.where(lane == 3, bh, jnp.where(lane == 4, bm,
                      jnp.where(lane == 5, bl, 0.0))))
        qh = (q_ref[0, :, cols(h)].astype(F32) * qscale).astype(BF16)
        qa.append(jnp.concatenate([qh, g.astype(BF16)], axis=1))

    def scores(h, j):
        ks = pl.multiple_of(j * tk, tk)
        ka = jnp.concatenate([k_ref[0, pl.ds(ks, tk), cols(h)], ke_ref[h, pl.ds(ks, tk), :]],
                             axis=1)
        return lax.dot_general(ka, qa[h], nt, preferred_element_type=F32)

    key = lax.broadcasted_iota(jnp.int32, (tk, tq), 0)
    qry = lax.broadcasted_iota(jnp.int32, (tk, tq), 1)

    def produce(h, j, keep):
        s = scores(h, j)
        if keep is not None:
            s = jnp.where(keep, s, NEG)
        s_ref[h] = s
        return jnp.max(s, axis=0, keepdims=True)

    def consume(h, j, stats):
        m, l, mb = stats
        ks = pl.multiple_of(j * tk, tk)
        m2 = jnp.maximum(m, mb)
        alpha = jnp.exp2(m - m2)
        p = jnp.exp2(s_ref[h] - m2)
        l2 = alpha * l + jnp.sum(p, axis=0, keepdims=True)
        pv = jnp.dot(vt_ref[h, :, pl.ds(ks, tk)], p.astype(BF16), preferred_element_type=F32)
        acc_ref[h] = alpha * acc_ref[h] + pv
        return m2, l2

    def step(j, carry, diagonal_next):
        keep = (key <= qry) if diagonal_next else None
        out = []
        for h, stats in enumerate(carry):
            m2, l2 = consume(h, j, stats)
            out.append((m2, l2, produce(h, j + 1, keep)))
        return tuple(out)

    first_keep = (key <= qry) | (qi > 0)
    carry = []
    for h in range(nh):
        acc_ref[h] = jnp.zeros((dh, tq), F32)
        carry.append((jnp.full((1, tq), NEG, F32), jnp.zeros((1, tq), F32),
                      produce(h, 0, first_keep)))
    carry = lax.fori_loop(0, qi - 1, functools.partial(step, diagonal_next=False), tuple(carry))
    carry = lax.fori_loop(jnp.maximum(qi - 1, 0), qi,
                          functools.partial(step, diagonal_next=True), carry)
    for h, stats in enumerate(carry):
        m, l = consume(h, qi, stats)
        o = (acc_ref[h] / l).T
        o_ref[0, :, cols(h)] = (o * _silu(z_ref[0, :, cols(h)].astype(F32))).astype(o_ref.dtype)


def _fox(proj3, cum4):
    b, l, _ = proj3.shape
    ng = C_HEADS // FOX_HEADS
    w = FOX_HEADS * C_DH
    tq = min(FOX_TQ, l)
    return pl.pallas_call(
        _fox_kernel,
        name="fox",
        grid=(b, ng, l // tq),
        in_specs=[pl.BlockSpec((1, tq, w), lambda bi, gi, qi: (bi, qi, gi)),
                  pl.BlockSpec((1, l, w), lambda bi, gi, qi: (bi, 0, ng + gi)),
                  pl.BlockSpec((1, l, w), lambda bi, gi, qi: (bi, 0, 2 * ng + gi)),
                  pl.BlockSpec((1, tq, w), lambda bi, gi, qi: (bi, qi, 3 * ng + gi)),
                  pl.BlockSpec((1, FOX_HEADS, 1, l), lambda bi, gi, qi: (bi, gi, 0, 0))],
        out_specs=pl.BlockSpec((1, tq, w), lambda bi, gi, qi: (bi, qi, gi)),
        out_shape=jax.ShapeDtypeStruct((b, l, C_HEADS * C_DH), BF16),
        scratch_shapes=[pltpu.VMEM((FOX_HEADS, l, C_DH), BF16),
                        pltpu.VMEM((FOX_HEADS, C_DH, l), BF16),
                        pltpu.VMEM((FOX_HEADS, tq, tq), F32),
                        pltpu.VMEM((FOX_HEADS, C_DH, tq), F32)],
        compiler_params=_cparams(("arbitrary", "arbitrary", "arbitrary")),
    )(proj3, proj3, proj3, proj3, cum4)


def _odd_out_kernel(x_ref, o_ref, w_ref, g_ref, out_ref):
    x2 = x_ref[...] + jnp.dot(o_ref[...], w_ref[...], preferred_element_type=F32)
    out_ref[...] = _rms(x2, g_ref[...])


def _odd_out(x1, o, w_out, g):
    m, d = x1.shape
    k = o.shape[1]
    tm = min(TM_OUT, m)
    return pl.pallas_call(
        _odd_out_kernel,
        name="odd_out",
        grid=(m // tm,),
        in_specs=[pl.BlockSpec((tm, d), lambda i: (i, 0)),
                  pl.BlockSpec((tm, k), lambda i: (i, 0)),
                  pl.BlockSpec((k, d), lambda i: (0, 0)),
                  pl.BlockSpec((1, d), lambda i: (0, 0))],
        out_specs=pl.BlockSpec((tm, d), lambda i: (i, 0)),
        out_shape=jax.ShapeDtypeStruct((m, d), F32),
        compiler_params=_cparams(("arbitrary",)),
    )(x1, o, w_out, g.reshape(1, d))


def kernel(x, norm_even_g, w_in_even, rel_bias, s5_lambda_re, s5_lambda_im, s5_log_dt,
           s5_b_re, s5_b_im, s5_c_re, s5_c_im, s5_d, w_glu, b_glu, w_out_even,
           norm_odd_g, w_in_odd, b_forget, w_out_odd, final_norm_g):
    b, l, d = x.shape
    assert norm_even_g.shape[0] == 1 and norm_odd_g.shape[0] == 1, "one even + one odd layer"
    assert l % FOX_TQ == 0 and l % VT_BLK == 0 and l % (A_QB * A_UNROLL) == 0 and l >= A_KB
    m = b * l
    a_w = A_HEADS * A_DH
    s5_w = s5_d.shape[1] * S5_GROUP
    c_w = C_HEADS * C_DH

    x2 = x.reshape(m, d)
    assert s5_w == TN_PROJ and l % S5_TC == 0
    proj, u3 = _norm_proj(x2, norm_even_g[0], w_in_even[0].astype(BF16), 3 * a_w)
    proj3 = proj.reshape(b, l, proj.shape[1])
    o_a = _attn_a(proj3, rel_bias[0])
    y = _s5_mixer(u3, b, s5_lambda_re[0], s5_lambda_im[0], s5_log_dt[0], s5_b_re[0],
                  s5_b_im[0], s5_c_re[0], s5_c_im[0], s5_d[0])
    x1, hn1 = _even_out(x2, o_a.reshape(m, a_w), y, proj,
                        w_glu[0].astype(BF16), b_glu[0].astype(F32),
                        w_out_even[0].astype(BF16), norm_odd_g[0])
    w_odd = w_in_odd[0]
    proj1, f_t = _proj_forget(hn1, w_odd[:, :4 * c_w].astype(BF16),
                              w_odd[:, 4 * c_w:].T.astype(BF16))
    cum = _cum_forget(f_t, b_forget[0], l)
    cum4 = cum.reshape(C_HEADS, b, 1, l).transpose(1, 0, 2, 3)
    o_c = _fox(proj1.reshape(b, l, 4 * c_w), cum4)
    out = _odd_out(x1, o_c.reshape(m, c_w), w_out_odd[0].astype(BF16), final_norm_g)
    return out.reshape(b, l, d)
```

```python
import functools
import math

import jax
import jax.numpy as jnp
from jax import lax
from jax.experimental import pallas as pl
from jax.experimental.pallas import tpu as pltpu

F32 = jnp.float32
BF16 = jnp.bfloat16

EPS = 1e-6
NEG = -1e30
LOG2E = math.log2(math.e)

CHUNK = 64
N_LEFT = 8
A_HEADS = 16
A_DH = 64
MAX_REL = 128
S5_GROUP = 16
S5_STATE = 64
C_HEADS = 16
C_DH = 128

LANES = 128
VMEM_LIMIT = 56 * 1024 * 1024

TM_PROJ = 1024
TN_PROJ = 1024
TM_OUT = 512
A_QB = 2 * CHUNK
A_KB = A_QB + N_LEFT * CHUNK
A_TW = A_KB + N_LEFT * CHUNK
A_UNROLL = 4
VT_BLK = 512
S5_TC = 16
S5_GPB = LANES // S5_GROUP
FOX_TQ = 512
FOX_HEADS = 4
FOX_PAD = 16


def _cparams(sem):
    return pltpu.CompilerParams(dimension_semantics=sem, vmem_limit_bytes=VMEM_LIMIT)


def _rms(x, g):
    ms = jnp.mean(x * x, axis=-1, keepdims=True)
    return x * lax.rsqrt(ms + EPS) * g


def _silu(x):
    return x * jax.nn.sigmoid(x)


def _gelu_tanh(x):
    c = math.sqrt(2.0 / math.pi)
    return 0.5 * x * (1.0 + jnp.tanh(c * (x + 0.044715 * (x * x * x))))


def _norm_proj_kernel(x_ref, g_ref, w_ref, o_ref, u_ref, hn_ref, us_ref, *, u_tile):
    j = pl.program_id(1)

    @pl.when(j == 0)
    def _():
        hn_ref[...] = _rms(x_ref[...], g_ref[...]).astype(BF16)

    acc = jnp.dot(hn_ref[...], w_ref[...], preferred_element_type=F32)
    o_ref[...] = acc.astype(o_ref.dtype)

    @pl.when(j == u_tile)
    def _():
        nchunk = us_ref.shape[1] // S5_TC
        for blk in range(u_ref.shape[0]):
            us_ref[blk] = acc[:, blk * LANES:(blk + 1) * LANES]
            for t in range(S5_TC):
                rows = us_ref[blk, pl.ds(t, nchunk, stride=S5_TC), :]
                u_ref[blk, :, t * LANES:(t + 1) * LANES] = rows.astype(u_ref.dtype)


def _norm_proj(x2, g, w, u_start):
    m, d = x2.shape
    n = w.shape[1]
    tm, tn = min(TM_PROJ, m), TN_PROJ
    assert u_start % tn == 0 and tm % (16 * S5_TC) == 0
    nblk = tn // LANES
    return pl.pallas_call(
        functools.partial(_norm_proj_kernel, u_tile=u_start // tn),
        name="norm_proj",
        grid=(m // tm, n // tn),
        in_specs=[pl.BlockSpec((tm, d), lambda i, j: (i, 0)),
                  pl.BlockSpec((1, d), lambda i, j: (0, 0)),
                  pl.BlockSpec((d, tn), lambda i, j: (0, j))],
        out_specs=[pl.BlockSpec((tm, tn), lambda i, j: (i, j)),
                   pl.BlockSpec((nblk, tm // S5_TC, S5_TC * LANES), lambda i, j: (0, i, 0))],
        out_shape=[jax.ShapeDtypeStruct((m, n), BF16),
                   jax.ShapeDtypeStruct((nblk, m // S5_TC, S5_TC * LANES), BF16)],
        scratch_shapes=[pltpu.VMEM((tm, d), BF16), pltpu.VMEM((nblk, tm, LANES), F32)],
        compiler_params=_cparams(("arbitrary", "arbitrary")),
    )(x2, g.reshape(1, d), w)


def _proj_forget_kernel(h_ref, w_ref, wf_ref, o_ref, f_ref):
    @pl.when(pl.program_id(1) == 0)
    def _():
        f_ref[...] = lax.dot_general(wf_ref[...], h_ref[...],
                                     (((1,), (1,)), ((), ())),
                                     preferred_element_type=F32)

    o_ref[...] = jnp.dot(h_ref[...], w_ref[...],
                         preferred_element_type=F32).astype(o_ref.dtype)


def _proj_forget(hn, w, wf_t):
    m, d = hn.shape
    n = w.shape[1]
    nh = wf_t.shape[0]
    tm, tn = min(TM_PROJ, m), TN_PROJ
    return pl.pallas_call(
        _proj_forget_kernel,
        name="proj_forget",
        grid=(m // tm, n // tn),
        in_specs=[pl.BlockSpec((tm, d), lambda i, j: (i, 0)),
                  pl.BlockSpec((d, tn), lambda i, j: (0, j)),
                  pl.BlockSpec((nh, d), lambda i, j: (0, 0))],
        out_specs=[pl.BlockSpec((tm, tn), lambda i, j: (i, j)),
                   pl.BlockSpec((nh, tm), lambda i, j: (0, i))],
        out_shape=[jax.ShapeDtypeStruct((m, n), BF16),
                   jax.ShapeDtypeStruct((nh, m), F32)],
        compiler_params=_cparams(("arbitrary", "arbitrary")),
    )(hn, w, wf_t)


def _attn_a_kernel(q_ref, k_ref, v_ref, z_ref, e_ref, o_ref, tt_ref, vt_ref):
    l_all = k_ref.shape[1]
    w = q_ref.shape[2]
    nt = (((1,), (1,)), ((), ()))
    band = N_LEFT * CHUNK

    @pl.when(pl.program_id(1) == 0)
    def _():
        i = lax.broadcasted_iota(jnp.int32, (A_QB, A_TW), 0)
        m = lax.broadcasted_iota(jnp.int32, (A_QB, A_TW), 1)
        dchunk = m // CHUNK - i // CHUNK
        ok = (dchunk >= 0) & (dchunk <= N_LEFT)
        for h in range(2):
            rows = jnp.broadcast_to(e_ref[0, h:h + 1, :], (A_QB, A_QB + A_TW))
            toep = pltpu.roll(rows, 0, 1, stride=1, stride_axis=0)[:, A_QB:]
            t = jnp.where(ok, toep * LOG2E, NEG)
            tt_ref[:, h * A_QB:(h + 1) * A_QB] = t.T

    r = lax.broadcasted_iota(jnp.int32, (w, w), 0)
    c = lax.broadcasted_iota(jnp.int32, (w, w), 1)
    eye = (r == c).astype(BF16)

    def vt_blk(j, carry):
        ks = pl.multiple_of(j * VT_BLK, VT_BLK)
        vt = lax.dot_general(eye, v_ref[0, pl.ds(ks, VT_BLK), :], nt, preferred_element_type=F32)
        vt_ref[:, pl.ds(ks, VT_BLK)] = vt.astype(BF16)
        return carry

    lax.fori_loop(0, l_all // VT_BLK, vt_blk, 0)

    lane = lax.broadcasted_iota(jnp.int32, (A_QB, w), 1)
    qscale = LOG2E / math.sqrt(A_DH)

    def block(qb, carry):
        qs = pl.multiple_of(qb * A_QB, A_QB)
        start = qs - band
        start_c = jnp.maximum(start, 0)
        off = pl.multiple_of(start_c - start, LANES)
        start_c = pl.multiple_of(start_c, LANES)
        q = (q_ref[0, pl.ds(qs, A_QB), :].astype(F32) * qscale).astype(BF16)
        zero = jnp.zeros_like(q)
        qbd = jnp.concatenate([jnp.where(lane < A_DH, q, zero), jnp.where(lane >= A_DH, q, zero)],
                              axis=0)
        kb = k_ref[0, pl.ds(start_c, A_KB), :]
        s = lax.dot_general(kb, qbd, nt, preferred_element_type=F32)
        s = s + tt_ref[pl.ds(off, A_KB), :]
        m = jnp.max(s, axis=0, keepdims=True)
        p = jnp.exp2(s - m)
        inv_l = 1.0 / jnp.sum(p, axis=0, keepdims=True)
        pb = p.astype(BF16)
        outs = []
        for h in range(2):
            vt = vt_ref[h * A_DH:(h + 1) * A_DH, pl.ds(start_c, A_KB)]
            oh = jnp.dot(vt, pb[:, h * A_QB:(h + 1) * A_QB], preferred_element_type=F32)
            outs.append(oh * inv_l[:, h * A_QB:(h + 1) * A_QB])
        o = jnp.concatenate(outs, axis=0).T
        gate = _silu(z_ref[0, pl.ds(qs, A_QB), :].astype(F32))
        o_ref[0, pl.ds(qs, A_QB), :] = (o * gate).astype(o_ref.dtype)
        return carry

    lax.fori_loop(0, l_all // A_QB, block, 0, unroll=A_UNROLL)


def _attn_a(proj3, rel_bias):
    b, l, _ = proj3.shape
    hp = A_HEADS // 2
    width = 2 * A_DH
    ncol = A_HEADS * A_DH // width
    zcol = (3 * A_HEADS * A_DH + (A_HEADS * A_DH)) // width
    left = N_LEFT * CHUNK + A_QB - MAX_REL
    right = A_QB + A_TW - left - (2 * MAX_REL + 1)
    ext = jnp.pad(rel_bias.astype(F32)[:, ::-1], ((0, 0), (left, right)), mode="edge")
    ext = ext.reshape(hp, 2, A_QB + A_TW)
    seq = lambda h, bi: (bi, 0, h)
    return pl.pallas_call(
        _attn_a_kernel,
        name="attn_a",
        grid=(hp, b),
        in_specs=[pl.BlockSpec((1, l, width), seq),
                  pl.BlockSpec((1, l, width), lambda h, bi: (bi, 0, ncol + h)),
                  pl.BlockSpec((1, l, width), lambda h, bi: (bi, 0, 2 * ncol + h)),
                  pl.BlockSpec((1, l, width), lambda h, bi: (bi, 0, zcol + h)),
                  pl.BlockSpec((1, 2, A_QB + A_TW), lambda h, bi: (h, 0, 0))],
        out_specs=pl.BlockSpec((1, l, width), seq),
        out_shape=jax.ShapeDtypeStruct((b, l, A_HEADS * A_DH), BF16),
        scratch_shapes=[pltpu.VMEM((A_TW, 2 * A_QB), F32),
                        pltpu.VMEM((width, l), BF16)],
        compiler_params=_cparams(("arbitrary", "arbitrary")),
    )(proj3, proj3, proj3, proj3, ext)


def _cmul(ar, ai, br, bi):
    return ar * br - ai * bi, ar * bi + ai * br


def _s5_discretise(lr, li, log_dt):
    dt = jnp.exp(log_dt)
    mag = jnp.exp(lr * dt)
    ang = li * dt
    a_re = mag * jnp.cos(ang)
    a_im = mag * jnp.sin(ang)
    den = lr * lr + li * li
    nr = a_re - 1.0
    coef_re = (nr * lr + a_im * li) / den
    coef_im = (a_im * lr - nr * li) / den
    return (a_re, a_im), (coef_re, coef_im)


def _s5_weights_kernel(lam_pg_ref, ldt_pg_ref, c_pg_ref, lam_gp_ref, ldt_gp_ref, b_gp_ref,
                       lam_row_ref, ldt_row_ref, b_tl_ref, wm_ref, wr_ref, wo_ref, a_ref):
    tc, ch, ns, gpb = S5_TC, S5_GROUP, S5_STATE, S5_GPB
    lw = gpb * ch
    hi = lax.Precision.HIGHEST

    a_pg, _ = _s5_discretise(lam_pg_ref[0], lam_pg_ref[1], ldt_pg_ref[...])
    pw = (jnp.ones_like(a_pg[0]), jnp.zeros_like(a_pg[0]))
    ca = []
    for n in range(tc + 1):
        ca.append(_cmul(c_pg_ref[0], c_pg_ref[1], *pw))
        pw = _cmul(*pw, *a_pg)

    _, coef_gp = _s5_discretise(lam_gp_ref[0], lam_gp_ref[1], ldt_gp_ref[...])
    bb_gp = _cmul(*coef_gp, b_gp_ref[0], b_gp_ref[1])
    ca_re = jnp.concatenate([ca[n][0] for n in range(tc)], axis=1)
    ca_im = jnp.concatenate([ca[n][1] for n in range(tc)], axis=1)
    z = (jnp.dot(bb_gp[0], ca_re, precision=hi, preferred_element_type=F32)
         - jnp.dot(bb_gp[1], ca_im, precision=hi, preferred_element_type=F32))
    row_g = lax.broadcasted_iota(jnp.int32, z.shape, 0) // ch
    col_g = (lax.broadcasted_iota(jnp.int32, z.shape, 1) % lw) // ch
    z = jnp.where(row_g == col_g, z, 0.0).astype(BF16)
    for t in range(tc):
        parts = [z[:, :(tc - t) * lw]]
        if t:
            parts.insert(0, jnp.zeros((lw, t * lw), BF16))
        wm_ref[0, t * lw:(t + 1) * lw, :] = jnp.concatenate(parts, axis=1)

    rg = lax.broadcasted_iota(jnp.int32, (gpb * ns, lw), 0) // ns
    cg = lax.broadcasted_iota(jnp.int32, (gpb * ns, lw), 1) // ch
    keep = rg == cg
    for t2 in range(tc):
        cr, ci = ca[t2 + 1]
        cr = jnp.where(keep, jnp.concatenate([cr] * gpb, axis=0), 0.0)
        ci = jnp.where(keep, jnp.concatenate([ci] * gpb, axis=0), 0.0)
        wo_ref[0, :gpb * ns, t2 * lw:(t2 + 1) * lw] = cr.astype(BF16)
        wo_ref[0, gpb * ns:, t2 * lw:(t2 + 1) * lw] = (-ci).astype(BF16)

    a_row, coef_row = _s5_discretise(lam_row_ref[0], lam_row_ref[1], ldt_row_ref[...])
    bb_tl = _cmul(*coef_row, b_tl_ref[0], b_tl_ref[1])
    rg = lax.broadcasted_iota(jnp.int32, bb_tl[0].shape, 0) // ch
    cg = lax.broadcasted_iota(jnp.int32, bb_tl[0].shape, 1) // ns
    bb_tl = (jnp.where(rg == cg, bb_tl[0], 0.0), jnp.where(rg == cg, bb_tl[1], 0.0))
    pw = (jnp.ones_like(a_row[0]), jnp.zeros_like(a_row[0]))
    for t in range(tc - 1, -1, -1):
        r_re, r_im = _cmul(*pw, *bb_tl)
        wr_ref[0, t * lw:(t + 1) * lw, :] = jnp.concatenate([r_re, r_im], axis=1).astype(BF16)
        pw = _cmul(*pw, *a_row)
    a_ref[0, 0] = pw[0]
    a_ref[0, 1] = pw[1]


def _s5_weights(lam_re, lam_im, log_dt, b_re, b_im, c_re, c_im):
    g, ns = lam_re.shape
    ch, tc, gpb = S5_GROUP, S5_TC, S5_GPB
    nblk = g // gpb
    lw = gpb * ch
    lam = jnp.stack([lam_re, lam_im]).astype(F32)
    ldt = log_dt.astype(F32)
    bst = jnp.stack([b_re, b_im]).astype(F32)
    cst = jnp.stack([c_re, c_im]).astype(F32)
    lam_pg = jnp.repeat(lam.transpose(0, 2, 1), ch, axis=2)
    ldt_pg = jnp.repeat(ldt, ch)[None]
    c_pg = cst.transpose(0, 3, 1, 2).reshape(2, ns, g * ch)
    lam_gp = jnp.repeat(lam, ch, axis=1)
    ldt_gp = jnp.broadcast_to(jnp.repeat(ldt, ch)[:, None], (g * ch, ns))
    b_gp = bst.transpose(0, 1, 3, 2).reshape(2, g * ch, ns)
    lam_row = lam.reshape(2, 1, g * ns)
    ldt_row = jnp.repeat(ldt, ns)[None]
    b_tl = jnp.tile(b_gp, (1, 1, gpb))
    blk3 = lambda i: (0, 0, i)
    return pl.pallas_call(
        _s5_weights_kernel,
        name="s5_weights",
        grid=(nblk,),
        in_specs=[pl.BlockSpec((2, ns, lw), blk3),
                  pl.BlockSpec((1, lw), lambda i: (0, i)),
                  pl.BlockSpec((2, ns, lw), blk3),
                  pl.BlockSpec((2, lw, ns), lambda i: (0, i, 0)),
                  pl.BlockSpec((lw, ns), lambda i: (i, 0)),
                  pl.BlockSpec((2, lw, ns), lambda i: (0, i, 0)),
                  pl.BlockSpec((2, 1, gpb * ns), blk3),
                  pl.BlockSpec((1, gpb * ns), lambda i: (0, i)),
                  pl.BlockSpec((2, lw, gpb * ns), lambda i: (0, i, 0))],
        out_specs=[pl.BlockSpec((1, tc * lw, tc * lw), lambda i: (i, 0, 0)),
                   pl.BlockSpec((1, tc * lw, 2 * gpb * ns), lambda i: (i, 0, 0)),
                   pl.BlockSpec((1, 2 * gpb * ns, tc * lw), lambda i: (i, 0, 0)),
                   pl.BlockSpec((1, 2, 1, gpb * ns), lambda i: (i, 0, 0, 0))],
        out_shape=[jax.ShapeDtypeStruct((nblk, tc * lw, tc * lw), BF16),
                   jax.ShapeDtypeStruct((nblk, tc * lw, 2 * gpb * ns), BF16),
                   jax.ShapeDtypeStruct((nblk, 2 * gpb * ns, tc * lw), BF16),
                   jax.ShapeDtypeStruct((nblk, 2, 1, gpb * ns), F32)],
        compiler_params=_cparams(("arbitrary",)),
    )(lam_pg, ldt_pg, c_pg, lam_gp, ldt_gp, b_gp, lam_row, ldt_row, b_tl)


def _s5_mix_kernel(u_ref, wm_ref, wr_ref, wo_ref, a_ref, d_ref, y_ref, s_ref, x_ref, ys_ref):
    nk = u_ref.shape[1]
    half = s_ref.shape[1] // 2
    sub = 8
    u = u_ref[0]
    s_ref[...] = jnp.dot(u, wr_ref[0], preferred_element_type=F32)

    a1 = (a_ref[0, 0], a_ref[0, 1])
    a2 = _cmul(*a1, *a1)
    a4 = _cmul(*a2, *a2)
    a8 = _cmul(*a4, *a4)
    a3 = _cmul(*a2, *a1)
    pows = [(jnp.ones_like(a1[0]), jnp.zeros_like(a1[0])), a1, a2, a3, a4,
            _cmul(*a4, *a1), _cmul(*a4, *a2), _cmul(*a4, *a3)]
    p_re = jnp.concatenate([p[0] for p in pows], axis=0)
    p_im = jnp.concatenate([p[1] for p in pows], axis=0)
    rowid = lax.broadcasted_iota(jnp.int32, (sub, half), 0)
    levels = [(d, jnp.broadcast_to(a[0], (sub, half)), jnp.broadcast_to(a[1], (sub, half)))
              for d, a in ((1, a1), (2, a2), (4, a4))]

    def shift_rows(v, d):
        return jnp.where(rowid >= d, pltpu.roll(v, d, 0), 0.0)

    def tile(j, carry):
        cr, ci = carry
        rows = pl.ds(pl.multiple_of(j * sub, sub), sub)
        zr = s_ref[rows, :half]
        zi = s_ref[rows, half:]
        for d, br, bi in levels:
            sr, si = shift_rows(zr, d), shift_rows(zi, d)
            zr, zi = zr + br * sr - bi * si, zi + br * si + bi * sr
        x_ref[rows, :half] = p_re * cr - p_im * ci + shift_rows(zr, 1)
        x_ref[rows, half:] = p_re * ci + p_im * cr + shift_rows(zi, 1)
        return (a8[0] * cr - a8[1] * ci + zr[sub - 1:], a8[0] * ci + a8[1] * cr + zi[sub - 1:])

    zero = jnp.zeros((1, half), F32)
    lax.fori_loop(0, nk // sub, tile, (zero, zero))

    y = jnp.dot(u, wm_ref[0], preferred_element_type=F32)
    y = y + jnp.dot(x_ref[...].astype(BF16), wo_ref[0], preferred_element_type=F32)
    y = _gelu_tanh(y + d_ref[0] * u.astype(F32))
    for t in range(S5_TC):
        ys_ref[pl.ds(t, nk, stride=S5_TC), :] = y[:, t * LANES:(t + 1) * LANES]
    y_ref[0] = ys_ref[...].astype(y_ref.dtype)


def _s5_mixer(u3, nb, lam_re, lam_im, log_dt, b_re, b_im, c_re, c_im, d_skip):
    nblk, rows, w = u3.shape
    tc = S5_TC
    nk = rows // nb
    assert nk % 8 == 0
    wm, wr, wo, a_pow = _s5_weights(lam_re, lam_im, log_dt, b_re, b_im, c_re, c_im)
    ns = wr.shape[-1]
    d_rows = jnp.tile(d_skip.astype(F32).reshape(nblk, 1, LANES), (1, 1, tc))
    blk = lambda i, b: (i, 0, 0)
    return pl.pallas_call(
        _s5_mix_kernel,
        name="s5_mix",
        grid=(nblk, nb),
        in_specs=[pl.BlockSpec((1, nk, w), lambda i, b: (i, b, 0)),
                  pl.BlockSpec((1, w, w), blk),
                  pl.BlockSpec((1, w, ns), blk),
                  pl.BlockSpec((1, ns, w), blk),
                  pl.BlockSpec((1, 2, 1, ns // 2), lambda i, b: (i, 0, 0, 0)),
                  pl.BlockSpec((1, 1, w), blk)],
        out_specs=pl.BlockSpec((1, nk * tc, LANES), lambda i, b: (i, b, 0)),
        out_shape=jax.ShapeDtypeStruct((nblk, rows * tc, LANES), BF16),
        scratch_shapes=[pltpu.VMEM((nk, ns), F32), pltpu.VMEM((nk, ns), F32),
                        pltpu.VMEM((nk * tc, LANES), F32)],
        compiler_params=_cparams(("arbitrary", "arbitrary")),
    )(u3, wm, wr, wo, a_pow, d_rows)


def _even_out_kernel(x_ref, oa_ref, y_ref, zb_ref, wg_ref, bg_ref, wo_ref, g_ref,
                     x1_ref, hn_ref):
    y = jnp.concatenate([y_ref[i] for i in range(y_ref.shape[0])], axis=1)
    t = jnp.dot(y, wg_ref[...], preferred_element_type=F32) + bg_ref[...]
    ob = y.astype(F32) * jax.nn.sigmoid(t) * _silu(zb_ref[...].astype(F32))
    wa = oa_ref.shape[1]
    acc = jnp.dot(oa_ref[...], wo_ref[:wa, :], preferred_element_type=F32)
    acc = acc + jnp.dot(ob.astype(BF16), wo_ref[wa:, :], preferred_element_type=F32)
    x1 = x_ref[...] + acc
    x1_ref[...] = x1
    hn_ref[...] = _rms(x1, g_ref[...]).astype(BF16)


def _even_out(x2, oa, y, proj, w_glu, b_glu, w_out, g_next):
    m, d = x2.shape
    wa = oa.shape[1]
    nblk, _, lw = y.shape
    wb = nblk * lw
    tm = min(TM_OUT, m)
    zb_col = (proj.shape[1] - wb) // wb
    const = lambda i: (0, 0)
    return pl.pallas_call(
        _even_out_kernel,
        name="even_out",
        grid=(m // tm,),
        in_specs=[pl.BlockSpec((tm, d), lambda i: (i, 0)),
                  pl.BlockSpec((tm, wa), lambda i: (i, 0)),
                  pl.BlockSpec((nblk, tm, lw), lambda i: (0, i, 0)),
                  pl.BlockSpec((tm, wb), lambda i: (i, zb_col)),
                  pl.BlockSpec((wb, wb), const),
                  pl.BlockSpec((1, wb), const),
                  pl.BlockSpec((wa + wb, d), const),
                  pl.BlockSpec((1, d), const)],
        out_specs=[pl.BlockSpec((tm, d), lambda i: (i, 0)),
                   pl.BlockSpec((tm, d), lambda i: (i, 0))],
        out_shape=[jax.ShapeDtypeStruct((m, d), F32),
                   jax.ShapeDtypeStruct((m, d), BF16)],
        compiler_params=_cparams(("arbitrary",)),
    )(x2, oa, y, proj, w_glu, b_glu.reshape(1, wb), w_out, g_next.reshape(1, d))


def _cum_forget_kernel(f_ref, b_ref, c_ref):
    x = f_ref[...] + b_ref[...]
    v = -(jnp.maximum(-x, 0.0) + jnp.log1p(jnp.exp(-jnp.abs(x))))
    n = v.shape[-1]
    pos = lax.broadcasted_iota(jnp.int32, v.shape, 1)
    d = 1
    while d < n:
        v = v + jnp.where(pos >= d, pltpu.roll(v, d, 1), 0.0)
        d *= 2
    c_ref[...] = v


def _cum_forget(f_t, b_forget, seq):
    nh, m = f_t.shape
    return pl.pallas_call(
        _cum_forget_kernel,
        name="cum_forget",
        grid=(m // seq,),
        in_specs=[pl.BlockSpec((nh, seq), lambda i: (0, i)),
                  pl.BlockSpec((nh, 1), lambda i: (0, 0))],
        out_specs=pl.BlockSpec((nh, seq), lambda i: (0, i)),
        out_shape=jax.ShapeDtypeStruct((nh, m), F32),
        compiler_params=_cparams(("arbitrary",)),
    )(f_t, b_forget.astype(F32).reshape(nh, 1))


def _split3(x):
    hi = x.astype(BF16).astype(F32)
    r = x - hi
    mid = r.astype(BF16).astype(F32)
    lo = (r - mid).astype(BF16).astype(F32)
    return hi, mid, lo


def _fox_kernel(q0_ref, qn_ref, k_ref, v_ref, z_ref, c_ref, o_ref,
                ke_ref, vt_ref, qa_ref, s_ref, mb_ref, acc_ref):
    qi = pl.program_id(2)
    nq = pl.num_programs(2)
    tq = qn_ref.shape[1]
    tk = tq
    dh = C_DH
    nh = c_ref.shape[1]
    nblk = k_ref.shape[1] // tk
    nt = (((1,), (1,)), ((), ()))
    key = lax.broadcasted_iota(jnp.int32, (tk, tq), 0)
    qry = lax.broadcasted_iota(jnp.int32, (tk, tq), 1)

    def cols(h):
        return slice(h * dh, (h + 1) * dh)

    def prep_queries(q_ref, qb):
        qs = pl.multiple_of(qb * tq, tq)
        lane = lax.broadcasted_iota(jnp.int32, (tq, dh), 1)
        for h in range(nh):
            c_base = jnp.max(c_ref[0, h, :, pl.ds(qs, tq)], axis=1, keepdims=True) * LOG2E
            bh, bm, bl = _split3(c_base)
            g = jnp.where(lane < 3, 1.0, jnp.where(lane == 3, bh, jnp.where(lane == 4, bm,
                          jnp.where(lane == 5, bl, 0.0))))
            qa_ref[h, :, :dh] = (q_ref[0, :, cols(h)].astype(F32) * (LOG2E / math.sqrt(dh))
                                 ).astype(BF16)
            qa_ref[h, :, dh:] = g.astype(BF16)

    def produce(h, j, keep):
        ks = pl.multiple_of(j * tk, tk)
        ka = jnp.concatenate([k_ref[0, pl.ds(ks, tk), cols(h)], ke_ref[h, pl.ds(ks, tk), :]],
                             axis=1)
        s = lax.dot_general(ka, qa_ref[h], nt, preferred_element_type=F32)
        if keep is not None:
            s = jnp.where(keep, s, NEG)
        s_ref[h] = s
        return jnp.max(s, axis=0, keepdims=True)

    def consume(h, j, stats):
        m, mb = stats
        ks = pl.multiple_of(j * tk, tk)
        m2 = jnp.maximum(m, mb)
        p = jnp.exp2(s_ref[h] - m2).astype(BF16)
        pv = jnp.dot(vt_ref[h, :, pl.ds(ks, tk)], p, preferred_element_type=F32)
        acc_ref[h] = jnp.exp2(m - m2) * acc_ref[h] + pv
        return m2

    def step(j, carry, diagonal_next):
        keep = (key <= qry) if diagonal_next else None
        return tuple((consume(h, j, stats), produce(h, j + 1, keep))
                     for h, stats in enumerate(carry))

    @pl.when(qi == 0)
    def _():
        r = lax.broadcasted_iota(jnp.int32, (dh, dh), 0)
        c = lax.broadcasted_iota(jnp.int32, (dh, dh), 1)
        eye = (r == c).astype(BF16)
        row = lax.broadcasted_iota(jnp.int32, (dh, tk), 0)
        ones_row = (lax.broadcasted_iota(jnp.int32, (FOX_PAD, tk), 0) == 0).astype(BF16)

        def blk(j, carry):
            ks = pl.multiple_of(j * tk, tk)
            for h in range(nh):
                hi, mid, lo = _split3(c_ref[0, h, :, pl.ds(ks, tk)] * (-LOG2E))
                a = jnp.where(row == 0, hi, jnp.where(row == 1, mid, jnp.where(row == 2, lo,
                              jnp.where(row < 6, 1.0, 0.0))))
                ke_ref[h, pl.ds(ks, tk), :] = a.T.astype(BF16)
                vt = lax.dot_general(eye, v_ref[0, pl.ds(ks, tk), cols(h)], nt,
                                     preferred_element_type=F32)
                vt_ref[h, :dh, pl.ds(ks, tk)] = vt.astype(BF16)
                vt_ref[h, dh:, pl.ds(ks, tk)] = ones_row
            return carry

        lax.fori_loop(0, nblk, blk, 0)
        prep_queries(q0_ref, 0)
        for h in range(nh):
            mb_ref[h] = produce(h, 0, key <= qry)

    carry = []
    for h in range(nh):
        acc_ref[h] = jnp.zeros(acc_ref.shape[1:], F32)
        carry.append((jnp.full((1, tq), NEG, F32), mb_ref[h]))
    plain = functools.partial(step, diagonal_next=False)
    n_plain = jnp.maximum(qi - 1, 0)
    n_pairs = n_plain // 2
    carry = lax.fori_loop(0, n_pairs, lambda i, c: plain(2 * i + 1, plain(2 * i, c)),
                          tuple(carry))
    carry = lax.fori_loop(2 * n_pairs, n_plain, plain, carry)
    carry = lax.fori_loop(n_plain, qi, functools.partial(step, diagonal_next=True), carry)
    for h, stats in enumerate(carry):
        consume(h, qi, stats)
    prep_queries(qn_ref, jnp.minimum(qi + 1, nq - 1))
    for h in range(nh):
        mb_ref[h] = produce(h, 0, None)
    for h in range(nh):
        acc = acc_ref[h]
        o = (acc[:dh] / acc[dh:dh + 1]).T
        o_ref[0, :, cols(h)] = (o * _silu(z_ref[0, :, cols(h)].astype(F32))).astype(o_ref.dtype)


def _fox(proj3, cum4):
    b, l, _ = proj3.shape
    ng = C_HEADS // FOX_HEADS
    w = FOX_HEADS * C_DH
    tq = min(FOX_TQ, l)
    nq = l // tq
    return pl.pallas_call(
        _fox_kernel,
        name="fox",
        grid=(b, ng, nq),
        in_specs=[pl.BlockSpec((1, tq, w), lambda bi, gi, qi: (bi, 0, gi)),
                  pl.BlockSpec((1, tq, w), lambda bi, gi, qi: (bi, jnp.minimum(qi + 1, nq - 1), gi)),
                  pl.BlockSpec((1, l, w), lambda bi, gi, qi: (bi, 0, ng + gi)),
                  pl.BlockSpec((1, l, w), lambda bi, gi, qi: (bi, 0, 2 * ng + gi)),
                  pl.BlockSpec((1, tq, w), lambda bi, gi, qi: (bi, qi, 3 * ng + gi)),
                  pl.BlockSpec((1, FOX_HEADS, 1, l), lambda bi, gi, qi: (bi, gi, 0, 0))],
        out_specs=pl.BlockSpec((1, tq, w), lambda bi, gi, qi: (bi, qi, gi)),
        out_shape=jax.ShapeDtypeStruct((b, l, C_HEADS * C_DH), BF16),
        scratch_shapes=[pltpu.VMEM((FOX_HEADS, l, C_DH), BF16),
                        pltpu.VMEM((FOX_HEADS, C_DH + FOX_PAD, l), BF16),
                        pltpu.VMEM((FOX_HEADS, tq, 2 * C_DH), BF16),
                        pltpu.VMEM((FOX_HEADS, tq, tq), F32),
                        pltpu.VMEM((FOX_HEADS, 1, tq), F32),
                        pltpu.VMEM((FOX_HEADS, C_DH + FOX_PAD, tq), F32)],
        compiler_params=_cparams(("arbitrary", "arbitrary", "arbitrary")),
    )(proj3, proj3, proj3, proj3, proj3, cum4)


def _odd_out_kernel(x_ref, o_ref, w_ref, g_ref, out_ref):
    x2 = x_ref[...] + jnp.dot(o_ref[...], w_ref[...], preferred_element_type=F32)
    out_ref[...] = _rms(x2, g_ref[...])


def _odd_out(x1, o, w_out, g):
    m, d = x1.shape
    k = o.shape[1]
    tm = min(TM_OUT, m)
    return pl.pallas_call(
        _odd_out_kernel,
        name="odd_out",
        grid=(m // tm,),
        in_specs=[pl.BlockSpec((tm, d), lambda i: (i, 0)),
                  pl.BlockSpec((tm, k), lambda i: (i, 0)),
                  pl.BlockSpec((k, d), lambda i: (0, 0)),
                  pl.BlockSpec((1, d), lambda i: (0, 0))],
        out_specs=pl.BlockSpec((tm, d), lambda i: (i, 0)),
        out_shape=jax.ShapeDtypeStruct((m, d), F32),
        compiler_params=_cparams(("arbitrary",)),
    )(x1, o, w_out, g.reshape(1, d))


def kernel(x, norm_even_g, w_in_even, rel_bias, s5_lambda_re, s5_lambda_im, s5_log_dt,
           s5_b_re, s5_b_im, s5_c_re, s5_c_im, s5_d, w_glu, b_glu, w_out_even,
           norm_odd_g, w_in_odd, b_forget, w_out_odd, final_norm_g):
    b, l, d = x.shape
    assert norm_even_g.shape[0] == 1 and norm_odd_g.shape[0] == 1, "one even + one odd layer"
    assert l % FOX_TQ == 0 and l % VT_BLK == 0 and l % (A_QB * A_UNROLL) == 0 and l >= A_KB
    m = b * l
    a_w = A_HEADS * A_DH
    s5_w = s5_d.shape[1] * S5_GROUP
    c_w = C_HEADS * C_DH

    x2 = x.reshape(m, d)
    assert s5_w == TN_PROJ and l % S5_TC == 0
    proj, u3 = _norm_proj(x2, norm_even_g[0], w_in_even[0].astype(BF16), 3 * a_w)
    proj3 = proj.reshape(b, l, proj.shape[1])
    o_a = _attn_a(proj3, rel_bias[0])
    y = _s5_mixer(u3, b, s5_lambda_re[0], s5_lambda_im[0], s5_log_dt[0], s5_b_re[0],
                  s5_b_im[0], s5_c_re[0], s5_c_im[0], s5_d[0])
    x1, hn1 = _even_out(x2, o_a.reshape(m, a_w), y, proj,
                        w_glu[0].astype(BF16), b_glu[0].astype(F32),
                        w_out_even[0].astype(BF16), norm_odd_g[0])
    w_odd = w_in_odd[0]
    proj1, f_t = _proj_forget(hn1, w_odd[:, :4 * c_w].astype(BF16),
                              w_odd[:, 4 * c_w:].T.astype(BF16))
    cum = _cum_forget(f_t, b_forget[0], l)
    cum4 = cum.reshape(C_HEADS, b, 1, l).transpose(1, 0, 2, 3)
    o_c = _fox(proj1.reshape(b, l, 4 * c_w), cum4)
    out = _odd_out(x1, o_c.reshape(m, c_w), w_out_odd[0].astype(BF16), final_norm_g)
    return out.reshape(b, l, d)
```

```python
import functools
import math

import jax
import jax.numpy as jnp
from jax import lax
from jax.experimental import pallas as pl
from jax.experimental.pallas import tpu as pltpu

F32 = jnp.float32
BF16 = jnp.bfloat16

EPS = 1e-6
NEG = -1e30
LOG2E = math.log2(math.e)

CHUNK = 64
N_LEFT = 8
A_HEADS = 16
A_DH = 64
MAX_REL = 128
S5_GROUP = 16
S5_STATE = 64
C_HEADS = 16
C_DH = 128

LANES = 128
VMEM_LIMIT = 56 * 1024 * 1024

TM_PROJ = 1024
TN_PROJ = 1024
TM_OUT = 512
A_QB = 2 * CHUNK
A_KB = A_QB + N_LEFT * CHUNK
A_TW = A_KB + N_LEFT * CHUNK
A_PAD = 16
VT_BLK = 512
S5_TC = 16
S5_GPB = LANES // S5_GROUP
FOX_TQ = 512
FOX_HEADS = 4
FOX_PAD = 16


def _cparams(sem):
    return pltpu.CompilerParams(dimension_semantics=sem, vmem_limit_bytes=VMEM_LIMIT)


def _rms(x, g):
    ms = jnp.mean(x * x, axis=-1, keepdims=True)
    return x * lax.rsqrt(ms + EPS) * g


def _silu(x):
    return x * jax.nn.sigmoid(x)


def _gelu_tanh(x):
    c = math.sqrt(2.0 / math.pi)
    return 0.5 * x * (1.0 + jnp.tanh(c * (x + 0.044715 * (x * x * x))))


def _norm_proj_kernel(x_ref, g_ref, w_ref, o_ref, u_ref, hn_ref, us_ref, *, u_tile):
    j = pl.program_id(1)

    @pl.when(j == 0)
    def _():
        hn_ref[...] = _rms(x_ref[...], g_ref[...]).astype(BF16)

    acc = jnp.dot(hn_ref[...], w_ref[...], preferred_element_type=F32)
    o_ref[...] = acc.astype(o_ref.dtype)

    @pl.when(j == u_tile)
    def _():
        nchunk = us_ref.shape[1] // S5_TC
        for blk in range(u_ref.shape[0]):
            us_ref[blk] = acc[:, blk * LANES:(blk + 1) * LANES]
            for t in range(S5_TC):
                rows = us_ref[blk, pl.ds(t, nchunk, stride=S5_TC), :]
                u_ref[blk, :, t * LANES:(t + 1) * LANES] = rows.astype(u_ref.dtype)


def _norm_proj(x2, g, w, u_start):
    m, d = x2.shape
    n = w.shape[1]
    tm, tn = min(TM_PROJ, m), TN_PROJ
    assert u_start % tn == 0 and tm % (16 * S5_TC) == 0
    nblk = tn // LANES
    return pl.pallas_call(
        functools.partial(_norm_proj_kernel, u_tile=u_start // tn),
        name="norm_proj",
        grid=(m // tm, n // tn),
        in_specs=[pl.BlockSpec((tm, d), lambda i, j: (i, 0)),
                  pl.BlockSpec((1, d), lambda i, j: (0, 0)),
                  pl.BlockSpec((d, tn), lambda i, j: (0, j))],
        out_specs=[pl.BlockSpec((tm, tn), lambda i, j: (i, j)),
                   pl.BlockSpec((nblk, tm // S5_TC, S5_TC * LANES), lambda i, j: (0, i, 0))],
        out_shape=[jax.ShapeDtypeStruct((m, n), BF16),
                   jax.ShapeDtypeStruct((nblk, m // S5_TC, S5_TC * LANES), BF16)],
        scratch_shapes=[pltpu.VMEM((tm, d), BF16), pltpu.VMEM((nblk, tm, LANES), F32)],
        compiler_params=_cparams(("arbitrary", "arbitrary")),
    )(x2, g.reshape(1, d), w)


def _proj_forget_kernel(h_ref, w_ref, wf_ref, o_ref, f_ref):
    @pl.when(pl.program_id(1) == 0)
    def _():
        f_ref[...] = lax.dot_general(wf_ref[...], h_ref[...],
                                     (((1,), (1,)), ((), ())),
                                     preferred_element_type=F32)

    o_ref[...] = jnp.dot(h_ref[...], w_ref[...],
                         preferred_element_type=F32).astype(o_ref.dtype)


def _proj_forget(hn, w, wf_t):
    m, d = hn.shape
    n = w.shape[1]
    nh = wf_t.shape[0]
    tm, tn = min(TM_PROJ, m), TN_PROJ
    return pl.pallas_call(
        _proj_forget_kernel,
        name="proj_forget",
        grid=(m // tm, n // tn),
        in_specs=[pl.BlockSpec((tm, d), lambda i, j: (i, 0)),
                  pl.BlockSpec((d, tn), lambda i, j: (0, j)),
                  pl.BlockSpec((nh, d), lambda i, j: (0, 0))],
        out_specs=[pl.BlockSpec((tm, tn), lambda i, j: (i, j)),
                   pl.BlockSpec((nh, tm), lambda i, j: (0, i))],
        out_shape=[jax.ShapeDtypeStruct((m, n), BF16),
                   jax.ShapeDtypeStruct((nh, m), F32)],
        compiler_params=_cparams(("arbitrary", "arbitrary")),
    )(hn, w, wf_t)


def _attn_a_kernel(q_ref, k_ref, v_ref, z_ref, e_ref, o_ref, tt_ref, vt_ref, s_ref, pv_ref):
    l_all = k_ref.shape[1]
    w = q_ref.shape[2]
    nt = (((1,), (1,)), ((), ()))
    band = N_LEFT * CHUNK

    @pl.when(pl.program_id(1) == 0)
    def _():
        i = lax.broadcasted_iota(jnp.int32, (A_QB, A_TW), 0)
        m = lax.broadcasted_iota(jnp.int32, (A_QB, A_TW), 1)
        dchunk = m // CHUNK - i // CHUNK
        ok = (dchunk >= 0) & (dchunk <= N_LEFT)
        for h in range(2):
            rows = jnp.broadcast_to(e_ref[0, h:h + 1, :], (A_QB, A_QB + A_TW))
            toep = pltpu.roll(rows, 0, 1, stride=1, stride_axis=0)[:, A_QB:]
            t = jnp.where(ok, toep * LOG2E, NEG)
            tt_ref[:, h * A_QB:(h + 1) * A_QB] = t.T

    r = lax.broadcasted_iota(jnp.int32, (w, w), 0)
    c = lax.broadcasted_iota(jnp.int32, (w, w), 1)
    eye = (r == c).astype(BF16)
    ones_row = (lax.broadcasted_iota(jnp.int32, (A_PAD, VT_BLK), 0) == 0).astype(BF16)

    def vt_blk(j, carry):
        ks = pl.multiple_of(j * VT_BLK, VT_BLK)
        vt = lax.dot_general(eye, v_ref[0, pl.ds(ks, VT_BLK), :], nt, preferred_element_type=F32)
        for h in range(2):
            vt_ref[h, :A_DH, pl.ds(ks, VT_BLK)] = vt[h * A_DH:(h + 1) * A_DH].astype(BF16)
            vt_ref[h, A_DH:, pl.ds(ks, VT_BLK)] = ones_row
        return carry

    lax.fori_loop(0, l_all // VT_BLK, vt_blk, 0)

    lane = lax.broadcasted_iota(jnp.int32, (A_QB, w), 1)
    qscale = LOG2E / math.sqrt(A_DH)

    def window(qb):
        start = qb * A_QB - band
        start_c = jnp.maximum(start, 0)
        return pl.multiple_of(start_c, LANES), pl.multiple_of(start_c - start, LANES)

    def produce(qb, slot):
        qs = pl.multiple_of(qb * A_QB, A_QB)
        start_c, off = window(qb)
        q = (q_ref[0, pl.ds(qs, A_QB), :].astype(F32) * qscale).astype(BF16)
        zero = jnp.zeros_like(q)
        qbd = jnp.concatenate([jnp.where(lane < A_DH, q, zero), jnp.where(lane >= A_DH, q, zero)],
                              axis=0)
        kb = k_ref[0, pl.ds(start_c, A_KB), :]
        s = lax.dot_general(kb, qbd, nt, preferred_element_type=F32)
        s = s + tt_ref[pl.ds(off, A_KB), :]
        s_ref[slot] = s
        return jnp.max(s, axis=0, keepdims=True)

    def consume(qb, slot, m):
        start_c, _ = window(qb)
        pb = jnp.exp2(s_ref[slot] - m).astype(BF16)
        for h in range(2):
            vt = vt_ref[h, :, pl.ds(start_c, A_KB)]
            pv_ref[slot, h] = jnp.dot(vt, pb[:, h * A_QB:(h + 1) * A_QB],
                                      preferred_element_type=F32)

    def finish(qb, slot):
        qs = pl.multiple_of(qb * A_QB, A_QB)
        outs = []
        for h in range(2):
            oh = pv_ref[slot, h]
            outs.append(oh[:A_DH] * (1.0 / oh[A_DH:A_DH + 1]))
        o = jnp.concatenate(outs, axis=0).T
        gate = _silu(z_ref[0, pl.ds(qs, A_QB), :].astype(F32))
        o_ref[0, pl.ds(qs, A_QB), :] = (o * gate).astype(o_ref.dtype)

    def pair(i, m, lead=True):
        j = 2 * i
        if lead:
            finish(j - 1, 1)
        m1 = produce(j + 1, 1)
        consume(j, 0, m)
        m2 = produce(j + 2, 0)
        finish(j, 0)
        consume(j + 1, 1, m1)
        return m2

    nb = l_all // A_QB
    npairs = (nb - 1) // 2
    m = pair(0, produce(0, 0), lead=False)
    m = lax.fori_loop(1, npairs, pair, m)
    last = 2 * npairs
    finish(last - 1, 1)
    if (nb - 1) % 2:
        m1 = produce(last + 1, 1)
        consume(last, 0, m)
        finish(last, 0)
        consume(last + 1, 1, m1)
        finish(last + 1, 1)
    else:
        consume(last, 0, m)
        finish(last, 0)


def _attn_a(proj3, rel_bias):
    b, l, _ = proj3.shape
    hp = A_HEADS // 2
    width = 2 * A_DH
    ncol = A_HEADS * A_DH // width
    zcol = (3 * A_HEADS * A_DH + (A_HEADS * A_DH)) // width
    left = N_LEFT * CHUNK + A_QB - MAX_REL
    right = A_QB + A_TW - left - (2 * MAX_REL + 1)
    ext = jnp.pad(rel_bias.astype(F32)[:, ::-1], ((0, 0), (left, right)), mode="edge")
    ext = ext.reshape(hp, 2, A_QB + A_TW)
    seq = lambda h, bi: (bi, 0, h)
    return pl.pallas_call(
        _attn_a_kernel,
        name="attn_a",
        grid=(hp, b),
        in_specs=[pl.BlockSpec((1, l, width), seq),
                  pl.BlockSpec((1, l, width), lambda h, bi: (bi, 0, ncol + h)),
                  pl.BlockSpec((1, l, width), lambda h, bi: (bi, 0, 2 * ncol + h)),
                  pl.BlockSpec((1, l, width), lambda h, bi: (bi, 0, zcol + h)),
                  pl.BlockSpec((1, 2, A_QB + A_TW), lambda h, bi: (h, 0, 0))],
        out_specs=pl.BlockSpec((1, l, width), seq),
        out_shape=jax.ShapeDtypeStruct((b, l, A_HEADS * A_DH), BF16),
        scratch_shapes=[pltpu.VMEM((A_TW, 2 * A_QB), F32),
                        pltpu.VMEM((2, A_DH + A_PAD, l), BF16),
                        pltpu.VMEM((2, A_KB, 2 * A_QB), F32),
                        pltpu.VMEM((2, 2, A_DH + A_PAD, A_QB), F32)],
        compiler_params=_cparams(("arbitrary", "arbitrary")),
    )(proj3, proj3, proj3, proj3, ext)


def _cmul(ar, ai, br, bi):
    return ar * br - ai * bi, ar * bi + ai * br


def _s5_discretise(lr, li, log_dt):
    dt = jnp.exp(log_dt)
    mag = jnp.exp(lr * dt)
    ang = li * dt
    a_re = mag * jnp.cos(ang)
    a_im = mag * jnp.sin(ang)
    den = lr * lr + li * li
    nr = a_re - 1.0
    coef_re = (nr * lr + a_im * li) / den
    coef_im = (a_im * lr - nr * li) / den
    return (a_re, a_im), (coef_re, coef_im)


def _s5_weights_kernel(lam_pg_ref, ldt_pg_ref, c_pg_ref, lam_gp_ref, ldt_gp_ref, b_gp_ref,
                       lam_row_ref, ldt_row_ref, b_tl_ref, wm_ref, wr_ref, wo_ref, a_ref):
    tc, ch, ns, gpb = S5_TC, S5_GROUP, S5_STATE, S5_GPB
    lw = gpb * ch
    hi = lax.Precision.HIGHEST

    a_pg, _ = _s5_discretise(lam_pg_ref[0], lam_pg_ref[1], ldt_pg_ref[...])
    pw = (jnp.ones_like(a_pg[0]), jnp.zeros_like(a_pg[0]))
    ca = []
    for n in range(tc + 1):
        ca.append(_cmul(c_pg_ref[0], c_pg_ref[1], *pw))
        pw = _cmul(*pw, *a_pg)

    _, coef_gp = _s5_discretise(lam_gp_ref[0], lam_gp_ref[1], ldt_gp_ref[...])
    bb_gp = _cmul(*coef_gp, b_gp_ref[0], b_gp_ref[1])
    ca_re = jnp.concatenate([ca[n][0] for n in range(tc)], axis=1)
    ca_im = jnp.concatenate([ca[n][1] for n in range(tc)], axis=1)
    z = (jnp.dot(bb_gp[0], ca_re, precision=hi, preferred_element_type=F32)
         - jnp.dot(bb_gp[1], ca_im, precision=hi, preferred_element_type=F32))
    row_g = lax.broadcasted_iota(jnp.int32, z.shape, 0) // ch
    col_g = (lax.broadcasted_iota(jnp.int32, z.shape, 1) % lw) // ch
    z = jnp.where(row_g == col_g, z, 0.0).astype(BF16)
    for t in range(tc):
        parts = [z[:, :(tc - t) * lw]]
        if t:
            parts.insert(0, jnp.zeros((lw, t * lw), BF16))
        wm_ref[0, t * lw:(t + 1) * lw, :] = jnp.concatenate(parts, axis=1)

    rg = lax.broadcasted_iota(jnp.int32, (gpb * ns, lw), 0) // ns
    cg = lax.broadcasted_iota(jnp.int32, (gpb * ns, lw), 1) // ch
    keep = rg == cg
    for t2 in range(tc):
        cr, ci = ca[t2 + 1]
        cr = jnp.where(keep, jnp.concatenate([cr] * gpb, axis=0), 0.0)
        ci = jnp.where(keep, jnp.concatenate([ci] * gpb, axis=0), 0.0)
        wo_ref[0, :gpb * ns, t2 * lw:(t2 + 1) * lw] = cr.astype(BF16)
        wo_ref[0, gpb * ns:, t2 * lw:(t2 + 1) * lw] = (-ci).astype(BF16)

    a_row, coef_row = _s5_discretise(lam_row_ref[0], lam_row_ref[1], ldt_row_ref[...])
    bb_tl = _cmul(*coef_row, b_tl_ref[0], b_tl_ref[1])
    rg = lax.broadcasted_iota(jnp.int32, bb_tl[0].shape, 0) // ch
    cg = lax.broadcasted_iota(jnp.int32, bb_tl[0].shape, 1) // ns
    bb_tl = (jnp.where(rg == cg, bb_tl[0], 0.0), jnp.where(rg == cg, bb_tl[1], 0.0))
    pw = (jnp.ones_like(a_row[0]), jnp.zeros_like(a_row[0]))
    for t in range(tc - 1, -1, -1):
        r_re, r_im = _cmul(*pw, *bb_tl)
        wr_ref[0, t * lw:(t + 1) * lw, :] = jnp.concatenate([r_re, r_im], axis=1).astype(BF16)
        pw = _cmul(*pw, *a_row)
    a_ref[0, 0] = pw[0]
    a_ref[0, 1] = pw[1]


def _s5_weights(lam_re, lam_im, log_dt, b_re, b_im, c_re, c_im):
    g, ns = lam_re.shape
    ch, tc, gpb = S5_GROUP, S5_TC, S5_GPB
    nblk = g // gpb
    lw = gpb * ch
    lam = jnp.stack([lam_re, lam_im]).astype(F32)
    ldt = log_dt.astype(F32)
    bst = jnp.stack([b_re, b_im]).astype(F32)
    cst = jnp.stack([c_re, c_im]).astype(F32)
    lam_pg = jnp.repeat(lam.transpose(0, 2, 1), ch, axis=2)
    ldt_pg = jnp.repeat(ldt, ch)[None]
    c_pg = cst.transpose(0, 3, 1, 2).reshape(2, ns, g * ch)
    lam_gp = jnp.repeat(lam, ch, axis=1)
    ldt_gp = jnp.broadcast_to(jnp.repeat(ldt, ch)[:, None], (g * ch, ns))
    b_gp = bst.transpose(0, 1, 3, 2).reshape(2, g * ch, ns)
    lam_row = lam.reshape(2, 1, g * ns)
    ldt_row = jnp.repeat(ldt, ns)[None]
    b_tl = jnp.tile(b_gp, (1, 1, gpb))
    blk3 = lambda i: (0, 0, i)
    return pl.pallas_call(
        _s5_weights_kernel,
        name="s5_weights",
        grid=(nblk,),
        in_specs=[pl.BlockSpec((2, ns, lw), blk3),
                  pl.BlockSpec((1, lw), lambda i: (0, i)),
                  pl.BlockSpec((2, ns, lw), blk3),
                  pl.BlockSpec((2, lw, ns), lambda i: (0, i, 0)),
                  pl.BlockSpec((lw, ns), lambda i: (i, 0)),
                  pl.BlockSpec((2, lw, ns), lambda i: (0, i, 0)),
                  pl.BlockSpec((2, 1, gpb * ns), blk3),
                  pl.BlockSpec((1, gpb * ns), lambda i: (0, i)),
                  pl.BlockSpec((2, lw, gpb * ns), lambda i: (0, i, 0))],
        out_specs=[pl.BlockSpec((1, tc * lw, tc * lw), lambda i: (i, 0, 0)),
                   pl.BlockSpec((1, tc * lw, 2 * gpb * ns), lambda i: (i, 0, 0)),
                   pl.BlockSpec((1, 2 * gpb * ns, tc * lw), lambda i: (i, 0, 0)),
                   pl.BlockSpec((1, 2, 1, gpb * ns), lambda i: (i, 0, 0, 0))],
        out_shape=[jax.ShapeDtypeStruct((nblk, tc * lw, tc * lw), BF16),
                   jax.ShapeDtypeStruct((nblk, tc * lw, 2 * gpb * ns), BF16),
                   jax.ShapeDtypeStruct((nblk, 2 * gpb * ns, tc * lw), BF16),
                   jax.ShapeDtypeStruct((nblk, 2, 1, gpb * ns), F32)],
        compiler_params=_cparams(("arbitrary",)),
    )(lam_pg, ldt_pg, c_pg, lam_gp, ldt_gp, b_gp, lam_row, ldt_row, b_tl)


def _s5_mix_kernel(u_ref, wm_ref, wr_ref, wo_ref, a_ref, d_ref, y_ref, s_ref, x_ref, ys_ref):
    nk = u_ref.shape[1]
    half = s_ref.shape[1] // 2
    sub = 8
    u = u_ref[0]
    s_ref[...] = jnp.dot(u, wr_ref[0], preferred_element_type=F32)

    a1 = (a_ref[0, 0], a_ref[0, 1])
    a2 = _cmul(*a1, *a1)
    a4 = _cmul(*a2, *a2)
    a8 = _cmul(*a4, *a4)
    a3 = _cmul(*a2, *a1)
    pows = [(jnp.ones_like(a1[0]), jnp.zeros_like(a1[0])), a1, a2, a3, a4,
            _cmul(*a4, *a1), _cmul(*a4, *a2), _cmul(*a4, *a3)]
    p_re = jnp.concatenate([p[0] for p in pows], axis=0)
    p_im = jnp.concatenate([p[1] for p in pows], axis=0)
    rowid = lax.broadcasted_iota(jnp.int32, (sub, half), 0)
    levels = [(d, jnp.broadcast_to(a[0], (sub, half)), jnp.broadcast_to(a[1], (sub, half)))
              for d, a in ((1, a1), (2, a2), (4, a4))]

    def shift_rows(v, d):
        return jnp.where(rowid >= d, pltpu.roll(v, d, 0), 0.0)

    def tile(j, carry):
        cr, ci = carry
        rows = pl.ds(pl.multiple_of(j * sub, sub), sub)
        zr = s_ref[rows, :half]
        zi = s_ref[rows, half:]
        for d, br, bi in levels:
            sr, si = shift_rows(zr, d), shift_rows(zi, d)
            zr, zi = zr + br * sr - bi * si, zi + br * si + bi * sr
        x_ref[rows, :half] = p_re * cr - p_im * ci + shift_rows(zr, 1)
        x_ref[rows, half:] = p_re * ci + p_im * cr + shift_rows(zi, 1)
        return (a8[0] * cr - a8[1] * ci + zr[sub - 1:], a8[0] * ci + a8[1] * cr + zi[sub - 1:])

    zero = jnp.zeros((1, half), F32)
    lax.fori_loop(0, nk // sub, tile, (zero, zero))

    y = jnp.dot(u, wm_ref[0], preferred_element_type=F32)
    y = y + jnp.dot(x_ref[...].astype(BF16), wo_ref[0], preferred_element_type=F32)
    y = _gelu_tanh(y + d_ref[0] * u.astype(F32))
    for t in range(S5_TC):
        ys_ref[pl.ds(t, nk, stride=S5_TC), :] = y[:, t * LANES:(t + 1) * LANES]
    y_ref[0] = ys_ref[...].astype(y_ref.dtype)


def _s5_mixer(u3, nb, lam_re, lam_im, log_dt, b_re, b_im, c_re, c_im, d_skip):
    nblk, rows, w = u3.shape
    tc = S5_TC
    nk = rows // nb
    assert nk % 8 == 0
    wm, wr, wo, a_pow = _s5_weights(lam_re, lam_im, log_dt, b_re, b_im, c_re, c_im)
    ns = wr.shape[-1]
    d_rows = jnp.tile(d_skip.astype(F32).reshape(nblk, 1, LANES), (1, 1, tc))
    blk = lambda i, b: (i, 0, 0)
    return pl.pallas_call(
        _s5_mix_kernel,
        name="s5_mix",
        grid=(nblk, nb),
        in_specs=[pl.BlockSpec((1, nk, w), lambda i, b: (i, b, 0)),
                  pl.BlockSpec((1, w, w), blk),
                  pl.BlockSpec((1, w, ns), blk),
                  pl.BlockSpec((1, ns, w), blk),
                  pl.BlockSpec((1, 2, 1, ns // 2), lambda i, b: (i, 0, 0, 0)),
                  pl.BlockSpec((1, 1, w), blk)],
        out_specs=pl.BlockSpec((1, nk * tc, LANES), lambda i, b: (i, b, 0)),
        out_shape=jax.ShapeDtypeStruct((nblk, rows * tc, LANES), BF16),
        scratch_shapes=[pltpu.VMEM((nk, ns), F32), pltpu.VMEM((nk, ns), F32),
                        pltpu.VMEM((nk * tc, LANES), F32)],
        compiler_params=_cparams(("arbitrary", "arbitrary")),
    )(u3, wm, wr, wo, a_pow, d_rows)


def _even_out_kernel(x_ref, oa_ref, y_ref, zb_ref, wg_ref, bg_ref, wo_ref, g_ref,
                     x1_ref, hn_ref):
    y = jnp.concatenate([y_ref[i] for i in range(y_ref.shape[0])], axis=1)
    t = jnp.dot(y, wg_ref[...], preferred_element_type=F32) + bg_ref[...]
    ob = y.astype(F32) * jax.nn.sigmoid(t) * _silu(zb_ref[...].astype(F32))
    wa = oa_ref.shape[1]
    acc = jnp.dot(oa_ref[...], wo_ref[:wa, :], preferred_element_type=F32)
    acc = acc + jnp.dot(ob.astype(BF16), wo_ref[wa:, :], preferred_element_type=F32)
    x1 = x_ref[...] + acc
    x1_ref[...] = x1
    hn_ref[...] = _rms(x1, g_ref[...]).astype(BF16)


def _even_out(x2, oa, y, proj, w_glu, b_glu, w_out, g_next):
    m, d = x2.shape
    wa = oa.shape[1]
    nblk, _, lw = y.shape
    wb = nblk * lw
    tm = min(TM_OUT, m)
    zb_col = (proj.shape[1] - wb) // wb
    const = lambda i: (0, 0)
    return pl.pallas_call(
        _even_out_kernel,
        name="even_out",
        grid=(m // tm,),
        in_specs=[pl.BlockSpec((tm, d), lambda i: (i, 0)),
                  pl.BlockSpec((tm, wa), lambda i: (i, 0)),
                  pl.BlockSpec((nblk, tm, lw), lambda i: (0, i, 0)),
                  pl.BlockSpec((tm, wb), lambda i: (i, zb_col)),
                  pl.BlockSpec((wb, wb), const),
                  pl.BlockSpec((1, wb), const),
                  pl.BlockSpec((wa + wb, d), const),
                  pl.BlockSpec((1, d), const)],
        out_specs=[pl.BlockSpec((tm, d), lambda i: (i, 0)),
                   pl.BlockSpec((tm, d), lambda i: (i, 0))],
        out_shape=[jax.ShapeDtypeStruct((m, d), F32),
                   jax.ShapeDtypeStruct((m, d), BF16)],
        compiler_params=_cparams(("arbitrary",)),
    )(x2, oa, y, proj, w_glu, b_glu.reshape(1, wb), w_out, g_next.reshape(1, d))


def _cum_forget_kernel(f_ref, b_ref, c_ref):
    x = f_ref[...] + b_ref[...]
    v = -(jnp.maximum(-x, 0.0) + jnp.log1p(jnp.exp(-jnp.abs(x))))
    n = v.shape[-1]
    pos = lax.broadcasted_iota(jnp.int32, v.shape, 1)
    d = 1
    while d < n:
        v = v + jnp.where(pos >= d, pltpu.roll(v, d, 1), 0.0)
        d *= 2
    c_ref[...] = v


def _cum_forget(f_t, b_forget, seq):
    nh, m = f_t.shape
    return pl.pallas_call(
        _cum_forget_kernel,
        name="cum_forget",
        grid=(m // seq,),
        in_specs=[pl.BlockSpec((nh, seq), lambda i: (0, i)),
                  pl.BlockSpec((nh, 1), lambda i: (0, 0))],
        out_specs=pl.BlockSpec((nh, seq), lambda i: (0, i)),
        out_shape=jax.ShapeDtypeStruct((nh, m), F32),
        compiler_params=_cparams(("arbitrary",)),
    )(f_t, b_forget.astype(F32).reshape(nh, 1))


def _split3(x):
    hi = x.astype(BF16).astype(F32)
    r = x - hi
    mid = r.astype(BF16).astype(F32)
    lo = (r - mid).astype(BF16).astype(F32)
    return hi, mid, lo


def _fox_kernel(q0_ref, qn_ref, k_ref, v_ref, z_ref, c_ref, o_ref,
                ke_ref, vt_ref, qa_ref, s_ref, mb_ref, acc_ref):
    qi = pl.program_id(2)
    nq = pl.num_programs(2)
    tq = qn_ref.shape[1]
    tk = tq
    dh = C_DH
    nh = c_ref.shape[1]
    nblk = k_ref.shape[1] // tk
    nt = (((1,), (1,)), ((), ()))
    key = lax.broadcasted_iota(jnp.int32, (tk, tq), 0)
    qry = lax.broadcasted_iota(jnp.int32, (tk, tq), 1)

    def cols(h):
        return slice(h * dh, (h + 1) * dh)

    def prep_queries(q_ref, qb):
        qs = pl.multiple_of(qb * tq, tq)
        lane = lax.broadcasted_iota(jnp.int32, (tq, dh), 1)
        for h in range(nh):
            c_base = jnp.max(c_ref[0, h, :, pl.ds(qs, tq)], axis=1, keepdims=True) * LOG2E
            bh, bm, bl = _split3(c_base)
            g = jnp.where(lane < 3, 1.0, jnp.where(lane == 3, bh, jnp.where(lane == 4, bm,
                          jnp.where(lane == 5, bl, 0.0))))
            qa_ref[h, :, :dh] = (q_ref[0, :, cols(h)].astype(F32) * (LOG2E / math.sqrt(dh))
                                 ).astype(BF16)
            qa_ref[h, :, dh:] = g.astype(BF16)

    def produce(h, j, keep):
        ks = pl.multiple_of(j * tk, tk)
        ka = jnp.concatenate([k_ref[0, pl.ds(ks, tk), cols(h)], ke_ref[h, pl.ds(ks, tk), :]],
                             axis=1)
        s = lax.dot_general(ka, qa_ref[h], nt, preferred_element_type=F32)
        if keep is not None:
            s = jnp.where(keep, s, NEG)
        s_ref[h] = s
        return jnp.max(s, axis=0, keepdims=True)

    def consume(h, j, stats):
        m, mb = stats
        ks = pl.multiple_of(j * tk, tk)
        m2 = jnp.maximum(m, mb)
        p = jnp.exp2(s_ref[h] - m2).astype(BF16)
        pv = jnp.dot(vt_ref[h, :, pl.ds(ks, tk)], p, preferred_element_type=F32)
        acc_ref[h] = jnp.exp2(m - m2) * acc_ref[h] + pv
        return m2

    def step(j, carry, diagonal_next):
        keep = (key <= qry) if diagonal_next else None
        return tuple((consume(h, j, stats), produce(h, j + 1, keep))
                     for h, stats in enumerate(carry))

    @pl.when(qi == 0)
    def _():
        r = lax.broadcasted_iota(jnp.int32, (dh, dh), 0)
        c = lax.broadcasted_iota(jnp.int32, (dh, dh), 1)
        eye = (r == c).astype(BF16)
        row = lax.broadcasted_iota(jnp.int32, (dh, tk), 0)
        ones_row = (lax.broadcasted_iota(jnp.int32, (FOX_PAD, tk), 0) == 0).astype(BF16)

        def blk(j, carry):
            ks = pl.multiple_of(j * tk, tk)
            for h in range(nh):
                hi, mid, lo = _split3(c_ref[0, h, :, pl.ds(ks, tk)] * (-LOG2E))
                a = jnp.where(row == 0, hi, jnp.where(row == 1, mid, jnp.where(row == 2, lo,
                              jnp.where(row < 6, 1.0, 0.0))))
                ke_ref[h, pl.ds(ks, tk), :] = a.T.astype(BF16)
                vt = lax.dot_general(eye, v_ref[0, pl.ds(ks, tk), cols(h)], nt,
                                     preferred_element_type=F32)
                vt_ref[h, :dh, pl.ds(ks, tk)] = vt.astype(BF16)
                vt_ref[h, dh:, pl.ds(ks, tk)] = ones_row
            return carry

        lax.fori_loop(0, nblk, blk, 0)
        prep_queries(q0_ref, 0)
        for h in range(nh):
            mb_ref[h] = produce(h, 0, key <= qry)

    carry = []
    for h in range(nh):
        acc_ref[h] = jnp.zeros(acc_ref.shape[1:], F32)
        carry.append((jnp.full((1, tq), NEG, F32), mb_ref[h]))
    plain = functools.partial(step, diagonal_next=False)
    n_plain = jnp.maximum(qi - 1, 0)
    n_pairs = n_plain // 2
    carry = lax.fori_loop(0, n_pairs, lambda i, c: plain(2 * i + 1, plain(2 * i, c)),
                          tuple(carry))
    carry = lax.fori_loop(2 * n_pairs, n_plain, plain, carry)
    carry = lax.fori_loop(n_plain, qi, functools.partial(step, diagonal_next=True), carry)
    for h, stats in enumerate(carry):
        consume(h, qi, stats)
    prep_queries(qn_ref, jnp.minimum(qi + 1, nq - 1))
    for h in range(nh):
        mb_ref[h] = produce(h, 0, None)
    for h in range(nh):
        acc = acc_ref[h]
        o = (acc[:dh] / acc[dh:dh + 1]).T
        o_ref[0, :, cols(h)] = (o * _silu(z_ref[0, :, cols(h)].astype(F32))).astype(o_ref.dtype)


def _fox(proj3, cum4):
    b, l, _ = proj3.shape
    ng = C_HEADS // FOX_HEADS
    w = FOX_HEADS * C_DH
    tq = min(FOX_TQ, l)
    nq = l // tq
    return pl.pallas_call(
        _fox_kernel,
        name="fox",
        grid=(b, ng, nq),
        in_specs=[pl.BlockSpec((1, tq, w), lambda bi, gi, qi: (bi, 0, gi)),
                  pl.BlockSpec((1, tq, w), lambda bi, gi, qi: (bi, jnp.minimum(qi + 1, nq - 1), gi)),
                  pl.BlockSpec((1, l, w), lambda bi, gi, qi: (bi, 0, ng + gi)),
                  pl.BlockSpec((1, l, w), lambda bi, gi, qi: (bi, 0, 2 * ng + gi)),
                  pl.BlockSpec((1, tq, w), lambda bi, gi, qi: (bi, qi, 3 * ng + gi)),
                  pl.BlockSpec((1, FOX_HEADS, 1, l), lambda bi, gi, qi: (bi, gi, 0, 0))],
        out_specs=pl.BlockSpec((1, tq, w), lambda bi, gi, qi: (bi, qi, gi)),
        out_shape=jax.ShapeDtypeStruct((b, l, C_HEADS * C_DH), BF16),
        scratch_shapes=[pltpu.VMEM((FOX_HEADS, l, C_DH), BF16),
                        pltpu.VMEM((FOX_HEADS, C_DH + FOX_PAD, l), BF16),
                        pltpu.VMEM((FOX_HEADS, tq, 2 * C_DH), BF16),
                        pltpu.VMEM((FOX_HEADS, tq, tq), F32),
                        pltpu.VMEM((FOX_HEADS, 1, tq), F32),
                        pltpu.VMEM((FOX_HEADS, C_DH + FOX_PAD, tq), F32)],
        compiler_params=_cparams(("arbitrary", "arbitrary", "arbitrary")),
    )(proj3, proj3, proj3, proj3, proj3, cum4)


def _odd_out_kernel(x_ref, o_ref, w_ref, g_ref, out_ref):
    x2 = x_ref[...] + jnp.dot(o_ref[...], w_ref[...], preferred_element_type=F32)
    out_ref[...] = _rms(x2, g_ref[...])


def _odd_out(x1, o, w_out, g):
    m, d = x1.shape
    k = o.shape[1]
    tm = min(TM_OUT, m)
    return pl.pallas_call(
        _odd_out_kernel,
        name="odd_out",
        grid=(m // tm,),
        in_specs=[pl.BlockSpec((tm, d), lambda i: (i, 0)),
                  pl.BlockSpec((tm, k), lambda i: (i, 0)),
                  pl.BlockSpec((k, d), lambda i: (0, 0)),
                  pl.BlockSpec((1, d), lambda i: (0, 0))],
        out_specs=pl.BlockSpec((tm, d), lambda i: (i, 0)),
        out_shape=jax.ShapeDtypeStruct((m, d), F32),
        compiler_params=_cparams(("arbitrary",)),
    )(x1, o, w_out, g.reshape(1, d))


def kernel(x, norm_even_g, w_in_even, rel_bias, s5_lambda_re, s5_lambda_im, s5_log_dt,
           s5_b_re, s5_b_im, s5_c_re, s5_c_im, s5_d, w_glu, b_glu, w_out_even,
           norm_odd_g, w_in_odd, b_forget, w_out_odd, final_norm_g):
    b, l, d = x.shape
    assert norm_even_g.shape[0] == 1 and norm_odd_g.shape[0] == 1, "one even + one odd layer"
    assert l % FOX_TQ == 0 and l % VT_BLK == 0 and l % A_QB == 0 and l >= A_KB
    m = b * l
    a_w = A_HEADS * A_DH
    s5_w = s5_d.shape[1] * S5_GROUP
    c_w = C_HEADS * C_DH

    x2 = x.reshape(m, d)
    assert s5_w == TN_PROJ and l % S5_TC == 0
    proj, u3 = _norm_proj(x2, norm_even_g[0], w_in_even[0].astype(BF16), 3 * a_w)
    proj3 = proj.reshape(b, l, proj.shape[1])
    o_a = _attn_a(proj3, rel_bias[0])
    y = _s5_mixer(u3, b, s5_lambda_re[0], s5_lambda_im[0], s5_log_dt[0], s5_b_re[0],
                  s5_b_im[0], s5_c_re[0], s5_c_im[0], s5_d[0])
    x1, hn1 = _even_out(x2, o_a.reshape(m, a_w), y, proj,
                        w_glu[0].astype(BF16), b_glu[0].astype(F32),
                        w_out_even[0].astype(BF16), norm_odd_g[0])
    w_odd = w_in_odd[0]
    proj1, f_t = _proj_forget(hn1, w_odd[:, :4 * c_w].astype(BF16),
                              w_odd[:, 4 * c_w:].T.astype(BF16))
    cum = _cum_forget(f_t, b_forget[0], l)
    cum4 = cum.reshape(C_HEADS, b, 1, l).transpose(1, 0, 2, 3)
    o_c = _fox(proj1.reshape(b, l, 4 * c_w), cum4)
    out = _odd_out(x1, o_c.reshape(m, c_w), w_out_odd[0].astype(BF16), final_norm_g)
    return out.reshape(b, l, d)
```

```python
import functools
import math

import jax
import jax.numpy as jnp
from jax import lax
from jax.experimental import pallas as pl
from jax.experimental.pallas import tpu as pltpu

F32 = jnp.float32
BF16 = jnp.bfloat16

EPS = 1e-6
NEG = -1e30
LOG2E = math.log2(math.e)

CHUNK = 64
N_LEFT = 8
A_HEADS = 16
A_DH = 64
MAX_REL = 128
S5_GROUP = 16
S5_STATE = 64
C_HEADS = 16
C_DH = 128

LANES = 128
VMEM_LIMIT = 56 * 1024 * 1024

TM_PROJ = 1024
TN_PROJ = 1024
TM_OUT = 512
A_QB = 2 * CHUNK
A_KB = A_QB + N_LEFT * CHUNK
A_TW = A_KB + N_LEFT * CHUNK
A_PAD = 16
VT_BLK = 512
S5_TC = 16
S5_GPB = LANES // S5_GROUP
FOX_TQ = 512
FOX_HEADS = 4
FOX_PAD = 16


def _cparams(sem):
    return pltpu.CompilerParams(dimension_semantics=sem, vmem_limit_bytes=VMEM_LIMIT)


def _rms(x, g):
    ms = jnp.mean(x * x, axis=-1, keepdims=True)
    return x * lax.rsqrt(ms + EPS) * g


def _silu(x):
    return x * jax.nn.sigmoid(x)


def _gelu_tanh(x):
    c = math.sqrt(2.0 / math.pi)
    return 0.5 * x * (1.0 + jnp.tanh(c * (x + 0.044715 * (x * x * x))))


def _norm_proj_kernel(x_ref, g_ref, w_ref, o_ref, u_ref, hn_ref, us_ref, *, u_tile):
    j = pl.program_id(1)

    @pl.when(j == 0)
    def _():
        hn_ref[...] = _rms(x_ref[...], g_ref[...]).astype(BF16)

    acc = jnp.dot(hn_ref[...], w_ref[...], preferred_element_type=F32)
    o_ref[...] = acc.astype(o_ref.dtype)

    @pl.when(j == u_tile)
    def _():
        nchunk = us_ref.shape[1] // S5_TC
        for blk in range(u_ref.shape[0]):
            us_ref[blk] = acc[:, blk * LANES:(blk + 1) * LANES]
            for t in range(S5_TC):
                rows = us_ref[blk, pl.ds(t, nchunk, stride=S5_TC), :]
                u_ref[blk, :, t * LANES:(t + 1) * LANES] = rows.astype(u_ref.dtype)


def _norm_proj(x2, g, w, u_start):
    m, d = x2.shape
    n = w.shape[1]
    tm, tn = min(TM_PROJ, m), TN_PROJ
    assert u_start % tn == 0 and tm % (16 * S5_TC) == 0
    nblk = tn // LANES
    return pl.pallas_call(
        functools.partial(_norm_proj_kernel, u_tile=u_start // tn),
        name="norm_proj",
        grid=(m // tm, n // tn),
        in_specs=[pl.BlockSpec((tm, d), lambda i, j: (i, 0)),
                  pl.BlockSpec((1, d), lambda i, j: (0, 0)),
                  pl.BlockSpec((d, tn), lambda i, j: (0, j))],
        out_specs=[pl.BlockSpec((tm, tn), lambda i, j: (i, j)),
                   pl.BlockSpec((nblk, tm // S5_TC, S5_TC * LANES), lambda i, j: (0, i, 0))],
        out_shape=[jax.ShapeDtypeStruct((m, n), BF16),
                   jax.ShapeDtypeStruct((nblk, m // S5_TC, S5_TC * LANES), BF16)],
        scratch_shapes=[pltpu.VMEM((tm, d), BF16), pltpu.VMEM((nblk, tm, LANES), F32)],
        compiler_params=_cparams(("arbitrary", "arbitrary")),
    )(x2, g.reshape(1, d), w)


def _proj_odd_kernel(h_ref, w_ref, o_ref, wb_ref):
    @pl.when(pl.program_id(1) == 0)
    def _():
        wb_ref[...] = w_ref[0].astype(BF16)

    o_ref[...] = jnp.dot(h_ref[...], wb_ref[...],
                         preferred_element_type=F32).astype(o_ref.dtype)


def _proj_odd(hn, w3, n):
    m, d = hn.shape
    tm, tn = min(TM_PROJ, m), TN_PROJ
    assert n % tn == 0
    return pl.pallas_call(
        _proj_odd_kernel,
        name="proj_odd",
        grid=(n // tn, m // tm),
        in_specs=[pl.BlockSpec((tm, d), lambda j, i: (i, 0)),
                  pl.BlockSpec((1, d, tn), lambda j, i: (0, 0, j))],
        out_specs=pl.BlockSpec((tm, tn), lambda j, i: (i, j)),
        out_shape=jax.ShapeDtypeStruct((m, n), BF16),
        scratch_shapes=[pltpu.VMEM((d, tn), BF16)],
        compiler_params=_cparams(("arbitrary", "arbitrary")),
    )(hn, w3)


def _attn_a_kernel(q_ref, k_ref, v_ref, z_ref, e_ref, o_ref, tt_ref, vt_ref, s_ref, pv_ref):
    l_all = k_ref.shape[1]
    w = q_ref.shape[2]
    nt = (((1,), (1,)), ((), ()))
    band = N_LEFT * CHUNK

    @pl.when(pl.program_id(1) == 0)
    def _():
        i = lax.broadcasted_iota(jnp.int32, (A_QB, A_TW), 0)
        m = lax.broadcasted_iota(jnp.int32, (A_QB, A_TW), 1)
        dchunk = m // CHUNK - i // CHUNK
        ok = (dchunk >= 0) & (dchunk <= N_LEFT)
        for h in range(2):
            rows = jnp.broadcast_to(e_ref[0, h:h + 1, :], (A_QB, A_QB + A_TW))
            toep = pltpu.roll(rows, 0, 1, stride=1, stride_axis=0)[:, A_QB:]
            t = jnp.where(ok, toep * LOG2E, NEG)
            tt_ref[:, h * A_QB:(h + 1) * A_QB] = t.T

    r = lax.broadcasted_iota(jnp.int32, (w, w), 0)
    c = lax.broadcasted_iota(jnp.int32, (w, w), 1)
    eye = (r == c).astype(BF16)
    ones_row = (lax.broadcasted_iota(jnp.int32, (A_PAD, VT_BLK), 0) == 0).astype(BF16)

    def vt_blk(j, carry):
        ks = pl.multiple_of(j * VT_BLK, VT_BLK)
        vt = lax.dot_general(eye, v_ref[0, pl.ds(ks, VT_BLK), :], nt, preferred_element_type=F32)
        for h in range(2):
            vt_ref[h, :A_DH, pl.ds(ks, VT_BLK)] = vt[h * A_DH:(h + 1) * A_DH].astype(BF16)
            vt_ref[h, A_DH:, pl.ds(ks, VT_BLK)] = ones_row
        return carry

    lax.fori_loop(0, l_all // VT_BLK, vt_blk, 0)

    lane = lax.broadcasted_iota(jnp.int32, (A_QB, w), 1)
    qscale = LOG2E / math.sqrt(A_DH)

    def window(qb):
        start = qb * A_QB - band
        start_c = jnp.maximum(start, 0)
        return pl.multiple_of(start_c, LANES), pl.multiple_of(start_c - start, LANES)

    def produce(qb, slot):
        qs = pl.multiple_of(qb * A_QB, A_QB)
        start_c, off = window(qb)
        q = (q_ref[0, pl.ds(qs, A_QB), :].astype(F32) * qscale).astype(BF16)
        zero = jnp.zeros_like(q)
        qbd = jnp.concatenate([jnp.where(lane < A_DH, q, zero), jnp.where(lane >= A_DH, q, zero)],
                              axis=0)
        kb = k_ref[0, pl.ds(start_c, A_KB), :]
        s = lax.dot_general(kb, qbd, nt, preferred_element_type=F32)
        s = s + tt_ref[pl.ds(off, A_KB), :]
        s_ref[slot] = s
        return jnp.max(s, axis=0, keepdims=True)

    def consume(qb, slot, m):
        start_c, _ = window(qb)
        pb = jnp.exp2(s_ref[slot] - m).astype(BF16)
        for h in range(2):
            vt = vt_ref[h, :, pl.ds(start_c, A_KB)]
            pv_ref[slot, h] = jnp.dot(vt, pb[:, h * A_QB:(h + 1) * A_QB],
                                      preferred_element_type=F32)

    def finish(qb, slot):
        qs = pl.multiple_of(qb * A_QB, A_QB)
        outs = []
        for h in range(2):
            oh = pv_ref[slot, h]
            outs.append(oh[:A_DH] * (1.0 / oh[A_DH:A_DH + 1]))
        o = jnp.concatenate(outs, axis=0).T
        gate = _silu(z_ref[0, pl.ds(qs, A_QB), :].astype(F32))
        o_ref[0, pl.ds(qs, A_QB), :] = (o * gate).astype(o_ref.dtype)

    def pair(i, m, lead=True):
        j = 2 * i
        if lead:
            finish(j - 1, 1)
        m1 = produce(j + 1, 1)
        consume(j, 0, m)
        m2 = produce(j + 2, 0)
        finish(j, 0)
        consume(j + 1, 1, m1)
        return m2

    nb = l_all // A_QB
    npairs = (nb - 1) // 2
    m = pair(0, produce(0, 0), lead=False)
    m = lax.fori_loop(1, npairs, pair, m)
    last = 2 * npairs
    finish(last - 1, 1)
    if (nb - 1) % 2:
        m1 = produce(last + 1, 1)
        consume(last, 0, m)
        finish(last, 0)
        consume(last + 1, 1, m1)
        finish(last + 1, 1)
    else:
        consume(last, 0, m)
        finish(last, 0)


def _attn_a(proj3, rel_bias):
    b, l, _ = proj3.shape
    hp = A_HEADS // 2
    width = 2 * A_DH
    ncol = A_HEADS * A_DH // width
    zcol = (3 * A_HEADS * A_DH + (A_HEADS * A_DH)) // width
    left = N_LEFT * CHUNK + A_QB - MAX_REL
    right = A_QB + A_TW - left - (2 * MAX_REL + 1)
    ext = jnp.pad(rel_bias.astype(F32)[:, ::-1], ((0, 0), (left, right)), mode="edge")
    ext = ext.reshape(hp, 2, A_QB + A_TW)
    seq = lambda h, bi: (bi, 0, h)
    return pl.pallas_call(
        _attn_a_kernel,
        name="attn_a",
        grid=(hp, b),
        in_specs=[pl.BlockSpec((1, l, width), seq),
                  pl.BlockSpec((1, l, width), lambda h, bi: (bi, 0, ncol + h)),
                  pl.BlockSpec((1, l, width), lambda h, bi: (bi, 0, 2 * ncol + h)),
                  pl.BlockSpec((1, l, width), lambda h, bi: (bi, 0, zcol + h)),
                  pl.BlockSpec((1, 2, A_QB + A_TW), lambda h, bi: (h, 0, 0))],
        out_specs=pl.BlockSpec((1, l, width), seq),
        out_shape=jax.ShapeDtypeStruct((b, l, A_HEADS * A_DH), BF16),
        scratch_shapes=[pltpu.VMEM((A_TW, 2 * A_QB), F32),
                        pltpu.VMEM((2, A_DH + A_PAD, l), BF16),
                        pltpu.VMEM((2, A_KB, 2 * A_QB), F32),
                        pltpu.VMEM((2, 2, A_DH + A_PAD, A_QB), F32)],
        compiler_params=_cparams(("arbitrary", "arbitrary")),
    )(proj3, proj3, proj3, proj3, ext)


def _cmul(ar, ai, br, bi):
    return ar * br - ai * bi, ar * bi + ai * br


def _s5_discretise(lr, li, log_dt):
    dt = jnp.exp(log_dt)
    mag = jnp.exp(lr * dt)
    ang = li * dt
    a_re = mag * jnp.cos(ang)
    a_im = mag * jnp.sin(ang)
    den = lr * lr + li * li
    nr = a_re - 1.0
    coef_re = (nr * lr + a_im * li) / den
    coef_im = (a_im * lr - nr * li) / den
    return (a_re, a_im), (coef_re, coef_im)


def _s5_weights_kernel(lam_pg_ref, ldt_pg_ref, c_pg_ref, lam_gp_ref, ldt_gp_ref, b_gp_ref,
                       lam_row_ref, ldt_row_ref, b_tl_ref, wm_ref, wr_ref, wo_ref, a_ref):
    tc, ch, ns, gpb = S5_TC, S5_GROUP, S5_STATE, S5_GPB
    lw = gpb * ch
    hi = lax.Precision.HIGHEST

    a_pg, _ = _s5_discretise(lam_pg_ref[0], lam_pg_ref[1], ldt_pg_ref[...])
    pw = (jnp.ones_like(a_pg[0]), jnp.zeros_like(a_pg[0]))
    ca = []
    for n in range(tc + 1):
        ca.append(_cmul(c_pg_ref[0], c_pg_ref[1], *pw))
        pw = _cmul(*pw, *a_pg)

    _, coef_gp = _s5_discretise(lam_gp_ref[0], lam_gp_ref[1], ldt_gp_ref[...])
    bb_gp = _cmul(*coef_gp, b_gp_ref[0], b_gp_ref[1])
    ca_re = jnp.concatenate([ca[n][0] for n in range(tc)], axis=1)
    ca_im = jnp.concatenate([ca[n][1] for n in range(tc)], axis=1)
    z = (jnp.dot(bb_gp[0], ca_re, precision=hi, preferred_element_type=F32)
         - jnp.dot(bb_gp[1], ca_im, precision=hi, preferred_element_type=F32))
    row_g = lax.broadcasted_iota(jnp.int32, z.shape, 0) // ch
    col_g = (lax.broadcasted_iota(jnp.int32, z.shape, 1) % lw) // ch
    z = jnp.where(row_g == col_g, z, 0.0).astype(BF16)
    for t in range(tc):
        parts = [z[:, :(tc - t) * lw]]
        if t:
            parts.insert(0, jnp.zeros((lw, t * lw), BF16))
        wm_ref[0, t * lw:(t + 1) * lw, :] = jnp.concatenate(parts, axis=1)

    rg = lax.broadcasted_iota(jnp.int32, (gpb * ns, lw), 0) // ns
    cg = lax.broadcasted_iota(jnp.int32, (gpb * ns, lw), 1) // ch
    keep = rg == cg
    for t2 in range(tc):
        cr, ci = ca[t2 + 1]
        cr = jnp.where(keep, jnp.concatenate([cr] * gpb, axis=0), 0.0)
        ci = jnp.where(keep, jnp.concatenate([ci] * gpb, axis=0), 0.0)
        wo_ref[0, :gpb * ns, t2 * lw:(t2 + 1) * lw] = cr.astype(BF16)
        wo_ref[0, gpb * ns:, t2 * lw:(t2 + 1) * lw] = (-ci).astype(BF16)

    a_row, coef_row = _s5_discretise(lam_row_ref[0], lam_row_ref[1], ldt_row_ref[...])
    bb_tl = _cmul(*coef_row, b_tl_ref[0], b_tl_ref[1])
    rg = lax.broadcasted_iota(jnp.int32, bb_tl[0].shape, 0) // ch
    cg = lax.broadcasted_iota(jnp.int32, bb_tl[0].shape, 1) // ns
    bb_tl = (jnp.where(rg == cg, bb_tl[0], 0.0), jnp.where(rg == cg, bb_tl[1], 0.0))
    pw = (jnp.ones_like(a_row[0]), jnp.zeros_like(a_row[0]))
    for t in range(tc - 1, -1, -1):
        r_re, r_im = _cmul(*pw, *bb_tl)
        wr_ref[0, t * lw:(t + 1) * lw, :] = jnp.concatenate([r_re, r_im], axis=1).astype(BF16)
        pw = _cmul(*pw, *a_row)
    a_ref[0, 0] = pw[0]
    a_ref[0, 1] = pw[1]


def _s5_weights(lam_re, lam_im, log_dt, b_re, b_im, c_re, c_im):
    g, ns = lam_re.shape
    ch, tc, gpb = S5_GROUP, S5_TC, S5_GPB
    nblk = g // gpb
    lw = gpb * ch
    lam = jnp.stack([lam_re, lam_im]).astype(F32)
    ldt = log_dt.astype(F32)
    bst = jnp.stack([b_re, b_im]).astype(F32)
    cst = jnp.stack([c_re, c_im]).astype(F32)
    lam_pg = jnp.repeat(lam.transpose(0, 2, 1), ch, axis=2)
    ldt_pg = jnp.repeat(ldt, ch)[None]
    c_pg = cst.transpose(0, 3, 1, 2).reshape(2, ns, g * ch)
    lam_gp = jnp.repeat(lam, ch, axis=1)
    ldt_gp = jnp.broadcast_to(jnp.repeat(ldt, ch)[:, None], (g * ch, ns))
    b_gp = bst.transpose(0, 1, 3, 2).reshape(2, g * ch, ns)
    lam_row = lam.reshape(2, 1, g * ns)
    ldt_row = jnp.repeat(ldt, ns)[None]
    b_tl = jnp.tile(b_gp, (1, 1, gpb))
    blk3 = lambda i: (0, 0, i)
    return pl.pallas_call(
        _s5_weights_kernel,
        name="s5_weights",
        grid=(nblk,),
        in_specs=[pl.BlockSpec((2, ns, lw), blk3),
                  pl.BlockSpec((1, lw), lambda i: (0, i)),
                  pl.BlockSpec((2, ns, lw), blk3),
                  pl.BlockSpec((2, lw, ns), lambda i: (0, i, 0)),
                  pl.BlockSpec((lw, ns), lambda i: (i, 0)),
                  pl.BlockSpec((2, lw, ns), lambda i: (0, i, 0)),
                  pl.BlockSpec((2, 1, gpb * ns), blk3),
                  pl.BlockSpec((1, gpb * ns), lambda i: (0, i)),
                  pl.BlockSpec((2, lw, gpb * ns), lambda i: (0, i, 0))],
        out_specs=[pl.BlockSpec((1, tc * lw, tc * lw), lambda i: (i, 0, 0)),
                   pl.BlockSpec((1, tc * lw, 2 * gpb * ns), lambda i: (i, 0, 0)),
                   pl.BlockSpec((1, 2 * gpb * ns, tc * lw), lambda i: (i, 0, 0)),
                   pl.BlockSpec((1, 2, 1, gpb * ns), lambda i: (i, 0, 0, 0))],
        out_shape=[jax.ShapeDtypeStruct((nblk, tc * lw, tc * lw), BF16),
                   jax.ShapeDtypeStruct((nblk, tc * lw, 2 * gpb * ns), BF16),
                   jax.ShapeDtypeStruct((nblk, 2 * gpb * ns, tc * lw), BF16),
                   jax.ShapeDtypeStruct((nblk, 2, 1, gpb * ns), F32)],
        compiler_params=_cparams(("arbitrary",)),
    )(lam_pg, ldt_pg, c_pg, lam_gp, ldt_gp, b_gp, lam_row, ldt_row, b_tl)


def _s5_mix_kernel(u_ref, wm_ref, wr_ref, wo_ref, a_ref, d_ref, y_ref, s_ref, x_ref, ys_ref):
    nk = u_ref.shape[1]
    half = s_ref.shape[1] // 2
    sub = 8
    u = u_ref[0]
    s_ref[...] = jnp.dot(u, wr_ref[0], preferred_element_type=F32)

    a1 = (a_ref[0, 0], a_ref[0, 1])
    a2 = _cmul(*a1, *a1)
    a4 = _cmul(*a2, *a2)
    a8 = _cmul(*a4, *a4)
    a3 = _cmul(*a2, *a1)
    pows = [(jnp.ones_like(a1[0]), jnp.zeros_like(a1[0])), a1, a2, a3, a4,
            _cmul(*a4, *a1), _cmul(*a4, *a2), _cmul(*a4, *a3)]
    p_re = jnp.concatenate([p[0] for p in pows], axis=0)
    p_im = jnp.concatenate([p[1] for p in pows], axis=0)
    rowid = lax.broadcasted_iota(jnp.int32, (sub, half), 0)
    levels = [(d, jnp.broadcast_to(a[0], (sub, half)), jnp.broadcast_to(a[1], (sub, half)))
              for d, a in ((1, a1), (2, a2), (4, a4))]

    def shift_rows(v, d):
        return jnp.where(rowid >= d, pltpu.roll(v, d, 0), 0.0)

    def tile(j, carry):
        cr, ci = carry
        rows = pl.ds(pl.multiple_of(j * sub, sub), sub)
        zr = s_ref[rows, :half]
        zi = s_ref[rows, half:]
        for d, br, bi in levels:
            sr, si = shift_rows(zr, d), shift_rows(zi, d)
            zr, zi = zr + br * sr - bi * si, zi + br * si + bi * sr
        x_ref[rows, :half] = p_re * cr - p_im * ci + shift_rows(zr, 1)
        x_ref[rows, half:] = p_re * ci + p_im * cr + shift_rows(zi, 1)
        return (a8[0] * cr - a8[1] * ci + zr[sub - 1:], a8[0] * ci + a8[1] * cr + zi[sub - 1:])

    zero = jnp.zeros((1, half), F32)
    lax.fori_loop(0, nk // sub, tile, (zero, zero))

    cw = 2 * LANES
    y = jnp.concatenate(
        [jnp.dot(u[:, :(n + 1) * cw], wm_ref[0, :(n + 1) * cw, n * cw:(n + 1) * cw],
                 preferred_element_type=F32) for n in range(u.shape[1] // cw)], axis=1)
    y = y + jnp.dot(x_ref[...].astype(BF16), wo_ref[0], preferred_element_type=F32)
    y = _gelu_tanh(y + d_ref[0] * u.astype(F32))
    for t in range(S5_TC):
        ys_ref[pl.ds(t, nk, stride=S5_TC), :] = y[:, t * LANES:(t + 1) * LANES]
    y_ref[0] = ys_ref[...].astype(y_ref.dtype)


def _s5_mixer(u3, nb, lam_re, lam_im, log_dt, b_re, b_im, c_re, c_im, d_skip):
    nblk, rows, w = u3.shape
    tc = S5_TC
    nk = rows // nb
    assert nk % 8 == 0
    wm, wr, wo, a_pow = _s5_weights(lam_re, lam_im, log_dt, b_re, b_im, c_re, c_im)
    ns = wr.shape[-1]
    d_rows = jnp.tile(d_skip.astype(F32).reshape(nblk, 1, LANES), (1, 1, tc))
    blk = lambda i, b: (i, 0, 0)
    return pl.pallas_call(
        _s5_mix_kernel,
        name="s5_mix",
        grid=(nblk, nb),
        in_specs=[pl.BlockSpec((1, nk, w), lambda i, b: (i, b, 0)),
                  pl.BlockSpec((1, w, w), blk),
                  pl.BlockSpec((1, w, ns), blk),
                  pl.BlockSpec((1, ns, w), blk),
                  pl.BlockSpec((1, 2, 1, ns // 2), lambda i, b: (i, 0, 0, 0)),
                  pl.BlockSpec((1, 1, w), blk)],
        out_specs=pl.BlockSpec((1, nk * tc, LANES), lambda i, b: (i, b, 0)),
        out_shape=jax.ShapeDtypeStruct((nblk, rows * tc, LANES), BF16),
        scratch_shapes=[pltpu.VMEM((nk, ns), F32), pltpu.VMEM((nk, ns), F32),
                        pltpu.VMEM((nk * tc, LANES), F32)],
        compiler_params=_cparams(("arbitrary", "arbitrary")),
    )(u3, wm, wr, wo, a_pow, d_rows)


def _even_out_kernel(x_ref, oa_ref, y_ref, zb_ref, wg_ref, bg_ref, wo_ref, g_ref, wf_ref,
                     x1_ref, hn_ref, f_ref):
    y = jnp.concatenate([y_ref[i] for i in range(y_ref.shape[0])], axis=1)
    t = jnp.dot(y, wg_ref[...], preferred_element_type=F32) + bg_ref[...]
    ob = y.astype(F32) * jax.nn.sigmoid(t) * _silu(zb_ref[...].astype(F32))
    wa = oa_ref.shape[1]
    acc = jnp.dot(oa_ref[...], wo_ref[:wa, :], preferred_element_type=F32)
    acc = acc + jnp.dot(ob.astype(BF16), wo_ref[wa:, :], preferred_element_type=F32)
    x1 = x_ref[...] + acc
    x1_ref[...] = x1
    hn = _rms(x1, g_ref[...]).astype(BF16)
    hn_ref[...] = hn
    f_ref[...] = lax.dot_general(wf_ref[...], hn, (((1,), (1,)), ((), ())),
                                 preferred_element_type=F32)


def _even_out(x2, oa, y, proj, w_glu, b_glu, w_out, g_next, wf_t):
    m, d = x2.shape
    nh = wf_t.shape[0]
    wa = oa.shape[1]
    nblk, _, lw = y.shape
    wb = nblk * lw
    tm = min(TM_OUT, m)
    zb_col = (proj.shape[1] - wb) // wb
    const = lambda i: (0, 0)
    return pl.pallas_call(
        _even_out_kernel,
        name="even_out",
        grid=(m // tm,),
        in_specs=[pl.BlockSpec((tm, d), lambda i: (i, 0)),
                  pl.BlockSpec((tm, wa), lambda i: (i, 0)),
                  pl.BlockSpec((nblk, tm, lw), lambda i: (0, i, 0)),
                  pl.BlockSpec((tm, wb), lambda i: (i, zb_col)),
                  pl.BlockSpec((wb, wb), const),
                  pl.BlockSpec((1, wb), const),
                  pl.BlockSpec((wa + wb, d), const),
                  pl.BlockSpec((1, d), const),
                  pl.BlockSpec((nh, d), const)],
        out_specs=[pl.BlockSpec((tm, d), lambda i: (i, 0)),
                   pl.BlockSpec((tm, d), lambda i: (i, 0)),
                   pl.BlockSpec((nh, tm), lambda i: (0, i))],
        out_shape=[jax.ShapeDtypeStruct((m, d), F32),
                   jax.ShapeDtypeStruct((m, d), BF16),
                   jax.ShapeDtypeStruct((nh, m), F32)],
        compiler_params=_cparams(("arbitrary",)),
    )(x2, oa, y, proj, w_glu, b_glu.reshape(1, wb), w_out, g_next.reshape(1, d), wf_t)


def _cum_forget_kernel(f_ref, b_ref, c_ref):
    x = f_ref[...] + b_ref[...]
    v = -(jnp.maximum(-x, 0.0) + jnp.log1p(jnp.exp(-jnp.abs(x))))
    n = v.shape[-1]
    pos = lax.broadcasted_iota(jnp.int32, v.shape, 1)
    d = 1
    while d < n:
        v = v + jnp.where(pos >= d, pltpu.roll(v, d, 1), 0.0)
        d *= 2
    c_ref[...] = v


def _cum_forget(f_t, b_forget, seq):
    nh, m = f_t.shape
    return pl.pallas_call(
        _cum_forget_kernel,
        name="cum_forget",
        grid=(m // seq,),
        in_specs=[pl.BlockSpec((nh, seq), lambda i: (0, i)),
                  pl.BlockSpec((nh, 1), lambda i: (0, 0))],
        out_specs=pl.BlockSpec((nh, seq), lambda i: (0, i)),
        out_shape=jax.ShapeDtypeStruct((nh, m), F32),
        compiler_params=_cparams(("arbitrary",)),
    )(f_t, b_forget.astype(F32).reshape(nh, 1))


def _split3(x):
    hi = x.astype(BF16).astype(F32)
    r = x - hi
    mid = r.astype(BF16).astype(F32)
    lo = (r - mid).astype(BF16).astype(F32)
    return hi, mid, lo


def _fox_kernel(q0_ref, qn_ref, k_ref, v_ref, z_ref, c_ref, o_ref,
                ke_ref, vt_ref, qa_ref, s_ref, mb_ref, acc_ref):
    qi = pl.program_id(2)
    nq = pl.num_programs(2)
    tq = qn_ref.shape[1]
    tk = tq
    dh = C_DH
    nh = c_ref.shape[1]
    nblk = k_ref.shape[1] // tk
    nt = (((1,), (1,)), ((), ()))
    key = lax.broadcasted_iota(jnp.int32, (tk, tq), 0)
    qry = lax.broadcasted_iota(jnp.int32, (tk, tq), 1)

    def cols(h):
        return slice(h * dh, (h + 1) * dh)

    def prep_queries(q_ref, qb):
        qs = pl.multiple_of(qb * tq, tq)
        lane = lax.broadcasted_iota(jnp.int32, (tq, dh), 1)
        for h in range(nh):
            c_base = jnp.max(c_ref[0, h, :, pl.ds(qs, tq)], axis=1, keepdims=True) * LOG2E
            bh, bm, bl = _split3(c_base)
            g = jnp.where(lane < 3, 1.0, jnp.where(lane == 3, bh, jnp.where(lane == 4, bm,
                          jnp.where(lane == 5, bl, 0.0))))
            qa_ref[h, :, :dh] = (q_ref[0, :, cols(h)].astype(F32) * (LOG2E / math.sqrt(dh))
                                 ).astype(BF16)
            qa_ref[h, :, dh:] = g.astype(BF16)

    def produce(h, j, keep):
        ks = pl.multiple_of(j * tk, tk)
        ka = jnp.concatenate([k_ref[0, pl.ds(ks, tk), cols(h)], ke_ref[h, pl.ds(ks, tk), :]],
                             axis=1)
        s = lax.dot_general(ka, qa_ref[h], nt, preferred_element_type=F32)
        if keep is not None:
            s = jnp.where(keep, s, NEG)
        s_ref[h] = s
        return jnp.max(s, axis=0, keepdims=True)

    def consume(h, j, stats):
        m, mb = stats
        ks = pl.multiple_of(j * tk, tk)
        m2 = jnp.maximum(m, mb)
        p = jnp.exp2(s_ref[h] - m2).astype(BF16)
        pv = jnp.dot(vt_ref[h, :, pl.ds(ks, tk)], p, preferred_element_type=F32)
        acc_ref[h] = jnp.exp2(m - m2) * acc_ref[h] + pv
        return m2

    def step(j, carry, diagonal_next):
        keep = (key <= qry) if diagonal_next else None
        return tuple((consume(h, j, stats), produce(h, j + 1, keep))
                     for h, stats in enumerate(carry))

    @pl.when(qi == 0)
    def _():
        r = lax.broadcasted_iota(jnp.int32, (dh, dh), 0)
        c = lax.broadcasted_iota(jnp.int32, (dh, dh), 1)
        eye = (r == c).astype(BF16)
        row = lax.broadcasted_iota(jnp.int32, (dh, tk), 0)
        ones_row = (lax.broadcasted_iota(jnp.int32, (FOX_PAD, tk), 0) == 0).astype(BF16)

        def blk(j, carry):
            ks = pl.multiple_of(j * tk, tk)
            for h in range(nh):
                hi, mid, lo = _split3(c_ref[0, h, :, pl.ds(ks, tk)] * (-LOG2E))
                a = jnp.where(row == 0, hi, jnp.where(row == 1, mid, jnp.where(row == 2, lo,
                              jnp.where(row < 6, 1.0, 0.0))))
                ke_ref[h, pl.ds(ks, tk), :] = a.T.astype(BF16)
                vt = lax.dot_general(eye, v_ref[0, pl.ds(ks, tk), cols(h)], nt,
                                     preferred_element_type=F32)
                vt_ref[h, :dh, pl.ds(ks, tk)] = vt.astype(BF16)
                vt_ref[h, dh:, pl.ds(ks, tk)] = ones_row
            return carry

        lax.fori_loop(0, nblk, blk, 0)
        prep_queries(q0_ref, 0)
        for h in range(nh):
            mb_ref[h] = produce(h, 0, key <= qry)

    carry = []
    for h in range(nh):
        acc_ref[h] = jnp.zeros(acc_ref.shape[1:], F32)
        carry.append((jnp.full((1, tq), NEG, F32), mb_ref[h]))
    plain = functools.partial(step, diagonal_next=False)
    n_plain = jnp.maximum(qi - 1, 0)
    n_pairs = n_plain // 2
    carry = lax.fori_loop(0, n_pairs, lambda i, c: plain(2 * i + 1, plain(2 * i, c)),
                          tuple(carry))
    carry = lax.fori_loop(2 * n_pairs, n_plain, plain, carry)
    carry = lax.fori_loop(n_plain, qi, functools.partial(step, diagonal_next=True), carry)
    for h, stats in enumerate(carry):
        consume(h, qi, stats)
    prep_queries(qn_ref, jnp.minimum(qi + 1, nq - 1))
    for h in range(nh):
        mb_ref[h] = produce(h, 0, None)
    for h in range(nh):
        acc = acc_ref[h]
        o = (acc[:dh] / acc[dh:dh + 1]).T
        o_ref[0, :, cols(h)] = (o * _silu(z_ref[0, :, cols(h)].astype(F32))).astype(o_ref.dtype)


def _fox(proj3, cum4):
    b, l, _ = proj3.shape
    ng = C_HEADS // FOX_HEADS
    w = FOX_HEADS * C_DH
    tq = min(FOX_TQ, l)
    nq = l // tq
    return pl.pallas_call(
        _fox_kernel,
        name="fox",
        grid=(b, ng, nq),
        in_specs=[pl.BlockSpec((1, tq, w), lambda bi, gi, qi: (bi, 0, gi)),
                  pl.BlockSpec((1, tq, w), lambda bi, gi, qi: (bi, jnp.minimum(qi + 1, nq - 1), gi)),
                  pl.BlockSpec((1, l, w), lambda bi, gi, qi: (bi, 0, ng + gi)),
                  pl.BlockSpec((1, l, w), lambda bi, gi, qi: (bi, 0, 2 * ng + gi)),
                  pl.BlockSpec((1, tq, w), lambda bi, gi, qi: (bi, qi, 3 * ng + gi)),
                  pl.BlockSpec((1, FOX_HEADS, 1, l), lambda bi, gi, qi: (bi, gi, 0, 0))],
        out_specs=pl.BlockSpec((1, tq, w), lambda bi, gi, qi: (bi, qi, gi)),
        out_shape=jax.ShapeDtypeStruct((b, l, C_HEADS * C_DH), BF16),
        scratch_shapes=[pltpu.VMEM((FOX_HEADS, l, C_DH), BF16),
                        pltpu.VMEM((FOX_HEADS, C_DH + FOX_PAD, l), BF16),
                        pltpu.VMEM((FOX_HEADS, tq, 2 * C_DH), BF16),
                        pltpu.VMEM((FOX_HEADS, tq, tq), F32),
                        pltpu.VMEM((FOX_HEADS, 1, tq), F32),
                        pltpu.VMEM((FOX_HEADS, C_DH + FOX_PAD, tq), F32)],
        compiler_params=_cparams(("arbitrary", "arbitrary", "arbitrary")),
    )(proj3, proj3, proj3, proj3, proj3, cum4)


def _odd_out_kernel(x_ref, o_ref, w_ref, g_ref, out_ref):
    x2 = x_ref[...] + jnp.dot(o_ref[...], w_ref[...], preferred_element_type=F32)
    out_ref[...] = _rms(x2, g_ref[...])


def _odd_out(x1, o, w_out, g):
    m, d = x1.shape
    k = o.shape[1]
    tm = min(TM_OUT, m)
    return pl.pallas_call(
        _odd_out_kernel,
        name="odd_out",
        grid=(m // tm,),
        in_specs=[pl.BlockSpec((tm, d), lambda i: (i, 0)),
                  pl.BlockSpec((tm, k), lambda i: (i, 0)),
                  pl.BlockSpec((k, d), lambda i: (0, 0)),
                  pl.BlockSpec((1, d), lambda i: (0, 0))],
        out_specs=pl.BlockSpec((tm, d), lambda i: (i, 0)),
        out_shape=jax.ShapeDtypeStruct((m, d), F32),
        compiler_params=_cparams(("arbitrary",)),
    )(x1, o, w_out, g.reshape(1, d))


def kernel(x, norm_even_g, w_in_even, rel_bias, s5_lambda_re, s5_lambda_im, s5_log_dt,
           s5_b_re, s5_b_im, s5_c_re, s5_c_im, s5_d, w_glu, b_glu, w_out_even,
           norm_odd_g, w_in_odd, b_forget, w_out_odd, final_norm_g):
    b, l, d = x.shape
    assert norm_even_g.shape[0] == 1 and norm_odd_g.shape[0] == 1, "one even + one odd layer"
    assert l % FOX_TQ == 0 and l % VT_BLK == 0 and l % A_QB == 0 and l >= A_KB
    m = b * l
    a_w = A_HEADS * A_DH
    s5_w = s5_d.shape[1] * S5_GROUP
    c_w = C_HEADS * C_DH

    x2 = x.reshape(m, d)
    assert s5_w == TN_PROJ and l % S5_TC == 0
    proj, u3 = _norm_proj(x2, norm_even_g[0], w_in_even[0].astype(BF16), 3 * a_w)
    proj3 = proj.reshape(b, l, proj.shape[1])
    o_a = _attn_a(proj3, rel_bias[0])
    y = _s5_mixer(u3, b, s5_lambda_re[0], s5_lambda_im[0], s5_log_dt[0], s5_b_re[0],
                  s5_b_im[0], s5_c_re[0], s5_c_im[0], s5_d[0])
    wf_t = w_in_odd[0, :, 4 * c_w:].T.astype(BF16)
    x1, hn1, f_t = _even_out(x2, o_a.reshape(m, a_w), y, proj,
                             w_glu[0].astype(BF16), b_glu[0].astype(F32),
                             w_out_even[0].astype(BF16), norm_odd_g[0], wf_t)
    proj1 = _proj_odd(hn1, w_in_odd, 4 * c_w)
    cum = _cum_forget(f_t, b_forget[0], l)
    cum4 = cum.reshape(C_HEADS, b, 1, l).transpose(1, 0, 2, 3)
    o_c = _fox(proj1.reshape(b, l, 4 * c_w), cum4)
    out = _odd_out(x1, o_c.reshape(m, c_w), w_out_odd[0].astype(BF16), final_norm_g)
    return out.reshape(b, l, d)
```

```python
import functools
import math

import jax
import jax.numpy as jnp
from jax import lax
from jax.experimental import pallas as pl
from jax.experimental.pallas import tpu as pltpu

F32 = jnp.float32
BF16 = jnp.bfloat16

EPS = 1e-6
NEG = -1e30
LOG2E = math.log2(math.e)

CHUNK = 64
N_LEFT = 8
A_HEADS = 16
A_DH = 64
MAX_REL = 128
S5_GROUP = 16
S5_STATE = 64
C_HEADS = 16
C_DH = 128

LANES = 128
VMEM_LIMIT = 56 * 1024 * 1024

TM_PROJ = 1024
TN_PROJ = 1024
TM_OUT = 512
A_QB = 2 * CHUNK
A_KB = A_QB + N_LEFT * CHUNK
A_TW = A_KB + N_LEFT * CHUNK
A_PAD = 16
VT_BLK = 512
S5_TC = 16
S5_GPB = LANES // S5_GROUP
FOX_TQ = 512
FOX_HEADS = 4
FOX_PAD = 16


def _cparams(sem):
    return pltpu.CompilerParams(dimension_semantics=sem, vmem_limit_bytes=VMEM_LIMIT)


def _rms(x, g):
    ms = jnp.mean(x * x, axis=-1, keepdims=True)
    return x * lax.rsqrt(ms + EPS) * g


def _silu(x):
    return x * jax.nn.sigmoid(x)


def _gelu_tanh(x):
    c = math.sqrt(2.0 / math.pi)
    return 0.5 * x * (1.0 + jnp.tanh(c * (x + 0.044715 * (x * x * x))))


def _norm_proj_kernel(x_ref, g_ref, w_ref, o_ref, u_ref, hn_ref, us_ref, *, u_tile):
    j = pl.program_id(1)

    @pl.when(j == 0)
    def _():
        hn_ref[...] = _rms(x_ref[...], g_ref[...]).astype(BF16)

    acc = jnp.dot(hn_ref[...], w_ref[...], preferred_element_type=F32)
    o_ref[...] = acc.astype(o_ref.dtype)

    @pl.when(j == u_tile)
    def _():
        nchunk = us_ref.shape[1] // S5_TC
        for blk in range(u_ref.shape[0]):
            us_ref[blk] = acc[:, blk * LANES:(blk + 1) * LANES]
            for t in range(S5_TC):
                rows = us_ref[blk, pl.ds(t, nchunk, stride=S5_TC), :]
                u_ref[blk, :, t * LANES:(t + 1) * LANES] = rows.astype(u_ref.dtype)


def _norm_proj(x2, g, w, u_start):
    m, d = x2.shape
    n = w.shape[1]
    tm, tn = min(TM_PROJ, m), TN_PROJ
    assert u_start % tn == 0 and tm % (16 * S5_TC) == 0
    nblk = tn // LANES
    return pl.pallas_call(
        functools.partial(_norm_proj_kernel, u_tile=u_start // tn),
        name="norm_proj",
        grid=(m // tm, n // tn),
        in_specs=[pl.BlockSpec((tm, d), lambda i, j: (i, 0)),
                  pl.BlockSpec((1, d), lambda i, j: (0, 0)),
                  pl.BlockSpec((d, tn), lambda i, j: (0, j))],
        out_specs=[pl.BlockSpec((tm, tn), lambda i, j: (i, j)),
                   pl.BlockSpec((nblk, tm // S5_TC, S5_TC * LANES), lambda i, j: (0, i, 0))],
        out_shape=[jax.ShapeDtypeStruct((m, n), BF16),
                   jax.ShapeDtypeStruct((nblk, m // S5_TC, S5_TC * LANES), BF16)],
        scratch_shapes=[pltpu.VMEM((tm, d), BF16), pltpu.VMEM((nblk, tm, LANES), F32)],
        compiler_params=_cparams(("arbitrary", "arbitrary")),
    )(x2, g.reshape(1, d), w)


def _proj_odd_kernel(h_ref, w_ref, o_ref, wb_ref):
    @pl.when(pl.program_id(1) == 0)
    def _():
        wb_ref[...] = w_ref[0].astype(BF16)

    o_ref[...] = jnp.dot(h_ref[...], wb_ref[...],
                         preferred_element_type=F32).astype(o_ref.dtype)


def _proj_odd(hn, w3, n):
    m, d = hn.shape
    tm, tn = min(TM_PROJ, m), TN_PROJ
    assert n % tn == 0
    return pl.pallas_call(
        _proj_odd_kernel,
        name="proj_odd",
        grid=(n // tn, m // tm),
        in_specs=[pl.BlockSpec((tm, d), lambda j, i: (i, 0)),
                  pl.BlockSpec((1, d, tn), lambda j, i: (0, 0, j))],
        out_specs=pl.BlockSpec((tm, tn), lambda j, i: (i, j)),
        out_shape=jax.ShapeDtypeStruct((m, n), BF16),
        scratch_shapes=[pltpu.VMEM((d, tn), BF16)],
        compiler_params=_cparams(("arbitrary", "arbitrary")),
    )(hn, w3)


def _attn_a_kernel(q_ref, k_ref, v_ref, z_ref, e_ref, o_ref, tt_ref, vt_ref, s_ref, pv_ref):
    l_all = k_ref.shape[1]
    w = q_ref.shape[2]
    nt = (((1,), (1,)), ((), ()))
    band = N_LEFT * CHUNK

    @pl.when(pl.program_id(1) == 0)
    def _():
        i = lax.broadcasted_iota(jnp.int32, (A_QB, A_TW), 0)
        m = lax.broadcasted_iota(jnp.int32, (A_QB, A_TW), 1)
        dchunk = m // CHUNK - i // CHUNK
        ok = (dchunk >= 0) & (dchunk <= N_LEFT)
        for h in range(2):
            rows = jnp.broadcast_to(e_ref[0, h:h + 1, :], (A_QB, A_QB + A_TW))
            toep = pltpu.roll(rows, 0, 1, stride=1, stride_axis=0)[:, A_QB:]
            t = jnp.where(ok, toep * LOG2E, NEG)
            tt_ref[:, h * A_QB:(h + 1) * A_QB] = t.T

    r = lax.broadcasted_iota(jnp.int32, (w, w), 0)
    c = lax.broadcasted_iota(jnp.int32, (w, w), 1)
    eye = (r == c).astype(BF16)
    ones_row = (lax.broadcasted_iota(jnp.int32, (A_PAD, VT_BLK), 0) == 0).astype(BF16)

    def vt_blk(j, carry):
        ks = pl.multiple_of(j * VT_BLK, VT_BLK)
        vt = lax.dot_general(eye, v_ref[0, pl.ds(ks, VT_BLK), :], nt, preferred_element_type=F32)
        for h in range(2):
            vt_ref[h, :A_DH, pl.ds(ks, VT_BLK)] = vt[h * A_DH:(h + 1) * A_DH].astype(BF16)
            vt_ref[h, A_DH:, pl.ds(ks, VT_BLK)] = ones_row
        return carry

    lax.fori_loop(0, l_all // VT_BLK, vt_blk, 0)

    lane = lax.broadcasted_iota(jnp.int32, (A_QB, w), 1)
    qscale = LOG2E / math.sqrt(A_DH)

    def window(qb):
        start = qb * A_QB - band
        start_c = jnp.maximum(start, 0)
        return pl.multiple_of(start_c, LANES), pl.multiple_of(start_c - start, LANES)

    def produce(qb, slot):
        qs = pl.multiple_of(qb * A_QB, A_QB)
        start_c, off = window(qb)
        q = (q_ref[0, pl.ds(qs, A_QB), :].astype(F32) * qscale).astype(BF16)
        zero = jnp.zeros_like(q)
        qbd = jnp.concatenate([jnp.where(lane < A_DH, q, zero), jnp.where(lane >= A_DH, q, zero)],
                              axis=0)
        kb = k_ref[0, pl.ds(start_c, A_KB), :]
        s = lax.dot_general(kb, qbd, nt, preferred_element_type=F32)
        s = s + tt_ref[pl.ds(off, A_KB), :]
        s_ref[slot] = s
        return jnp.max(s, axis=0, keepdims=True)

    def consume(qb, slot, m):
        start_c, _ = window(qb)
        pb = jnp.exp2(s_ref[slot] - m).astype(BF16)
        for h in range(2):
            vt = vt_ref[h, :, pl.ds(start_c, A_KB)]
            pv_ref[slot, h] = jnp.dot(vt, pb[:, h * A_QB:(h + 1) * A_QB],
                                      preferred_element_type=F32)

    def finish(qb, slot):
        qs = pl.multiple_of(qb * A_QB, A_QB)
        outs = []
        for h in range(2):
            oh = pv_ref[slot, h]
            outs.append(oh[:A_DH] * (1.0 / oh[A_DH:A_DH + 1]))
        o = jnp.concatenate(outs, axis=0).T
        gate = _silu(z_ref[0, pl.ds(qs, A_QB), :].astype(F32))
        o_ref[0, pl.ds(qs, A_QB), :] = (o * gate).astype(o_ref.dtype)

    def pair(i, m, lead=True):
        j = 2 * i
        if lead:
            finish(j - 1, 1)
        m1 = produce(j + 1, 1)
        consume(j, 0, m)
        m2 = produce(j + 2, 0)
        finish(j, 0)
        consume(j + 1, 1, m1)
        return m2

    nb = l_all // A_QB
    npairs = (nb - 1) // 2
    m = pair(0, produce(0, 0), lead=False)
    m = lax.fori_loop(1, npairs, pair, m)
    last = 2 * npairs
    finish(last - 1, 1)
    if (nb - 1) % 2:
        m1 = produce(last + 1, 1)
        consume(last, 0, m)
        finish(last, 0)
        consume(last + 1, 1, m1)
        finish(last + 1, 1)
    else:
        consume(last, 0, m)
        finish(last, 0)


def _attn_a(proj3, rel_bias):
    b, l, _ = proj3.shape
    hp = A_HEADS // 2
    width = 2 * A_DH
    ncol = A_HEADS * A_DH // width
    zcol = (3 * A_HEADS * A_DH + (A_HEADS * A_DH)) // width
    left = N_LEFT * CHUNK + A_QB - MAX_REL
    right = A_QB + A_TW - left - (2 * MAX_REL + 1)
    ext = jnp.pad(rel_bias.astype(F32)[:, ::-1], ((0, 0), (left, right)), mode="edge")
    ext = ext.reshape(hp, 2, A_QB + A_TW)
    seq = lambda h, bi: (bi, 0, h)
    return pl.pallas_call(
        _attn_a_kernel,
        name="attn_a",
        grid=(hp, b),
        in_specs=[pl.BlockSpec((1, l, width), seq),
                  pl.BlockSpec((1, l, width), lambda h, bi: (bi, 0, ncol + h)),
                  pl.BlockSpec((1, l, width), lambda h, bi: (bi, 0, 2 * ncol + h)),
                  pl.BlockSpec((1, l, width), lambda h, bi: (bi, 0, zcol + h)),
                  pl.BlockSpec((1, 2, A_QB + A_TW), lambda h, bi: (h, 0, 0))],
        out_specs=pl.BlockSpec((1, l, width), seq),
        out_shape=jax.ShapeDtypeStruct((b, l, A_HEADS * A_DH), BF16),
        scratch_shapes=[pltpu.VMEM((A_TW, 2 * A_QB), F32),
                        pltpu.VMEM((2, A_DH + A_PAD, l), BF16),
                        pltpu.VMEM((2, A_KB, 2 * A_QB), F32),
                        pltpu.VMEM((2, 2, A_DH + A_PAD, A_QB), F32)],
        compiler_params=_cparams(("arbitrary", "arbitrary")),
    )(proj3, proj3, proj3, proj3, ext)


def _cmul(ar, ai, br, bi):
    return ar * br - ai * bi, ar * bi + ai * br


def _s5_discretise(lr, li, log_dt):
    dt = jnp.exp(log_dt)
    mag = jnp.exp(lr * dt)
    ang = li * dt
    a_re = mag * jnp.cos(ang)
    a_im = mag * jnp.sin(ang)
    den = lr * lr + li * li
    nr = a_re - 1.0
    coef_re = (nr * lr + a_im * li) / den
    coef_im = (a_im * lr - nr * li) / den
    return (a_re, a_im), (coef_re, coef_im)


def _s5_weights_kernel(lam_pg_ref, ldt_pg_ref, c_pg_ref, lam_gp_ref, ldt_gp_ref, b_gp_ref,
                       lam_row_ref, ldt_row_ref, b_tl_ref, wm_ref, wr_ref, wo_ref, a_ref):
    tc, ch, ns, gpb = S5_TC, S5_GROUP, S5_STATE, S5_GPB
    lw = gpb * ch
    hi = lax.Precision.HIGHEST

    a_pg, _ = _s5_discretise(lam_pg_ref[0], lam_pg_ref[1], ldt_pg_ref[...])
    pw = (jnp.ones_like(a_pg[0]), jnp.zeros_like(a_pg[0]))
    ca = []
    for n in range(tc + 1):
        ca.append(_cmul(c_pg_ref[0], c_pg_ref[1], *pw))
        pw = _cmul(*pw, *a_pg)

    _, coef_gp = _s5_discretise(lam_gp_ref[0], lam_gp_ref[1], ldt_gp_ref[...])
    bb_gp = _cmul(*coef_gp, b_gp_ref[0], b_gp_ref[1])
    ca_re = jnp.concatenate([ca[n][0] for n in range(tc)], axis=1)
    ca_im = jnp.concatenate([ca[n][1] for n in range(tc)], axis=1)
    z = (jnp.dot(bb_gp[0], ca_re, precision=hi, preferred_element_type=F32)
         - jnp.dot(bb_gp[1], ca_im, precision=hi, preferred_element_type=F32))
    row_g = lax.broadcasted_iota(jnp.int32, z.shape, 0) // ch
    col_g = (lax.broadcasted_iota(jnp.int32, z.shape, 1) % lw) // ch
    z = jnp.where(row_g == col_g, z, 0.0).astype(BF16)
    for t in range(tc):
        parts = [z[:, :(tc - t) * lw]]
        if t:
            parts.insert(0, jnp.zeros((lw, t * lw), BF16))
        wm_ref[0, t * lw:(t + 1) * lw, :] = jnp.concatenate(parts, axis=1)

    rg = lax.broadcasted_iota(jnp.int32, (gpb * ns, lw), 0) // ns
    cg = lax.broadcasted_iota(jnp.int32, (gpb * ns, lw), 1) // ch
    keep = rg == cg
    for t2 in range(tc):
        cr, ci = ca[t2 + 1]
        cr = jnp.where(keep, jnp.concatenate([cr] * gpb, axis=0), 0.0)
        ci = jnp.where(keep, jnp.concatenate([ci] * gpb, axis=0), 0.0)
        wo_ref[0, :gpb * ns, t2 * lw:(t2 + 1) * lw] = cr.astype(BF16)
        wo_ref[0, gpb * ns:, t2 * lw:(t2 + 1) * lw] = (-ci).astype(BF16)

    a_row, coef_row = _s5_discretise(lam_row_ref[0], lam_row_ref[1], ldt_row_ref[...])
    bb_tl = _cmul(*coef_row, b_tl_ref[0], b_tl_ref[1])
    rg = lax.broadcasted_iota(jnp.int32, bb_tl[0].shape, 0) // ch
    cg = lax.broadcasted_iota(jnp.int32, bb_tl[0].shape, 1) // ns
    bb_tl = (jnp.where(rg == cg, bb_tl[0], 0.0), jnp.where(rg == cg, bb_tl[1], 0.0))
    pw = (jnp.ones_like(a_row[0]), jnp.zeros_like(a_row[0]))
    for t in range(tc - 1, -1, -1):
        r_re, r_im = _cmul(*pw, *bb_tl)
        wr_ref[0, t * lw:(t + 1) * lw, :] = jnp.concatenate([r_re, r_im], axis=1).astype(BF16)
        pw = _cmul(*pw, *a_row)
    a_ref[0, 0] = pw[0]
    a_ref[0, 1] = pw[1]


def _s5_weights(lam_re, lam_im, log_dt, b_re, b_im, c_re, c_im):
    g, ns = lam_re.shape
    ch, tc, gpb = S5_GROUP, S5_TC, S5_GPB
    nblk = g // gpb
    lw = gpb * ch
    lam = jnp.stack([lam_re, lam_im]).astype(F32)
    ldt = log_dt.astype(F32)
    bst = jnp.stack([b_re, b_im]).astype(F32)
    cst = jnp.stack([c_re, c_im]).astype(F32)
    lam_pg = jnp.repeat(lam.transpose(0, 2, 1), ch, axis=2)
    ldt_pg = jnp.repeat(ldt, ch)[None]
    c_pg = cst.transpose(0, 3, 1, 2).reshape(2, ns, g * ch)
    lam_gp = jnp.repeat(lam, ch, axis=1)
    ldt_gp = jnp.broadcast_to(jnp.repeat(ldt, ch)[:, None], (g * ch, ns))
    b_gp = bst.transpose(0, 1, 3, 2).reshape(2, g * ch, ns)
    lam_row = lam.reshape(2, 1, g * ns)
    ldt_row = jnp.repeat(ldt, ns)[None]
    b_tl = jnp.tile(b_gp, (1, 1, gpb))
    blk3 = lambda i: (0, 0, i)
    return pl.pallas_call(
        _s5_weights_kernel,
        name="s5_weights",
        grid=(nblk,),
        in_specs=[pl.BlockSpec((2, ns, lw), blk3),
                  pl.BlockSpec((1, lw), lambda i: (0, i)),
                  pl.BlockSpec((2, ns, lw), blk3),
                  pl.BlockSpec((2, lw, ns), lambda i: (0, i, 0)),
                  pl.BlockSpec((lw, ns), lambda i: (i, 0)),
                  pl.BlockSpec((2, lw, ns), lambda i: (0, i, 0)),
                  pl.BlockSpec((2, 1, gpb * ns), blk3),
                  pl.BlockSpec((1, gpb * ns), lambda i: (0, i)),
                  pl.BlockSpec((2, lw, gpb * ns), lambda i: (0, i, 0))],
        out_specs=[pl.BlockSpec((1, tc * lw, tc * lw), lambda i: (i, 0, 0)),
                   pl.BlockSpec((1, tc * lw, 2 * gpb * ns), lambda i: (i, 0, 0)),
                   pl.BlockSpec((1, 2 * gpb * ns, tc * lw), lambda i: (i, 0, 0)),
                   pl.BlockSpec((1, 2, 1, gpb * ns), lambda i: (i, 0, 0, 0))],
        out_shape=[jax.ShapeDtypeStruct((nblk, tc * lw, tc * lw), BF16),
                   jax.ShapeDtypeStruct((nblk, tc * lw, 2 * gpb * ns), BF16),
                   jax.ShapeDtypeStruct((nblk, 2 * gpb * ns, tc * lw), BF16),
                   jax.ShapeDtypeStruct((nblk, 2, 1, gpb * ns), F32)],
        compiler_params=_cparams(("arbitrary",)),
    )(lam_pg, ldt_pg, c_pg, lam_gp, ldt_gp, b_gp, lam_row, ldt_row, b_tl)


def _s5_mix_kernel(u_ref, wm_ref, wr_ref, wo_ref, a_ref, d_ref, y_ref, s_ref, x_ref, ys_ref):
    nk = u_ref.shape[1]
    half = s_ref.shape[1] // 2
    sub = 8
    u = u_ref[0]
    s_ref[...] = jnp.dot(u, wr_ref[0], preferred_element_type=F32)

    a1 = (a_ref[0, 0], a_ref[0, 1])
    a2 = _cmul(*a1, *a1)
    a4 = _cmul(*a2, *a2)
    a8 = _cmul(*a4, *a4)
    a3 = _cmul(*a2, *a1)
    pows = [(jnp.ones_like(a1[0]), jnp.zeros_like(a1[0])), a1, a2, a3, a4,
            _cmul(*a4, *a1), _cmul(*a4, *a2), _cmul(*a4, *a3)]
    p_re = jnp.concatenate([p[0] for p in pows], axis=0)
    p_im = jnp.concatenate([p[1] for p in pows], axis=0)
    rowid = lax.broadcasted_iota(jnp.int32, (sub, half), 0)
    levels = [(d, jnp.broadcast_to(a[0], (sub, half)), jnp.broadcast_to(a[1], (sub, half)))
              for d, a in ((1, a1), (2, a2), (4, a4))]

    def shift_rows(v, d):
        return jnp.where(rowid >= d, pltpu.roll(v, d, 0), 0.0)

    def tile(j, carry):
        cr, ci = carry
        rows = pl.ds(pl.multiple_of(j * sub, sub), sub)
        zr = s_ref[rows, :half]
        zi = s_ref[rows, half:]
        for d, br, bi in levels:
            sr, si = shift_rows(zr, d), shift_rows(zi, d)
            zr, zi = zr + br * sr - bi * si, zi + br * si + bi * sr
        x_ref[rows, :half] = p_re * cr - p_im * ci + shift_rows(zr, 1)
        x_ref[rows, half:] = p_re * ci + p_im * cr + shift_rows(zi, 1)
        return (a8[0] * cr - a8[1] * ci + zr[sub - 1:], a8[0] * ci + a8[1] * cr + zi[sub - 1:])

    zero = jnp.zeros((1, half), F32)
    lax.fori_loop(0, nk // sub, tile, (zero, zero))

    cw = 2 * LANES
    y = jnp.concatenate(
        [jnp.dot(u[:, :(n + 1) * cw], wm_ref[0, :(n + 1) * cw, n * cw:(n + 1) * cw],
                 preferred_element_type=F32) for n in range(u.shape[1] // cw)], axis=1)
    y = y + jnp.dot(x_ref[...].astype(BF16), wo_ref[0], preferred_element_type=F32)
    y = _gelu_tanh(y + d_ref[0] * u.astype(F32))
    for t in range(S5_TC):
        ys_ref[pl.ds(t, nk, stride=S5_TC), :] = y[:, t * LANES:(t + 1) * LANES]
    y_ref[0] = ys_ref[...].astype(y_ref.dtype)


def _s5_mixer(u3, nb, lam_re, lam_im, log_dt, b_re, b_im, c_re, c_im, d_skip):
    nblk, rows, w = u3.shape
    tc = S5_TC
    nk = rows // nb
    assert nk % 8 == 0
    wm, wr, wo, a_pow = _s5_weights(lam_re, lam_im, log_dt, b_re, b_im, c_re, c_im)
    ns = wr.shape[-1]
    d_rows = jnp.tile(d_skip.astype(F32).reshape(nblk, 1, LANES), (1, 1, tc))
    blk = lambda i, b: (i, 0, 0)
    return pl.pallas_call(
        _s5_mix_kernel,
        name="s5_mix",
        grid=(nblk, nb),
        in_specs=[pl.BlockSpec((1, nk, w), lambda i, b: (i, b, 0)),
                  pl.BlockSpec((1, w, w), blk),
                  pl.BlockSpec((1, w, ns), blk),
                  pl.BlockSpec((1, ns, w), blk),
                  pl.BlockSpec((1, 2, 1, ns // 2), lambda i, b: (i, 0, 0, 0)),
                  pl.BlockSpec((1, 1, w), blk)],
        out_specs=pl.BlockSpec((1, nk * tc, LANES), lambda i, b: (i, b, 0)),
        out_shape=jax.ShapeDtypeStruct((nblk, rows * tc, LANES), BF16),
        scratch_shapes=[pltpu.VMEM((nk, ns), F32), pltpu.VMEM((nk, ns), F32),
                        pltpu.VMEM((nk * tc, LANES), F32)],
        compiler_params=_cparams(("arbitrary", "arbitrary")),
    )(u3, wm, wr, wo, a_pow, d_rows)


def _even_out_kernel(x_ref, oa_ref, y_ref, zb_ref, wg_ref, bg_ref, wo_ref, g_ref, wf_ref,
                     x1_ref, hn_ref, f_ref):
    y = jnp.concatenate([y_ref[i] for i in range(y_ref.shape[0])], axis=1)
    t = jnp.dot(y, wg_ref[...], preferred_element_type=F32) + bg_ref[...]
    ob = y.astype(F32) * jax.nn.sigmoid(t) * _silu(zb_ref[...].astype(F32))
    wa = oa_ref.shape[1]
    acc = jnp.dot(oa_ref[...], wo_ref[:wa, :], preferred_element_type=F32)
    acc = acc + jnp.dot(ob.astype(BF16), wo_ref[wa:, :], preferred_element_type=F32)
    x1 = x_ref[...] + acc
    x1_ref[...] = x1
    hn = _rms(x1, g_ref[...]).astype(BF16)
    hn_ref[...] = hn
    f = jnp.dot(hn, wf_ref[...], preferred_element_type=F32)
    f_ref[...] = f.T[:f_ref.shape[0]]


def _even_out(x2, oa, y, proj, w_glu, b_glu, w_out, g_next, wf, nh):
    m, d = x2.shape
    wa = oa.shape[1]
    nblk, _, lw = y.shape
    wb = nblk * lw
    tm = min(TM_OUT, m)
    zb_col = (proj.shape[1] - wb) // wb
    const = lambda i: (0, 0)
    return pl.pallas_call(
        _even_out_kernel,
        name="even_out",
        grid=(m // tm,),
        in_specs=[pl.BlockSpec((tm, d), lambda i: (i, 0)),
                  pl.BlockSpec((tm, wa), lambda i: (i, 0)),
                  pl.BlockSpec((nblk, tm, lw), lambda i: (0, i, 0)),
                  pl.BlockSpec((tm, wb), lambda i: (i, zb_col)),
                  pl.BlockSpec((wb, wb), const),
                  pl.BlockSpec((1, wb), const),
                  pl.BlockSpec((wa + wb, d), const),
                  pl.BlockSpec((1, d), const),
                  pl.BlockSpec(wf.shape, const)],
        out_specs=[pl.BlockSpec((tm, d), lambda i: (i, 0)),
                   pl.BlockSpec((tm, d), lambda i: (i, 0)),
                   pl.BlockSpec((nh, tm), lambda i: (0, i))],
        out_shape=[jax.ShapeDtypeStruct((m, d), F32),
                   jax.ShapeDtypeStruct((m, d), BF16),
                   jax.ShapeDtypeStruct((nh, m), F32)],
        compiler_params=_cparams(("arbitrary",)),
    )(x2, oa, y, proj, w_glu, b_glu.reshape(1, wb), w_out, g_next.reshape(1, d), wf)


def _cum_forget_kernel(f_ref, b_ref, c_ref):
    x = f_ref[...] + b_ref[...]
    v = -(jnp.maximum(-x, 0.0) + jnp.log1p(jnp.exp(-jnp.abs(x))))
    n = v.shape[-1]
    pos = lax.broadcasted_iota(jnp.int32, v.shape, 1)
    d = 1
    while d < n:
        v = v + jnp.where(pos >= d, pltpu.roll(v, d, 1), 0.0)
        d *= 2
    c_ref[...] = v


def _cum_forget(f_t, b_forget, seq):
    nh, m = f_t.shape
    return pl.pallas_call(
        _cum_forget_kernel,
        name="cum_forget",
        grid=(m // seq,),
        in_specs=[pl.BlockSpec((nh, seq), lambda i: (0, i)),
                  pl.BlockSpec((nh, 1), lambda i: (0, 0))],
        out_specs=pl.BlockSpec((nh, seq), lambda i: (0, i)),
        out_shape=jax.ShapeDtypeStruct((nh, m), F32),
        compiler_params=_cparams(("arbitrary",)),
    )(f_t, b_forget.astype(F32).reshape(nh, 1))


def _split3(x):
    hi = x.astype(BF16).astype(F32)
    r = x - hi
    mid = r.astype(BF16).astype(F32)
    lo = (r - mid).astype(BF16).astype(F32)
    return hi, mid, lo


def _fox_kernel(q0_ref, qn_ref, k_ref, v_ref, z_ref, c_ref, o_ref,
                ke_ref, vt_ref, qa_ref, s_ref, mb_ref, acc_ref):
    qi = pl.program_id(2)
    nq = pl.num_programs(2)
    tq = qn_ref.shape[1]
    tk = tq
    dh = C_DH
    nh = c_ref.shape[1]
    nblk = k_ref.shape[1] // tk
    nt = (((1,), (1,)), ((), ()))
    key = lax.broadcasted_iota(jnp.int32, (tk, tq), 0)
    qry = lax.broadcasted_iota(jnp.int32, (tk, tq), 1)

    def cols(h):
        return slice(h * dh, (h + 1) * dh)

    def prep_queries(q_ref, qb):
        qs = pl.multiple_of(qb * tq, tq)
        lane = lax.broadcasted_iota(jnp.int32, (tq, dh), 1)
        for h in range(nh):
            c_base = jnp.max(c_ref[0, h, :, pl.ds(qs, tq)], axis=1, keepdims=True) * LOG2E
            bh, bm, bl = _split3(c_base)
            g = jnp.where(lane < 3, 1.0, jnp.where(lane == 3, bh, jnp.where(lane == 4, bm,
                          jnp.where(lane == 5, bl, 0.0))))
            qa_ref[h, :, :dh] = (q_ref[0, :, cols(h)].astype(F32) * (LOG2E / math.sqrt(dh))
                                 ).astype(BF16)
            qa_ref[h, :, dh:] = g.astype(BF16)

    def produce(h, j, keep):
        ks = pl.multiple_of(j * tk, tk)
        ka = jnp.concatenate([k_ref[0, pl.ds(ks, tk), cols(h)], ke_ref[h, pl.ds(ks, tk), :]],
                             axis=1)
        s = lax.dot_general(ka, qa_ref[h], nt, preferred_element_type=F32)
        if keep is not None:
            s = jnp.where(keep, s, NEG)
        s_ref[h] = s
        return jnp.max(s, axis=0, keepdims=True)

    def consume(h, j, stats):
        m, mb = stats
        ks = pl.multiple_of(j * tk, tk)
        m2 = jnp.maximum(m, mb)
        p = jnp.exp2(s_ref[h] - m2).astype(BF16)
        pv = jnp.dot(vt_ref[h, :, pl.ds(ks, tk)], p, preferred_element_type=F32)
        acc_ref[h] = jnp.exp2(m - m2) * acc_ref[h] + pv
        return m2

    def step(j, carry, diagonal_next):
        keep = (key <= qry) if diagonal_next else None
        return tuple((consume(h, j, stats), produce(h, j + 1, keep))
                     for h, stats in enumerate(carry))

    @pl.when(qi == 0)
    def _():
        r = lax.broadcasted_iota(jnp.int32, (dh, dh), 0)
        c = lax.broadcasted_iota(jnp.int32, (dh, dh), 1)
        eye = (r == c).astype(BF16)
        row = lax.broadcasted_iota(jnp.int32, (dh, tk), 0)
        ones_row = (lax.broadcasted_iota(jnp.int32, (FOX_PAD, tk), 0) == 0).astype(BF16)

        def blk(j, carry):
            ks = pl.multiple_of(j * tk, tk)
            for h in range(nh):
                hi, mid, lo = _split3(c_ref[0, h, :, pl.ds(ks, tk)] * (-LOG2E))
                a = jnp.where(row == 0, hi, jnp.where(row == 1, mid, jnp.where(row == 2, lo,
                              jnp.where(row < 6, 1.0, 0.0))))
                ke_ref[h, pl.ds(ks, tk), :] = a.T.astype(BF16)
                vt = lax.dot_general(eye, v_ref[0, pl.ds(ks, tk), cols(h)], nt,
                                     preferred_element_type=F32)
                vt_ref[h, :dh, pl.ds(ks, tk)] = vt.astype(BF16)
                vt_ref[h, dh:, pl.ds(ks, tk)] = ones_row
            return carry

        lax.fori_loop(0, nblk, blk, 0)
        prep_queries(q0_ref, 0)
        for h in range(nh):
            mb_ref[h] = produce(h, 0, key <= qry)

    carry = []
    for h in range(nh):
        acc_ref[h] = jnp.zeros(acc_ref.shape[1:], F32)
        carry.append((jnp.full((1, tq), NEG, F32), mb_ref[h]))
    plain = functools.partial(step, diagonal_next=False)
    n_plain = jnp.maximum(qi - 1, 0)
    n_pairs = n_plain // 2
    carry = lax.fori_loop(0, n_pairs, lambda i, c: plain(2 * i + 1, plain(2 * i, c)),
                          tuple(carry))
    carry = lax.fori_loop(2 * n_pairs, n_plain, plain, carry)
    carry = lax.fori_loop(n_plain, qi, functools.partial(step, diagonal_next=True), carry)
    for h, stats in enumerate(carry):
        consume(h, qi, stats)
    prep_queries(qn_ref, jnp.minimum(qi + 1, nq - 1))
    for h in range(nh):
        mb_ref[h] = produce(h, 0, None)
    for h in range(nh):
        acc = acc_ref[h]
        o = (acc[:dh] / acc[dh:dh + 1]).T
        o_ref[0, :, cols(h)] = (o * _silu(z_ref[0, :, cols(h)].astype(F32))).astype(o_ref.dtype)


def _fox(proj3, cum4):
    b, l, _ = proj3.shape
    ng = C_HEADS // FOX_HEADS
    w = FOX_HEADS * C_DH
    tq = min(FOX_TQ, l)
    nq = l // tq
    return pl.pallas_call(
        _fox_kernel,
        name="fox",
        grid=(b, ng, nq),
        in_specs=[pl.BlockSpec((1, tq, w), lambda bi, gi, qi: (bi, 0, gi)),
                  pl.BlockSpec((1, tq, w), lambda bi, gi, qi: (bi, jnp.minimum(qi + 1, nq - 1), gi)),
                  pl.BlockSpec((1, l, w), lambda bi, gi, qi: (bi, 0, ng + gi)),
                  pl.BlockSpec((1, l, w), lambda bi, gi, qi: (bi, 0, 2 * ng + gi)),
                  pl.BlockSpec((1, tq, w), lambda bi, gi, qi: (bi, qi, 3 * ng + gi)),
                  pl.BlockSpec((1, FOX_HEADS, 1, l), lambda bi, gi, qi: (bi, gi, 0, 0))],
        out_specs=pl.BlockSpec((1, tq, w), lambda bi, gi, qi: (bi, qi, gi)),
        out_shape=jax.ShapeDtypeStruct((b, l, C_HEADS * C_DH), BF16),
        scratch_shapes=[pltpu.VMEM((FOX_HEADS, l, C_DH), BF16),
                        pltpu.VMEM((FOX_HEADS, C_DH + FOX_PAD, l), BF16),
                        pltpu.VMEM((FOX_HEADS, tq, 2 * C_DH), BF16),
                        pltpu.VMEM((FOX_HEADS, tq, tq), F32),
                        pltpu.VMEM((FOX_HEADS, 1, tq), F32),
                        pltpu.VMEM((FOX_HEADS, C_DH + FOX_PAD, tq), F32)],
        compiler_params=_cparams(("arbitrary", "arbitrary", "arbitrary")),
    )(proj3, proj3, proj3, proj3, proj3, cum4)


def _odd_out_kernel(x_ref, o_ref, w_ref, g_ref, out_ref):
    x2 = x_ref[...] + jnp.dot(o_ref[...], w_ref[...], preferred_element_type=F32)
    out_ref[...] = _rms(x2, g_ref[...])


def _odd_out(x1, o, w_out, g):
    m, d = x1.shape
    k = o.shape[1]
    tm = min(TM_OUT, m)
    return pl.pallas_call(
        _odd_out_kernel,
        name="odd_out",
        grid=(m // tm,),
        in_specs=[pl.BlockSpec((tm, d), lambda i: (i, 0)),
                  pl.BlockSpec((tm, k), lambda i: (i, 0)),
                  pl.BlockSpec((k, d), lambda i: (0, 0)),
                  pl.BlockSpec((1, d), lambda i: (0, 0))],
        out_specs=pl.BlockSpec((tm, d), lambda i: (i, 0)),
        out_shape=jax.ShapeDtypeStruct((m, d), F32),
        compiler_params=_cparams(("arbitrary",)),
    )(x1, o, w_out, g.reshape(1, d))


def kernel(x, norm_even_g, w_in_even, rel_bias, s5_lambda_re, s5_lambda_im, s5_log_dt,
           s5_b_re, s5_b_im, s5_c_re, s5_c_im, s5_d, w_glu, b_glu, w_out_even,
           norm_odd_g, w_in_odd, b_forget, w_out_odd, final_norm_g):
    b, l, d = x.shape
    assert norm_even_g.shape[0] == 1 and norm_odd_g.shape[0] == 1, "one even + one odd layer"
    assert l % FOX_TQ == 0 and l % VT_BLK == 0 and l % A_QB == 0 and l >= A_KB
    m = b * l
    a_w = A_HEADS * A_DH
    s5_w = s5_d.shape[1] * S5_GROUP
    c_w = C_HEADS * C_DH

    x2 = x.reshape(m, d)
    assert s5_w == TN_PROJ and l % S5_TC == 0
    proj, u3 = _norm_proj(x2, norm_even_g[0], w_in_even[0].astype(BF16), 3 * a_w)
    proj3 = proj.reshape(b, l, proj.shape[1])
    o_a = _attn_a(proj3, rel_bias[0])
    y = _s5_mixer(u3, b, s5_lambda_re[0], s5_lambda_im[0], s5_log_dt[0], s5_b_re[0],
                  s5_b_im[0], s5_c_re[0], s5_c_im[0], s5_d[0])
    wf = jnp.pad(w_in_odd[0, :, 4 * c_w:], ((0, 0), (0, LANES - C_HEADS))).astype(BF16)
    x1, hn1, f_t = _even_out(x2, o_a.reshape(m, a_w), y, proj,
                             w_glu[0].astype(BF16), b_glu[0].astype(F32),
                             w_out_even[0].astype(BF16), norm_odd_g[0], wf, C_HEADS)
    proj1 = _proj_odd(hn1, w_in_odd, 4 * c_w)
    cum = _cum_forget(f_t, b_forget[0], l)
    cum4 = cum.reshape(C_HEADS, b, 1, l).transpose(1, 0, 2, 3)
    o_c = _fox(proj1.reshape(b, l, 4 * c_w), cum4)
    out = _odd_out(x1, o_c.reshape(m, c_w), w_out_odd[0].astype(BF16), final_norm_g)
    return out.reshape(b, l, d)
```

```python
import functools
import math

import jax
import jax.numpy as jnp
from jax import lax
from jax.experimental import pallas as pl
from jax.experimental.pallas import tpu as pltpu

F32 = jnp.float32
BF16 = jnp.bfloat16

EPS = 1e-6
NEG = -1e30
LOG2E = math.log2(math.e)

CHUNK = 64
N_LEFT = 8
A_HEADS = 16
A_DH = 64
MAX_REL = 128
S5_GROUP = 16
S5_STATE = 64
C_HEADS = 16
C_DH = 128

LANES = 128
VMEM_LIMIT = 56 * 1024 * 1024

TM_PROJ = 1024
TN_PROJ = 1024
TM_OUT = 512
A_QB = 2 * CHUNK
A_KB = A_QB + N_LEFT * CHUNK
A_TW = A_KB + N_LEFT * CHUNK
A_PAD = 16
VT_BLK = 512
S5_TC = 16
S5_GPB = LANES // S5_GROUP
FOX_TQ = 512
FOX_HEADS = 4
FOX_PAD = 16


def _cparams(sem):
    return pltpu.CompilerParams(dimension_semantics=sem, vmem_limit_bytes=VMEM_LIMIT)


def _rms(x, g):
    ms = jnp.mean(x * x, axis=-1, keepdims=True)
    return x * lax.rsqrt(ms + EPS) * g


def _silu(x):
    return x * jax.nn.sigmoid(x)


def _gelu_tanh(x):
    c = math.sqrt(2.0 / math.pi)
    return 0.5 * x * (1.0 + jnp.tanh(c * (x + 0.044715 * (x * x * x))))


def _norm_proj_kernel(x_ref, g_ref, w_ref, o_ref, u_ref, hn_ref, us_ref, *, u_tile):
    j = pl.program_id(1)

    @pl.when(j == 0)
    def _():
        hn_ref[...] = _rms(x_ref[...], g_ref[...]).astype(BF16)

    acc = jnp.dot(hn_ref[...], w_ref[...], preferred_element_type=F32)
    o_ref[...] = acc.astype(o_ref.dtype)

    @pl.when(j == u_tile)
    def _():
        nchunk = us_ref.shape[1] // S5_TC
        for blk in range(u_ref.shape[0]):
            us_ref[blk] = acc[:, blk * LANES:(blk + 1) * LANES]
            for t in range(S5_TC):
                rows = us_ref[blk, pl.ds(t, nchunk, stride=S5_TC), :]
                u_ref[blk, :, t * LANES:(t + 1) * LANES] = rows.astype(u_ref.dtype)


def _norm_proj(x2, g, w, u_start):
    m, d = x2.shape
    n = w.shape[1]
    tm, tn = min(TM_PROJ, m), TN_PROJ
    assert u_start % tn == 0 and tm % (16 * S5_TC) == 0
    nblk = tn // LANES
    return pl.pallas_call(
        functools.partial(_norm_proj_kernel, u_tile=u_start // tn),
        name="norm_proj",
        grid=(m // tm, n // tn),
        in_specs=[pl.BlockSpec((tm, d), lambda i, j: (i, 0)),
                  pl.BlockSpec((1, d), lambda i, j: (0, 0)),
                  pl.BlockSpec((d, tn), lambda i, j: (0, j))],
        out_specs=[pl.BlockSpec((tm, tn), lambda i, j: (i, j)),
                   pl.BlockSpec((nblk, tm // S5_TC, S5_TC * LANES), lambda i, j: (0, i, 0))],
        out_shape=[jax.ShapeDtypeStruct((m, n), BF16),
                   jax.ShapeDtypeStruct((nblk, m // S5_TC, S5_TC * LANES), BF16)],
        scratch_shapes=[pltpu.VMEM((tm, d), BF16), pltpu.VMEM((nblk, tm, LANES), F32)],
        compiler_params=_cparams(("arbitrary", "arbitrary")),
    )(x2, g.reshape(1, d), w)


def _proj_odd_kernel(h_ref, wt_ref, o_ref, wb_ref):
    @pl.when(pl.program_id(1) == 0)
    def _():
        wb_ref[...] = wt_ref[...].T.astype(BF16)

    o_ref[...] = jnp.dot(h_ref[...], wb_ref[...],
                         preferred_element_type=F32).astype(o_ref.dtype)


def _proj_odd(hn, wt, n):
    m, d = hn.shape
    tm, tn = min(TM_PROJ, m), TN_PROJ
    assert n % tn == 0
    return pl.pallas_call(
        _proj_odd_kernel,
        name="proj_odd",
        grid=(n // tn, m // tm),
        in_specs=[pl.BlockSpec((tm, d), lambda j, i: (i, 0)),
                  pl.BlockSpec((tn, d), lambda j, i: (j, 0))],
        out_specs=pl.BlockSpec((tm, tn), lambda j, i: (i, j)),
        out_shape=jax.ShapeDtypeStruct((m, n), BF16),
        scratch_shapes=[pltpu.VMEM((d, tn), BF16)],
        compiler_params=_cparams(("arbitrary", "arbitrary")),
    )(hn, wt)


def _attn_a_kernel(q_ref, k_ref, v_ref, z_ref, e_ref, o_ref, tt_ref, vt_ref, s_ref, pv_ref):
    l_all = k_ref.shape[1]
    w = q_ref.shape[2]
    nt = (((1,), (1,)), ((), ()))
    band = N_LEFT * CHUNK

    @pl.when(pl.program_id(1) == 0)
    def _():
        i = lax.broadcasted_iota(jnp.int32, (A_QB, A_TW), 0)
        m = lax.broadcasted_iota(jnp.int32, (A_QB, A_TW), 1)
        dchunk = m // CHUNK - i // CHUNK
        ok = (dchunk >= 0) & (dchunk <= N_LEFT)
        for h in range(2):
            rows = jnp.broadcast_to(e_ref[0, h:h + 1, :], (A_QB, A_QB + A_TW))
            toep = pltpu.roll(rows, 0, 1, stride=1, stride_axis=0)[:, A_QB:]
            t = jnp.where(ok, toep * LOG2E, NEG)
            tt_ref[:, h * A_QB:(h + 1) * A_QB] = t.T

    r = lax.broadcasted_iota(jnp.int32, (w, w), 0)
    c = lax.broadcasted_iota(jnp.int32, (w, w), 1)
    eye = (r == c).astype(BF16)
    ones_row = (lax.broadcasted_iota(jnp.int32, (A_PAD, VT_BLK), 0) == 0).astype(BF16)

    def vt_blk(j, carry):
        ks = pl.multiple_of(j * VT_BLK, VT_BLK)
        vt = lax.dot_general(eye, v_ref[0, pl.ds(ks, VT_BLK), :], nt, preferred_element_type=F32)
        for h in range(2):
            vt_ref[h, :A_DH, pl.ds(ks, VT_BLK)] = vt[h * A_DH:(h + 1) * A_DH].astype(BF16)
            vt_ref[h, A_DH:, pl.ds(ks, VT_BLK)] = ones_row
        return carry

    lax.fori_loop(0, l_all // VT_BLK, vt_blk, 0)

    lane = lax.broadcasted_iota(jnp.int32, (A_QB, w), 1)
    qscale = LOG2E / math.sqrt(A_DH)

    def window(qb):
        start = qb * A_QB - band
        start_c = jnp.maximum(start, 0)
        return pl.multiple_of(start_c, LANES), pl.multiple_of(start_c - start, LANES)

    def produce(qb, slot):
        qs = pl.multiple_of(qb * A_QB, A_QB)
        start_c, off = window(qb)
        q = (q_ref[0, pl.ds(qs, A_QB), :].astype(F32) * qscale).astype(BF16)
        zero = jnp.zeros_like(q)
        qbd = jnp.concatenate([jnp.where(lane < A_DH, q, zero), jnp.where(lane >= A_DH, q, zero)],
                              axis=0)
        kb = k_ref[0, pl.ds(start_c, A_KB), :]
        s = lax.dot_general(kb, qbd, nt, preferred_element_type=F32)
        s = s + tt_ref[pl.ds(off, A_KB), :]
        s_ref[slot] = s
        return jnp.max(s, axis=0, keepdims=True)

    def consume(qb, slot, m):
        start_c, _ = window(qb)
        pb = jnp.exp2(s_ref[slot] - m).astype(BF16)
        for h in range(2):
            vt = vt_ref[h, :, pl.ds(start_c, A_KB)]
            pv_ref[slot, h] = jnp.dot(vt, pb[:, h * A_QB:(h + 1) * A_QB],
                                      preferred_element_type=F32)

    def finish(qb, slot):
        qs = pl.multiple_of(qb * A_QB, A_QB)
        outs = []
        for h in range(2):
            oh = pv_ref[slot, h]
            outs.append(oh[:A_DH] * (1.0 / oh[A_DH:A_DH + 1]))
        o = jnp.concatenate(outs, axis=0).T
        gate = _silu(z_ref[0, pl.ds(qs, A_QB), :].astype(F32))
        o_ref[0, pl.ds(qs, A_QB), :] = (o * gate).astype(o_ref.dtype)

    def pair(i, m, lead=True):
        j = 2 * i
        if lead:
            finish(j - 1, 1)
        m1 = produce(j + 1, 1)
        consume(j, 0, m)
        m2 = produce(j + 2, 0)
        finish(j, 0)
        consume(j + 1, 1, m1)
        return m2

    nb = l_all // A_QB
    npairs = (nb - 1) // 2
    m = pair(0, produce(0, 0), lead=False)
    m = lax.fori_loop(1, npairs, pair, m)
    last = 2 * npairs
    finish(last - 1, 1)
    if (nb - 1) % 2:
        m1 = produce(last + 1, 1)
        consume(last, 0, m)
        finish(last, 0)
        consume(last + 1, 1, m1)
        finish(last + 1, 1)
    else:
        consume(last, 0, m)
        finish(last, 0)


def _attn_a(proj3, rel_bias):
    b, l, _ = proj3.shape
    hp = A_HEADS // 2
    width = 2 * A_DH
    ncol = A_HEADS * A_DH // width
    zcol = (3 * A_HEADS * A_DH + (A_HEADS * A_DH)) // width
    left = N_LEFT * CHUNK + A_QB - MAX_REL
    right = A_QB + A_TW - left - (2 * MAX_REL + 1)
    ext = jnp.pad(rel_bias.astype(F32)[:, ::-1], ((0, 0), (left, right)), mode="edge")
    ext = ext.reshape(hp, 2, A_QB + A_TW)
    seq = lambda h, bi: (bi, 0, h)
    return pl.pallas_call(
        _attn_a_kernel,
        name="attn_a",
        grid=(hp, b),
        in_specs=[pl.BlockSpec((1, l, width), seq),
                  pl.BlockSpec((1, l, width), lambda h, bi: (bi, 0, ncol + h)),
                  pl.BlockSpec((1, l, width), lambda h, bi: (bi, 0, 2 * ncol + h)),
                  pl.BlockSpec((1, l, width), lambda h, bi: (bi, 0, zcol + h)),
                  pl.BlockSpec((1, 2, A_QB + A_TW), lambda h, bi: (h, 0, 0))],
        out_specs=pl.BlockSpec((1, l, width), seq),
        out_shape=jax.ShapeDtypeStruct((b, l, A_HEADS * A_DH), BF16),
        scratch_shapes=[pltpu.VMEM((A_TW, 2 * A_QB), F32),
                        pltpu.VMEM((2, A_DH + A_PAD, l), BF16),
                        pltpu.VMEM((2, A_KB, 2 * A_QB), F32),
                        pltpu.VMEM((2, 2, A_DH + A_PAD, A_QB), F32)],
        compiler_params=_cparams(("arbitrary", "arbitrary")),
    )(proj3, proj3, proj3, proj3, ext)


def _cmul(ar, ai, br, bi):
    return ar * br - ai * bi, ar * bi + ai * br


def _s5_discretise(lr, li, log_dt):
    dt = jnp.exp(log_dt)
    mag = jnp.exp(lr * dt)
    ang = li * dt
    a_re = mag * jnp.cos(ang)
    a_im = mag * jnp.sin(ang)
    den = lr * lr + li * li
    nr = a_re - 1.0
    coef_re = (nr * lr + a_im * li) / den
    coef_im = (a_im * lr - nr * li) / den
    return (a_re, a_im), (coef_re, coef_im)


def _s5_weights_kernel(lam_pg_ref, ldt_pg_ref, c_pg_ref, lam_gp_ref, ldt_gp_ref, b_gp_ref,
                       lam_row_ref, ldt_row_ref, b_tl_ref, wm_ref, wr_ref, wo_ref, a_ref):
    tc, ch, ns, gpb = S5_TC, S5_GROUP, S5_STATE, S5_GPB
    lw = gpb * ch
    hi = lax.Precision.HIGHEST

    a_pg, _ = _s5_discretise(lam_pg_ref[0], lam_pg_ref[1], ldt_pg_ref[...])
    pw = (jnp.ones_like(a_pg[0]), jnp.zeros_like(a_pg[0]))
    ca = []
    for n in range(tc + 1):
        ca.append(_cmul(c_pg_ref[0], c_pg_ref[1], *pw))
        pw = _cmul(*pw, *a_pg)

    _, coef_gp = _s5_discretise(lam_gp_ref[0], lam_gp_ref[1], ldt_gp_ref[...])
    bb_gp = _cmul(*coef_gp, b_gp_ref[0], b_gp_ref[1])
    ca_re = jnp.concatenate([ca[n][0] for n in range(tc)], axis=1)
    ca_im = jnp.concatenate([ca[n][1] for n in range(tc)], axis=1)
    z = (jnp.dot(bb_gp[0], ca_re, precision=hi, preferred_element_type=F32)
         - jnp.dot(bb_gp[1], ca_im, precision=hi, preferred_element_type=F32))
    row_g = lax.broadcasted_iota(jnp.int32, z.shape, 0) // ch
    col_g = (lax.broadcasted_iota(jnp.int32, z.shape, 1) % lw) // ch
    z = jnp.where(row_g == col_g, z, 0.0).astype(BF16)
    for t in range(tc):
        parts = [z[:, :(tc - t) * lw]]
        if t:
            parts.insert(0, jnp.zeros((lw, t * lw), BF16))
        wm_ref[0, t * lw:(t + 1) * lw, :] = jnp.concatenate(parts, axis=1)

    rg = lax.broadcasted_iota(jnp.int32, (gpb * ns, lw), 0) // ns
    cg = lax.broadcasted_iota(jnp.int32, (gpb * ns, lw), 1) // ch
    keep = rg == cg
    for t2 in range(tc):
        cr, ci = ca[t2 + 1]
        cr = jnp.where(keep, jnp.concatenate([cr] * gpb, axis=0), 0.0)
        ci = jnp.where(keep, jnp.concatenate([ci] * gpb, axis=0), 0.0)
        wo_ref[0, :gpb * ns, t2 * lw:(t2 + 1) * lw] = cr.astype(BF16)
        wo_ref[0, gpb * ns:, t2 * lw:(t2 + 1) * lw] = (-ci).astype(BF16)

    a_row, coef_row = _s5_discretise(lam_row_ref[0], lam_row_ref[1], ldt_row_ref[...])
    bb_tl = _cmul(*coef_row, b_tl_ref[0], b_tl_ref[1])
    rg = lax.broadcasted_iota(jnp.int32, bb_tl[0].shape, 0) // ch
    cg = lax.broadcasted_iota(jnp.int32, bb_tl[0].shape, 1) // ns
    bb_tl = (jnp.where(rg == cg, bb_tl[0], 0.0), jnp.where(rg == cg, bb_tl[1], 0.0))
    pw = (jnp.ones_like(a_row[0]), jnp.zeros_like(a_row[0]))
    for t in range(tc - 1, -1, -1):
        r_re, r_im = _cmul(*pw, *bb_tl)
        wr_ref[0, t * lw:(t + 1) * lw, :] = jnp.concatenate([r_re, r_im], axis=1).astype(BF16)
        pw = _cmul(*pw, *a_row)
    a_ref[0, 0] = pw[0]
    a_ref[0, 1] = pw[1]


def _s5_weights(lam_re, lam_im, log_dt, b_re, b_im, c_re, c_im):
    g, ns = lam_re.shape
    ch, tc, gpb = S5_GROUP, S5_TC, S5_GPB
    nblk = g // gpb
    lw = gpb * ch
    lam = jnp.stack([lam_re, lam_im]).astype(F32)
    ldt = log_dt.astype(F32)
    bst = jnp.stack([b_re, b_im]).astype(F32)
    cst = jnp.stack([c_re, c_im]).astype(F32)
    lam_pg = jnp.repeat(lam.transpose(0, 2, 1), ch, axis=2)
    ldt_pg = jnp.repeat(ldt, ch)[None]
    c_pg = cst.transpose(0, 3, 1, 2).reshape(2, ns, g * ch)
    lam_gp = jnp.repeat(lam, ch, axis=1)
    ldt_gp = jnp.broadcast_to(jnp.repeat(ldt, ch)[:, None], (g * ch, ns))
    b_gp = bst.transpose(0, 1, 3, 2).reshape(2, g * ch, ns)
    lam_row = lam.reshape(2, 1, g * ns)
    ldt_row = jnp.repeat(ldt, ns)[None]
    b_tl = jnp.tile(b_gp, (1, 1, gpb))
    blk3 = lambda i: (0, 0, i)
    return pl.pallas_call(
        _s5_weights_kernel,
        name="s5_weights",
        grid=(nblk,),
        in_specs=[pl.BlockSpec((2, ns, lw), blk3),
                  pl.BlockSpec((1, lw), lambda i: (0, i)),
                  pl.BlockSpec((2, ns, lw), blk3),
                  pl.BlockSpec((2, lw, ns), lambda i: (0, i, 0)),
                  pl.BlockSpec((lw, ns), lambda i: (i, 0)),
                  pl.BlockSpec((2, lw, ns), lambda i: (0, i, 0)),
                  pl.BlockSpec((2, 1, gpb * ns), blk3),
                  pl.BlockSpec((1, gpb * ns), lambda i: (0, i)),
                  pl.BlockSpec((2, lw, gpb * ns), lambda i: (0, i, 0))],
        out_specs=[pl.BlockSpec((1, tc * lw, tc * lw), lambda i: (i, 0, 0)),
                   pl.BlockSpec((1, tc * lw, 2 * gpb * ns), lambda i: (i, 0, 0)),
                   pl.BlockSpec((1, 2 * gpb * ns, tc * lw), lambda i: (i, 0, 0)),
                   pl.BlockSpec((1, 2, 1, gpb * ns), lambda i: (i, 0, 0, 0))],
        out_shape=[jax.ShapeDtypeStruct((nblk, tc * lw, tc * lw), BF16),
                   jax.ShapeDtypeStruct((nblk, tc * lw, 2 * gpb * ns), BF16),
                   jax.ShapeDtypeStruct((nblk, 2 * gpb * ns, tc * lw), BF16),
                   jax.ShapeDtypeStruct((nblk, 2, 1, gpb * ns), F32)],
        compiler_params=_cparams(("arbitrary",)),
    )(lam_pg, ldt_pg, c_pg, lam_gp, ldt_gp, b_gp, lam_row, ldt_row, b_tl)


def _s5_mix_kernel(u_ref, wm_ref, wr_ref, wo_ref, a_ref, d_ref, y_ref, s_ref, x_ref, ys_ref):
    nk = u_ref.shape[1]
    half = s_ref.shape[1] // 2
    sub = 8
    u = u_ref[0]
    s_ref[...] = jnp.dot(u, wr_ref[0], preferred_element_type=F32)

    a1 = (a_ref[0, 0], a_ref[0, 1])
    a2 = _cmul(*a1, *a1)
    a4 = _cmul(*a2, *a2)
    a8 = _cmul(*a4, *a4)
    a3 = _cmul(*a2, *a1)
    pows = [(jnp.ones_like(a1[0]), jnp.zeros_like(a1[0])), a1, a2, a3, a4,
            _cmul(*a4, *a1), _cmul(*a4, *a2), _cmul(*a4, *a3)]
    p_re = jnp.concatenate([p[0] for p in pows], axis=0)
    p_im = jnp.concatenate([p[1] for p in pows], axis=0)
    rowid = lax.broadcasted_iota(jnp.int32, (sub, half), 0)
    levels = [(d, jnp.broadcast_to(a[0], (sub, half)), jnp.broadcast_to(a[1], (sub, half)))
              for d, a in ((1, a1), (2, a2), (4, a4))]

    def shift_rows(v, d):
        return jnp.where(rowid >= d, pltpu.roll(v, d, 0), 0.0)

    def tile(j, carry):
        cr, ci = carry
        rows = pl.ds(pl.multiple_of(j * sub, sub), sub)
        zr = s_ref[rows, :half]
        zi = s_ref[rows, half:]
        for d, br, bi in levels:
            sr, si = shift_rows(zr, d), shift_rows(zi, d)
            zr, zi = zr + br * sr - bi * si, zi + br * si + bi * sr
        x_ref[rows, :half] = p_re * cr - p_im * ci + shift_rows(zr, 1)
        x_ref[rows, half:] = p_re * ci + p_im * cr + shift_rows(zi, 1)
        return (a8[0] * cr - a8[1] * ci + zr[sub - 1:], a8[0] * ci + a8[1] * cr + zi[sub - 1:])

    zero = jnp.zeros((1, half), F32)
    lax.fori_loop(0, nk // sub, tile, (zero, zero))

    cw = 2 * LANES
    y = jnp.concatenate(
        [jnp.dot(u[:, :(n + 1) * cw], wm_ref[0, :(n + 1) * cw, n * cw:(n + 1) * cw],
                 preferred_element_type=F32) for n in range(u.shape[1] // cw)], axis=1)
    y = y + jnp.dot(x_ref[...].astype(BF16), wo_ref[0], preferred_element_type=F32)
    y = _gelu_tanh(y + d_ref[0] * u.astype(F32))
    for t in range(S5_TC):
        ys_ref[pl.ds(t, nk, stride=S5_TC), :] = y[:, t * LANES:(t + 1) * LANES]
    y_ref[0] = ys_ref[...].astype(y_ref.dtype)


def _s5_mixer(u3, nb, lam_re, lam_im, log_dt, b_re, b_im, c_re, c_im, d_skip):
    nblk, rows, w = u3.shape
    tc = S5_TC
    nk = rows // nb
    assert nk % 8 == 0
    wm, wr, wo, a_pow = _s5_weights(lam_re, lam_im, log_dt, b_re, b_im, c_re, c_im)
    ns = wr.shape[-1]
    d_rows = jnp.tile(d_skip.astype(F32).reshape(nblk, 1, LANES), (1, 1, tc))
    blk = lambda i, b: (i, 0, 0)
    return pl.pallas_call(
        _s5_mix_kernel,
        name="s5_mix",
        grid=(nblk, nb),
        in_specs=[pl.BlockSpec((1, nk, w), lambda i, b: (i, b, 0)),
                  pl.BlockSpec((1, w, w), blk),
                  pl.BlockSpec((1, w, ns), blk),
                  pl.BlockSpec((1, ns, w), blk),
                  pl.BlockSpec((1, 2, 1, ns // 2), lambda i, b: (i, 0, 0, 0)),
                  pl.BlockSpec((1, 1, w), blk)],
        out_specs=pl.BlockSpec((1, nk * tc, LANES), lambda i, b: (i, b, 0)),
        out_shape=jax.ShapeDtypeStruct((nblk, rows * tc, LANES), BF16),
        scratch_shapes=[pltpu.VMEM((nk, ns), F32), pltpu.VMEM((nk, ns), F32),
                        pltpu.VMEM((nk * tc, LANES), F32)],
        compiler_params=_cparams(("arbitrary", "arbitrary")),
    )(u3, wm, wr, wo, a_pow, d_rows)


def _even_out_kernel(x_ref, oa_ref, y_ref, zb_ref, wg_ref, bg_ref, wo_ref, g_ref, wf_ref,
                     x1_ref, hn_ref, f_ref):
    y = jnp.concatenate([y_ref[i] for i in range(y_ref.shape[0])], axis=1)
    t = jnp.dot(y, wg_ref[...], preferred_element_type=F32) + bg_ref[...]
    ob = y.astype(F32) * jax.nn.sigmoid(t) * _silu(zb_ref[...].astype(F32))
    wa = oa_ref.shape[1]
    acc = jnp.dot(oa_ref[...], wo_ref[:wa, :], preferred_element_type=F32)
    acc = acc + jnp.dot(ob.astype(BF16), wo_ref[wa:, :], preferred_element_type=F32)
    x1 = x_ref[...] + acc
    x1_ref[...] = x1
    hn = _rms(x1, g_ref[...]).astype(BF16)
    hn_ref[...] = hn
    f_ref[...] = lax.dot_general(wf_ref[...], hn, (((1,), (1,)), ((), ())),
                                 preferred_element_type=F32)


def _even_out(x2, oa, y, proj, w_glu, b_glu, w_out, g_next, wf_t):
    m, d = x2.shape
    nh = wf_t.shape[0]
    wa = oa.shape[1]
    nblk, _, lw = y.shape
    wb = nblk * lw
    tm = min(TM_OUT, m)
    zb_col = (proj.shape[1] - wb) // wb
    const = lambda i: (0, 0)
    return pl.pallas_call(
        _even_out_kernel,
        name="even_out",
        grid=(m // tm,),
        in_specs=[pl.BlockSpec((tm, d), lambda i: (i, 0)),
                  pl.BlockSpec((tm, wa), lambda i: (i, 0)),
                  pl.BlockSpec((nblk, tm, lw), lambda i: (0, i, 0)),
                  pl.BlockSpec((tm, wb), lambda i: (i, zb_col)),
                  pl.BlockSpec((wb, wb), const),
                  pl.BlockSpec((1, wb), const),
                  pl.BlockSpec((wa + wb, d), const),
                  pl.BlockSpec((1, d), const),
                  pl.BlockSpec((nh, d), const)],
        out_specs=[pl.BlockSpec((tm, d), lambda i: (i, 0)),
                   pl.BlockSpec((tm, d), lambda i: (i, 0)),
                   pl.BlockSpec((nh, tm), lambda i: (0, i))],
        out_shape=[jax.ShapeDtypeStruct((m, d), F32),
                   jax.ShapeDtypeStruct((m, d), BF16),
                   jax.ShapeDtypeStruct((nh, m), F32)],
        compiler_params=_cparams(("arbitrary",)),
    )(x2, oa, y, proj, w_glu, b_glu.reshape(1, wb), w_out, g_next.reshape(1, d), wf_t)


def _cum_forget_kernel(f_ref, b_ref, c_ref):
    x = f_ref[...] + b_ref[...]
    v = -(jnp.maximum(-x, 0.0) + jnp.log1p(jnp.exp(-jnp.abs(x))))
    n = v.shape[-1]
    pos = lax.broadcasted_iota(jnp.int32, v.shape, 1)
    d = 1
    while d < n:
        v = v + jnp.where(pos >= d, pltpu.roll(v, d, 1), 0.0)
        d *= 2
    c_ref[...] = v


def _cum_forget(f_t, b_forget, seq):
    nh, m = f_t.shape
    return pl.pallas_call(
        _cum_forget_kernel,
        name="cum_forget",
        grid=(m // seq,),
        in_specs=[pl.BlockSpec((nh, seq), lambda i: (0, i)),
                  pl.BlockSpec((nh, 1), lambda i: (0, 0))],
        out_specs=pl.BlockSpec((nh, seq), lambda i: (0, i)),
        out_shape=jax.ShapeDtypeStruct((nh, m), F32),
        compiler_params=_cparams(("arbitrary",)),
    )(f_t, b_forget.astype(F32).reshape(nh, 1))


def _split3(x):
    hi = x.astype(BF16).astype(F32)
    r = x - hi
    mid = r.astype(BF16).astype(F32)
    lo = (r - mid).astype(BF16).astype(F32)
    return hi, mid, lo


def _fox_kernel(q0_ref, qn_ref, k_ref, v_ref, z_ref, c_ref, o_ref,
                ke_ref, vt_ref, qa_ref, s_ref, mb_ref, acc_ref):
    qi = pl.program_id(2)
    nq = pl.num_programs(2)
    tq = qn_ref.shape[1]
    tk = tq
    dh = C_DH
    nh = c_ref.shape[1]
    nblk = k_ref.shape[1] // tk
    nt = (((1,), (1,)), ((), ()))
    key = lax.broadcasted_iota(jnp.int32, (tk, tq), 0)
    qry = lax.broadcasted_iota(jnp.int32, (tk, tq), 1)

    def cols(h):
        return slice(h * dh, (h + 1) * dh)

    def prep_queries(q_ref, qb):
        qs = pl.multiple_of(qb * tq, tq)
        lane = lax.broadcasted_iota(jnp.int32, (tq, dh), 1)
        for h in range(nh):
            c_base = jnp.max(c_ref[0, h, :, pl.ds(qs, tq)], axis=1, keepdims=True) * LOG2E
            bh, bm, bl = _split3(c_base)
            g = jnp.where(lane < 3, 1.0, jnp.where(lane == 3, bh, jnp.where(lane == 4, bm,
                          jnp.where(lane == 5, bl, 0.0))))
            qa_ref[h, :, :dh] = (q_ref[0, :, cols(h)].astype(F32) * (LOG2E / math.sqrt(dh))
                                 ).astype(BF16)
            qa_ref[h, :, dh:] = g.astype(BF16)

    def produce(h, j, keep):
        ks = pl.multiple_of(j * tk, tk)
        ka = jnp.concatenate([k_ref[0, pl.ds(ks, tk), cols(h)], ke_ref[h, pl.ds(ks, tk), :]],
                             axis=1)
        s = lax.dot_general(ka, qa_ref[h], nt, preferred_element_type=F32)
        if keep is not None:
            s = jnp.where(keep, s, NEG)
        s_ref[h] = s
        return jnp.max(s, axis=0, keepdims=True)

    def consume(h, j, stats):
        m, mb = stats
        ks = pl.multiple_of(j * tk, tk)
        m2 = jnp.maximum(m, mb)
        p = jnp.exp2(s_ref[h] - m2).astype(BF16)
        pv = jnp.dot(vt_ref[h, :, pl.ds(ks, tk)], p, preferred_element_type=F32)
        acc_ref[h] = jnp.exp2(m - m2) * acc_ref[h] + pv
        return m2

    def step(j, carry, diagonal_next):
        keep = (key <= qry) if diagonal_next else None
        return tuple((consume(h, j, stats), produce(h, j + 1, keep))
                     for h, stats in enumerate(carry))

    @pl.when(qi == 0)
    def _():
        r = lax.broadcasted_iota(jnp.int32, (dh, dh), 0)
        c = lax.broadcasted_iota(jnp.int32, (dh, dh), 1)
        eye = (r == c).astype(BF16)
        row = lax.broadcasted_iota(jnp.int32, (dh, tk), 0)
        ones_row = (lax.broadcasted_iota(jnp.int32, (FOX_PAD, tk), 0) == 0).astype(BF16)

        def blk(j, carry):
            ks = pl.multiple_of(j * tk, tk)
            for h in range(nh):
                hi, mid, lo = _split3(c_ref[0, h, :, pl.ds(ks, tk)] * (-LOG2E))
                a = jnp.where(row == 0, hi, jnp.where(row == 1, mid, jnp.where(row == 2, lo,
                              jnp.where(row < 6, 1.0, 0.0))))
                ke_ref[h, pl.ds(ks, tk), :] = a.T.astype(BF16)
                vt = lax.dot_general(eye, v_ref[0, pl.ds(ks, tk), cols(h)], nt,
                                     preferred_element_type=F32)
                vt_ref[h, :dh, pl.ds(ks, tk)] = vt.astype(BF16)
                vt_ref[h, dh:, pl.ds(ks, tk)] = ones_row
            return carry

        lax.fori_loop(0, nblk, blk, 0)
        prep_queries(q0_ref, 0)
        for h in range(nh):
            mb_ref[h] = produce(h, 0, key <= qry)

    carry = []
    for h in range(nh):
        acc_ref[h] = jnp.zeros(acc_ref.shape[1:], F32)
        carry.append((jnp.full((1, tq), NEG, F32), mb_ref[h]))
    plain = functools.partial(step, diagonal_next=False)
    n_plain = jnp.maximum(qi - 1, 0)
    n_pairs = n_plain // 2
    carry = lax.fori_loop(0, n_pairs, lambda i, c: plain(2 * i + 1, plain(2 * i, c)),
                          tuple(carry))
    carry = lax.fori_loop(2 * n_pairs, n_plain, plain, carry)
    carry = lax.fori_loop(n_plain, qi, functools.partial(step, diagonal_next=True), carry)
    for h, stats in enumerate(carry):
        consume(h, qi, stats)
    prep_queries(qn_ref, jnp.minimum(qi + 1, nq - 1))
    for h in range(nh):
        mb_ref[h] = produce(h, 0, None)
    for h in range(nh):
        acc = acc_ref[h]
        o = (acc[:dh] / acc[dh:dh + 1]).T
        o_ref[0, :, cols(h)] = (o * _silu(z_ref[0, :, cols(h)].astype(F32))).astype(o_ref.dtype)


def _fox(proj3, cum4):
    b, l, _ = proj3.shape
    ng = C_HEADS // FOX_HEADS
    w = FOX_HEADS * C_DH
    tq = min(FOX_TQ, l)
    nq = l // tq
    return pl.pallas_call(
        _fox_kernel,
        name="fox",
        grid=(b, ng, nq),
        in_specs=[pl.BlockSpec((1, tq, w), lambda bi, gi, qi: (bi, 0, gi)),
                  pl.BlockSpec((1, tq, w), lambda bi, gi, qi: (bi, jnp.minimum(qi + 1, nq - 1), gi)),
                  pl.BlockSpec((1, l, w), lambda bi, gi, qi: (bi, 0, ng + gi)),
                  pl.BlockSpec((1, l, w), lambda bi, gi, qi: (bi, 0, 2 * ng + gi)),
                  pl.BlockSpec((1, tq, w), lambda bi, gi, qi: (bi, qi, 3 * ng + gi)),
                  pl.BlockSpec((1, FOX_HEADS, 1, l), lambda bi, gi, qi: (bi, gi, 0, 0))],
        out_specs=pl.BlockSpec((1, tq, w), lambda bi, gi, qi: (bi, qi, gi)),
        out_shape=jax.ShapeDtypeStruct((b, l, C_HEADS * C_DH), BF16),
        scratch_shapes=[pltpu.VMEM((FOX_HEADS, l, C_DH), BF16),
                        pltpu.VMEM((FOX_HEADS, C_DH + FOX_PAD, l), BF16),
                        pltpu.VMEM((FOX_HEADS, tq, 2 * C_DH), BF16),
                        pltpu.VMEM((FOX_HEADS, tq, tq), F32),
                        pltpu.VMEM((FOX_HEADS, 1, tq), F32),
                        pltpu.VMEM((FOX_HEADS, C_DH + FOX_PAD, tq), F32)],
        compiler_params=_cparams(("arbitrary", "arbitrary", "arbitrary")),
    )(proj3, proj3, proj3, proj3, proj3, cum4)


def _odd_out_kernel(x_ref, o_ref, w_ref, g_ref, out_ref):
    x2 = x_ref[...] + jnp.dot(o_ref[...], w_ref[...], preferred_element_type=F32)
    out_ref[...] = _rms(x2, g_ref[...])


def _odd_out(x1, o, w_out, g):
    m, d = x1.shape
    k = o.shape[1]
    tm = min(TM_OUT, m)
    return pl.pallas_call(
        _odd_out_kernel,
        name="odd_out",
        grid=(m // tm,),
        in_specs=[pl.BlockSpec((tm, d), lambda i: (i, 0)),
                  pl.BlockSpec((tm, k), lambda i: (i, 0)),
                  pl.BlockSpec((k, d), lambda i: (0, 0)),
                  pl.BlockSpec((1, d), lambda i: (0, 0))],
        out_specs=pl.BlockSpec((tm, d), lambda i: (i, 0)),
        out_shape=jax.ShapeDtypeStruct((m, d), F32),
        compiler_params=_cparams(("arbitrary",)),
    )(x1, o, w_out, g.reshape(1, d))


def kernel(x, norm_even_g, w_in_even, rel_bias, s5_lambda_re, s5_lambda_im, s5_log_dt,
           s5_b_re, s5_b_im, s5_c_re, s5_c_im, s5_d, w_glu, b_glu, w_out_even,
           norm_odd_g, w_in_odd, b_forget, w_out_odd, final_norm_g):
    b, l, d = x.shape
    assert norm_even_g.shape[0] == 1 and norm_odd_g.shape[0] == 1, "one even + one odd layer"
    assert l % FOX_TQ == 0 and l % VT_BLK == 0 and l % A_QB == 0 and l >= A_KB
    m = b * l
    a_w = A_HEADS * A_DH
    s5_w = s5_d.shape[1] * S5_GROUP
    c_w = C_HEADS * C_DH

    x2 = x.reshape(m, d)
    assert s5_w == TN_PROJ and l % S5_TC == 0
    proj, u3 = _norm_proj(x2, norm_even_g[0], w_in_even[0].astype(BF16), 3 * a_w)
    proj3 = proj.reshape(b, l, proj.shape[1])
    o_a = _attn_a(proj3, rel_bias[0])
    y = _s5_mixer(u3, b, s5_lambda_re[0], s5_lambda_im[0], s5_log_dt[0], s5_b_re[0],
                  s5_b_im[0], s5_c_re[0], s5_c_im[0], s5_d[0])
    w_odd_t = w_in_odd[0].T
    wf_t = w_odd_t[4 * c_w:].astype(BF16)
    x1, hn1, f_t = _even_out(x2, o_a.reshape(m, a_w), y, proj,
                             w_glu[0].astype(BF16), b_glu[0].astype(F32),
                             w_out_even[0].astype(BF16), norm_odd_g[0], wf_t)
    proj1 = _proj_odd(hn1, w_odd_t, 4 * c_w)
    cum = _cum_forget(f_t, b_forget[0], l)
    cum4 = cum.reshape(C_HEADS, b, 1, l).transpose(1, 0, 2, 3)
    o_c = _fox(proj1.reshape(b, l, 4 * c_w), cum4)
    out = _odd_out(x1, o_c.reshape(m, c_w), w_out_odd[0].astype(BF16), final_norm_g)
    return out.reshape(b, l, d)
```

```python
import functools
import math

import jax
import jax.numpy as jnp
from jax import lax
from jax.experimental import pallas as pl
from jax.experimental.pallas import tpu as pltpu

F32 = jnp.float32
BF16 = jnp.bfloat16

EPS = 1e-6
NEG = -1e30
LOG2E = math.log2(math.e)

CHUNK = 64
N_LEFT = 8
A_HEADS = 16
A_DH = 64
MAX_REL = 128
S5_GROUP = 16
S5_STATE = 64
C_HEADS = 16
C_DH = 128

LANES = 128
VMEM_LIMIT = 56 * 1024 * 1024

TM_PROJ = 1024
TN_PROJ = 1024
TM_OUT = 512
A_QB = 2 * CHUNK
A_KB = A_QB + N_LEFT * CHUNK
A_TW = A_KB + N_LEFT * CHUNK
A_PAD = 16
VT_BLK = 512
S5_TC = 16
S5_GPB = LANES // S5_GROUP
FOX_TQ = 512
FOX_HEADS = 4
FOX_PAD = 16


def _cparams(sem):
    return pltpu.CompilerParams(dimension_semantics=sem, vmem_limit_bytes=VMEM_LIMIT)


def _rms(x, g):
    ms = jnp.mean(x * x, axis=-1, keepdims=True)
    return x * lax.rsqrt(ms + EPS) * g


def _silu(x):
    return x * jax.nn.sigmoid(x)


def _gelu_tanh(x):
    c = math.sqrt(2.0 / math.pi)
    return 0.5 * x * (1.0 + jnp.tanh(c * (x + 0.044715 * (x * x * x))))


def _norm_proj_kernel(x_ref, g_ref, w_ref, o_ref, u_ref, hn_ref, us_ref, *, u_tile):
    j = pl.program_id(1)

    @pl.when(j == 0)
    def _():
        hn_ref[...] = _rms(x_ref[...], g_ref[...]).astype(BF16)

    acc = jnp.dot(hn_ref[...], w_ref[...], preferred_element_type=F32)
    o_ref[...] = acc.astype(o_ref.dtype)

    @pl.when(j == u_tile)
    def _():
        nchunk = us_ref.shape[1] // S5_TC
        for blk in range(u_ref.shape[0]):
            us_ref[blk] = acc[:, blk * LANES:(blk + 1) * LANES]
            for t in range(S5_TC):
                rows = us_ref[blk, pl.ds(t, nchunk, stride=S5_TC), :]
                u_ref[blk, :, t * LANES:(t + 1) * LANES] = rows.astype(u_ref.dtype)


def _norm_proj(x2, g, w, u_start):
    m, d = x2.shape
    n = w.shape[1]
    tm, tn = min(TM_PROJ, m), TN_PROJ
    assert u_start % tn == 0 and tm % (16 * S5_TC) == 0
    nblk = tn // LANES
    return pl.pallas_call(
        functools.partial(_norm_proj_kernel, u_tile=u_start // tn),
        name="norm_proj",
        grid=(m // tm, n // tn),
        in_specs=[pl.BlockSpec((tm, d), lambda i, j: (i, 0)),
                  pl.BlockSpec((1, d), lambda i, j: (0, 0)),
                  pl.BlockSpec((d, tn), lambda i, j: (0, j))],
        out_specs=[pl.BlockSpec((tm, tn), lambda i, j: (i, j)),
                   pl.BlockSpec((nblk, tm // S5_TC, S5_TC * LANES), lambda i, j: (0, i, 0))],
        out_shape=[jax.ShapeDtypeStruct((m, n), BF16),
                   jax.ShapeDtypeStruct((nblk, m // S5_TC, S5_TC * LANES), BF16)],
        scratch_shapes=[pltpu.VMEM((tm, d), BF16), pltpu.VMEM((nblk, tm, LANES), F32)],
        compiler_params=_cparams(("arbitrary", "arbitrary")),
    )(x2, g.reshape(1, d), w)


def _proj_odd_kernel(h_ref, wt_ref, o_ref, wb_ref):
    @pl.when(pl.program_id(1) == 0)
    def _():
        wb_ref[...] = wt_ref[...].T.astype(BF16)

    o_ref[...] = jnp.dot(h_ref[...], wb_ref[...],
                         preferred_element_type=F32).astype(o_ref.dtype)


def _proj_odd(hn, wt, n):
    m, d = hn.shape
    tm, tn = min(TM_PROJ, m), TN_PROJ
    assert n % tn == 0
    return pl.pallas_call(
        _proj_odd_kernel,
        name="proj_odd",
        grid=(n // tn, m // tm),
        in_specs=[pl.BlockSpec((tm, d), lambda j, i: (i, 0)),
                  pl.BlockSpec((tn, d), lambda j, i: (j, 0))],
        out_specs=pl.BlockSpec((tm, tn), lambda j, i: (i, j)),
        out_shape=jax.ShapeDtypeStruct((m, n), BF16),
        scratch_shapes=[pltpu.VMEM((d, tn), BF16)],
        compiler_params=_cparams(("arbitrary", "arbitrary")),
    )(hn, wt)


def _attn_a_kernel(q_ref, k_ref, v_ref, z_ref, e_ref, o_ref, tt_ref, vt_ref, s_ref, pv_ref):
    l_all = k_ref.shape[1]
    w = q_ref.shape[2]
    nt = (((1,), (1,)), ((), ()))
    band = N_LEFT * CHUNK

    @pl.when(pl.program_id(1) == 0)
    def _():
        i = lax.broadcasted_iota(jnp.int32, (A_QB, A_TW), 0)
        m = lax.broadcasted_iota(jnp.int32, (A_QB, A_TW), 1)
        dchunk = m // CHUNK - i // CHUNK
        ok = (dchunk >= 0) & (dchunk <= N_LEFT)
        for h in range(2):
            rows = jnp.broadcast_to(e_ref[0, h:h + 1, :], (A_QB, A_QB + A_TW))
            toep = pltpu.roll(rows, 0, 1, stride=1, stride_axis=0)[:, A_QB:]
            t = jnp.where(ok, toep * LOG2E, NEG)
            tt_ref[:, h * A_QB:(h + 1) * A_QB] = t.T

    r = lax.broadcasted_iota(jnp.int32, (w, w), 0)
    c = lax.broadcasted_iota(jnp.int32, (w, w), 1)
    eye = (r == c).astype(BF16)
    ones_row = (lax.broadcasted_iota(jnp.int32, (A_PAD, VT_BLK), 0) == 0).astype(BF16)

    def vt_blk(j, carry):
        ks = pl.multiple_of(j * VT_BLK, VT_BLK)
        vt = lax.dot_general(eye, v_ref[0, pl.ds(ks, VT_BLK), :], nt, preferred_element_type=F32)
        for h in range(2):
            vt_ref[h, :A_DH, pl.ds(ks, VT_BLK)] = vt[h * A_DH:(h + 1) * A_DH].astype(BF16)
            vt_ref[h, A_DH:, pl.ds(ks, VT_BLK)] = ones_row
        return carry

    lax.fori_loop(0, l_all // VT_BLK, vt_blk, 0)

    head_row = lax.broadcasted_iota(jnp.int32, (w, A_QB), 0)
    qscale = LOG2E / math.sqrt(A_DH)

    def window(qb):
        start = qb * A_QB - band
        start_c = jnp.maximum(start, 0)
        return pl.multiple_of(start_c, LANES), pl.multiple_of(start_c - start, LANES)

    def produce(qb, slot):
        qs = pl.multiple_of(qb * A_QB, A_QB)
        start_c, off = window(qb)
        qt = (q_ref[0, pl.ds(qs, A_QB), :].astype(F32) * qscale).T.astype(BF16)
        zero = jnp.zeros_like(qt)
        qbd = jnp.concatenate([jnp.where(head_row < A_DH, qt, zero),
                               jnp.where(head_row >= A_DH, qt, zero)], axis=1)
        kb = k_ref[0, pl.ds(start_c, A_KB), :]
        s = jnp.dot(kb, qbd, preferred_element_type=F32)
        s = s + tt_ref[pl.ds(off, A_KB), :]
        s_ref[slot] = s
        return jnp.max(s, axis=0, keepdims=True)

    def consume(qb, slot, m):
        start_c, _ = window(qb)
        pb = jnp.exp2(s_ref[slot] - m).astype(BF16)
        for h in range(2):
            vt = vt_ref[h, :, pl.ds(start_c, A_KB)]
            pv_ref[slot, h] = jnp.dot(vt, pb[:, h * A_QB:(h + 1) * A_QB],
                                      preferred_element_type=F32)

    def finish(qb, slot):
        qs = pl.multiple_of(qb * A_QB, A_QB)
        outs = []
        for h in range(2):
            oh = pv_ref[slot, h]
            outs.append(oh[:A_DH] * (1.0 / oh[A_DH:A_DH + 1]))
        o = jnp.concatenate(outs, axis=0).T
        gate = _silu(z_ref[0, pl.ds(qs, A_QB), :].astype(F32))
        o_ref[0, pl.ds(qs, A_QB), :] = (o * gate).astype(o_ref.dtype)

    def pair(i, m, lead=True):
        j = 2 * i
        if lead:
            finish(j - 1, 1)
        m1 = produce(j + 1, 1)
        consume(j, 0, m)
        m2 = produce(j + 2, 0)
        finish(j, 0)
        consume(j + 1, 1, m1)
        return m2

    nb = l_all // A_QB
    npairs = (nb - 1) // 2
    m = pair(0, produce(0, 0), lead=False)
    m = lax.fori_loop(1, npairs, pair, m)
    last = 2 * npairs
    finish(last - 1, 1)
    if (nb - 1) % 2:
        m1 = produce(last + 1, 1)
        consume(last, 0, m)
        finish(last, 0)
        consume(last + 1, 1, m1)
        finish(last + 1, 1)
    else:
        consume(last, 0, m)
        finish(last, 0)


def _attn_a(proj3, rel_bias):
    b, l, _ = proj3.shape
    hp = A_HEADS // 2
    width = 2 * A_DH
    ncol = A_HEADS * A_DH // width
    zcol = (3 * A_HEADS * A_DH + (A_HEADS * A_DH)) // width
    left = N_LEFT * CHUNK + A_QB - MAX_REL
    right = A_QB + A_TW - left - (2 * MAX_REL + 1)
    ext = jnp.pad(rel_bias.astype(F32)[:, ::-1], ((0, 0), (left, right)), mode="edge")
    ext = ext.reshape(hp, 2, A_QB + A_TW)
    seq = lambda h, bi: (bi, 0, h)
    return pl.pallas_call(
        _attn_a_kernel,
        name="attn_a",
        grid=(hp, b),
        in_specs=[pl.BlockSpec((1, l, width), seq),
                  pl.BlockSpec((1, l, width), lambda h, bi: (bi, 0, ncol + h)),
                  pl.BlockSpec((1, l, width), lambda h, bi: (bi, 0, 2 * ncol + h)),
                  pl.BlockSpec((1, l, width), lambda h, bi: (bi, 0, zcol + h)),
                  pl.BlockSpec((1, 2, A_QB + A_TW), lambda h, bi: (h, 0, 0))],
        out_specs=pl.BlockSpec((1, l, width), seq),
        out_shape=jax.ShapeDtypeStruct((b, l, A_HEADS * A_DH), BF16),
        scratch_shapes=[pltpu.VMEM((A_TW, 2 * A_QB), F32),
                        pltpu.VMEM((2, A_DH + A_PAD, l), BF16),
                        pltpu.VMEM((2, A_KB, 2 * A_QB), F32),
                        pltpu.VMEM((2, 2, A_DH + A_PAD, A_QB), F32)],
        compiler_params=_cparams(("arbitrary", "arbitrary")),
    )(proj3, proj3, proj3, proj3, ext)


def _cmul(ar, ai, br, bi):
    return ar * br - ai * bi, ar * bi + ai * br


def _s5_discretise(lr, li, log_dt):
    dt = jnp.exp(log_dt)
    mag = jnp.exp(lr * dt)
    ang = li * dt
    a_re = mag * jnp.cos(ang)
    a_im = mag * jnp.sin(ang)
    den = lr * lr + li * li
    nr = a_re - 1.0
    coef_re = (nr * lr + a_im * li) / den
    coef_im = (a_im * lr - nr * li) / den
    return (a_re, a_im), (coef_re, coef_im)


def _s5_weights_kernel(lam_pg_ref, ldt_pg_ref, c_pg_ref, lam_gp_ref, ldt_gp_ref, b_gp_ref,
                       lam_row_ref, ldt_row_ref, b_tl_ref, wm_ref, wr_ref, wo_ref, a_ref):
    tc, ch, ns, gpb = S5_TC, S5_GROUP, S5_STATE, S5_GPB
    lw = gpb * ch
    hi = lax.Precision.HIGHEST

    a_pg, _ = _s5_discretise(lam_pg_ref[0], lam_pg_ref[1], ldt_pg_ref[...])
    pw = (jnp.ones_like(a_pg[0]), jnp.zeros_like(a_pg[0]))
    ca = []
    for n in range(tc + 1):
        ca.append(_cmul(c_pg_ref[0], c_pg_ref[1], *pw))
        pw = _cmul(*pw, *a_pg)

    _, coef_gp = _s5_discretise(lam_gp_ref[0], lam_gp_ref[1], ldt_gp_ref[...])
    bb_gp = _cmul(*coef_gp, b_gp_ref[0], b_gp_ref[1])
    ca_re = jnp.concatenate([ca[n][0] for n in range(tc)], axis=1)
    ca_im = jnp.concatenate([ca[n][1] for n in range(tc)], axis=1)
    z = (jnp.dot(bb_gp[0], ca_re, precision=hi, preferred_element_type=F32)
         - jnp.dot(bb_gp[1], ca_im, precision=hi, preferred_element_type=F32))
    row_g = lax.broadcasted_iota(jnp.int32, z.shape, 0) // ch
    col_g = (lax.broadcasted_iota(jnp.int32, z.shape, 1) % lw) // ch
    z = jnp.where(row_g == col_g, z, 0.0).astype(BF16)
    for t in range(tc):
        parts = [z[:, :(tc - t) * lw]]
        if t:
            parts.insert(0, jnp.zeros((lw, t * lw), BF16))
        wm_ref[0, t * lw:(t + 1) * lw, :] = jnp.concatenate(parts, axis=1)

    rg = lax.broadcasted_iota(jnp.int32, (gpb * ns, lw), 0) // ns
    cg = lax.broadcasted_iota(jnp.int32, (gpb * ns, lw), 1) // ch
    keep = rg == cg
    for t2 in range(tc):
        cr, ci = ca[t2 + 1]
        cr = jnp.where(keep, jnp.concatenate([cr] * gpb, axis=0), 0.0)
        ci = jnp.where(keep, jnp.concatenate([ci] * gpb, axis=0), 0.0)
        wo_ref[0, :gpb * ns, t2 * lw:(t2 + 1) * lw] = cr.astype(BF16)
        wo_ref[0, gpb * ns:, t2 * lw:(t2 + 1) * lw] = (-ci).astype(BF16)

    a_row, coef_row = _s5_discretise(lam_row_ref[0], lam_row_ref[1], ldt_row_ref[...])
    bb_tl = _cmul(*coef_row, b_tl_ref[0], b_tl_ref[1])
    rg = lax.broadcasted_iota(jnp.int32, bb_tl[0].shape, 0) // ch
    cg = lax.broadcasted_iota(jnp.int32, bb_tl[0].shape, 1) // ns
    bb_tl = (jnp.where(rg == cg, bb_tl[0], 0.0), jnp.where(rg == cg, bb_tl[1], 0.0))
    pw = (jnp.ones_like(a_row[0]), jnp.zeros_like(a_row[0]))
    for t in range(tc - 1, -1, -1):
        r_re, r_im = _cmul(*pw, *bb_tl)
        wr_ref[0, t * lw:(t + 1) * lw, :] = jnp.concatenate([r_re, r_im], axis=1).astype(BF16)
        pw = _cmul(*pw, *a_row)
    a_ref[0, 0] = pw[0]
    a_ref[0, 1] = pw[1]


def _s5_weights(lam_re, lam_im, log_dt, b_re, b_im, c_re, c_im):
    g, ns = lam_re.shape
    ch, tc, gpb = S5_GROUP, S5_TC, S5_GPB
    nblk = g // gpb
    lw = gpb * ch
    lam = jnp.stack([lam_re, lam_im]).astype(F32)
    ldt = log_dt.astype(F32)
    bst = jnp.stack([b_re, b_im]).astype(F32)
    cst = jnp.stack([c_re, c_im]).astype(F32)
    lam_pg = jnp.repeat(lam.transpose(0, 2, 1), ch, axis=2)
    ldt_pg = jnp.repeat(ldt, ch)[None]
    c_pg = cst.transpose(0, 3, 1, 2).reshape(2, ns, g * ch)
    lam_gp = jnp.repeat(lam, ch, axis=1)
    ldt_gp = jnp.broadcast_to(jnp.repeat(ldt, ch)[:, None], (g * ch, ns))
    b_gp = bst.transpose(0, 1, 3, 2).reshape(2, g * ch, ns)
    lam_row = lam.reshape(2, 1, g * ns)
    ldt_row = jnp.repeat(ldt, ns)[None]
    b_tl = jnp.tile(b_gp, (1, 1, gpb))
    blk3 = lambda i: (0, 0, i)
    return pl.pallas_call(
        _s5_weights_kernel,
        name="s5_weights",
        grid=(nblk,),
        in_specs=[pl.BlockSpec((2, ns, lw), blk3),
                  pl.BlockSpec((1, lw), lambda i: (0, i)),
                  pl.BlockSpec((2, ns, lw), blk3),
                  pl.BlockSpec((2, lw, ns), lambda i: (0, i, 0)),
                  pl.BlockSpec((lw, ns), lambda i: (i, 0)),
                  pl.BlockSpec((2, lw, ns), lambda i: (0, i, 0)),
                  pl.BlockSpec((2, 1, gpb * ns), blk3),
                  pl.BlockSpec((1, gpb * ns), lambda i: (0, i)),
                  pl.BlockSpec((2, lw, gpb * ns), lambda i: (0, i, 0))],
        out_specs=[pl.BlockSpec((1, tc * lw, tc * lw), lambda i: (i, 0, 0)),
                   pl.BlockSpec((1, tc * lw, 2 * gpb * ns), lambda i: (i, 0, 0)),
                   pl.BlockSpec((1, 2 * gpb * ns, tc * lw), lambda i: (i, 0, 0)),
                   pl.BlockSpec((1, 2, 1, gpb * ns), lambda i: (i, 0, 0, 0))],
        out_shape=[jax.ShapeDtypeStruct((nblk, tc * lw, tc * lw), BF16),
                   jax.ShapeDtypeStruct((nblk, tc * lw, 2 * gpb * ns), BF16),
                   jax.ShapeDtypeStruct((nblk, 2 * gpb * ns, tc * lw), BF16),
                   jax.ShapeDtypeStruct((nblk, 2, 1, gpb * ns), F32)],
        compiler_params=_cparams(("arbitrary",)),
    )(lam_pg, ldt_pg, c_pg, lam_gp, ldt_gp, b_gp, lam_row, ldt_row, b_tl)


def _s5_mix_kernel(u_ref, wm_ref, wr_ref, wo_ref, a_ref, d_ref, y_ref, s_ref, x_ref, ys_ref):
    nk = u_ref.shape[1]
    half = s_ref.shape[1] // 2
    sub = 8
    u = u_ref[0]
    s_ref[...] = jnp.dot(u, wr_ref[0], preferred_element_type=F32)

    a1 = (a_ref[0, 0], a_ref[0, 1])
    a2 = _cmul(*a1, *a1)
    a4 = _cmul(*a2, *a2)
    a8 = _cmul(*a4, *a4)
    a3 = _cmul(*a2, *a1)
    pows = [(jnp.ones_like(a1[0]), jnp.zeros_like(a1[0])), a1, a2, a3, a4,
            _cmul(*a4, *a1), _cmul(*a4, *a2), _cmul(*a4, *a3)]
    p_re = jnp.concatenate([p[0] for p in pows], axis=0)
    p_im = jnp.concatenate([p[1] for p in pows], axis=0)
    rowid = lax.broadcasted_iota(jnp.int32, (sub, half), 0)
    levels = [(d, jnp.broadcast_to(a[0], (sub, half)), jnp.broadcast_to(a[1], (sub, half)))
              for d, a in ((1, a1), (2, a2), (4, a4))]

    def shift_rows(v, d):
        return jnp.where(rowid >= d, pltpu.roll(v, d, 0), 0.0)

    def tile(j, carry):
        cr, ci = carry
        rows = pl.ds(pl.multiple_of(j * sub, sub), sub)
        zr = s_ref[rows, :half]
        zi = s_ref[rows, half:]
        for d, br, bi in levels:
            sr, si = shift_rows(zr, d), shift_rows(zi, d)
            zr, zi = zr + br * sr - bi * si, zi + br * si + bi * sr
        x_ref[rows, :half] = p_re * cr - p_im * ci + shift_rows(zr, 1)
        x_ref[rows, half:] = p_re * ci + p_im * cr + shift_rows(zi, 1)
        return (a8[0] * cr - a8[1] * ci + zr[sub - 1:], a8[0] * ci + a8[1] * cr + zi[sub - 1:])

    zero = jnp.zeros((1, half), F32)
    lax.fori_loop(0, nk // sub, tile, (zero, zero))

    cw = 2 * LANES
    y = jnp.concatenate(
        [jnp.dot(u[:, :(n + 1) * cw], wm_ref[0, :(n + 1) * cw, n * cw:(n + 1) * cw],
                 preferred_element_type=F32) for n in range(u.shape[1] // cw)], axis=1)
    y = y + jnp.dot(x_ref[...].astype(BF16), wo_ref[0], preferred_element_type=F32)
    y = _gelu_tanh(y + d_ref[0] * u.astype(F32))
    for t in range(S5_TC):
        ys_ref[pl.ds(t, nk, stride=S5_TC), :] = y[:, t * LANES:(t + 1) * LANES]
    y_ref[0] = ys_ref[...].astype(y_ref.dtype)


def _s5_mixer(u3, nb, lam_re, lam_im, log_dt, b_re, b_im, c_re, c_im, d_skip):
    nblk, rows, w = u3.shape
    tc = S5_TC
    nk = rows // nb
    assert nk % 8 == 0
    wm, wr, wo, a_pow = _s5_weights(lam_re, lam_im, log_dt, b_re, b_im, c_re, c_im)
    ns = wr.shape[-1]
    d_rows = jnp.tile(d_skip.astype(F32).reshape(nblk, 1, LANES), (1, 1, tc))
    blk = lambda i, b: (i, 0, 0)
    return pl.pallas_call(
        _s5_mix_kernel,
        name="s5_mix",
        grid=(nblk, nb),
        in_specs=[pl.BlockSpec((1, nk, w), lambda i, b: (i, b, 0)),
                  pl.BlockSpec((1, w, w), blk),
                  pl.BlockSpec((1, w, ns), blk),
                  pl.BlockSpec((1, ns, w), blk),
                  pl.BlockSpec((1, 2, 1, ns // 2), lambda i, b: (i, 0, 0, 0)),
                  pl.BlockSpec((1, 1, w), blk)],
        out_specs=pl.BlockSpec((1, nk * tc, LANES), lambda i, b: (i, b, 0)),
        out_shape=jax.ShapeDtypeStruct((nblk, rows * tc, LANES), BF16),
        scratch_shapes=[pltpu.VMEM((nk, ns), F32), pltpu.VMEM((nk, ns), F32),
                        pltpu.VMEM((nk * tc, LANES), F32)],
        compiler_params=_cparams(("arbitrary", "arbitrary")),
    )(u3, wm, wr, wo, a_pow, d_rows)


def _even_out_kernel(x_ref, oa_ref, y_ref, zb_ref, wg_ref, bg_ref, wo_ref, g_ref, wf_ref,
                     x1_ref, hn_ref, f_ref):
    y = jnp.concatenate([y_ref[i] for i in range(y_ref.shape[0])], axis=1)
    t = jnp.dot(y, wg_ref[...], preferred_element_type=F32) + bg_ref[...]
    ob = y.astype(F32) * jax.nn.sigmoid(t) * _silu(zb_ref[...].astype(F32))
    wa = oa_ref.shape[1]
    acc = jnp.dot(oa_ref[...], wo_ref[:wa, :], preferred_element_type=F32)
    acc = acc + jnp.dot(ob.astype(BF16), wo_ref[wa:, :], preferred_element_type=F32)
    x1 = x_ref[...] + acc
    x1_ref[...] = x1
    hn = _rms(x1, g_ref[...]).astype(BF16)
    hn_ref[...] = hn
    f_ref[...] = lax.dot_general(wf_ref[...], hn, (((1,), (1,)), ((), ())),
                                 preferred_element_type=F32)


def _even_out(x2, oa, y, proj, w_glu, b_glu, w_out, g_next, wf_t):
    m, d = x2.shape
    nh = wf_t.shape[0]
    wa = oa.shape[1]
    nblk, _, lw = y.shape
    wb = nblk * lw
    tm = min(TM_OUT, m)
    zb_col = (proj.shape[1] - wb) // wb
    const = lambda i: (0, 0)
    return pl.pallas_call(
        _even_out_kernel,
        name="even_out",
        grid=(m // tm,),
        in_specs=[pl.BlockSpec((tm, d), lambda i: (i, 0)),
                  pl.BlockSpec((tm, wa), lambda i: (i, 0)),
                  pl.BlockSpec((nblk, tm, lw), lambda i: (0, i, 0)),
                  pl.BlockSpec((tm, wb), lambda i: (i, zb_col)),
                  pl.BlockSpec((wb, wb), const),
                  pl.BlockSpec((1, wb), const),
                  pl.BlockSpec((wa + wb, d), const),
                  pl.BlockSpec((1, d), const),
                  pl.BlockSpec((nh, d), const)],
        out_specs=[pl.BlockSpec((tm, d), lambda i: (i, 0)),
                   pl.BlockSpec((tm, d), lambda i: (i, 0)),
                   pl.BlockSpec((nh, tm), lambda i: (0, i))],
        out_shape=[jax.ShapeDtypeStruct((m, d), F32),
                   jax.ShapeDtypeStruct((m, d), BF16),
                   jax.ShapeDtypeStruct((nh, m), F32)],
        compiler_params=_cparams(("arbitrary",)),
    )(x2, oa, y, proj, w_glu, b_glu.reshape(1, wb), w_out, g_next.reshape(1, d), wf_t)


def _cum_forget_kernel(f_ref, b_ref, c_ref):
    x = f_ref[...] + b_ref[...]
    v = -(jnp.maximum(-x, 0.0) + jnp.log1p(jnp.exp(-jnp.abs(x))))
    n = v.shape[-1]
    pos = lax.broadcasted_iota(jnp.int32, v.shape, 1)
    d = 1
    while d < n:
        v = v + jnp.where(pos >= d, pltpu.roll(v, d, 1), 0.0)
        d *= 2
    c_ref[...] = v


def _cum_forget(f_t, b_forget, seq):
    nh, m = f_t.shape
    return pl.pallas_call(
        _cum_forget_kernel,
        name="cum_forget",
        grid=(m // seq,),
        in_specs=[pl.BlockSpec((nh, seq), lambda i: (0, i)),
                  pl.BlockSpec((nh, 1), lambda i: (0, 0))],
        out_specs=pl.BlockSpec((nh, seq), lambda i: (0, i)),
        out_shape=jax.ShapeDtypeStruct((nh, m), F32),
        compiler_params=_cparams(("arbitrary",)),
    )(f_t, b_forget.astype(F32).reshape(nh, 1))


def _split3(x):
    hi = x.astype(BF16).astype(F32)
    r = x - hi
    mid = r.astype(BF16).astype(F32)
    lo = (r - mid).astype(BF16).astype(F32)
    return hi, mid, lo


def _fox_kernel(q0_ref, qn_ref, k_ref, v_ref, z_ref, c_ref, o_ref,
                ke_ref, vt_ref, qa_ref, s_ref, mb_ref, acc_ref):
    qi = pl.program_id(2)
    nq = pl.num_programs(2)
    tq = qn_ref.shape[1]
    tk = tq
    dh = C_DH
    nh = c_ref.shape[1]
    nblk = k_ref.shape[1] // tk
    nt = (((1,), (1,)), ((), ()))
    key = lax.broadcasted_iota(jnp.int32, (tk, tq), 0)
    qry = lax.broadcasted_iota(jnp.int32, (tk, tq), 1)

    def cols(h):
        return slice(h * dh, (h + 1) * dh)

    def prep_queries(q_ref, qb):
        qs = pl.multiple_of(qb * tq, tq)
        row = lax.broadcasted_iota(jnp.int32, (dh, tq), 0)
        for h in range(nh):
            c_base = jnp.max(c_ref[0, h, :, pl.ds(qs, tq)], axis=1, keepdims=True) * LOG2E
            bh, bm, bl = _split3(c_base)
            g = jnp.where(row < 3, 1.0, jnp.where(row == 3, bh, jnp.where(row == 4, bm,
                          jnp.where(row == 5, bl, 0.0))))
            qt = (q_ref[0, :, cols(h)].astype(F32) * (LOG2E / math.sqrt(dh))).T
            qa_ref[h, :dh, :] = qt.astype(BF16)
            qa_ref[h, dh:, :] = g.astype(BF16)

    def produce(h, j, keep):
        ks = pl.multiple_of(j * tk, tk)
        ka = jnp.concatenate([k_ref[0, pl.ds(ks, tk), cols(h)], ke_ref[h, pl.ds(ks, tk), :]],
                             axis=1)
        s = jnp.dot(ka, qa_ref[h], preferred_element_type=F32)
        if keep is not None:
            s = jnp.where(keep, s, NEG)
        s_ref[h] = s
        return jnp.max(s, axis=0, keepdims=True)

    def consume(h, j, stats):
        m, mb = stats
        ks = pl.multiple_of(j * tk, tk)
        m2 = jnp.maximum(m, mb)
        p = jnp.exp2(s_ref[h] - m2).astype(BF16)
        pv = jnp.dot(vt_ref[h, :, pl.ds(ks, tk)], p, preferred_element_type=F32)
        acc_ref[h] = jnp.exp2(m - m2) * acc_ref[h] + pv
        return m2

    def step(j, carry, diagonal_next):
        keep = (key <= qry) if diagonal_next else None
        return tuple((consume(h, j, stats), produce(h, j + 1, keep))
                     for h, stats in enumerate(carry))

    @pl.when(qi == 0)
    def _():
        r = lax.broadcasted_iota(jnp.int32, (dh, dh), 0)
        c = lax.broadcasted_iota(jnp.int32, (dh, dh), 1)
        eye = (r == c).astype(BF16)
        row = lax.broadcasted_iota(jnp.int32, (dh, tk), 0)
        ones_row = (lax.broadcasted_iota(jnp.int32, (FOX_PAD, tk), 0) == 0).astype(BF16)

        def blk(j, carry):
            ks = pl.multiple_of(j * tk, tk)
            for h in range(nh):
                hi, mid, lo = _split3(c_ref[0, h, :, pl.ds(ks, tk)] * (-LOG2E))
                a = jnp.where(row == 0, hi, jnp.where(row == 1, mid, jnp.where(row == 2, lo,
                              jnp.where(row < 6, 1.0, 0.0))))
                ke_ref[h, pl.ds(ks, tk), :] = a.T.astype(BF16)
                vt = lax.dot_general(eye, v_ref[0, pl.ds(ks, tk), cols(h)], nt,
                                     preferred_element_type=F32)
                vt_ref[h, :dh, pl.ds(ks, tk)] = vt.astype(BF16)
                vt_ref[h, dh:, pl.ds(ks, tk)] = ones_row
            return carry

        lax.fori_loop(0, nblk, blk, 0)
        prep_queries(q0_ref, 0)
        for h in range(nh):
            mb_ref[h] = produce(h, 0, key <= qry)

    carry = []
    for h in range(nh):
        acc_ref[h] = jnp.zeros(acc_ref.shape[1:], F32)
        carry.append((jnp.full((1, tq), NEG, F32), mb_ref[h]))
    plain = functools.partial(step, diagonal_next=False)
    n_plain = jnp.maximum(qi - 1, 0)
    n_pairs = n_plain // 2
    carry = lax.fori_loop(0, n_pairs, lambda i, c: plain(2 * i + 1, plain(2 * i, c)),
                          tuple(carry))
    carry = lax.fori_loop(2 * n_pairs, n_plain, plain, carry)
    carry = lax.fori_loop(n_plain, qi, functools.partial(step, diagonal_next=True), carry)
    for h, stats in enumerate(carry):
        consume(h, qi, stats)
    prep_queries(qn_ref, jnp.minimum(qi + 1, nq - 1))
    for h in range(nh):
        mb_ref[h] = produce(h, 0, None)
    for h in range(nh):
        acc = acc_ref[h]
        o = (acc[:dh] / acc[dh:dh + 1]).T
        o_ref[0, :, cols(h)] = (o * _silu(z_ref[0, :, cols(h)].astype(F32))).astype(o_ref.dtype)


def _fox(proj3, cum4):
    b, l, _ = proj3.shape
    ng = C_HEADS // FOX_HEADS
    w = FOX_HEADS * C_DH
    tq = min(FOX_TQ, l)
    nq = l // tq
    return pl.pallas_call(
        _fox_kernel,
        name="fox",
        grid=(b, ng, nq),
        in_specs=[pl.BlockSpec((1, tq, w), lambda bi, gi, qi: (bi, 0, gi)),
                  pl.BlockSpec((1, tq, w), lambda bi, gi, qi: (bi, jnp.minimum(qi + 1, nq - 1), gi)),
                  pl.BlockSpec((1, l, w), lambda bi, gi, qi: (bi, 0, ng + gi)),
                  pl.BlockSpec((1, l, w), lambda bi, gi, qi: (bi, 0, 2 * ng + gi)),
                  pl.BlockSpec((1, tq, w), lambda bi, gi, qi: (bi, qi, 3 * ng + gi)),
                  pl.BlockSpec((1, FOX_HEADS, 1, l), lambda bi, gi, qi: (bi, gi, 0, 0))],
        out_specs=pl.BlockSpec((1, tq, w), lambda bi, gi, qi: (bi, qi, gi)),
        out_shape=jax.ShapeDtypeStruct((b, l, C_HEADS * C_DH), BF16),
        scratch_shapes=[pltpu.VMEM((FOX_HEADS, l, C_DH), BF16),
                        pltpu.VMEM((FOX_HEADS, C_DH + FOX_PAD, l), BF16),
                        pltpu.VMEM((FOX_HEADS, 2 * C_DH, tq), BF16),
                        pltpu.VMEM((FOX_HEADS, tq, tq), F32),
                        pltpu.VMEM((FOX_HEADS, 1, tq), F32),
                        pltpu.VMEM((FOX_HEADS, C_DH + FOX_PAD, tq), F32)],
        compiler_params=_cparams(("arbitrary", "arbitrary", "arbitrary")),
    )(proj3, proj3, proj3, proj3, proj3, cum4)


def _odd_out_kernel(x_ref, o_ref, w_ref, g_ref, out_ref):
    x2 = x_ref[...] + jnp.dot(o_ref[...], w_ref[...], preferred_element_type=F32)
    out_ref[...] = _rms(x2, g_ref[...])


def _odd_out(x1, o, w_out, g):
    m, d = x1.shape
    k = o.shape[1]
    tm = min(TM_OUT, m)
    return pl.pallas_call(
        _odd_out_kernel,
        name="odd_out",
        grid=(m // tm,),
        in_specs=[pl.BlockSpec((tm, d), lambda i: (i, 0)),
                  pl.BlockSpec((tm, k), lambda i: (i, 0)),
                  pl.BlockSpec((k, d), lambda i: (0, 0)),
                  pl.BlockSpec((1, d), lambda i: (0, 0))],
        out_specs=pl.BlockSpec((tm, d), lambda i: (i, 0)),
        out_shape=jax.ShapeDtypeStruct((m, d), F32),
        compiler_params=_cparams(("arbitrary",)),
    )(x1, o, w_out, g.reshape(1, d))


def kernel(x, norm_even_g, w_in_even, rel_bias, s5_lambda_re, s5_lambda_im, s5_log_dt,
           s5_b_re, s5_b_im, s5_c_re, s5_c_im, s5_d, w_glu, b_glu, w_out_even,
           norm_odd_g, w_in_odd, b_forget, w_out_odd, final_norm_g):
    b, l, d = x.shape
    assert norm_even_g.shape[0] == 1 and norm_odd_g.shape[0] == 1, "one even + one odd layer"
    assert l % FOX_TQ == 0 and l % VT_BLK == 0 and l % A_QB == 0 and l >= A_KB
    m = b * l
    a_w = A_HEADS * A_DH
    s5_w = s5_d.shape[1] * S5_GROUP
    c_w = C_HEADS * C_DH

    x2 = x.reshape(m, d)
    assert s5_w == TN_PROJ and l % S5_TC == 0
    proj, u3 = _norm_proj(x2, norm_even_g[0], w_in_even[0].astype(BF16), 3 * a_w)
    proj3 = proj.reshape(b, l, proj.shape[1])
    o_a = _attn_a(proj3, rel_bias[0])
    y = _s5_mixer(u3, b, s5_lambda_re[0], s5_lambda_im[0], s5_log_dt[0], s5_b_re[0],
                  s5_b_im[0], s5_c_re[0], s5_c_im[0], s5_d[0])
    w_odd_t = w_in_odd[0].T
    wf_t = w_odd_t[4 * c_w:].astype(BF16)
    x1, hn1, f_t = _even_out(x2, o_a.reshape(m, a_w), y, proj,
                             w_glu[0].astype(BF16), b_glu[0].astype(F32),
                             w_out_even[0].astype(BF16), norm_odd_g[0], wf_t)
    proj1 = _proj_odd(hn1, w_odd_t, 4 * c_w)
    cum = _cum_forget(f_t, b_forget[0], l)
    cum4 = cum.reshape(C_HEADS, b, 1, l).transpose(1, 0, 2, 3)
    o_c = _fox(proj1.reshape(b, l, 4 * c_w), cum4)
    out = _odd_out(x1, o_c.reshape(m, c_w), w_out_odd[0].astype(BF16), final_norm_g)
    return out.reshape(b, l, d)
```

```python
import functools
import math

import jax
import jax.numpy as jnp
from jax import lax
from jax.experimental import pallas as pl
from jax.experimental.pallas import tpu as pltpu

F32 = jnp.float32
BF16 = jnp.bfloat16

EPS = 1e-6
NEG = -1e30
LOG2E = math.log2(math.e)

CHUNK = 64
N_LEFT = 8
A_HEADS = 16
A_DH = 64
MAX_REL = 128
S5_GROUP = 16
S5_STATE = 64
C_HEADS = 16
C_DH = 128

LANES = 128
VMEM_LIMIT = 56 * 1024 * 1024

TM_PROJ = 1024
TN_PROJ = 1024
TM_OUT = 512
A_QB = 2 * CHUNK
A_KB = A_QB + N_LEFT * CHUNK
A_TW = A_KB + N_LEFT * CHUNK
A_PAD = 16
VT_BLK = 512
S5_TC = 16
S5_GPB = LANES // S5_GROUP
FOX_TQ = 512
FOX_HEADS = 4
FOX_PAD = 16


def _cparams(sem):
    return pltpu.CompilerParams(dimension_semantics=sem, vmem_limit_bytes=VMEM_LIMIT)


def _rms(x, g):
    ms = jnp.mean(x * x, axis=-1, keepdims=True)
    return x * lax.rsqrt(ms + EPS) * g


def _silu(x):
    return x * jax.nn.sigmoid(x)


def _gelu_tanh(x):
    c = math.sqrt(2.0 / math.pi)
    return 0.5 * x * (1.0 + jnp.tanh(c * (x + 0.044715 * (x * x * x))))


def _norm_proj_kernel(x_ref, g_ref, w_ref, o_ref, u_ref, hn_ref, us_ref, *, u_tile):
    j = pl.program_id(1)

    @pl.when(j == 0)
    def _():
        hn_ref[...] = _rms(x_ref[...], g_ref[...]).astype(BF16)

    acc = jnp.dot(hn_ref[...], w_ref[...], preferred_element_type=F32)
    o_ref[...] = acc.astype(o_ref.dtype)

    @pl.when(j == u_tile)
    def _():
        nchunk = us_ref.shape[1] // S5_TC
        for blk in range(u_ref.shape[0]):
            us_ref[blk] = acc[:, blk * LANES:(blk + 1) * LANES]
            for t in range(S5_TC):
                rows = us_ref[blk, pl.ds(t, nchunk, stride=S5_TC), :]
                u_ref[blk, :, t * LANES:(t + 1) * LANES] = rows.astype(u_ref.dtype)


def _norm_proj(x2, g, w, u_start):
    m, d = x2.shape
    n = w.shape[1]
    tm, tn = min(TM_PROJ, m), TN_PROJ
    assert u_start % tn == 0 and tm % (16 * S5_TC) == 0
    nblk = tn // LANES
    return pl.pallas_call(
        functools.partial(_norm_proj_kernel, u_tile=u_start // tn),
        name="norm_proj",
        grid=(m // tm, n // tn),
        in_specs=[pl.BlockSpec((tm, d), lambda i, j: (i, 0)),
                  pl.BlockSpec((1, d), lambda i, j: (0, 0)),
                  pl.BlockSpec((d, tn), lambda i, j: (0, j))],
        out_specs=[pl.BlockSpec((tm, tn), lambda i, j: (i, j)),
                   pl.BlockSpec((nblk, tm // S5_TC, S5_TC * LANES), lambda i, j: (0, i, 0))],
        out_shape=[jax.ShapeDtypeStruct((m, n), BF16),
                   jax.ShapeDtypeStruct((nblk, m // S5_TC, S5_TC * LANES), BF16)],
        scratch_shapes=[pltpu.VMEM((tm, d), BF16), pltpu.VMEM((nblk, tm, LANES), F32)],
        compiler_params=_cparams(("arbitrary", "arbitrary")),
    )(x2, g.reshape(1, d), w)


def _proj_odd_kernel(h_ref, wt_ref, o_ref, wb_ref):
    @pl.when(pl.program_id(1) == 0)
    def _():
        wb_ref[...] = wt_ref[...].T.astype(BF16)

    o_ref[...] = jnp.dot(h_ref[...], wb_ref[...],
                         preferred_element_type=F32).astype(o_ref.dtype)


def _proj_odd(hn, wt, n):
    m, d = hn.shape
    tm, tn = min(TM_PROJ, m), TN_PROJ
    assert n % tn == 0
    return pl.pallas_call(
        _proj_odd_kernel,
        name="proj_odd",
        grid=(n // tn, m // tm),
        in_specs=[pl.BlockSpec((tm, d), lambda j, i: (i, 0)),
                  pl.BlockSpec((tn, d), lambda j, i: (j, 0))],
        out_specs=pl.BlockSpec((tm, tn), lambda j, i: (i, j)),
        out_shape=jax.ShapeDtypeStruct((m, n), BF16),
        scratch_shapes=[pltpu.VMEM((d, tn), BF16)],
        compiler_params=_cparams(("arbitrary", "arbitrary")),
    )(hn, wt)


def _attn_a_kernel(q_ref, k_ref, v_ref, z_ref, e_ref, o_ref,
                   tt_ref, vt_ref, qt_ref, s_ref, pv_ref):
    l_all = k_ref.shape[1]
    w = q_ref.shape[2]
    nt = (((1,), (1,)), ((), ()))
    band = N_LEFT * CHUNK

    @pl.when(pl.program_id(1) == 0)
    def _():
        i = lax.broadcasted_iota(jnp.int32, (A_QB, A_TW), 0)
        m = lax.broadcasted_iota(jnp.int32, (A_QB, A_TW), 1)
        dchunk = m // CHUNK - i // CHUNK
        ok = (dchunk >= 0) & (dchunk <= N_LEFT)
        for h in range(2):
            rows = jnp.broadcast_to(e_ref[0, h:h + 1, :], (A_QB, A_QB + A_TW))
            toep = pltpu.roll(rows, 0, 1, stride=1, stride_axis=0)[:, A_QB:]
            t = jnp.where(ok, toep * LOG2E, NEG)
            tt_ref[:, h * A_QB:(h + 1) * A_QB] = t.T

    r = lax.broadcasted_iota(jnp.int32, (w, w), 0)
    c = lax.broadcasted_iota(jnp.int32, (w, w), 1)
    eye = (r == c).astype(BF16)
    ones_row = (lax.broadcasted_iota(jnp.int32, (A_PAD, VT_BLK), 0) == 0).astype(BF16)

    qscale = LOG2E / math.sqrt(A_DH)

    def vt_blk(j, carry):
        ks = pl.multiple_of(j * VT_BLK, VT_BLK)
        vt = lax.dot_general(eye, v_ref[0, pl.ds(ks, VT_BLK), :], nt, preferred_element_type=F32)
        for h in range(2):
            vt_ref[h, :A_DH, pl.ds(ks, VT_BLK)] = vt[h * A_DH:(h + 1) * A_DH].astype(BF16)
            vt_ref[h, A_DH:, pl.ds(ks, VT_BLK)] = ones_row
        qs = (q_ref[0, pl.ds(ks, VT_BLK), :].astype(F32) * qscale).astype(BF16)
        qt_ref[:, pl.ds(ks, VT_BLK)] = lax.dot_general(
            eye, qs, nt, preferred_element_type=F32).astype(BF16)
        return carry

    lax.fori_loop(0, l_all // VT_BLK, vt_blk, 0)

    head_row = lax.broadcasted_iota(jnp.int32, (w, A_QB), 0)

    def window(qb):
        start = qb * A_QB - band
        start_c = jnp.maximum(start, 0)
        return pl.multiple_of(start_c, LANES), pl.multiple_of(start_c - start, LANES)

    def produce(qb, slot):
        qs = pl.multiple_of(qb * A_QB, A_QB)
        start_c, off = window(qb)
        qt = qt_ref[:, pl.ds(qs, A_QB)]
        zero = jnp.zeros_like(qt)
        qbd = jnp.concatenate([jnp.where(head_row < A_DH, qt, zero),
                               jnp.where(head_row >= A_DH, qt, zero)], axis=1)
        kb = k_ref[0, pl.ds(start_c, A_KB), :]
        s = jnp.dot(kb, qbd, preferred_element_type=F32)
        s = s + tt_ref[pl.ds(off, A_KB), :]
        s_ref[slot] = s
        return jnp.max(s, axis=0, keepdims=True)

    def consume(qb, slot, m):
        start_c, _ = window(qb)
        pb = jnp.exp2(s_ref[slot] - m).astype(BF16)
        for h in range(2):
            vt = vt_ref[h, :, pl.ds(start_c, A_KB)]
            pv_ref[slot, h] = jnp.dot(vt, pb[:, h * A_QB:(h + 1) * A_QB],
                                      preferred_element_type=F32)

    def finish(qb, slot):
        qs = pl.multiple_of(qb * A_QB, A_QB)
        outs = []
        for h in range(2):
            oh = pv_ref[slot, h]
            outs.append(oh[:A_DH] * (1.0 / oh[A_DH:A_DH + 1]))
        o = jnp.concatenate(outs, axis=0).T
        gate = _silu(z_ref[0, pl.ds(qs, A_QB), :].astype(F32))
        o_ref[0, pl.ds(qs, A_QB), :] = (o * gate).astype(o_ref.dtype)

    def pair(i, m, lead=True):
        j = 2 * i
        if lead:
            finish(j - 1, 1)
        m1 = produce(j + 1, 1)
        consume(j, 0, m)
        m2 = produce(j + 2, 0)
        finish(j, 0)
        consume(j + 1, 1, m1)
        return m2

    nb = l_all // A_QB
    npairs = (nb - 1) // 2
    m = pair(0, produce(0, 0), lead=False)
    m = lax.fori_loop(1, npairs, pair, m)
    last = 2 * npairs
    finish(last - 1, 1)
    if (nb - 1) % 2:
        m1 = produce(last + 1, 1)
        consume(last, 0, m)
        finish(last, 0)
        consume(last + 1, 1, m1)
        finish(last + 1, 1)
    else:
        consume(last, 0, m)
        finish(last, 0)


def _attn_a(proj3, rel_bias):
    b, l, _ = proj3.shape
    hp = A_HEADS // 2
    width = 2 * A_DH
    ncol = A_HEADS * A_DH // width
    zcol = (3 * A_HEADS * A_DH + (A_HEADS * A_DH)) // width
    left = N_LEFT * CHUNK + A_QB - MAX_REL
    right = A_QB + A_TW - left - (2 * MAX_REL + 1)
    ext = jnp.pad(rel_bias.astype(F32)[:, ::-1], ((0, 0), (left, right)), mode="edge")
    ext = ext.reshape(hp, 2, A_QB + A_TW)
    seq = lambda h, bi: (bi, 0, h)
    return pl.pallas_call(
        _attn_a_kernel,
        name="attn_a",
        grid=(hp, b),
        in_specs=[pl.BlockSpec((1, l, width), seq),
                  pl.BlockSpec((1, l, width), lambda h, bi: (bi, 0, ncol + h)),
                  pl.BlockSpec((1, l, width), lambda h, bi: (bi, 0, 2 * ncol + h)),
                  pl.BlockSpec((1, l, width), lambda h, bi: (bi, 0, zcol + h)),
                  pl.BlockSpec((1, 2, A_QB + A_TW), lambda h, bi: (h, 0, 0))],
        out_specs=pl.BlockSpec((1, l, width), seq),
        out_shape=jax.ShapeDtypeStruct((b, l, A_HEADS * A_DH), BF16),
        scratch_shapes=[pltpu.VMEM((A_TW, 2 * A_QB), F32),
                        pltpu.VMEM((2, A_DH + A_PAD, l), BF16),
                        pltpu.VMEM((width, l), BF16),
                        pltpu.VMEM((2, A_KB, 2 * A_QB), F32),
                        pltpu.VMEM((2, 2, A_DH + A_PAD, A_QB), F32)],
        compiler_params=_cparams(("arbitrary", "arbitrary")),
    )(proj3, proj3, proj3, proj3, ext)


def _cmul(ar, ai, br, bi):
    return ar * br - ai * bi, ar * bi + ai * br


def _s5_discretise(lr, li, log_dt):
    dt = jnp.exp(log_dt)
    mag = jnp.exp(lr * dt)
    ang = li * dt
    a_re = mag * jnp.cos(ang)
    a_im = mag * jnp.sin(ang)
    den = lr * lr + li * li
    nr = a_re - 1.0
    coef_re = (nr * lr + a_im * li) / den
    coef_im = (a_im * lr - nr * li) / den
    return (a_re, a_im), (coef_re, coef_im)


def _s5_weights_kernel(lam_pg_ref, ldt_pg_ref, c_pg_ref, lam_gp_ref, ldt_gp_ref, b_gp_ref,
                       lam_row_ref, ldt_row_ref, b_tl_ref, wm_ref, wr_ref, wo_ref, a_ref):
    tc, ch, ns, gpb = S5_TC, S5_GROUP, S5_STATE, S5_GPB
    lw = gpb * ch
    hi = lax.Precision.HIGHEST

    a_pg, _ = _s5_discretise(lam_pg_ref[0], lam_pg_ref[1], ldt_pg_ref[...])
    pw = (jnp.ones_like(a_pg[0]), jnp.zeros_like(a_pg[0]))
    ca = []
    for n in range(tc + 1):
        ca.append(_cmul(c_pg_ref[0], c_pg_ref[1], *pw))
        pw = _cmul(*pw, *a_pg)

    _, coef_gp = _s5_discretise(lam_gp_ref[0], lam_gp_ref[1], ldt_gp_ref[...])
    bb_gp = _cmul(*coef_gp, b_gp_ref[0], b_gp_ref[1])
    ca_re = jnp.concatenate([ca[n][0] for n in range(tc)], axis=1)
    ca_im = jnp.concatenate([ca[n][1] for n in range(tc)], axis=1)
    z = (jnp.dot(bb_gp[0], ca_re, precision=hi, preferred_element_type=F32)
         - jnp.dot(bb_gp[1], ca_im, precision=hi, preferred_element_type=F32))
    row_g = lax.broadcasted_iota(jnp.int32, z.shape, 0) // ch
    col_g = (lax.broadcasted_iota(jnp.int32, z.shape, 1) % lw) // ch
    z = jnp.where(row_g == col_g, z, 0.0).astype(BF16)
    for t in range(tc):
        parts = [z[:, :(tc - t) * lw]]
        if t:
            parts.insert(0, jnp.zeros((lw, t * lw), BF16))
        wm_ref[0, t * lw:(t + 1) * lw, :] = jnp.concatenate(parts, axis=1)

    rg = lax.broadcasted_iota(jnp.int32, (gpb * ns, lw), 0) // ns
    cg = lax.broadcasted_iota(jnp.int32, (gpb * ns, lw), 1) // ch
    keep = rg == cg
    for t2 in range(tc):
        cr, ci = ca[t2 + 1]
        cr = jnp.where(keep, jnp.concatenate([cr] * gpb, axis=0), 0.0)
        ci = jnp.where(keep, jnp.concatenate([ci] * gpb, axis=0), 0.0)
        wo_ref[0, :gpb * ns, t2 * lw:(t2 + 1) * lw] = cr.astype(BF16)
        wo_ref[0, gpb * ns:, t2 * lw:(t2 + 1) * lw] = (-ci).astype(BF16)

    a_row, coef_row = _s5_discretise(lam_row_ref[0], lam_row_ref[1], ldt_row_ref[...])
    bb_tl = _cmul(*coef_row, b_tl_ref[0], b_tl_ref[1])
    rg = lax.broadcasted_iota(jnp.int32, bb_tl[0].shape, 0) // ch
    cg = lax.broadcasted_iota(jnp.int32, bb_tl[0].shape, 1) // ns
    bb_tl = (jnp.where(rg == cg, bb_tl[0], 0.0), jnp.where(rg == cg, bb_tl[1], 0.0))
    pw = (jnp.ones_like(a_row[0]), jnp.zeros_like(a_row[0]))
    for t in range(tc - 1, -1, -1):
        r_re, r_im = _cmul(*pw, *bb_tl)
        wr_ref[0, t * lw:(t + 1) * lw, :] = jnp.concatenate([r_re, r_im], axis=1).astype(BF16)
        pw = _cmul(*pw, *a_row)
    a_ref[0, 0] = pw[0]
    a_ref[0, 1] = pw[1]


def _s5_weights(lam_re, lam_im, log_dt, b_re, b_im, c_re, c_im):
    g, ns = lam_re.shape
    ch, tc, gpb = S5_GROUP, S5_TC, S5_GPB
    nblk = g // gpb
    lw = gpb * ch
    lam = jnp.stack([lam_re, lam_im]).astype(F32)
    ldt = log_dt.astype(F32)
    bst = jnp.stack([b_re, b_im]).astype(F32)
    cst = jnp.stack([c_re, c_im]).astype(F32)
    lam_pg = jnp.repeat(lam.transpose(0, 2, 1), ch, axis=2)
    ldt_pg = jnp.repeat(ldt, ch)[None]
    c_pg = cst.transpose(0, 3, 1, 2).reshape(2, ns, g * ch)
    lam_gp = jnp.repeat(lam, ch, axis=1)
    ldt_gp = jnp.broadcast_to(jnp.repeat(ldt, ch)[:, None], (g * ch, ns))
    b_gp = bst.transpose(0, 1, 3, 2).reshape(2, g * ch, ns)
    lam_row = lam.reshape(2, 1, g * ns)
    ldt_row = jnp.repeat(ldt, ns)[None]
    b_tl = jnp.tile(b_gp, (1, 1, gpb))
    blk3 = lambda i: (0, 0, i)
    return pl.pallas_call(
        _s5_weights_kernel,
        name="s5_weights",
        grid=(nblk,),
        in_specs=[pl.BlockSpec((2, ns, lw), blk3),
                  pl.BlockSpec((1, lw), lambda i: (0, i)),
                  pl.BlockSpec((2, ns, lw), blk3),
                  pl.BlockSpec((2, lw, ns), lambda i: (0, i, 0)),
                  pl.BlockSpec((lw, ns), lambda i: (i, 0)),
                  pl.BlockSpec((2, lw, ns), lambda i: (0, i, 0)),
                  pl.BlockSpec((2, 1, gpb * ns), blk3),
                  pl.BlockSpec((1, gpb * ns), lambda i: (0, i)),
                  pl.BlockSpec((2, lw, gpb * ns), lambda i: (0, i, 0))],
        out_specs=[pl.BlockSpec((1, tc * lw, tc * lw), lambda i: (i, 0, 0)),
                   pl.BlockSpec((1, tc * lw, 2 * gpb * ns), lambda i: (i, 0, 0)),
                   pl.BlockSpec((1, 2 * gpb * ns, tc * lw), lambda i: (i, 0, 0)),
                   pl.BlockSpec((1, 2, 1, gpb * ns), lambda i: (i, 0, 0, 0))],
        out_shape=[jax.ShapeDtypeStruct((nblk, tc * lw, tc * lw), BF16),
                   jax.ShapeDtypeStruct((nblk, tc * lw, 2 * gpb * ns), BF16),
                   jax.ShapeDtypeStruct((nblk, 2 * gpb * ns, tc * lw), BF16),
                   jax.ShapeDtypeStruct((nblk, 2, 1, gpb * ns), F32)],
        compiler_params=_cparams(("arbitrary",)),
    )(lam_pg, ldt_pg, c_pg, lam_gp, ldt_gp, b_gp, lam_row, ldt_row, b_tl)


def _s5_mix_kernel(u_ref, wm_ref, wr_ref, wo_ref, a_ref, d_ref, y_ref, s_ref, x_ref, ys_ref):
    nk = u_ref.shape[1]
    half = s_ref.shape[1] // 2
    sub = 8
    u = u_ref[0]
    s_ref[...] = jnp.dot(u, wr_ref[0], preferred_element_type=F32)

    a1 = (a_ref[0, 0], a_ref[0, 1])
    a2 = _cmul(*a1, *a1)
    a4 = _cmul(*a2, *a2)
    a8 = _cmul(*a4, *a4)
    a3 = _cmul(*a2, *a1)
    pows = [(jnp.ones_like(a1[0]), jnp.zeros_like(a1[0])), a1, a2, a3, a4,
            _cmul(*a4, *a1), _cmul(*a4, *a2), _cmul(*a4, *a3)]
    p_re = jnp.concatenate([p[0] for p in pows], axis=0)
    p_im = jnp.concatenate([p[1] for p in pows], axis=0)
    rowid = lax.broadcasted_iota(jnp.int32, (sub, half), 0)
    levels = [(d, jnp.broadcast_to(a[0], (sub, half)), jnp.broadcast_to(a[1], (sub, half)))
              for d, a in ((1, a1), (2, a2), (4, a4))]

    def shift_rows(v, d):
        return jnp.where(rowid >= d, pltpu.roll(v, d, 0), 0.0)

    def tile(j, carry):
        cr, ci = carry
        rows = pl.ds(pl.multiple_of(j * sub, sub), sub)
        zr = s_ref[rows, :half]
        zi = s_ref[rows, half:]
        for d, br, bi in levels:
            sr, si = shift_rows(zr, d), shift_rows(zi, d)
            zr, zi = zr + br * sr - bi * si, zi + br * si + bi * sr
        x_ref[rows, :half] = p_re * cr - p_im * ci + shift_rows(zr, 1)
        x_ref[rows, half:] = p_re * ci + p_im * cr + shift_rows(zi, 1)
        return (a8[0] * cr - a8[1] * ci + zr[sub - 1:], a8[0] * ci + a8[1] * cr + zi[sub - 1:])

    zero = jnp.zeros((1, half), F32)
    lax.fori_loop(0, nk // sub, tile, (zero, zero))

    cw = 2 * LANES
    y = jnp.concatenate(
        [jnp.dot(u[:, :(n + 1) * cw], wm_ref[0, :(n + 1) * cw, n * cw:(n + 1) * cw],
                 preferred_element_type=F32) for n in range(u.shape[1] // cw)], axis=1)
    y = y + jnp.dot(x_ref[...].astype(BF16), wo_ref[0], preferred_element_type=F32)
    y = _gelu_tanh(y + d_ref[0] * u.astype(F32))
    for t in range(S5_TC):
        ys_ref[pl.ds(t, nk, stride=S5_TC), :] = y[:, t * LANES:(t + 1) * LANES]
    y_ref[0] = ys_ref[...].astype(y_ref.dtype)


def _s5_mixer(u3, nb, lam_re, lam_im, log_dt, b_re, b_im, c_re, c_im, d_skip):
    nblk, rows, w = u3.shape
    tc = S5_TC
    nk = rows // nb
    assert nk % 8 == 0
    wm, wr, wo, a_pow = _s5_weights(lam_re, lam_im, log_dt, b_re, b_im, c_re, c_im)
    ns = wr.shape[-1]
    d_rows = jnp.tile(d_skip.astype(F32).reshape(nblk, 1, LANES), (1, 1, tc))
    blk = lambda i, b: (i, 0, 0)
    return pl.pallas_call(
        _s5_mix_kernel,
        name="s5_mix",
        grid=(nblk, nb),
        in_specs=[pl.BlockSpec((1, nk, w), lambda i, b: (i, b, 0)),
                  pl.BlockSpec((1, w, w), blk),
                  pl.BlockSpec((1, w, ns), blk),
                  pl.BlockSpec((1, ns, w), blk),
                  pl.BlockSpec((1, 2, 1, ns // 2), lambda i, b: (i, 0, 0, 0)),
                  pl.BlockSpec((1, 1, w), blk)],
        out_specs=pl.BlockSpec((1, nk * tc, LANES), lambda i, b: (i, b, 0)),
        out_shape=jax.ShapeDtypeStruct((nblk, rows * tc, LANES), BF16),
        scratch_shapes=[pltpu.VMEM((nk, ns), F32), pltpu.VMEM((nk, ns), F32),
                        pltpu.VMEM((nk * tc, LANES), F32)],
        compiler_params=_cparams(("arbitrary", "arbitrary")),
    )(u3, wm, wr, wo, a_pow, d_rows)


def _even_out_kernel(x_ref, oa_ref, y_ref, zb_ref, wg_ref, bg_ref, wo_ref, g_ref, wf_ref,
                     x1_ref, hn_ref, f_ref):
    y = jnp.concatenate([y_ref[i] for i in range(y_ref.shape[0])], axis=1)
    t = jnp.dot(y, wg_ref[...], preferred_element_type=F32) + bg_ref[...]
    ob = y.astype(F32) * jax.nn.sigmoid(t) * _silu(zb_ref[...].astype(F32))
    wa = oa_ref.shape[1]
    acc = jnp.dot(oa_ref[...], wo_ref[:wa, :], preferred_element_type=F32)
    acc = acc + jnp.dot(ob.astype(BF16), wo_ref[wa:, :], preferred_element_type=F32)
    x1 = x_ref[...] + acc
    x1_ref[...] = x1
    hn = _rms(x1, g_ref[...]).astype(BF16)
    hn_ref[...] = hn
    f_ref[...] = lax.dot_general(wf_ref[...], hn, (((1,), (1,)), ((), ())),
                                 preferred_element_type=F32)


def _even_out(x2, oa, y, proj, w_glu, b_glu, w_out, g_next, wf_t):
    m, d = x2.shape
    nh = wf_t.shape[0]
    wa = oa.shape[1]
    nblk, _, lw = y.shape
    wb = nblk * lw
    tm = min(TM_OUT, m)
    zb_col = (proj.shape[1] - wb) // wb
    const = lambda i: (0, 0)
    return pl.pallas_call(
        _even_out_kernel,
        name="even_out",
        grid=(m // tm,),
        in_specs=[pl.BlockSpec((tm, d), lambda i: (i, 0)),
                  pl.BlockSpec((tm, wa), lambda i: (i, 0)),
                  pl.BlockSpec((nblk, tm, lw), lambda i: (0, i, 0)),
                  pl.BlockSpec((tm, wb), lambda i: (i, zb_col)),
                  pl.BlockSpec((wb, wb), const),
                  pl.BlockSpec((1, wb), const),
                  pl.BlockSpec((wa + wb, d), const),
                  pl.BlockSpec((1, d), const),
                  pl.BlockSpec((nh, d), const)],
        out_specs=[pl.BlockSpec((tm, d), lambda i: (i, 0)),
                   pl.BlockSpec((tm, d), lambda i: (i, 0)),
                   pl.BlockSpec((nh, tm), lambda i: (0, i))],
        out_shape=[jax.ShapeDtypeStruct((m, d), F32),
                   jax.ShapeDtypeStruct((m, d), BF16),
                   jax.ShapeDtypeStruct((nh, m), F32)],
        compiler_params=_cparams(("arbitrary",)),
    )(x2, oa, y, proj, w_glu, b_glu.reshape(1, wb), w_out, g_next.reshape(1, d), wf_t)


def _cum_forget_kernel(f_ref, b_ref, c_ref):
    x = f_ref[...] + b_ref[...]
    v = -(jnp.maximum(-x, 0.0) + jnp.log1p(jnp.exp(-jnp.abs(x))))
    n = v.shape[-1]
    pos = lax.broadcasted_iota(jnp.int32, v.shape, 1)
    d = 1
    while d < n:
        v = v + jnp.where(pos >= d, pltpu.roll(v, d, 1), 0.0)
        d *= 2
    c_ref[...] = v


def _cum_forget(f_t, b_forget, seq):
    nh, m = f_t.shape
    return pl.pallas_call(
        _cum_forget_kernel,
        name="cum_forget",
        grid=(m // seq,),
        in_specs=[pl.BlockSpec((nh, seq), lambda i: (0, i)),
                  pl.BlockSpec((nh, 1), lambda i: (0, 0))],
        out_specs=pl.BlockSpec((nh, seq), lambda i: (0, i)),
        out_shape=jax.ShapeDtypeStruct((nh, m), F32),
        compiler_params=_cparams(("arbitrary",)),
    )(f_t, b_forget.astype(F32).reshape(nh, 1))


def _split3(x):
    hi = x.astype(BF16).astype(F32)
    r = x - hi
    mid = r.astype(BF16).astype(F32)
    lo = (r - mid).astype(BF16).astype(F32)
    return hi, mid, lo


def _fox_kernel(q0_ref, qn_ref, k_ref, v_ref, z_ref, c_ref, o_ref,
                ke_ref, vt_ref, qa_ref, s_ref, mb_ref, acc_ref):
    qi = pl.program_id(2)
    nq = pl.num_programs(2)
    tq = qn_ref.shape[1]
    tk = tq
    dh = C_DH
    nh = c_ref.shape[1]
    nblk = k_ref.shape[1] // tk
    nt = (((1,), (1,)), ((), ()))

    def cols(h):
        return slice(h * dh, (h + 1) * dh)

    def prep_queries(q_ref, qb):
        qs = pl.multiple_of(qb * tq, tq)
        row = lax.broadcasted_iota(jnp.int32, (dh, tq), 0)
        for h in range(nh):
            c_base = jnp.max(c_ref[0, h, :, pl.ds(qs, tq)], axis=1, keepdims=True) * LOG2E
            bh, bm, bl = _split3(c_base)
            g = jnp.where(row < 3, 1.0, jnp.where(row == 3, bh, jnp.where(row == 4, bm,
                          jnp.where(row == 5, bl, 0.0))))
            qt = (q_ref[0, :, cols(h)].astype(F32) * (LOG2E / math.sqrt(dh))).T
            qa_ref[h, :dh, :] = qt.astype(BF16)
            qa_ref[h, dh:, :] = g.astype(BF16)

    half = tk // 2

    def produce(h, j, diagonal):
        ks = pl.multiple_of(j * tk, tk)

        def keys(start, n):
            return jnp.concatenate([k_ref[0, pl.ds(start, n), cols(h)],
                                    ke_ref[h, pl.ds(start, n), :]], axis=1)

        if not diagonal:
            s = jnp.dot(keys(ks, tk), qa_ref[h], preferred_element_type=F32)
            s_ref[h] = s
            return jnp.max(s, axis=0, keepdims=True)
        top = jnp.dot(keys(ks, half), qa_ref[h], preferred_element_type=F32)
        bot = jnp.dot(keys(ks + half, half), qa_ref[h, :, half:],
                      preferred_element_type=F32)
        top = jnp.where(lax.broadcasted_iota(jnp.int32, top.shape, 0)
                        <= lax.broadcasted_iota(jnp.int32, top.shape, 1), top, NEG)
        bot = jnp.where(lax.broadcasted_iota(jnp.int32, bot.shape, 0)
                        <= lax.broadcasted_iota(jnp.int32, bot.shape, 1), bot, NEG)
        s_ref[h, :half, :] = top
        s_ref[h, half:, half:] = bot
        bot_wide = jnp.concatenate([jnp.full((half, half), NEG, F32), bot], axis=1)
        return jnp.max(jnp.maximum(top, bot_wide), axis=0, keepdims=True)

    def consume(h, j, stats, diagonal=False):
        m, mb = stats
        ks = pl.multiple_of(j * tk, tk)
        m2 = jnp.maximum(m, mb)
        if not diagonal:
            p = jnp.exp2(s_ref[h] - m2).astype(BF16)
            pv = jnp.dot(vt_ref[h, :, pl.ds(ks, tk)], p, preferred_element_type=F32)
            acc_ref[h] = jnp.exp2(m - m2) * acc_ref[h] + pv
            return m2
        p_top = jnp.exp2(s_ref[h, :half, :] - m2).astype(BF16)
        mb_ref[h] = m2
        p_bot = jnp.exp2(s_ref[h, half:, half:] - mb_ref[h, :, half:]).astype(BF16)
        acc = jnp.exp2(m - m2) * acc_ref[h] + jnp.dot(vt_ref[h, :, pl.ds(ks, half)], p_top,
                                                      preferred_element_type=F32)
        acc_ref[h, :, :half] = acc[:, :half]
        acc_ref[h, :, half:] = acc[:, half:] + jnp.dot(vt_ref[h, :, pl.ds(ks + half, half)], p_bot,
                                                       preferred_element_type=F32)
        return m2

    def step(j, carry, diagonal_next):
        return tuple((consume(h, j, stats), produce(h, j + 1, diagonal_next))
                     for h, stats in enumerate(carry))

    @pl.when(qi == 0)
    def _():
        r = lax.broadcasted_iota(jnp.int32, (dh, dh), 0)
        c = lax.broadcasted_iota(jnp.int32, (dh, dh), 1)
        eye = (r == c).astype(BF16)
        row = lax.broadcasted_iota(jnp.int32, (dh, tk), 0)
        ones_row = (lax.broadcasted_iota(jnp.int32, (FOX_PAD, tk), 0) == 0).astype(BF16)

        def blk(j, carry):
            ks = pl.multiple_of(j * tk, tk)
            for h in range(nh):
                hi, mid, lo = _split3(c_ref[0, h, :, pl.ds(ks, tk)] * (-LOG2E))
                a = jnp.where(row == 0, hi, jnp.where(row == 1, mid, jnp.where(row == 2, lo,
                              jnp.where(row < 6, 1.0, 0.0))))
                ke_ref[h, pl.ds(ks, tk), :] = a.T.astype(BF16)
                vt = lax.dot_general(eye, v_ref[0, pl.ds(ks, tk), cols(h)], nt,
                                     preferred_element_type=F32)
                vt_ref[h, :dh, pl.ds(ks, tk)] = vt.astype(BF16)
                vt_ref[h, dh:, pl.ds(ks, tk)] = ones_row
            return carry

        lax.fori_loop(0, nblk, blk, 0)
        prep_queries(q0_ref, 0)
        for h in range(nh):
            mb_ref[h] = produce(h, 0, True)

    carry = []
    for h in range(nh):
        acc_ref[h] = jnp.zeros(acc_ref.shape[1:], F32)
        carry.append((jnp.full((1, tq), NEG, F32), mb_ref[h]))
    plain = functools.partial(step, diagonal_next=False)
    n_plain = jnp.maximum(qi - 1, 0)
    n_pairs = n_plain // 2
    carry = lax.fori_loop(0, n_pairs, lambda i, c: plain(2 * i + 1, plain(2 * i, c)),
                          tuple(carry))
    carry = lax.fori_loop(2 * n_pairs, n_plain, plain, carry)
    carry = lax.fori_loop(n_plain, qi, functools.partial(step, diagonal_next=True), carry)
    for h, stats in enumerate(carry):
        consume(h, qi, stats, diagonal=True)
    prep_queries(qn_ref, jnp.minimum(qi + 1, nq - 1))
    for h in range(nh):
        mb_ref[h] = produce(h, 0, False)
    for h in range(nh):
        acc = acc_ref[h]
        o = (acc[:dh] / acc[dh:dh + 1]).T
        o_ref[0, :, cols(h)] = (o * _silu(z_ref[0, :, cols(h)].astype(F32))).astype(o_ref.dtype)


def _fox(proj3, cum4):
    b, l, _ = proj3.shape
    ng = C_HEADS // FOX_HEADS
    w = FOX_HEADS * C_DH
    tq = min(FOX_TQ, l)
    nq = l // tq
    return pl.pallas_call(
        _fox_kernel,
        name="fox",
        grid=(b, ng, nq),
        in_specs=[pl.BlockSpec((1, tq, w), lambda bi, gi, qi: (bi, 0, gi)),
                  pl.BlockSpec((1, tq, w), lambda bi, gi, qi: (bi, jnp.minimum(qi + 1, nq - 1), gi)),
                  pl.BlockSpec((1, l, w), lambda bi, gi, qi: (bi, 0, ng + gi)),
                  pl.BlockSpec((1, l, w), lambda bi, gi, qi: (bi, 0, 2 * ng + gi)),
                  pl.BlockSpec((1, tq, w), lambda bi, gi, qi: (bi, qi, 3 * ng + gi)),
                  pl.BlockSpec((1, FOX_HEADS, 1, l), lambda bi, gi, qi: (bi, gi, 0, 0))],
        out_specs=pl.BlockSpec((1, tq, w), lambda bi, gi, qi: (bi, qi, gi)),
        out_shape=jax.ShapeDtypeStruct((b, l, C_HEADS * C_DH), BF16),
        scratch_shapes=[pltpu.VMEM((FOX_HEADS, l, C_DH), BF16),
                        pltpu.VMEM((FOX_HEADS, C_DH + FOX_PAD, l), BF16),
                        pltpu.VMEM((FOX_HEADS, 2 * C_DH, tq), BF16),
                        pltpu.VMEM((FOX_HEADS, tq, tq), F32),
                        pltpu.VMEM((FOX_HEADS, 1, tq), F32),
                        pltpu.VMEM((FOX_HEADS, C_DH + FOX_PAD, tq), F32)],
        compiler_params=_cparams(("arbitrary", "arbitrary", "arbitrary")),
    )(proj3, proj3, proj3, proj3, proj3, cum4)


def _odd_out_kernel(x_ref, o_ref, w_ref, g_ref, out_ref):
    x2 = x_ref[...] + jnp.dot(o_ref[...], w_ref[...], preferred_element_type=F32)
    out_ref[...] = _rms(x2, g_ref[...])


def _odd_out(x1, o, w_out, g):
    m, d = x1.shape
    k = o.shape[1]
    tm = min(TM_OUT, m)
    return pl.pallas_call(
        _odd_out_kernel,
        name="odd_out",
        grid=(m // tm,),
        in_specs=[pl.BlockSpec((tm, d), lambda i: (i, 0)),
                  pl.BlockSpec((tm, k), lambda i: (i, 0)),
                  pl.BlockSpec((k, d), lambda i: (0, 0)),
                  pl.BlockSpec((1, d), lambda i: (0, 0))],
        out_specs=pl.BlockSpec((tm, d), lambda i: (i, 0)),
        out_shape=jax.ShapeDtypeStruct((m, d), F32),
        compiler_params=_cparams(("arbitrary",)),
    )(x1, o, w_out, g.reshape(1, d))


def kernel(x, norm_even_g, w_in_even, rel_bias, s5_lambda_re, s5_lambda_im, s5_log_dt,
           s5_b_re, s5_b_im, s5_c_re, s5_c_im, s5_d, w_glu, b_glu, w_out_even,
           norm_odd_g, w_in_odd, b_forget, w_out_odd, final_norm_g):
    b, l, d = x.shape
    assert norm_even_g.shape[0] == 1 and norm_odd_g.shape[0] == 1, "one even + one odd layer"
    assert l % FOX_TQ == 0 and l % VT_BLK == 0 and l % A_QB == 0 and l >= A_KB
    m = b * l
    a_w = A_HEADS * A_DH
    s5_w = s5_d.shape[1] * S5_GROUP
    c_w = C_HEADS * C_DH

    x2 = x.reshape(m, d)
    assert s5_w == TN_PROJ and l % S5_TC == 0
    proj, u3 = _norm_proj(x2, norm_even_g[0], w_in_even[0].astype(BF16), 3 * a_w)
    proj3 = proj.reshape(b, l, proj.shape[1])
    o_a = _attn_a(proj3, rel_bias[0])
    y = _s5_mixer(u3, b, s5_lambda_re[0], s5_lambda_im[0], s5_log_dt[0], s5_b_re[0],
                  s5_b_im[0], s5_c_re[0], s5_c_im[0], s5_d[0])
    w_odd_t = w_in_odd[0].T
    wf_t = w_odd_t[4 * c_w:].astype(BF16)
    x1, hn1, f_t = _even_out(x2, o_a.reshape(m, a_w), y, proj,
                             w_glu[0].astype(BF16), b_glu[0].astype(F32),
                             w_out_even[0].astype(BF16), norm_odd_g[0], wf_t)
    proj1 = _proj_odd(hn1, w_odd_t, 4 * c_w)
    cum = _cum_forget(f_t, b_forget[0], l)
    cum4 = cum.reshape(C_HEADS, b, 1, l).transpose(1, 0, 2, 3)
    o_c = _fox(proj1.reshape(b, l, 4 * c_w), cum4)
    out = _odd_out(x1, o_c.reshape(m, c_w), w_out_odd[0].astype(BF16), final_norm_g)
    return out.reshape(b, l, d)
```

```python
import functools
import math

import jax
import jax.numpy as jnp
from jax import lax
from jax.experimental import pallas as pl
from jax.experimental.pallas import tpu as pltpu

F32 = jnp.float32
BF16 = jnp.bfloat16

EPS = 1e-6
NEG = -1e30
LOG2E = math.log2(math.e)

CHUNK = 64
N_LEFT = 8
A_HEADS = 16
A_DH = 64
MAX_REL = 128
S5_GROUP = 16
S5_STATE = 64
C_HEADS = 16
C_DH = 128

LANES = 128
VMEM_LIMIT = 56 * 1024 * 1024

TM_PROJ = 1024
TN_PROJ = 1024
TM_OUT = 512
TM_PROJ_ODD = 2048
A_QB = 2 * CHUNK
A_KB = A_QB + N_LEFT * CHUNK
A_TW = A_KB + N_LEFT * CHUNK
A_PAD = 16
VT_BLK = 512
S5_TC = 16
S5_GPB = LANES // S5_GROUP
FOX_TQ = 512
FOX_HEADS = 4
FOX_PAD = 16


def _cparams(sem):
    return pltpu.CompilerParams(dimension_semantics=sem, vmem_limit_bytes=VMEM_LIMIT)


def _rms(x, g):
    ms = jnp.mean(x * x, axis=-1, keepdims=True)
    return x * lax.rsqrt(ms + EPS) * g


def _silu(x):
    return x * jax.nn.sigmoid(x)


def _gelu_tanh(x):
    c = math.sqrt(2.0 / math.pi)
    return 0.5 * x * (1.0 + jnp.tanh(c * (x + 0.044715 * (x * x * x))))


def _norm_proj_kernel(x_ref, g_ref, w_ref, o_ref, u_ref, hn_ref, us_ref, *, u_tile):
    j = pl.program_id(1)

    @pl.when(j == 0)
    def _():
        hn_ref[...] = _rms(x_ref[...], g_ref[...]).astype(BF16)

    acc = jnp.dot(hn_ref[...], w_ref[...], preferred_element_type=F32)
    o_ref[...] = acc.astype(o_ref.dtype)

    @pl.when(j == u_tile)
    def _():
        nchunk = us_ref.shape[1] // S5_TC
        for blk in range(u_ref.shape[0]):
            us_ref[blk] = acc[:, blk * LANES:(blk + 1) * LANES]
            for t in range(S5_TC):
                rows = us_ref[blk, pl.ds(t, nchunk, stride=S5_TC), :]
                u_ref[blk, :, t * LANES:(t + 1) * LANES] = rows.astype(u_ref.dtype)


def _norm_proj(x2, g, w, u_start):
    m, d = x2.shape
    n = w.shape[1]
    tm, tn = min(TM_PROJ, m), TN_PROJ
    assert u_start % tn == 0 and tm % (16 * S5_TC) == 0
    nblk = tn // LANES
    return pl.pallas_call(
        functools.partial(_norm_proj_kernel, u_tile=u_start // tn),
        name="norm_proj",
        grid=(m // tm, n // tn),
        in_specs=[pl.BlockSpec((tm, d), lambda i, j: (i, 0)),
                  pl.BlockSpec((1, d), lambda i, j: (0, 0)),
                  pl.BlockSpec((d, tn), lambda i, j: (0, j))],
        out_specs=[pl.BlockSpec((tm, tn), lambda i, j: (i, j)),
                   pl.BlockSpec((nblk, tm // S5_TC, S5_TC * LANES), lambda i, j: (0, i, 0))],
        out_shape=[jax.ShapeDtypeStruct((m, n), BF16),
                   jax.ShapeDtypeStruct((nblk, m // S5_TC, S5_TC * LANES), BF16)],
        scratch_shapes=[pltpu.VMEM((tm, d), BF16), pltpu.VMEM((nblk, tm, LANES), F32)],
        compiler_params=_cparams(("arbitrary", "arbitrary")),
    )(x2, g.reshape(1, d), w)


def _proj_odd_kernel(h_ref, wt_ref, o_ref, wb_ref):
    @pl.when(pl.program_id(1) == 0)
    def _():
        wb_ref[...] = wt_ref[...].T.astype(BF16)

    o_ref[...] = jnp.dot(h_ref[...], wb_ref[...],
                         preferred_element_type=F32).astype(o_ref.dtype)


def _proj_odd(hn, wt, n):
    m, d = hn.shape
    tm, tn = min(TM_PROJ_ODD, m), TN_PROJ
    assert n % tn == 0 and m % tm == 0
    return pl.pallas_call(
        _proj_odd_kernel,
        name="proj_odd",
        grid=(n // tn, m // tm),
        in_specs=[pl.BlockSpec((tm, d), lambda j, i: (i, 0)),
                  pl.BlockSpec((tn, d), lambda j, i: (j, 0))],
        out_specs=pl.BlockSpec((tm, tn), lambda j, i: (i, j)),
        out_shape=jax.ShapeDtypeStruct((m, n), BF16),
        scratch_shapes=[pltpu.VMEM((d, tn), BF16)],
        compiler_params=_cparams(("arbitrary", "arbitrary")),
    )(hn, wt)


def _attn_a_kernel(q_ref, k_ref, v_ref, z_ref, e_ref, o_ref,
                   tt_ref, vt_ref, qt_ref, s_ref, pv_ref):
    l_all = k_ref.shape[1]
    w = q_ref.shape[2]
    nt = (((1,), (1,)), ((), ()))
    band = N_LEFT * CHUNK

    @pl.when(pl.program_id(1) == 0)
    def _():
        i = lax.broadcasted_iota(jnp.int32, (A_QB, A_TW), 0)
        m = lax.broadcasted_iota(jnp.int32, (A_QB, A_TW), 1)
        dchunk = m // CHUNK - i // CHUNK
        ok = (dchunk >= 0) & (dchunk <= N_LEFT)
        for h in range(2):
            rows = jnp.broadcast_to(e_ref[0, h:h + 1, :], (A_QB, A_QB + A_TW))
            toep = pltpu.roll(rows, 0, 1, stride=1, stride_axis=0)[:, A_QB:]
            t = jnp.where(ok, toep * LOG2E, NEG)
            tt_ref[:, h * A_QB:(h + 1) * A_QB] = t.T

    r = lax.broadcasted_iota(jnp.int32, (w, w), 0)
    c = lax.broadcasted_iota(jnp.int32, (w, w), 1)
    eye = (r == c).astype(BF16)
    ones_row = (lax.broadcasted_iota(jnp.int32, (A_PAD, VT_BLK), 0) == 0).astype(BF16)

    qscale = LOG2E / math.sqrt(A_DH)

    def vt_blk(j, carry):
        ks = pl.multiple_of(j * VT_BLK, VT_BLK)
        vt = lax.dot_general(eye, v_ref[0, pl.ds(ks, VT_BLK), :], nt, preferred_element_type=F32)
        for h in range(2):
            vt_ref[h, :A_DH, pl.ds(ks, VT_BLK)] = vt[h * A_DH:(h + 1) * A_DH].astype(BF16)
            vt_ref[h, A_DH:, pl.ds(ks, VT_BLK)] = ones_row
        qs = (q_ref[0, pl.ds(ks, VT_BLK), :].astype(F32) * qscale).astype(BF16)
        qt_ref[:, pl.ds(ks, VT_BLK)] = lax.dot_general(
            eye, qs, nt, preferred_element_type=F32).astype(BF16)
        return carry

    lax.fori_loop(0, l_all // VT_BLK, vt_blk, 0)

    head_row = lax.broadcasted_iota(jnp.int32, (w, A_QB), 0)

    def window(qb):
        start = qb * A_QB - band
        start_c = jnp.maximum(start, 0)
        return pl.multiple_of(start_c, LANES), pl.multiple_of(start_c - start, LANES)

    def produce(qb, slot):
        qs = pl.multiple_of(qb * A_QB, A_QB)
        start_c, off = window(qb)
        qt = qt_ref[:, pl.ds(qs, A_QB)]
        zero = jnp.zeros_like(qt)
        qbd = jnp.concatenate([jnp.where(head_row < A_DH, qt, zero),
                               jnp.where(head_row >= A_DH, qt, zero)], axis=1)
        kb = k_ref[0, pl.ds(start_c, A_KB), :]
        s = jnp.dot(kb, qbd, preferred_element_type=F32)
        s = s + tt_ref[pl.ds(off, A_KB), :]
        s_ref[slot] = s
        return jnp.max(s, axis=0, keepdims=True)

    def consume(qb, slot, m):
        start_c, _ = window(qb)
        pb = jnp.exp2(s_ref[slot] - m).astype(BF16)
        for h in range(2):
            vt = vt_ref[h, :, pl.ds(start_c, A_KB)]
            pv_ref[slot, h] = jnp.dot(vt, pb[:, h * A_QB:(h + 1) * A_QB],
                                      preferred_element_type=F32)

    def finish(qb, slot):
        qs = pl.multiple_of(qb * A_QB, A_QB)
        outs = []
        for h in range(2):
            oh = pv_ref[slot, h]
            outs.append(oh[:A_DH] * (1.0 / oh[A_DH:A_DH + 1]))
        o = jnp.concatenate(outs, axis=0).T
        gate = _silu(z_ref[0, pl.ds(qs, A_QB), :].astype(F32))
        o_ref[0, pl.ds(qs, A_QB), :] = (o * gate).astype(o_ref.dtype)

    def pair(i, m, lead=True):
        j = 2 * i
        if lead:
            finish(j - 1, 1)
        m1 = produce(j + 1, 1)
        consume(j, 0, m)
        m2 = produce(j + 2, 0)
        finish(j, 0)
        consume(j + 1, 1, m1)
        return m2

    nb = l_all // A_QB
    npairs = (nb - 1) // 2
    m = pair(0, produce(0, 0), lead=False)
    m = lax.fori_loop(1, npairs, pair, m)
    last = 2 * npairs
    finish(last - 1, 1)
    if (nb - 1) % 2:
        m1 = produce(last + 1, 1)
        consume(last, 0, m)
        finish(last, 0)
        consume(last + 1, 1, m1)
        finish(last + 1, 1)
    else:
        consume(last, 0, m)
        finish(last, 0)


def _attn_a(proj3, rel_bias):
    b, l, _ = proj3.shape
    hp = A_HEADS // 2
    width = 2 * A_DH
    ncol = A_HEADS * A_DH // width
    zcol = (3 * A_HEADS * A_DH + (A_HEADS * A_DH)) // width
    left = N_LEFT * CHUNK + A_QB - MAX_REL
    right = A_QB + A_TW - left - (2 * MAX_REL + 1)
    ext = jnp.pad(rel_bias.astype(F32)[:, ::-1], ((0, 0), (left, right)), mode="edge")
    ext = ext.reshape(hp, 2, A_QB + A_TW)
    seq = lambda h, bi: (bi, 0, h)
    return pl.pallas_call(
        _attn_a_kernel,
        name="attn_a",
        grid=(hp, b),
        in_specs=[pl.BlockSpec((1, l, width), seq),
                  pl.BlockSpec((1, l, width), lambda h, bi: (bi, 0, ncol + h)),
                  pl.BlockSpec((1, l, width), lambda h, bi: (bi, 0, 2 * ncol + h)),
                  pl.BlockSpec((1, l, width), lambda h, bi: (bi, 0, zcol + h)),
                  pl.BlockSpec((1, 2, A_QB + A_TW), lambda h, bi: (h, 0, 0))],
        out_specs=pl.BlockSpec((1, l, width), seq),
        out_shape=jax.ShapeDtypeStruct((b, l, A_HEADS * A_DH), BF16),
        scratch_shapes=[pltpu.VMEM((A_TW, 2 * A_QB), F32),
                        pltpu.VMEM((2, A_DH + A_PAD, l), BF16),
                        pltpu.VMEM((width, l), BF16),
                        pltpu.VMEM((2, A_KB, 2 * A_QB), F32),
                        pltpu.VMEM((2, 2, A_DH + A_PAD, A_QB), F32)],
        compiler_params=_cparams(("arbitrary", "arbitrary")),
    )(proj3, proj3, proj3, proj3, ext)


def _cmul(ar, ai, br, bi):
    return ar * br - ai * bi, ar * bi + ai * br


def _s5_discretise(lr, li, log_dt):
    dt = jnp.exp(log_dt)
    mag = jnp.exp(lr * dt)
    ang = li * dt
    a_re = mag * jnp.cos(ang)
    a_im = mag * jnp.sin(ang)
    den = lr * lr + li * li
    nr = a_re - 1.0
    coef_re = (nr * lr + a_im * li) / den
    coef_im = (a_im * lr - nr * li) / den
    return (a_re, a_im), (coef_re, coef_im)


def _s5_weights_kernel(lam_pg_ref, ldt_pg_ref, c_pg_ref, lam_gp_ref, ldt_gp_ref, b_gp_ref,
                       lam_row_ref, ldt_row_ref, b_tl_ref, wm_ref, wr_ref, wo_ref, a_ref):
    tc, ch, ns, gpb = S5_TC, S5_GROUP, S5_STATE, S5_GPB
    lw = gpb * ch
    hi = lax.Precision.HIGHEST

    a_pg, _ = _s5_discretise(lam_pg_ref[0], lam_pg_ref[1], ldt_pg_ref[...])
    pw = (jnp.ones_like(a_pg[0]), jnp.zeros_like(a_pg[0]))
    ca = []
    for n in range(tc + 1):
        ca.append(_cmul(c_pg_ref[0], c_pg_ref[1], *pw))
        pw = _cmul(*pw, *a_pg)

    _, coef_gp = _s5_discretise(lam_gp_ref[0], lam_gp_ref[1], ldt_gp_ref[...])
    bb_gp = _cmul(*coef_gp, b_gp_ref[0], b_gp_ref[1])
    ca_re = jnp.concatenate([ca[n][0] for n in range(tc)], axis=1)
    ca_im = jnp.concatenate([ca[n][1] for n in range(tc)], axis=1)
    z = (jnp.dot(bb_gp[0], ca_re, precision=hi, preferred_element_type=F32)
         - jnp.dot(bb_gp[1], ca_im, precision=hi, preferred_element_type=F32))
    row_g = lax.broadcasted_iota(jnp.int32, z.shape, 0) // ch
    col_g = (lax.broadcasted_iota(jnp.int32, z.shape, 1) % lw) // ch
    z = jnp.where(row_g == col_g, z, 0.0).astype(BF16)
    for t in range(tc):
        parts = [z[:, :(tc - t) * lw]]
        if t:
            parts.insert(0, jnp.zeros((lw, t * lw), BF16))
        wm_ref[0, t * lw:(t + 1) * lw, :] = jnp.concatenate(parts, axis=1)

    rg = lax.broadcasted_iota(jnp.int32, (gpb * ns, lw), 0) // ns
    cg = lax.broadcasted_iota(jnp.int32, (gpb * ns, lw), 1) // ch
    keep = rg == cg
    for t2 in range(tc):
        cr, ci = ca[t2 + 1]
        cr = jnp.where(keep, jnp.concatenate([cr] * gpb, axis=0), 0.0)
        ci = jnp.where(keep, jnp.concatenate([ci] * gpb, axis=0), 0.0)
        wo_ref[0, :gpb * ns, t2 * lw:(t2 + 1) * lw] = cr.astype(BF16)
        wo_ref[0, gpb * ns:, t2 * lw:(t2 + 1) * lw] = (-ci).astype(BF16)

    a_row, coef_row = _s5_discretise(lam_row_ref[0], lam_row_ref[1], ldt_row_ref[...])
    bb_tl = _cmul(*coef_row, b_tl_ref[0], b_tl_ref[1])
    rg = lax.broadcasted_iota(jnp.int32, bb_tl[0].shape, 0) // ch
    cg = lax.broadcasted_iota(jnp.int32, bb_tl[0].shape, 1) // ns
    bb_tl = (jnp.where(rg == cg, bb_tl[0], 0.0), jnp.where(rg == cg, bb_tl[1], 0.0))
    pw = (jnp.ones_like(a_row[0]), jnp.zeros_like(a_row[0]))
    for t in range(tc - 1, -1, -1):
        r_re, r_im = _cmul(*pw, *bb_tl)
        wr_ref[0, t * lw:(t + 1) * lw, :] = jnp.concatenate([r_re, r_im], axis=1).astype(BF16)
        pw = _cmul(*pw, *a_row)
    a_ref[0, 0] = pw[0]
    a_ref[0, 1] = pw[1]


def _s5_weights(lam_re, lam_im, log_dt, b_re, b_im, c_re, c_im):
    g, ns = lam_re.shape
    ch, tc, gpb = S5_GROUP, S5_TC, S5_GPB
    nblk = g // gpb
    lw = gpb * ch
    lam = jnp.stack([lam_re, lam_im]).astype(F32)
    ldt = log_dt.astype(F32)
    bst = jnp.stack([b_re, b_im]).astype(F32)
    cst = jnp.stack([c_re, c_im]).astype(F32)
    lam_pg = jnp.repeat(lam.transpose(0, 2, 1), ch, axis=2)
    ldt_pg = jnp.repeat(ldt, ch)[None]
    c_pg = cst.transpose(0, 3, 1, 2).reshape(2, ns, g * ch)
    lam_gp = jnp.repeat(lam, ch, axis=1)
    ldt_gp = jnp.broadcast_to(jnp.repeat(ldt, ch)[:, None], (g * ch, ns))
    b_gp = bst.transpose(0, 1, 3, 2).reshape(2, g * ch, ns)
    lam_row = lam.reshape(2, 1, g * ns)
    ldt_row = jnp.repeat(ldt, ns)[None]
    b_tl = jnp.tile(b_gp, (1, 1, gpb))
    blk3 = lambda i: (0, 0, i)
    return pl.pallas_call(
        _s5_weights_kernel,
        name="s5_weights",
        grid=(nblk,),
        in_specs=[pl.BlockSpec((2, ns, lw), blk3),
                  pl.BlockSpec((1, lw), lambda i: (0, i)),
                  pl.BlockSpec((2, ns, lw), blk3),
                  pl.BlockSpec((2, lw, ns), lambda i: (0, i, 0)),
                  pl.BlockSpec((lw, ns), lambda i: (i, 0)),
                  pl.BlockSpec((2, lw, ns), lambda i: (0, i, 0)),
                  pl.BlockSpec((2, 1, gpb * ns), blk3),
                  pl.BlockSpec((1, gpb * ns), lambda i: (0, i)),
                  pl.BlockSpec((2, lw, gpb * ns), lambda i: (0, i, 0))],
        out_specs=[pl.BlockSpec((1, tc * lw, tc * lw), lambda i: (i, 0, 0)),
                   pl.BlockSpec((1, tc * lw, 2 * gpb * ns), lambda i: (i, 0, 0)),
                   pl.BlockSpec((1, 2 * gpb * ns, tc * lw), lambda i: (i, 0, 0)),
                   pl.BlockSpec((1, 2, 1, gpb * ns), lambda i: (i, 0, 0, 0))],
        out_shape=[jax.ShapeDtypeStruct((nblk, tc * lw, tc * lw), BF16),
                   jax.ShapeDtypeStruct((nblk, tc * lw, 2 * gpb * ns), BF16),
                   jax.ShapeDtypeStruct((nblk, 2 * gpb * ns, tc * lw), BF16),
                   jax.ShapeDtypeStruct((nblk, 2, 1, gpb * ns), F32)],
        compiler_params=_cparams(("arbitrary",)),
    )(lam_pg, ldt_pg, c_pg, lam_gp, ldt_gp, b_gp, lam_row, ldt_row, b_tl)


def _s5_mix_kernel(u_ref, wm_ref, wr_ref, wo_ref, a_ref, d_ref, y_ref, s_ref, x_ref, ys_ref):
    nk = u_ref.shape[1]
    half = s_ref.shape[1] // 2
    sub = 8
    u = u_ref[0]
    s_ref[...] = jnp.dot(u, wr_ref[0], preferred_element_type=F32)

    a1 = (a_ref[0, 0], a_ref[0, 1])
    a2 = _cmul(*a1, *a1)
    a4 = _cmul(*a2, *a2)
    a8 = _cmul(*a4, *a4)
    a3 = _cmul(*a2, *a1)
    pows = [(jnp.ones_like(a1[0]), jnp.zeros_like(a1[0])), a1, a2, a3, a4,
            _cmul(*a4, *a1), _cmul(*a4, *a2), _cmul(*a4, *a3)]
    p_re = jnp.concatenate([p[0] for p in pows], axis=0)
    p_im = jnp.concatenate([p[1] for p in pows], axis=0)
    rowid = lax.broadcasted_iota(jnp.int32, (sub, half), 0)
    levels = [(d, jnp.broadcast_to(a[0], (sub, half)), jnp.broadcast_to(a[1], (sub, half)))
              for d, a in ((1, a1), (2, a2), (4, a4))]

    def shift_rows(v, d):
        return jnp.where(rowid >= d, pltpu.roll(v, d, 0), 0.0)

    def tile(j, carry):
        cr, ci = carry
        rows = slice(j * sub, (j + 1) * sub)
        zr = s_ref[rows, :half]
        zi = s_ref[rows, half:]
        for d, br, bi in levels:
            sr, si = shift_rows(zr, d), shift_rows(zi, d)
            zr, zi = zr + br * sr - bi * si, zi + br * si + bi * sr
        x_ref[rows, :half] = p_re * cr - p_im * ci + shift_rows(zr, 1)
        x_ref[rows, half:] = p_re * ci + p_im * cr + shift_rows(zi, 1)
        return (a8[0] * cr - a8[1] * ci + zr[sub - 1:], a8[0] * ci + a8[1] * cr + zi[sub - 1:])

    carry = (jnp.zeros((1, half), F32), jnp.zeros((1, half), F32))
    for j in range(nk // sub):
        carry = tile(j, carry)
    xb = x_ref[...].astype(BF16)

    cw = 2 * LANES
    for n in range(u.shape[1] // cw):
        cs = slice(n * cw, (n + 1) * cw)
        y = jnp.dot(u[:, :(n + 1) * cw], wm_ref[0, :(n + 1) * cw, cs], preferred_element_type=F32)
        y = y + jnp.dot(xb, wo_ref[0, :, cs], preferred_element_type=F32)
        y = _gelu_tanh(y + d_ref[0, :, cs] * u[:, cs].astype(F32))
        for t in range(2 * n, 2 * n + 2):
            ys_ref[pl.ds(t, nk, stride=S5_TC), :] = y[:, (t - 2 * n) * LANES:(t - 2 * n + 1) * LANES]
    y_ref[0] = ys_ref[...].astype(y_ref.dtype)


def _s5_mixer(u3, nb, lam_re, lam_im, log_dt, b_re, b_im, c_re, c_im, d_skip):
    nblk, rows, w = u3.shape
    tc = S5_TC
    nk = rows // nb
    assert nk % 8 == 0
    wm, wr, wo, a_pow = _s5_weights(lam_re, lam_im, log_dt, b_re, b_im, c_re, c_im)
    ns = wr.shape[-1]
    d_rows = jnp.tile(d_skip.astype(F32).reshape(nblk, 1, LANES), (1, 1, tc))
    blk = lambda i, b: (i, 0, 0)
    return pl.pallas_call(
        _s5_mix_kernel,
        name="s5_mix",
        grid=(nblk, nb),
        in_specs=[pl.BlockSpec((1, nk, w), lambda i, b: (i, b, 0)),
                  pl.BlockSpec((1, w, w), blk),
                  pl.BlockSpec((1, w, ns), blk),
                  pl.BlockSpec((1, ns, w), blk),
                  pl.BlockSpec((1, 2, 1, ns // 2), lambda i, b: (i, 0, 0, 0)),
                  pl.BlockSpec((1, 1, w), blk)],
        out_specs=pl.BlockSpec((1, nk * tc, LANES), lambda i, b: (i, b, 0)),
        out_shape=jax.ShapeDtypeStruct((nblk, rows * tc, LANES), BF16),
        scratch_shapes=[pltpu.VMEM((nk, ns), F32), pltpu.VMEM((nk, ns), F32),
                        pltpu.VMEM((nk * tc, LANES), F32)],
        compiler_params=_cparams(("arbitrary", "arbitrary")),
    )(u3, wm, wr, wo, a_pow, d_rows)


def _even_out_kernel(x_ref, oa_ref, y_ref, zb_ref, wg_ref, bg_ref, wo_ref, g_ref, wf_ref,
                     x1_ref, hn_ref, f_ref):
    y = jnp.concatenate([y_ref[i] for i in range(y_ref.shape[0])], axis=1)
    t = jnp.dot(y, wg_ref[...], preferred_element_type=F32) + bg_ref[...]
    ob = y.astype(F32) * jax.nn.sigmoid(t) * _silu(zb_ref[...].astype(F32))
    wa = oa_ref.shape[1]
    acc = jnp.dot(oa_ref[...], wo_ref[:wa, :], preferred_element_type=F32)
    acc = acc + jnp.dot(ob.astype(BF16), wo_ref[wa:, :], preferred_element_type=F32)
    x1 = x_ref[...] + acc
    x1_ref[...] = x1
    hn = _rms(x1, g_ref[...]).astype(BF16)
    hn_ref[...] = hn
    f_ref[...] = lax.dot_general(wf_ref[...], hn, (((1,), (1,)), ((), ())),
                                 preferred_element_type=F32)


def _even_out(x2, oa, y, proj, w_glu, b_glu, w_out, g_next, wf_t):
    m, d = x2.shape
    nh = wf_t.shape[0]
    wa = oa.shape[1]
    nblk, _, lw = y.shape
    wb = nblk * lw
    tm = min(TM_OUT, m)
    zb_col = (proj.shape[1] - wb) // wb
    const = lambda i: (0, 0)
    return pl.pallas_call(
        _even_out_kernel,
        name="even_out",
        grid=(m // tm,),
        in_specs=[pl.BlockSpec((tm, d), lambda i: (i, 0)),
                  pl.BlockSpec((tm, wa), lambda i: (i, 0)),
                  pl.BlockSpec((nblk, tm, lw), lambda i: (0, i, 0)),
                  pl.BlockSpec((tm, wb), lambda i: (i, zb_col)),
                  pl.BlockSpec((wb, wb), const),
                  pl.BlockSpec((1, wb), const),
                  pl.BlockSpec((wa + wb, d), const),
                  pl.BlockSpec((1, d), const),
                  pl.BlockSpec((nh, d), const)],
        out_specs=[pl.BlockSpec((tm, d), lambda i: (i, 0)),
                   pl.BlockSpec((tm, d), lambda i: (i, 0)),
                   pl.BlockSpec((nh, tm), lambda i: (0, i))],
        out_shape=[jax.ShapeDtypeStruct((m, d), F32),
                   jax.ShapeDtypeStruct((m, d), BF16),
                   jax.ShapeDtypeStruct((nh, m), F32)],
        compiler_params=_cparams(("arbitrary",)),
    )(x2, oa, y, proj, w_glu, b_glu.reshape(1, wb), w_out, g_next.reshape(1, d), wf_t)


def _cum_forget_kernel(f_ref, b_ref, c_ref):
    x = f_ref[...] + b_ref[...]
    v = -(jnp.maximum(-x, 0.0) + jnp.log1p(jnp.exp(-jnp.abs(x))))
    n = v.shape[-1]
    pos = lax.broadcasted_iota(jnp.int32, v.shape, 1)
    d = 1
    while d < n:
        v = v + jnp.where(pos >= d, pltpu.roll(v, d, 1), 0.0)
        d *= 2
    c_ref[...] = v


def _cum_forget(f_t, b_forget, seq):
    nh, m = f_t.shape
    return pl.pallas_call(
        _cum_forget_kernel,
        name="cum_forget",
        grid=(m // seq,),
        in_specs=[pl.BlockSpec((nh, seq), lambda i: (0, i)),
                  pl.BlockSpec((nh, 1), lambda i: (0, 0))],
        out_specs=pl.BlockSpec((nh, seq), lambda i: (0, i)),
        out_shape=jax.ShapeDtypeStruct((nh, m), F32),
        compiler_params=_cparams(("arbitrary",)),
    )(f_t, b_forget.astype(F32).reshape(nh, 1))


def _split3(x):
    hi = x.astype(BF16).astype(F32)
    r = x - hi
    mid = r.astype(BF16).astype(F32)
    lo = (r - mid).astype(BF16).astype(F32)
    return hi, mid, lo


def _fox_kernel(q0_ref, qn_ref, k_ref, v_ref, z_ref, c_ref, o_ref,
                ke_ref, vt_ref, qa_ref, s_ref, mb_ref, acc_ref):
    qi = pl.program_id(2)
    nq = pl.num_programs(2)
    tq = qn_ref.shape[1]
    tk = tq
    dh = C_DH
    nh = c_ref.shape[1]
    nblk = k_ref.shape[1] // tk
    nt = (((1,), (1,)), ((), ()))

    def cols(h):
        return slice(h * dh, (h + 1) * dh)

    def prep_queries(q_ref, qb):
        qs = pl.multiple_of(qb * tq, tq)
        row = lax.broadcasted_iota(jnp.int32, (dh, tq), 0)
        for h in range(nh):
            c_base = jnp.max(c_ref[0, h, :, pl.ds(qs, tq)], axis=1, keepdims=True) * LOG2E
            bh, bm, bl = _split3(c_base)
            g = jnp.where(row < 3, 1.0, jnp.where(row == 3, bh, jnp.where(row == 4, bm,
                          jnp.where(row == 5, bl, 0.0))))
            qt = (q_ref[0, :, cols(h)].astype(F32) * (LOG2E / math.sqrt(dh))).T
            qa_ref[h, :dh, :] = qt.astype(BF16)
            qa_ref[h, dh:, :] = g.astype(BF16)

    half = tk // 2

    def produce(h, j, diagonal):
        ks = pl.multiple_of(j * tk, tk)

        def keys(start, n):
            return jnp.concatenate([k_ref[0, pl.ds(start, n), cols(h)],
                                    ke_ref[h, pl.ds(start, n), :]], axis=1)

        if not diagonal:
            s = jnp.dot(keys(ks, tk), qa_ref[h], preferred_element_type=F32)
            s_ref[h] = s
            return jnp.max(s, axis=0, keepdims=True)
        top = jnp.dot(keys(ks, half), qa_ref[h], preferred_element_type=F32)
        bot = jnp.dot(keys(ks + half, half), qa_ref[h, :, half:],
                      preferred_element_type=F32)
        top = jnp.where(lax.broadcasted_iota(jnp.int32, top.shape, 0)
                        <= lax.broadcasted_iota(jnp.int32, top.shape, 1), top, NEG)
        bot = jnp.where(lax.broadcasted_iota(jnp.int32, bot.shape, 0)
                        <= lax.broadcasted_iota(jnp.int32, bot.shape, 1), bot, NEG)
        s_ref[h, :half, :] = top
        s_ref[h, half:, half:] = bot
        bot_wide = jnp.concatenate([jnp.full((half, half), NEG, F32), bot], axis=1)
        return jnp.max(jnp.maximum(top, bot_wide), axis=0, keepdims=True)

    def consume(h, j, stats, diagonal=False):
        m, mb = stats
        ks = pl.multiple_of(j * tk, tk)
        m2 = jnp.maximum(m, mb)
        if not diagonal:
            p = jnp.exp2(s_ref[h] - m2).astype(BF16)
            pv = jnp.dot(vt_ref[h, :, pl.ds(ks, tk)], p, preferred_element_type=F32)
            acc_ref[h] = jnp.exp2(m - m2) * acc_ref[h] + pv
            return m2
        p_top = jnp.exp2(s_ref[h, :half, :] - m2).astype(BF16)
        mb_ref[h] = m2
        p_bot = jnp.exp2(s_ref[h, half:, half:] - mb_ref[h, :, half:]).astype(BF16)
        acc = jnp.exp2(m - m2) * acc_ref[h] + jnp.dot(vt_ref[h, :, pl.ds(ks, half)], p_top,
                                                      preferred_element_type=F32)
        acc_ref[h, :, :half] = acc[:, :half]
        acc_ref[h, :, half:] = acc[:, half:] + jnp.dot(vt_ref[h, :, pl.ds(ks + half, half)], p_bot,
                                                       preferred_element_type=F32)
        return m2

    def step(j, carry, diagonal_next):
        return tuple((consume(h, j, stats), produce(h, j + 1, diagonal_next))
                     for h, stats in enumerate(carry))

    @pl.when(qi == 0)
    def _():
        r = lax.broadcasted_iota(jnp.int32, (dh, dh), 0)
        c = lax.broadcasted_iota(jnp.int32, (dh, dh), 1)
        eye = (r == c).astype(BF16)
        row = lax.broadcasted_iota(jnp.int32, (dh, tk), 0)
        ones_row = (lax.broadcasted_iota(jnp.int32, (FOX_PAD, tk), 0) == 0).astype(BF16)

        def blk(j, carry):
            ks = pl.multiple_of(j * tk, tk)
            for h in range(nh):
                hi, mid, lo = _split3(c_ref[0, h, :, pl.ds(ks, tk)] * (-LOG2E))
                a = jnp.where(row == 0, hi, jnp.where(row == 1, mid, jnp.where(row == 2, lo,
                              jnp.where(row < 6, 1.0, 0.0))))
                ke_ref[h, pl.ds(ks, tk), :] = a.T.astype(BF16)
                vt = lax.dot_general(eye, v_ref[0, pl.ds(ks, tk), cols(h)], nt,
                                     preferred_element_type=F32)
                vt_ref[h, :dh, pl.ds(ks, tk)] = vt.astype(BF16)
                vt_ref[h, dh:, pl.ds(ks, tk)] = ones_row
            return carry

        lax.fori_loop(0, nblk, blk, 0)
        prep_queries(q0_ref, 0)
        for h in range(nh):
            mb_ref[h] = produce(h, 0, True)

    carry = []
    for h in range(nh):
        acc_ref[h] = jnp.zeros(acc_ref.shape[1:], F32)
        carry.append((jnp.full((1, tq), NEG, F32), mb_ref[h]))
    plain = functools.partial(step, diagonal_next=False)
    n_plain = jnp.maximum(qi - 1, 0)
    n_pairs = n_plain // 2
    carry = lax.fori_loop(0, n_pairs, lambda i, c: plain(2 * i + 1, plain(2 * i, c)),
                          tuple(carry))
    carry = lax.fori_loop(2 * n_pairs, n_plain, plain, carry)
    carry = lax.fori_loop(n_plain, qi, functools.partial(step, diagonal_next=True), carry)
    for h, stats in enumerate(carry):
        consume(h, qi, stats, diagonal=True)
    prep_queries(qn_ref, jnp.minimum(qi + 1, nq - 1))
    for h in range(nh):
        mb_ref[h] = produce(h, 0, False)
    for h in range(nh):
        acc = acc_ref[h]
        o = (acc[:dh] / acc[dh:dh + 1]).T
        o_ref[0, :, cols(h)] = (o * _silu(z_ref[0, :, cols(h)].astype(F32))).astype(o_ref.dtype)


def _fox(proj3, cum4):
    b, l, _ = proj3.shape
    ng = C_HEADS // FOX_HEADS
    w = FOX_HEADS * C_DH
    tq = min(FOX_TQ, l)
    nq = l // tq
    return pl.pallas_call(
        _fox_kernel,
        name="fox",
        grid=(b, ng, nq),
        in_specs=[pl.BlockSpec((1, tq, w), lambda bi, gi, qi: (bi, 0, gi)),
                  pl.BlockSpec((1, tq, w), lambda bi, gi, qi: (bi, jnp.minimum(qi + 1, nq - 1), gi)),
                  pl.BlockSpec((1, l, w), lambda bi, gi, qi: (bi, 0, ng + gi)),
                  pl.BlockSpec((1, l, w), lambda bi, gi, qi: (bi, 0, 2 * ng + gi)),
                  pl.BlockSpec((1, tq, w), lambda bi, gi, qi: (bi, qi, 3 * ng + gi)),
                  pl.BlockSpec((1, FOX_HEADS, 1, l), lambda bi, gi, qi: (bi, gi, 0, 0))],
        out_specs=pl.BlockSpec((1, tq, w), lambda bi, gi, qi: (bi, qi, gi)),
        out_shape=jax.ShapeDtypeStruct((b, l, C_HEADS * C_DH), BF16),
        scratch_shapes=[pltpu.VMEM((FOX_HEADS, l, C_DH), BF16),
                        pltpu.VMEM((FOX_HEADS, C_DH + FOX_PAD, l), BF16),
                        pltpu.VMEM((FOX_HEADS, 2 * C_DH, tq), BF16),
                        pltpu.VMEM((FOX_HEADS, tq, tq), F32),
                        pltpu.VMEM((FOX_HEADS, 1, tq), F32),
                        pltpu.VMEM((FOX_HEADS, C_DH + FOX_PAD, tq), F32)],
        compiler_params=_cparams(("arbitrary", "arbitrary", "arbitrary")),
    )(proj3, proj3, proj3, proj3, proj3, cum4)


def _odd_out_kernel(x_ref, o_ref, w_ref, g_ref, out_ref):
    x2 = x_ref[...] + jnp.dot(o_ref[...], w_ref[...], preferred_element_type=F32)
    out_ref[...] = _rms(x2, g_ref[...])


def _odd_out(x1, o, w_out, g):
    m, d = x1.shape
    k = o.shape[1]
    tm = min(TM_OUT, m)
    return pl.pallas_call(
        _odd_out_kernel,
        name="odd_out",
        grid=(m // tm,),
        in_specs=[pl.BlockSpec((tm, d), lambda i: (i, 0)),
                  pl.BlockSpec((tm, k), lambda i: (i, 0)),
                  pl.BlockSpec((k, d), lambda i: (0, 0)),
                  pl.BlockSpec((1, d), lambda i: (0, 0))],
        out_specs=pl.BlockSpec((tm, d), lambda i: (i, 0)),
        out_shape=jax.ShapeDtypeStruct((m, d), F32),
        compiler_params=_cparams(("arbitrary",)),
    )(x1, o, w_out, g.reshape(1, d))


def kernel(x, norm_even_g, w_in_even, rel_bias, s5_lambda_re, s5_lambda_im, s5_log_dt,
           s5_b_re, s5_b_im, s5_c_re, s5_c_im, s5_d, w_glu, b_glu, w_out_even,
           norm_odd_g, w_in_odd, b_forget, w_out_odd, final_norm_g):
    b, l, d = x.shape
    assert norm_even_g.shape[0] == 1 and norm_odd_g.shape[0] == 1, "one even + one odd layer"
    assert l % FOX_TQ == 0 and l % VT_BLK == 0 and l % A_QB == 0 and l >= A_KB
    m = b * l
    a_w = A_HEADS * A_DH
    s5_w = s5_d.shape[1] * S5_GROUP
    c_w = C_HEADS * C_DH

    x2 = x.reshape(m, d)
    assert s5_w == TN_PROJ and l % S5_TC == 0
    proj, u3 = _norm_proj(x2, norm_even_g[0], w_in_even[0].astype(BF16), 3 * a_w)
    proj3 = proj.reshape(b, l, proj.shape[1])
    o_a = _attn_a(proj3, rel_bias[0])
    y = _s5_mixer(u3, b, s5_lambda_re[0], s5_lambda_im[0], s5_log_dt[0], s5_b_re[0],
                  s5_b_im[0], s5_c_re[0], s5_c_im[0], s5_d[0])
    w_odd_t = w_in_odd[0].T
    wf_t = w_odd_t[4 * c_w:].astype(BF16)
    x1, hn1, f_t = _even_out(x2, o_a.reshape(m, a_w), y, proj,
                             w_glu[0].astype(BF16), b_glu[0].astype(F32),
                             w_out_even[0].astype(BF16), norm_odd_g[0], wf_t)
    proj1 = _proj_odd(hn1, w_odd_t, 4 * c_w)
    cum = _cum_forget(f_t, b_forget[0], l)
    cum4 = cum.reshape(C_HEADS, b, 1, l).transpose(1, 0, 2, 3)
    o_c = _fox(proj1.reshape(b, l, 4 * c_w), cum4)
    out = _odd_out(x1, o_c.reshape(m, c_w), w_out_odd[0].astype(BF16), final_norm_g)
    return out.reshape(b, l, d)
```

```python
import functools
import math

import jax
import jax.numpy as jnp
from jax import lax
from jax.experimental import pallas as pl
from jax.experimental.pallas import tpu as pltpu

F32 = jnp.float32
BF16 = jnp.bfloat16

EPS = 1e-6
NEG = -1e30
LOG2E = math.log2(math.e)

CHUNK = 64
N_LEFT = 8
A_HEADS = 16
A_DH = 64
MAX_REL = 128
S5_GROUP = 16
S5_STATE = 64
C_HEADS = 16
C_DH = 128

LANES = 128
VMEM_LIMIT = 56 * 1024 * 1024

TM_PROJ = 1024
TN_PROJ = 1024
PROJ_CHUNKS = 4
TM_OUT = 512
TM_PROJ_ODD = 2048
A_QB = 2 * CHUNK
A_KB = A_QB + N_LEFT * CHUNK
A_TW = A_KB + N_LEFT * CHUNK
A_PAD = 16
VT_BLK = 512
S5_TC = 16
S5_GPB = LANES // S5_GROUP
FOX_TQ = 512
FOX_HEADS = 4
FOX_PAD = 16


def _cparams(sem):
    return pltpu.CompilerParams(dimension_semantics=sem, vmem_limit_bytes=VMEM_LIMIT)


def _rms(x, g):
    ms = jnp.mean(x * x, axis=-1, keepdims=True)
    return x * lax.rsqrt(ms + EPS) * g


def _silu(x):
    return x * jax.nn.sigmoid(x)


def _gelu_tanh(x):
    c = math.sqrt(2.0 / math.pi)
    return 0.5 * x * (1.0 + jnp.tanh(c * (x + 0.044715 * (x * x * x))))


def _norm_proj_kernel(x_ref, g_ref, w_ref, o_ref, u_ref, hn_ref, us_ref, *, u_tile):
    j = pl.program_id(1)
    tm = x_ref.shape[0]

    @pl.when(j == 0)
    def _():
        for c in range(PROJ_CHUNKS):
            rows = slice(c * tm // PROJ_CHUNKS, (c + 1) * tm // PROJ_CHUNKS)
            h = _rms(x_ref[rows, :], g_ref[...]).astype(BF16)
            hn_ref[rows, :] = h
            o_ref[rows, :] = jnp.dot(h, w_ref[...], preferred_element_type=F32).astype(o_ref.dtype)

    @pl.when(j != 0)
    def _():
        acc = jnp.dot(hn_ref[...], w_ref[...], preferred_element_type=F32)
        o_ref[...] = acc.astype(o_ref.dtype)

        @pl.when(j == u_tile)
        def _():
            nchunk = us_ref.shape[1] // S5_TC
            for blk in range(u_ref.shape[0]):
                us_ref[blk] = acc[:, blk * LANES:(blk + 1) * LANES]
                for t in range(S5_TC):
                    rows = us_ref[blk, pl.ds(t, nchunk, stride=S5_TC), :]
                    u_ref[blk, :, t * LANES:(t + 1) * LANES] = rows.astype(u_ref.dtype)


def _norm_proj(x2, g, w, u_start):
    m, d = x2.shape
    n = w.shape[1]
    tm, tn = min(TM_PROJ, m), TN_PROJ
    assert u_start % tn == 0 and u_start > 0 and tm % (16 * S5_TC) == 0
    nblk = tn // LANES
    return pl.pallas_call(
        functools.partial(_norm_proj_kernel, u_tile=u_start // tn),
        name="norm_proj",
        grid=(m // tm, n // tn),
        in_specs=[pl.BlockSpec((tm, d), lambda i, j: (i, 0)),
                  pl.BlockSpec((1, d), lambda i, j: (0, 0)),
                  pl.BlockSpec((d, tn), lambda i, j: (0, j))],
        out_specs=[pl.BlockSpec((tm, tn), lambda i, j: (i, j)),
                   pl.BlockSpec((nblk, tm // S5_TC, S5_TC * LANES), lambda i, j: (0, i, 0))],
        out_shape=[jax.ShapeDtypeStruct((m, n), BF16),
                   jax.ShapeDtypeStruct((nblk, m // S5_TC, S5_TC * LANES), BF16)],
        scratch_shapes=[pltpu.VMEM((tm, d), BF16), pltpu.VMEM((nblk, tm, LANES), F32)],
        compiler_params=_cparams(("arbitrary", "arbitrary")),
    )(x2, g.reshape(1, d), w)


def _proj_odd_kernel(h_ref, wt_ref, o_ref, wb_ref):
    @pl.when(pl.program_id(1) == 0)
    def _():
        wb_ref[...] = wt_ref[...].T.astype(BF16)

    o_ref[...] = jnp.dot(h_ref[...], wb_ref[...],
                         preferred_element_type=F32).astype(o_ref.dtype)


def _proj_odd(hn, wt, n):
    m, d = hn.shape
    tm, tn = min(TM_PROJ_ODD, m), TN_PROJ
    assert n % tn == 0 and m % tm == 0
    return pl.pallas_call(
        _proj_odd_kernel,
        name="proj_odd",
        grid=(n // tn, m // tm),
        in_specs=[pl.BlockSpec((tm, d), lambda j, i: (i, 0)),
                  pl.BlockSpec((tn, d), lambda j, i: (j, 0))],
        out_specs=pl.BlockSpec((tm, tn), lambda j, i: (i, j)),
        out_shape=jax.ShapeDtypeStruct((m, n), BF16),
        scratch_shapes=[pltpu.VMEM((d, tn), BF16)],
        compiler_params=_cparams(("arbitrary", "arbitrary")),
    )(hn, wt)


def _attn_a_kernel(q_ref, k_ref, v_ref, z_ref, e_ref, o_ref,
                   tt_ref, vt_ref, qt_ref, s_ref, pv_ref):
    l_all = k_ref.shape[1]
    w = q_ref.shape[2]
    nt = (((1,), (1,)), ((), ()))
    band = N_LEFT * CHUNK

    @pl.when(pl.program_id(1) == 0)
    def _():
        i = lax.broadcasted_iota(jnp.int32, (A_QB, A_TW), 0)
        m = lax.broadcasted_iota(jnp.int32, (A_QB, A_TW), 1)
        dchunk = m // CHUNK - i // CHUNK
        ok = (dchunk >= 0) & (dchunk <= N_LEFT)
        for h in range(2):
            rows = jnp.broadcast_to(e_ref[0, h:h + 1, :], (A_QB, A_QB + A_TW))
            toep = pltpu.roll(rows, 0, 1, stride=1, stride_axis=0)[:, A_QB:]
            t = jnp.where(ok, toep * LOG2E, NEG)
            tt_ref[:, h * A_QB:(h + 1) * A_QB] = t.T

    r = lax.broadcasted_iota(jnp.int32, (w, w), 0)
    c = lax.broadcasted_iota(jnp.int32, (w, w), 1)
    eye = (r == c).astype(BF16)
    ones_row = (lax.broadcasted_iota(jnp.int32, (A_PAD, VT_BLK), 0) == 0).astype(BF16)

    qscale = LOG2E / math.sqrt(A_DH)

    def vt_blk(j, carry):
        ks = pl.multiple_of(j * VT_BLK, VT_BLK)
        vt = lax.dot_general(eye, v_ref[0, pl.ds(ks, VT_BLK), :], nt, preferred_element_type=F32)
        for h in range(2):
            vt_ref[h, :A_DH, pl.ds(ks, VT_BLK)] = vt[h * A_DH:(h + 1) * A_DH].astype(BF16)
            vt_ref[h, A_DH:, pl.ds(ks, VT_BLK)] = ones_row
        qs = (q_ref[0, pl.ds(ks, VT_BLK), :].astype(F32) * qscale).astype(BF16)
        qt_ref[:, pl.ds(ks, VT_BLK)] = lax.dot_general(
            eye, qs, nt, preferred_element_type=F32).astype(BF16)
        return carry

    lax.fori_loop(0, l_all // VT_BLK, vt_blk, 0)

    head_row = lax.broadcasted_iota(jnp.int32, (w, A_QB), 0)

    def window(qb):
        start = qb * A_QB - band
        start_c = jnp.maximum(start, 0)
        return pl.multiple_of(start_c, LANES), pl.multiple_of(start_c - start, LANES)

    def produce(qb, slot):
        qs = pl.multiple_of(qb * A_QB, A_QB)
        start_c, off = window(qb)
        qt = qt_ref[:, pl.ds(qs, A_QB)]
        zero = jnp.zeros_like(qt)
        qbd = jnp.concatenate([jnp.where(head_row < A_DH, qt, zero),
                               jnp.where(head_row >= A_DH, qt, zero)], axis=1)
        kb = k_ref[0, pl.ds(start_c, A_KB), :]
        s = jnp.dot(kb, qbd, preferred_element_type=F32)
        s = s + tt_ref[pl.ds(off, A_KB), :]
        s_ref[slot] = s
        return jnp.max(s, axis=0, keepdims=True)

    def consume(qb, slot, m):
        start_c, _ = window(qb)
        pb = jnp.exp2(s_ref[slot] - m).astype(BF16)
        for h in range(2):
            vt = vt_ref[h, :, pl.ds(start_c, A_KB)]
            pv_ref[slot, h] = jnp.dot(vt, pb[:, h * A_QB:(h + 1) * A_QB],
                                      preferred_element_type=F32)

    def finish(qb, slot):
        qs = pl.multiple_of(qb * A_QB, A_QB)
        outs = []
        for h in range(2):
            oh = pv_ref[slot, h]
            outs.append(oh[:A_DH] * (1.0 / oh[A_DH:A_DH + 1]))
        o = jnp.concatenate(outs, axis=0).T
        gate = _silu(z_ref[0, pl.ds(qs, A_QB), :].astype(F32))
        o_ref[0, pl.ds(qs, A_QB), :] = (o * gate).astype(o_ref.dtype)

    def pair(i, m, lead=True):
        j = 2 * i
        if lead:
            finish(j - 1, 1)
        m1 = produce(j + 1, 1)
        consume(j, 0, m)
        m2 = produce(j + 2, 0)
        finish(j, 0)
        consume(j + 1, 1, m1)
        return m2

    nb = l_all // A_QB
    npairs = (nb - 1) // 2
    m = pair(0, produce(0, 0), lead=False)
    m = lax.fori_loop(1, npairs, pair, m)
    last = 2 * npairs
    finish(last - 1, 1)
    if (nb - 1) % 2:
        m1 = produce(last + 1, 1)
        consume(last, 0, m)
        finish(last, 0)
        consume(last + 1, 1, m1)
        finish(last + 1, 1)
    else:
        consume(last, 0, m)
        finish(last, 0)


def _attn_a(proj3, rel_bias):
    b, l, _ = proj3.shape
    hp = A_HEADS // 2
    width = 2 * A_DH
    ncol = A_HEADS * A_DH // width
    zcol = (3 * A_HEADS * A_DH + (A_HEADS * A_DH)) // width
    left = N_LEFT * CHUNK + A_QB - MAX_REL
    right = A_QB + A_TW - left - (2 * MAX_REL + 1)
    ext = jnp.pad(rel_bias.astype(F32)[:, ::-1], ((0, 0), (left, right)), mode="edge")
    ext = ext.reshape(hp, 2, A_QB + A_TW)
    seq = lambda h, bi: (bi, 0, h)
    return pl.pallas_call(
        _attn_a_kernel,
        name="attn_a",
        grid=(hp, b),
        in_specs=[pl.BlockSpec((1, l, width), seq),
                  pl.BlockSpec((1, l, width), lambda h, bi: (bi, 0, ncol + h)),
                  pl.BlockSpec((1, l, width), lambda h, bi: (bi, 0, 2 * ncol + h)),
                  pl.BlockSpec((1, l, width), lambda h, bi: (bi, 0, zcol + h)),
                  pl.BlockSpec((1, 2, A_QB + A_TW), lambda h, bi: (h, 0, 0))],
        out_specs=pl.BlockSpec((1, l, width), seq),
        out_shape=jax.ShapeDtypeStruct((b, l, A_HEADS * A_DH), BF16),
        scratch_shapes=[pltpu.VMEM((A_TW, 2 * A_QB), F32),
                        pltpu.VMEM((2, A_DH + A_PAD, l), BF16),
                        pltpu.VMEM((width, l), BF16),
                        pltpu.VMEM((2, A_KB, 2 * A_QB), F32),
                        pltpu.VMEM((2, 2, A_DH + A_PAD, A_QB), F32)],
        compiler_params=_cparams(("arbitrary", "arbitrary")),
    )(proj3, proj3, proj3, proj3, ext)


def _cmul(ar, ai, br, bi):
    return ar * br - ai * bi, ar * bi + ai * br


def _s5_discretise(lr, li, log_dt):
    dt = jnp.exp(log_dt)
    mag = jnp.exp(lr * dt)
    ang = li * dt
    a_re = mag * jnp.cos(ang)
    a_im = mag * jnp.sin(ang)
    den = lr * lr + li * li
    nr = a_re - 1.0
    coef_re = (nr * lr + a_im * li) / den
    coef_im = (a_im * lr - nr * li) / den
    return (a_re, a_im), (coef_re, coef_im)


def _s5_weights_kernel(lam_pg_ref, ldt_pg_ref, c_pg_ref, lam_gp_ref, ldt_gp_ref, b_gp_ref,
                       lam_row_ref, ldt_row_ref, b_tl_ref, wm_ref, wr_ref, wo_ref, a_ref):
    tc, ch, ns, gpb = S5_TC, S5_GROUP, S5_STATE, S5_GPB
    lw = gpb * ch
    hi = lax.Precision.HIGHEST

    a_pg, _ = _s5_discretise(lam_pg_ref[0], lam_pg_ref[1], ldt_pg_ref[...])
    pw = (jnp.ones_like(a_pg[0]), jnp.zeros_like(a_pg[0]))
    ca = []
    for n in range(tc + 1):
        ca.append(_cmul(c_pg_ref[0], c_pg_ref[1], *pw))
        pw = _cmul(*pw, *a_pg)

    _, coef_gp = _s5_discretise(lam_gp_ref[0], lam_gp_ref[1], ldt_gp_ref[...])
    bb_gp = _cmul(*coef_gp, b_gp_ref[0], b_gp_ref[1])
    ca_re = jnp.concatenate([ca[n][0] for n in range(tc)], axis=1)
    ca_im = jnp.concatenate([ca[n][1] for n in range(tc)], axis=1)
    z = (jnp.dot(bb_gp[0], ca_re, precision=hi, preferred_element_type=F32)
         - jnp.dot(bb_gp[1], ca_im, precision=hi, preferred_element_type=F32))
    row_g = lax.broadcasted_iota(jnp.int32, z.shape, 0) // ch
    col_g = (lax.broadcasted_iota(jnp.int32, z.shape, 1) % lw) // ch
    z = jnp.where(row_g == col_g, z, 0.0).astype(BF16)
    for t in range(tc):
        parts = [z[:, :(tc - t) * lw]]
        if t:
            parts.insert(0, jnp.zeros((lw, t * lw), BF16))
        wm_ref[0, t * lw:(t + 1) * lw, :] = jnp.concatenate(parts, axis=1)

    rg = lax.broadcasted_iota(jnp.int32, (gpb * ns, lw), 0) // ns
    cg = lax.broadcasted_iota(jnp.int32, (gpb * ns, lw), 1) // ch
    keep = rg == cg
    for t2 in range(tc):
        cr, ci = ca[t2 + 1]
        cr = jnp.where(keep, jnp.concatenate([cr] * gpb, axis=0), 0.0)
        ci = jnp.where(keep, jnp.concatenate([ci] * gpb, axis=0), 0.0)
        wo_ref[0, :gpb * ns, t2 * lw:(t2 + 1) * lw] = cr.astype(BF16)
        wo_ref[0, gpb * ns:, t2 * lw:(t2 + 1) * lw] = (-ci).astype(BF16)

    a_row, coef_row = _s5_discretise(lam_row_ref[0], lam_row_ref[1], ldt_row_ref[...])
    bb_tl = _cmul(*coef_row, b_tl_ref[0], b_tl_ref[1])
    rg = lax.broadcasted_iota(jnp.int32, bb_tl[0].shape, 0) // ch
    cg = lax.broadcasted_iota(jnp.int32, bb_tl[0].shape, 1) // ns
    bb_tl = (jnp.where(rg == cg, bb_tl[0], 0.0), jnp.where(rg == cg, bb_tl[1], 0.0))
    pw = (jnp.ones_like(a_row[0]), jnp.zeros_like(a_row[0]))
    for t in range(tc - 1, -1, -1):
        r_re, r_im = _cmul(*pw, *bb_tl)
        wr_ref[0, t * lw:(t + 1) * lw, :] = jnp.concatenate([r_re, r_im], axis=1).astype(BF16)
        pw = _cmul(*pw, *a_row)
    a_ref[0, 0] = pw[0]
    a_ref[0, 1] = pw[1]


def _s5_weights(lam_re, lam_im, log_dt, b_re, b_im, c_re, c_im):
    g, ns = lam_re.shape
    ch, tc, gpb = S5_GROUP, S5_TC, S5_GPB
    nblk = g // gpb
    lw = gpb * ch
    lam = jnp.stack([lam_re, lam_im]).astype(F32)
    ldt = log_dt.astype(F32)
    bst = jnp.stack([b_re, b_im]).astype(F32)
    cst = jnp.stack([c_re, c_im]).astype(F32)
    lam_pg = jnp.repeat(lam.transpose(0, 2, 1), ch, axis=2)
    ldt_pg = jnp.repeat(ldt, ch)[None]
    c_pg = cst.transpose(0, 3, 1, 2).reshape(2, ns, g * ch)
    lam_gp = jnp.repeat(lam, ch, axis=1)
    ldt_gp = jnp.broadcast_to(jnp.repeat(ldt, ch)[:, None], (g * ch, ns))
    b_gp = bst.transpose(0, 1, 3, 2).reshape(2, g * ch, ns)
    lam_row = lam.reshape(2, 1, g * ns)
    ldt_row = jnp.repeat(ldt, ns)[None]
    b_tl = jnp.tile(b_gp, (1, 1, gpb))
    blk3 = lambda i: (0, 0, i)
    return pl.pallas_call(
        _s5_weights_kernel,
        name="s5_weights",
        grid=(nblk,),
        in_specs=[pl.BlockSpec((2, ns, lw), blk3),
                  pl.BlockSpec((1, lw), lambda i: (0, i)),
                  pl.BlockSpec((2, ns, lw), blk3),
                  pl.BlockSpec((2, lw, ns), lambda i: (0, i, 0)),
                  pl.BlockSpec((lw, ns), lambda i: (i, 0)),
                  pl.BlockSpec((2, lw, ns), lambda i: (0, i, 0)),
                  pl.BlockSpec((2, 1, gpb * ns), blk3),
                  pl.BlockSpec((1, gpb * ns), lambda i: (0, i)),
                  pl.BlockSpec((2, lw, gpb * ns), lambda i: (0, i, 0))],
        out_specs=[pl.BlockSpec((1, tc * lw, tc * lw), lambda i: (i, 0, 0)),
                   pl.BlockSpec((1, tc * lw, 2 * gpb * ns), lambda i: (i, 0, 0)),
                   pl.BlockSpec((1, 2 * gpb * ns, tc * lw), lambda i: (i, 0, 0)),
                   pl.BlockSpec((1, 2, 1, gpb * ns), lambda i: (i, 0, 0, 0))],
        out_shape=[jax.ShapeDtypeStruct((nblk, tc * lw, tc * lw), BF16),
                   jax.ShapeDtypeStruct((nblk, tc * lw, 2 * gpb * ns), BF16),
                   jax.ShapeDtypeStruct((nblk, 2 * gpb * ns, tc * lw), BF16),
                   jax.ShapeDtypeStruct((nblk, 2, 1, gpb * ns), F32)],
        compiler_params=_cparams(("arbitrary",)),
    )(lam_pg, ldt_pg, c_pg, lam_gp, ldt_gp, b_gp, lam_row, ldt_row, b_tl)


def _s5_mix_kernel(u_ref, wm_ref, wr_ref, wo_ref, a_ref, d_ref, y_ref, s_ref, x_ref, ys_ref):
    nk = u_ref.shape[1]
    half = s_ref.shape[1] // 2
    sub = 8
    u = u_ref[0]
    s_ref[...] = jnp.dot(u, wr_ref[0], preferred_element_type=F32)

    a1 = (a_ref[0, 0], a_ref[0, 1])
    a2 = _cmul(*a1, *a1)
    a4 = _cmul(*a2, *a2)
    a8 = _cmul(*a4, *a4)
    a3 = _cmul(*a2, *a1)
    pows = [(jnp.ones_like(a1[0]), jnp.zeros_like(a1[0])), a1, a2, a3, a4,
            _cmul(*a4, *a1), _cmul(*a4, *a2), _cmul(*a4, *a3)]
    p_re = jnp.concatenate([p[0] for p in pows], axis=0)
    p_im = jnp.concatenate([p[1] for p in pows], axis=0)
    rowid = lax.broadcasted_iota(jnp.int32, (sub, half), 0)
    levels = [(d, jnp.broadcast_to(a[0], (sub, half)), jnp.broadcast_to(a[1], (sub, half)))
              for d, a in ((1, a1), (2, a2), (4, a4))]

    def shift_rows(v, d):
        return jnp.where(rowid >= d, pltpu.roll(v, d, 0), 0.0)

    def tile(j, carry):
        cr, ci = carry
        rows = slice(j * sub, (j + 1) * sub)
        zr = s_ref[rows, :half]
        zi = s_ref[rows, half:]
        for d, br, bi in levels:
            sr, si = shift_rows(zr, d), shift_rows(zi, d)
            zr, zi = zr + br * sr - bi * si, zi + br * si + bi * sr
        x_ref[rows, :half] = p_re * cr - p_im * ci + shift_rows(zr, 1)
        x_ref[rows, half:] = p_re * ci + p_im * cr + shift_rows(zi, 1)
        return (a8[0] * cr - a8[1] * ci + zr[sub - 1:], a8[0] * ci + a8[1] * cr + zi[sub - 1:])

    carry = (jnp.zeros((1, half), F32), jnp.zeros((1, half), F32))
    for j in range(nk // sub):
        carry = tile(j, carry)
    xb = x_ref[...].astype(BF16)

    cw = 2 * LANES
    for n in range(u.shape[1] // cw):
        cs = slice(n * cw, (n + 1) * cw)
        y = jnp.dot(u[:, :(n + 1) * cw], wm_ref[0, :(n + 1) * cw, cs], preferred_element_type=F32)
        y = y + jnp.dot(xb, wo_ref[0, :, cs], preferred_element_type=F32)
        y = _gelu_tanh(y + d_ref[0, :, cs] * u[:, cs].astype(F32))
        for t in range(2 * n, 2 * n + 2):
            ys_ref[pl.ds(t, nk, stride=S5_TC), :] = y[:, (t - 2 * n) * LANES:(t - 2 * n + 1) * LANES]
    y_ref[0] = ys_ref[...].astype(y_ref.dtype)


def _s5_mixer(u3, nb, lam_re, lam_im, log_dt, b_re, b_im, c_re, c_im, d_skip):
    nblk, rows, w = u3.shape
    tc = S5_TC
    nk = rows // nb
    assert nk % 8 == 0
    wm, wr, wo, a_pow = _s5_weights(lam_re, lam_im, log_dt, b_re, b_im, c_re, c_im)
    ns = wr.shape[-1]
    d_rows = jnp.tile(d_skip.astype(F32).reshape(nblk, 1, LANES), (1, 1, tc))
    blk = lambda i, b: (i, 0, 0)
    return pl.pallas_call(
        _s5_mix_kernel,
        name="s5_mix",
        grid=(nblk, nb),
        in_specs=[pl.BlockSpec((1, nk, w), lambda i, b: (i, b, 0)),
                  pl.BlockSpec((1, w, w), blk),
                  pl.BlockSpec((1, w, ns), blk),
                  pl.BlockSpec((1, ns, w), blk),
                  pl.BlockSpec((1, 2, 1, ns // 2), lambda i, b: (i, 0, 0, 0)),
                  pl.BlockSpec((1, 1, w), blk)],
        out_specs=pl.BlockSpec((1, nk * tc, LANES), lambda i, b: (i, b, 0)),
        out_shape=jax.ShapeDtypeStruct((nblk, rows * tc, LANES), BF16),
        scratch_shapes=[pltpu.VMEM((nk, ns), F32), pltpu.VMEM((nk, ns), F32),
                        pltpu.VMEM((nk * tc, LANES), F32)],
        compiler_params=_cparams(("arbitrary", "arbitrary")),
    )(u3, wm, wr, wo, a_pow, d_rows)


def _even_out_kernel(x_ref, oa_ref, y_ref, zb_ref, wg_ref, bg_ref, wo_ref, g_ref, wf_ref,
                     x1_ref, hn_ref, f_ref):
    y = jnp.concatenate([y_ref[i] for i in range(y_ref.shape[0])], axis=1)
    t = jnp.dot(y, wg_ref[...], preferred_element_type=F32) + bg_ref[...]
    ob = y.astype(F32) * jax.nn.sigmoid(t) * _silu(zb_ref[...].astype(F32))
    wa = oa_ref.shape[1]
    acc = jnp.dot(oa_ref[...], wo_ref[:wa, :], preferred_element_type=F32)
    acc = acc + jnp.dot(ob.astype(BF16), wo_ref[wa:, :], preferred_element_type=F32)
    x1 = x_ref[...] + acc
    x1_ref[...] = x1
    hn = _rms(x1, g_ref[...]).astype(BF16)
    hn_ref[...] = hn
    f_ref[...] = lax.dot_general(wf_ref[...], hn, (((1,), (1,)), ((), ())),
                                 preferred_element_type=F32)


def _even_out(x2, oa, y, proj, w_glu, b_glu, w_out, g_next, wf_t):
    m, d = x2.shape
    nh = wf_t.shape[0]
    wa = oa.shape[1]
    nblk, _, lw = y.shape
    wb = nblk * lw
    tm = min(TM_OUT, m)
    zb_col = (proj.shape[1] - wb) // wb
    const = lambda i: (0, 0)
    return pl.pallas_call(
        _even_out_kernel,
        name="even_out",
        grid=(m // tm,),
        in_specs=[pl.BlockSpec((tm, d), lambda i: (i, 0)),
                  pl.BlockSpec((tm, wa), lambda i: (i, 0)),
                  pl.BlockSpec((nblk, tm, lw), lambda i: (0, i, 0)),
                  pl.BlockSpec((tm, wb), lambda i: (i, zb_col)),
                  pl.BlockSpec((wb, wb), const),
                  pl.BlockSpec((1, wb), const),
                  pl.BlockSpec((wa + wb, d), const),
                  pl.BlockSpec((1, d), const),
                  pl.BlockSpec((nh, d), const)],
        out_specs=[pl.BlockSpec((tm, d), lambda i: (i, 0)),
                   pl.BlockSpec((tm, d), lambda i: (i, 0)),
                   pl.BlockSpec((nh, tm), lambda i: (0, i))],
        out_shape=[jax.ShapeDtypeStruct((m, d), F32),
                   jax.ShapeDtypeStruct((m, d), BF16),
                   jax.ShapeDtypeStruct((nh, m), F32)],
        compiler_params=_cparams(("arbitrary",)),
    )(x2, oa, y, proj, w_glu, b_glu.reshape(1, wb), w_out, g_next.reshape(1, d), wf_t)


def _cum_forget_kernel(f_ref, b_ref, c_ref):
    x = f_ref[...] + b_ref[...]
    v = -(jnp.maximum(-x, 0.0) + jnp.log1p(jnp.exp(-jnp.abs(x))))
    n = v.shape[-1]
    pos = lax.broadcasted_iota(jnp.int32, v.shape, 1)
    d = 1
    while d < n:
        v = v + jnp.where(pos >= d, pltpu.roll(v, d, 1), 0.0)
        d *= 2
    c_ref[...] = v


def _cum_forget(f_t, b_forget, seq):
    nh, m = f_t.shape
    return pl.pallas_call(
        _cum_forget_kernel,
        name="cum_forget",
        grid=(m // seq,),
        in_specs=[pl.BlockSpec((nh, seq), lambda i: (0, i)),
                  pl.BlockSpec((nh, 1), lambda i: (0, 0))],
        out_specs=pl.BlockSpec((nh, seq), lambda i: (0, i)),
        out_shape=jax.ShapeDtypeStruct((nh, m), F32),
        compiler_params=_cparams(("arbitrary",)),
    )(f_t, b_forget.astype(F32).reshape(nh, 1))


def _split3(x):
    hi = x.astype(BF16).astype(F32)
    r = x - hi
    mid = r.astype(BF16).astype(F32)
    lo = (r - mid).astype(BF16).astype(F32)
    return hi, mid, lo


def _fox_kernel(q0_ref, qn_ref, k_ref, v_ref, z_ref, c_ref, o_ref,
                ke_ref, vt_ref, qa_ref, s_ref, mb_ref, acc_ref):
    qi = pl.program_id(2)
    nq = pl.num_programs(2)
    tq = qn_ref.shape[1]
    tk = tq
    dh = C_DH
    nh = c_ref.shape[1]
    nblk = k_ref.shape[1] // tk
    nt = (((1,), (1,)), ((), ()))

    def cols(h):
        return slice(h * dh, (h + 1) * dh)

    def prep_queries(q_ref, qb):
        qs = pl.multiple_of(qb * tq, tq)
        row = lax.broadcasted_iota(jnp.int32, (dh, tq), 0)
        for h in range(nh):
            c_base = jnp.max(c_ref[0, h, :, pl.ds(qs, tq)], axis=1, keepdims=True) * LOG2E
            bh, bm, bl = _split3(c_base)
            g = jnp.where(row < 3, 1.0, jnp.where(row == 3, bh, jnp.where(row == 4, bm,
                          jnp.where(row == 5, bl, 0.0))))
            qt = (q_ref[0, :, cols(h)].astype(F32) * (LOG2E / math.sqrt(dh))).T
            qa_ref[h, :dh, :] = qt.astype(BF16)
            qa_ref[h, dh:, :] = g.astype(BF16)

    half = tk // 2

    def produce(h, j, diagonal):
        ks = pl.multiple_of(j * tk, tk)

        def keys(start, n):
            return jnp.concatenate([k_ref[0, pl.ds(start, n), cols(h)],
                                    ke_ref[h, pl.ds(start, n), :]], axis=1)

        if not diagonal:
            s = jnp.dot(keys(ks, tk), qa_ref[h], preferred_element_type=F32)
            s_ref[h] = s
            return jnp.max(s, axis=0, keepdims=True)
        top = jnp.dot(keys(ks, half), qa_ref[h], preferred_element_type=F32)
        bot = jnp.dot(keys(ks + half, half), qa_ref[h, :, half:],
                      preferred_element_type=F32)
        top = jnp.where(lax.broadcasted_iota(jnp.int32, top.shape, 0)
                        <= lax.broadcasted_iota(jnp.int32, top.shape, 1), top, NEG)
        bot = jnp.where(lax.broadcasted_iota(jnp.int32, bot.shape, 0)
                        <= lax.broadcasted_iota(jnp.int32, bot.shape, 1), bot, NEG)
        s_ref[h, :half, :] = top
        s_ref[h, half:, half:] = bot
        bot_wide = jnp.concatenate([jnp.full((half, half), NEG, F32), bot], axis=1)
        return jnp.max(jnp.maximum(top, bot_wide), axis=0, keepdims=True)

    def consume(h, j, stats, diagonal=False):
        m, mb = stats
        ks = pl.multiple_of(j * tk, tk)
        m2 = jnp.maximum(m, mb)
        if not diagonal:
            p = jnp.exp2(s_ref[h] - m2).astype(BF16)
            pv = jnp.dot(vt_ref[h, :, pl.ds(ks, tk)], p, preferred_element_type=F32)
            acc_ref[h] = jnp.exp2(m - m2) * acc_ref[h] + pv
            return m2
        p_top = jnp.exp2(s_ref[h, :half, :] - m2).astype(BF16)
        mb_ref[h] = m2
        p_bot = jnp.exp2(s_ref[h, half:, half:] - mb_ref[h, :, half:]).astype(BF16)
        acc = jnp.exp2(m - m2) * acc_ref[h] + jnp.dot(vt_ref[h, :, pl.ds(ks, half)], p_top,
                                                      preferred_element_type=F32)
        acc_ref[h, :, :half] = acc[:, :half]
        acc_ref[h, :, half:] = acc[:, half:] + jnp.dot(vt_ref[h, :, pl.ds(ks + half, half)], p_bot,
                                                       preferred_element_type=F32)
        return m2

    def step(j, carry, diagonal_next):
        return tuple((consume(h, j, stats), produce(h, j + 1, diagonal_next))
                     for h, stats in enumerate(carry))

    @pl.when(qi == 0)
    def _():
        r = lax.broadcasted_iota(jnp.int32, (dh, dh), 0)
        c = lax.broadcasted_iota(jnp.int32, (dh, dh), 1)
        eye = (r == c).astype(BF16)
        row = lax.broadcasted_iota(jnp.int32, (dh, tk), 0)
        ones_row = (lax.broadcasted_iota(jnp.int32, (FOX_PAD, tk), 0) == 0).astype(BF16)

        def blk(j, carry):
            ks = pl.multiple_of(j * tk, tk)
            for h in range(nh):
                hi, mid, lo = _split3(c_ref[0, h, :, pl.ds(ks, tk)] * (-LOG2E))
                a = jnp.where(row == 0, hi, jnp.where(row == 1, mid, jnp.where(row == 2, lo,
                              jnp.where(row < 6, 1.0, 0.0))))
                ke_ref[h, pl.ds(ks, tk), :] = a.T.astype(BF16)
                vt = lax.dot_general(eye, v_ref[0, pl.ds(ks, tk), cols(h)], nt,
                                     preferred_element_type=F32)
                vt_ref[h, :dh, pl.ds(ks, tk)] = vt.astype(BF16)
                vt_ref[h, dh:, pl.ds(ks, tk)] = ones_row
            return carry

        lax.fori_loop(0, nblk, blk, 0)
        prep_queries(q0_ref, 0)
        for h in range(nh):
            mb_ref[h] = produce(h, 0, True)

    carry = []
    for h in range(nh):
        acc_ref[h] = jnp.zeros(acc_ref.shape[1:], F32)
        carry.append((jnp.full((1, tq), NEG, F32), mb_ref[h]))
    plain = functools.partial(step, diagonal_next=False)
    n_plain = jnp.maximum(qi - 1, 0)
    n_pairs = n_plain // 2
    carry = lax.fori_loop(0, n_pairs, lambda i, c: plain(2 * i + 1, plain(2 * i, c)),
                          tuple(carry))
    carry = lax.fori_loop(2 * n_pairs, n_plain, plain, carry)
    carry = lax.fori_loop(n_plain, qi, functools.partial(step, diagonal_next=True), carry)
    for h, stats in enumerate(carry):
        consume(h, qi, stats, diagonal=True)
    prep_queries(qn_ref, jnp.minimum(qi + 1, nq - 1))
    for h in range(nh):
        mb_ref[h] = produce(h, 0, False)
    for h in range(nh):
        acc = acc_ref[h]
        o = (acc[:dh] / acc[dh:dh + 1]).T
        o_ref[0, :, cols(h)] = (o * _silu(z_ref[0, :, cols(h)].astype(F32))).astype(o_ref.dtype)


def _fox(proj3, cum4):
    b, l, _ = proj3.shape
    ng = C_HEADS // FOX_HEADS
    w = FOX_HEADS * C_DH
    tq = min(FOX_TQ, l)
    nq = l // tq
    return pl.pallas_call(
        _fox_kernel,
        name="fox",
        grid=(b, ng, nq),
        in_specs=[pl.BlockSpec((1, tq, w), lambda bi, gi, qi: (bi, 0, gi)),
                  pl.BlockSpec((1, tq, w), lambda bi, gi, qi: (bi, jnp.minimum(qi + 1, nq - 1), gi)),
                  pl.BlockSpec((1, l, w), lambda bi, gi, qi: (bi, 0, ng + gi)),
                  pl.BlockSpec((1, l, w), lambda bi, gi, qi: (bi, 0, 2 * ng + gi)),
                  pl.BlockSpec((1, tq, w), lambda bi, gi, qi: (bi, qi, 3 * ng + gi)),
                  pl.BlockSpec((1, FOX_HEADS, 1, l), lambda bi, gi, qi: (bi, gi, 0, 0))],
        out_specs=pl.BlockSpec((1, tq, w), lambda bi, gi, qi: (bi, qi, gi)),
        out_shape=jax.ShapeDtypeStruct((b, l, C_HEADS * C_DH), BF16),
        scratch_shapes=[pltpu.VMEM((FOX_HEADS, l, C_DH), BF16),
                        pltpu.VMEM((FOX_HEADS, C_DH + FOX_PAD, l), BF16),
                        pltpu.VMEM((FOX_HEADS, 2 * C_DH, tq), BF16),
                        pltpu.VMEM((FOX_HEADS, tq, tq), F32),
                        pltpu.VMEM((FOX_HEADS, 1, tq), F32),
                        pltpu.VMEM((FOX_HEADS, C_DH + FOX_PAD, tq), F32)],
        compiler_params=_cparams(("arbitrary", "arbitrary", "arbitrary")),
    )(proj3, proj3, proj3, proj3, proj3, cum4)


def _odd_out_kernel(x_ref, o_ref, w_ref, g_ref, out_ref):
    x2 = x_ref[...] + jnp.dot(o_ref[...], w_ref[...], preferred_element_type=F32)
    out_ref[...] = _rms(x2, g_ref[...])


def _odd_out(x1, o, w_out, g):
    m, d = x1.shape
    k = o.shape[1]
    tm = min(TM_OUT, m)
    return pl.pallas_call(
        _odd_out_kernel,
        name="odd_out",
        grid=(m // tm,),
        in_specs=[pl.BlockSpec((tm, d), lambda i: (i, 0)),
                  pl.BlockSpec((tm, k), lambda i: (i, 0)),
                  pl.BlockSpec((k, d), lambda i: (0, 0)),
                  pl.BlockSpec((1, d), lambda i: (0, 0))],
        out_specs=pl.BlockSpec((tm, d), lambda i: (i, 0)),
        out_shape=jax.ShapeDtypeStruct((m, d), F32),
        compiler_params=_cparams(("arbitrary",)),
    )(x1, o, w_out, g.reshape(1, d))


def kernel(x, norm_even_g, w_in_even, rel_bias, s5_lambda_re, s5_lambda_im, s5_log_dt,
           s5_b_re, s5_b_im, s5_c_re, s5_c_im, s5_d, w_glu, b_glu, w_out_even,
           norm_odd_g, w_in_odd, b_forget, w_out_odd, final_norm_g):
    b, l, d = x.shape
    assert norm_even_g.shape[0] == 1 and norm_odd_g.shape[0] == 1, "one even + one odd layer"
    assert l % FOX_TQ == 0 and l % VT_BLK == 0 and l % A_QB == 0 and l >= A_KB
    m = b * l
    a_w = A_HEADS * A_DH
    s5_w = s5_d.shape[1] * S5_GROUP
    c_w = C_HEADS * C_DH

    x2 = x.reshape(m, d)
    assert s5_w == TN_PROJ and l % S5_TC == 0
    proj, u3 = _norm_proj(x2, norm_even_g[0], w_in_even[0].astype(BF16), 3 * a_w)
    proj3 = proj.reshape(b, l, proj.shape[1])
    o_a = _attn_a(proj3, rel_bias[0])
    y = _s5_mixer(u3, b, s5_lambda_re[0], s5_lambda_im[0], s5_log_dt[0], s5_b_re[0],
                  s5_b_im[0], s5_c_re[0], s5_c_im[0], s5_d[0])
    w_odd_t = w_in_odd[0].T
    wf_t = w_odd_t[4 * c_w:].astype(BF16)
    x1, hn1, f_t = _even_out(x2, o_a.reshape(m, a_w), y, proj,
                             w_glu[0].astype(BF16), b_glu[0].astype(F32),
                             w_out_even[0].astype(BF16), norm_odd_g[0], wf_t)
    proj1 = _proj_odd(hn1, w_odd_t, 4 * c_w)
    cum = _cum_forget(f_t, b_forget[0], l)
    cum4 = cum.reshape(C_HEADS, b, 1, l).transpose(1, 0, 2, 3)
    o_c = _fox(proj1.reshape(b, l, 4 * c_w), cum4)
    out = _odd_out(x1, o_c.reshape(m, c_w), w_out_odd[0].astype(BF16), final_norm_g)
    return out.reshape(b, l, d)
```

```python
import functools
import math

import jax
import jax.numpy as jnp
from jax import lax
from jax.experimental import pallas as pl
from jax.experimental.pallas import tpu as pltpu

F32 = jnp.float32
BF16 = jnp.bfloat16

EPS = 1e-6
NEG = -1e30
LOG2E = math.log2(math.e)

CHUNK = 64
N_LEFT = 8
A_HEADS = 16
A_DH = 64
MAX_REL = 128
S5_GROUP = 16
S5_STATE = 64
C_HEADS = 16
C_DH = 128

LANES = 128
VMEM_LIMIT = 56 * 1024 * 1024

TM_PROJ = 1024
TN_PROJ = 1024
PROJ_CHUNKS = 4
TM_OUT = 512
TM_PROJ_ODD = 2048
A_QB = 2 * CHUNK
A_KB = A_QB + N_LEFT * CHUNK
A_TW = A_KB + N_LEFT * CHUNK
A_PAD = 16
A_PAIRS = 2
VT_BLK = 512
S5_TC = 16
S5_GPB = LANES // S5_GROUP
FOX_TQ = 512
FOX_HEADS = 4
FOX_PAD = 16


def _cparams(sem):
    return pltpu.CompilerParams(dimension_semantics=sem, vmem_limit_bytes=VMEM_LIMIT)


def _rms(x, g):
    ms = jnp.mean(x * x, axis=-1, keepdims=True)
    return x * lax.rsqrt(ms + EPS) * g


def _silu(x):
    return x * jax.nn.sigmoid(x)


def _gelu_tanh(x):
    c = math.sqrt(2.0 / math.pi)
    return 0.5 * x * (1.0 + jnp.tanh(c * (x + 0.044715 * (x * x * x))))


def _norm_proj_kernel(x_ref, g_ref, w_ref, o_ref, u_ref, hn_ref, us_ref, *, u_tile):
    j = pl.program_id(1)
    tm = x_ref.shape[0]

    @pl.when(j == 0)
    def _():
        for c in range(PROJ_CHUNKS):
            rows = slice(c * tm // PROJ_CHUNKS, (c + 1) * tm // PROJ_CHUNKS)
            h = _rms(x_ref[rows, :], g_ref[...]).astype(BF16)
            hn_ref[rows, :] = h
            o_ref[rows, :] = jnp.dot(h, w_ref[...], preferred_element_type=F32).astype(o_ref.dtype)

    @pl.when(j != 0)
    def _():
        acc = jnp.dot(hn_ref[...], w_ref[...], preferred_element_type=F32)
        o_ref[...] = acc.astype(o_ref.dtype)

        @pl.when(j == u_tile)
        def _():
            nchunk = us_ref.shape[1] // S5_TC
            for blk in range(u_ref.shape[0]):
                us_ref[blk] = acc[:, blk * LANES:(blk + 1) * LANES]
                for t in range(S5_TC):
                    rows = us_ref[blk, pl.ds(t, nchunk, stride=S5_TC), :]
                    u_ref[blk, :, t * LANES:(t + 1) * LANES] = rows.astype(u_ref.dtype)


def _norm_proj(x2, g, w, u_start):
    m, d = x2.shape
    n = w.shape[1]
    tm, tn = min(TM_PROJ, m), TN_PROJ
    assert u_start % tn == 0 and u_start > 0 and tm % (16 * S5_TC) == 0
    nblk = tn // LANES
    return pl.pallas_call(
        functools.partial(_norm_proj_kernel, u_tile=u_start // tn),
        name="norm_proj",
        grid=(m // tm, n // tn),
        in_specs=[pl.BlockSpec((tm, d), lambda i, j: (i, 0)),
                  pl.BlockSpec((1, d), lambda i, j: (0, 0)),
                  pl.BlockSpec((d, tn), lambda i, j: (0, j))],
        out_specs=[pl.BlockSpec((tm, tn), lambda i, j: (i, j)),
                   pl.BlockSpec((nblk, tm // S5_TC, S5_TC * LANES), lambda i, j: (0, i, 0))],
        out_shape=[jax.ShapeDtypeStruct((m, n), BF16),
                   jax.ShapeDtypeStruct((nblk, m // S5_TC, S5_TC * LANES), BF16)],
        scratch_shapes=[pltpu.VMEM((tm, d), BF16), pltpu.VMEM((nblk, tm, LANES), F32)],
        compiler_params=_cparams(("arbitrary", "arbitrary")),
    )(x2, g.reshape(1, d), w)


def _proj_odd_kernel(h_ref, wt_ref, o_ref, wb_ref):
    @pl.when(pl.program_id(1) == 0)
    def _():
        wb_ref[...] = wt_ref[...].T.astype(BF16)

    o_ref[...] = jnp.dot(h_ref[...], wb_ref[...],
                         preferred_element_type=F32).astype(o_ref.dtype)


def _proj_odd(hn, wt, n):
    m, d = hn.shape
    tm, tn = min(TM_PROJ_ODD, m), TN_PROJ
    assert n % tn == 0 and m % tm == 0
    return pl.pallas_call(
        _proj_odd_kernel,
        name="proj_odd",
        grid=(n // tn, m // tm),
        in_specs=[pl.BlockSpec((tm, d), lambda j, i: (i, 0)),
                  pl.BlockSpec((tn, d), lambda j, i: (j, 0))],
        out_specs=pl.BlockSpec((tm, tn), lambda j, i: (i, j)),
        out_shape=jax.ShapeDtypeStruct((m, n), BF16),
        scratch_shapes=[pltpu.VMEM((d, tn), BF16)],
        compiler_params=_cparams(("arbitrary", "arbitrary")),
    )(hn, wt)


def _attn_a_kernel(q_ref, k_ref, v_ref, z_ref, e_ref, o_ref,
                   tt_ref, vt_ref, qt_ref, s_ref, pv_ref):
    l_all = k_ref.shape[1]
    w = q_ref.shape[2]
    pw = 2 * A_DH
    npair = w // pw
    nt = (((1,), (1,)), ((), ()))
    band = N_LEFT * CHUNK

    @pl.when(pl.program_id(1) == 0)
    def _():
        i = lax.broadcasted_iota(jnp.int32, (A_QB, A_TW), 0)
        m = lax.broadcasted_iota(jnp.int32, (A_QB, A_TW), 1)
        dchunk = m // CHUNK - i // CHUNK
        ok = (dchunk >= 0) & (dchunk <= N_LEFT)
        for h in range(2 * npair):
            rows = jnp.broadcast_to(e_ref[0, h:h + 1, :], (A_QB, A_QB + A_TW))
            toep = pltpu.roll(rows, 0, 1, stride=1, stride_axis=0)[:, A_QB:]
            t = jnp.where(ok, toep * LOG2E, NEG)
            tt_ref[:, h * A_QB:(h + 1) * A_QB] = t.T

    r = lax.broadcasted_iota(jnp.int32, (w, w), 0)
    c = lax.broadcasted_iota(jnp.int32, (w, w), 1)
    eye = (r == c).astype(BF16)
    ones_row = (lax.broadcasted_iota(jnp.int32, (A_PAD, VT_BLK), 0) == 0).astype(BF16)

    qscale = LOG2E / math.sqrt(A_DH)

    def vt_blk(j, carry):
        ks = pl.multiple_of(j * VT_BLK, VT_BLK)
        vt = lax.dot_general(eye, v_ref[0, pl.ds(ks, VT_BLK), :], nt, preferred_element_type=F32)
        for h in range(2 * npair):
            vt_ref[h, :A_DH, pl.ds(ks, VT_BLK)] = vt[h * A_DH:(h + 1) * A_DH].astype(BF16)
            vt_ref[h, A_DH:, pl.ds(ks, VT_BLK)] = ones_row
        qs = (q_ref[0, pl.ds(ks, VT_BLK), :].astype(F32) * qscale).astype(BF16)
        qt_ref[:, pl.ds(ks, VT_BLK)] = lax.dot_general(
            eye, qs, nt, preferred_element_type=F32).astype(BF16)
        return carry

    lax.fori_loop(0, l_all // VT_BLK, vt_blk, 0)

    head_row = lax.broadcasted_iota(jnp.int32, (pw, A_QB), 0)

    def window(qb):
        start = qb * A_QB - band
        start_c = jnp.maximum(start, 0)
        return pl.multiple_of(start_c, LANES), pl.multiple_of(start_c - start, LANES)

    def produce(qb, slot):
        qs = pl.multiple_of(qb * A_QB, A_QB)
        start_c, off = window(qb)
        out = []
        for p in range(npair):
            qt = qt_ref[p * pw:(p + 1) * pw, pl.ds(qs, A_QB)]
            zero = jnp.zeros_like(qt)
            qbd = jnp.concatenate([jnp.where(head_row < A_DH, qt, zero),
                                   jnp.where(head_row >= A_DH, qt, zero)], axis=1)
            kb = k_ref[0, pl.ds(start_c, A_KB), p * pw:(p + 1) * pw]
            s = jnp.dot(kb, qbd, preferred_element_type=F32)
            s = s + tt_ref[pl.ds(off, A_KB), 2 * p * A_QB:2 * (p + 1) * A_QB]
            s_ref[slot, p] = s
            out.append(jnp.max(s, axis=0, keepdims=True))
        return tuple(out)

    def consume(qb, slot, ms):
        start_c, _ = window(qb)
        for p in range(npair):
            pb = jnp.exp2(s_ref[slot, p] - ms[p]).astype(BF16)
            for h in range(2):
                vt = vt_ref[2 * p + h, :, pl.ds(start_c, A_KB)]
                pv_ref[slot, 2 * p + h] = jnp.dot(vt, pb[:, h * A_QB:(h + 1) * A_QB],
                                                  preferred_element_type=F32)

    def finish(qb, slot):
        qs = pl.multiple_of(qb * A_QB, A_QB)
        outs = []
        for h in range(2 * npair):
            oh = pv_ref[slot, h]
            outs.append(oh[:A_DH] * (1.0 / oh[A_DH:A_DH + 1]))
        o = jnp.concatenate(outs, axis=0).T
        gate = _silu(z_ref[0, pl.ds(qs, A_QB), :].astype(F32))
        o_ref[0, pl.ds(qs, A_QB), :] = (o * gate).astype(o_ref.dtype)

    def pair(i, m, lead=True):
        j = 2 * i
        if lead:
            finish(j - 1, 1)
        m1 = produce(j + 1, 1)
        consume(j, 0, m)
        m2 = produce(j + 2, 0)
        finish(j, 0)
        consume(j + 1, 1, m1)
        return m2

    nb = l_all // A_QB
    npairs = (nb - 1) // 2
    m = pair(0, produce(0, 0), lead=False)
    m = lax.fori_loop(1, npairs, pair, m)
    last = 2 * npairs
    finish(last - 1, 1)
    if (nb - 1) % 2:
        m1 = produce(last + 1, 1)
        consume(last, 0, m)
        finish(last, 0)
        consume(last + 1, 1, m1)
        finish(last + 1, 1)
    else:
        consume(last, 0, m)
        finish(last, 0)


def _attn_a(proj3, rel_bias):
    b, l, _ = proj3.shape
    nh = 2 * A_PAIRS
    ng = A_HEADS // nh
    width = nh * A_DH
    ncol = A_HEADS * A_DH // width
    zcol = (3 * A_HEADS * A_DH + (A_HEADS * A_DH)) // width
    left = N_LEFT * CHUNK + A_QB - MAX_REL
    right = A_QB + A_TW - left - (2 * MAX_REL + 1)
    ext = jnp.pad(rel_bias.astype(F32)[:, ::-1], ((0, 0), (left, right)), mode="edge")
    ext = ext.reshape(ng, nh, A_QB + A_TW)
    seq = lambda h, bi: (bi, 0, h)
    return pl.pallas_call(
        _attn_a_kernel,
        name="attn_a",
        grid=(ng, b),
        in_specs=[pl.BlockSpec((1, l, width), seq),
                  pl.BlockSpec((1, l, width), lambda h, bi: (bi, 0, ncol + h)),
                  pl.BlockSpec((1, l, width), lambda h, bi: (bi, 0, 2 * ncol + h)),
                  pl.BlockSpec((1, l, width), lambda h, bi: (bi, 0, zcol + h)),
                  pl.BlockSpec((1, nh, A_QB + A_TW), lambda h, bi: (h, 0, 0))],
        out_specs=pl.BlockSpec((1, l, width), seq),
        out_shape=jax.ShapeDtypeStruct((b, l, A_HEADS * A_DH), BF16),
        scratch_shapes=[pltpu.VMEM((A_TW, nh * A_QB), F32),
                        pltpu.VMEM((nh, A_DH + A_PAD, l), BF16),
                        pltpu.VMEM((width, l), BF16),
                        pltpu.VMEM((2, A_PAIRS, A_KB, 2 * A_QB), F32),
                        pltpu.VMEM((2, nh, A_DH + A_PAD, A_QB), F32)],
        compiler_params=_cparams(("arbitrary", "arbitrary")),
    )(proj3, proj3, proj3, proj3, ext)


def _cmul(ar, ai, br, bi):
    return ar * br - ai * bi, ar * bi + ai * br


def _s5_discretise(lr, li, log_dt):
    dt = jnp.exp(log_dt)
    mag = jnp.exp(lr * dt)
    ang = li * dt
    a_re = mag * jnp.cos(ang)
    a_im = mag * jnp.sin(ang)
    den = lr * lr + li * li
    nr = a_re - 1.0
    coef_re = (nr * lr + a_im * li) / den
    coef_im = (a_im * lr - nr * li) / den
    return (a_re, a_im), (coef_re, coef_im)


def _s5_weights_kernel(lam_pg_ref, ldt_pg_ref, c_pg_ref, lam_gp_ref, ldt_gp_ref, b_gp_ref,
                       lam_row_ref, ldt_row_ref, b_tl_ref, wm_ref, wr_ref, wo_ref, a_ref):
    tc, ch, ns, gpb = S5_TC, S5_GROUP, S5_STATE, S5_GPB
    lw = gpb * ch
    hi = lax.Precision.HIGHEST

    a_pg, _ = _s5_discretise(lam_pg_ref[0], lam_pg_ref[1], ldt_pg_ref[...])
    pw = (jnp.ones_like(a_pg[0]), jnp.zeros_like(a_pg[0]))
    ca = []
    for n in range(tc + 1):
        ca.append(_cmul(c_pg_ref[0], c_pg_ref[1], *pw))
        pw = _cmul(*pw, *a_pg)

    _, coef_gp = _s5_discretise(lam_gp_ref[0], lam_gp_ref[1], ldt_gp_ref[...])
    bb_gp = _cmul(*coef_gp, b_gp_ref[0], b_gp_ref[1])
    ca_re = jnp.concatenate([ca[n][0] for n in range(tc)], axis=1)
    ca_im = jnp.concatenate([ca[n][1] for n in range(tc)], axis=1)
    z = (jnp.dot(bb_gp[0], ca_re, precision=hi, preferred_element_type=F32)
         - jnp.dot(bb_gp[1], ca_im, precision=hi, preferred_element_type=F32))
    row_g = lax.broadcasted_iota(jnp.int32, z.shape, 0) // ch
    col_g = (lax.broadcasted_iota(jnp.int32, z.shape, 1) % lw) // ch
    z = jnp.where(row_g == col_g, z, 0.0).astype(BF16)
    for t in range(tc):
        parts = [z[:, :(tc - t) * lw]]
        if t:
            parts.insert(0, jnp.zeros((lw, t * lw), BF16))
        wm_ref[0, t * lw:(t + 1) * lw, :] = jnp.concatenate(parts, axis=1)

    rg = lax.broadcasted_iota(jnp.int32, (gpb * ns, lw), 0) // ns
    cg = lax.broadcasted_iota(jnp.int32, (gpb * ns, lw), 1) // ch
    keep = rg == cg
    for t2 in range(tc):
        cr, ci = ca[t2 + 1]
        cr = jnp.where(keep, jnp.concatenate([cr] * gpb, axis=0), 0.0)
        ci = jnp.where(keep, jnp.concatenate([ci] * gpb, axis=0), 0.0)
        wo_ref[0, :gpb * ns, t2 * lw:(t2 + 1) * lw] = cr.astype(BF16)
        wo_ref[0, gpb * ns:, t2 * lw:(t2 + 1) * lw] = (-ci).astype(BF16)

    a_row, coef_row = _s5_discretise(lam_row_ref[0], lam_row_ref[1], ldt_row_ref[...])
    bb_tl = _cmul(*coef_row, b_tl_ref[0], b_tl_ref[1])
    rg = lax.broadcasted_iota(jnp.int32, bb_tl[0].shape, 0) // ch
    cg = lax.broadcasted_iota(jnp.int32, bb_tl[0].shape, 1) // ns
    bb_tl = (jnp.where(rg == cg, bb_tl[0], 0.0), jnp.where(rg == cg, bb_tl[1], 0.0))
    pw = (jnp.ones_like(a_row[0]), jnp.zeros_like(a_row[0]))
    for t in range(tc - 1, -1, -1):
        r_re, r_im = _cmul(*pw, *bb_tl)
        wr_ref[0, t * lw:(t + 1) * lw, :] = jnp.concatenate([r_re, r_im], axis=1).astype(BF16)
        pw = _cmul(*pw, *a_row)
    a_ref[0, 0] = pw[0]
    a_ref[0, 1] = pw[1]


def _s5_weights(lam_re, lam_im, log_dt, b_re, b_im, c_re, c_im):
    g, ns = lam_re.shape
    ch, tc, gpb = S5_GROUP, S5_TC, S5_GPB
    nblk = g // gpb
    lw = gpb * ch
    lam = jnp.stack([lam_re, lam_im]).astype(F32)
    ldt = log_dt.astype(F32)
    bst = jnp.stack([b_re, b_im]).astype(F32)
    cst = jnp.stack([c_re, c_im]).astype(F32)
    lam_pg = jnp.repeat(lam.transpose(0, 2, 1), ch, axis=2)
    ldt_pg = jnp.repeat(ldt, ch)[None]
    c_pg = cst.transpose(0, 3, 1, 2).reshape(2, ns, g * ch)
    lam_gp = jnp.repeat(lam, ch, axis=1)
    ldt_gp = jnp.broadcast_to(jnp.repeat(ldt, ch)[:, None], (g * ch, ns))
    b_gp = bst.transpose(0, 1, 3, 2).reshape(2, g * ch, ns)
    lam_row = lam.reshape(2, 1, g * ns)
    ldt_row = jnp.repeat(ldt, ns)[None]
    b_tl = jnp.tile(b_gp, (1, 1, gpb))
    blk3 = lambda i: (0, 0, i)
    return pl.pallas_call(
        _s5_weights_kernel,
        name="s5_weights",
        grid=(nblk,),
        in_specs=[pl.BlockSpec((2, ns, lw), blk3),
                  pl.BlockSpec((1, lw), lambda i: (0, i)),
                  pl.BlockSpec((2, ns, lw), blk3),
                  pl.BlockSpec((2, lw, ns), lambda i: (0, i, 0)),
                  pl.BlockSpec((lw, ns), lambda i: (i, 0)),
                  pl.BlockSpec((2, lw, ns), lambda i: (0, i, 0)),
                  pl.BlockSpec((2, 1, gpb * ns), blk3),
                  pl.BlockSpec((1, gpb * ns), lambda i: (0, i)),
                  pl.BlockSpec((2, lw, gpb * ns), lambda i: (0, i, 0))],
        out_specs=[pl.BlockSpec((1, tc * lw, tc * lw), lambda i: (i, 0, 0)),
                   pl.BlockSpec((1, tc * lw, 2 * gpb * ns), lambda i: (i, 0, 0)),
                   pl.BlockSpec((1, 2 * gpb * ns, tc * lw), lambda i: (i, 0, 0)),
                   pl.BlockSpec((1, 2, 1, gpb * ns), lambda i: (i, 0, 0, 0))],
        out_shape=[jax.ShapeDtypeStruct((nblk, tc * lw, tc * lw), BF16),
                   jax.ShapeDtypeStruct((nblk, tc * lw, 2 * gpb * ns), BF16),
                   jax.ShapeDtypeStruct((nblk, 2 * gpb * ns, tc * lw), BF16),
                   jax.ShapeDtypeStruct((nblk, 2, 1, gpb * ns), F32)],
        compiler_params=_cparams(("arbitrary",)),
    )(lam_pg, ldt_pg, c_pg, lam_gp, ldt_gp, b_gp, lam_row, ldt_row, b_tl)


def _s5_mix_kernel(u_ref, wm_ref, wr_ref, wo_ref, a_ref, d_ref, y_ref, s_ref, x_ref, ys_ref):
    nk = u_ref.shape[1]
    half = s_ref.shape[1] // 2
    sub = 8
    u = u_ref[0]
    s_ref[...] = jnp.dot(u, wr_ref[0], preferred_element_type=F32)

    a1 = (a_ref[0, 0], a_ref[0, 1])
    a2 = _cmul(*a1, *a1)
    a4 = _cmul(*a2, *a2)
    a8 = _cmul(*a4, *a4)
    a3 = _cmul(*a2, *a1)
    pows = [(jnp.ones_like(a1[0]), jnp.zeros_like(a1[0])), a1, a2, a3, a4,
            _cmul(*a4, *a1), _cmul(*a4, *a2), _cmul(*a4, *a3)]
    p_re = jnp.concatenate([p[0] for p in pows], axis=0)
    p_im = jnp.concatenate([p[1] for p in pows], axis=0)
    rowid = lax.broadcasted_iota(jnp.int32, (sub, half), 0)
    levels = [(d, jnp.broadcast_to(a[0], (sub, half)), jnp.broadcast_to(a[1], (sub, half)))
              for d, a in ((1, a1), (2, a2), (4, a4))]

    def shift_rows(v, d):
        return jnp.where(rowid >= d, pltpu.roll(v, d, 0), 0.0)

    def tile(j, carry):
        cr, ci = carry
        rows = slice(j * sub, (j + 1) * sub)
        zr = s_ref[rows, :half]
        zi = s_ref[rows, half:]
        for d, br, bi in levels:
            sr, si = shift_rows(zr, d), shift_rows(zi, d)
            zr, zi = zr + br * sr - bi * si, zi + br * si + bi * sr
        x_ref[rows, :half] = p_re * cr - p_im * ci + shift_rows(zr, 1)
        x_ref[rows, half:] = p_re * ci + p_im * cr + shift_rows(zi, 1)
        return (a8[0] * cr - a8[1] * ci + zr[sub - 1:], a8[0] * ci + a8[1] * cr + zi[sub - 1:])

    carry = (jnp.zeros((1, half), F32), jnp.zeros((1, half), F32))
    for j in range(nk // sub):
        carry = tile(j, carry)
    xb = x_ref[...].astype(BF16)

    cw = 2 * LANES
    for n in range(u.shape[1] // cw):
        cs = slice(n * cw, (n + 1) * cw)
        y = jnp.dot(u[:, :(n + 1) * cw], wm_ref[0, :(n + 1) * cw, cs], preferred_element_type=F32)
        y = y + jnp.dot(xb, wo_ref[0, :, cs], preferred_element_type=F32)
        y = _gelu_tanh(y + d_ref[0, :, cs] * u[:, cs].astype(F32))
        for t in range(2 * n, 2 * n + 2):
            ys_ref[pl.ds(t, nk, stride=S5_TC), :] = y[:, (t - 2 * n) * LANES:(t - 2 * n + 1) * LANES]
    y_ref[0] = ys_ref[...].astype(y_ref.dtype)


def _s5_mixer(u3, nb, lam_re, lam_im, log_dt, b_re, b_im, c_re, c_im, d_skip):
    nblk, rows, w = u3.shape
    tc = S5_TC
    nk = rows // nb
    assert nk % 8 == 0
    wm, wr, wo, a_pow = _s5_weights(lam_re, lam_im, log_dt, b_re, b_im, c_re, c_im)
    ns = wr.shape[-1]
    d_rows = jnp.tile(d_skip.astype(F32).reshape(nblk, 1, LANES), (1, 1, tc))
    blk = lambda i, b: (i, 0, 0)
    return pl.pallas_call(
        _s5_mix_kernel,
        name="s5_mix",
        grid=(nblk, nb),
        in_specs=[pl.BlockSpec((1, nk, w), lambda i, b: (i, b, 0)),
                  pl.BlockSpec((1, w, w), blk),
                  pl.BlockSpec((1, w, ns), blk),
                  pl.BlockSpec((1, ns, w), blk),
                  pl.BlockSpec((1, 2, 1, ns // 2), lambda i, b: (i, 0, 0, 0)),
                  pl.BlockSpec((1, 1, w), blk)],
        out_specs=pl.BlockSpec((1, nk * tc, LANES), lambda i, b: (i, b, 0)),
        out_shape=jax.ShapeDtypeStruct((nblk, rows * tc, LANES), BF16),
        scratch_shapes=[pltpu.VMEM((nk, ns), F32), pltpu.VMEM((nk, ns), F32),
                        pltpu.VMEM((nk * tc, LANES), F32)],
        compiler_params=_cparams(("arbitrary", "arbitrary")),
    )(u3, wm, wr, wo, a_pow, d_rows)


def _even_out_kernel(x_ref, oa_ref, y_ref, zb_ref, wg_ref, bg_ref, wo_ref, g_ref, wf_ref,
                     x1_ref, hn_ref, f_ref):
    y = jnp.concatenate([y_ref[i] for i in range(y_ref.shape[0])], axis=1)
    t = jnp.dot(y, wg_ref[...], preferred_element_type=F32) + bg_ref[...]
    ob = y.astype(F32) * jax.nn.sigmoid(t) * _silu(zb_ref[...].astype(F32))
    wa = oa_ref.shape[1]
    acc = jnp.dot(oa_ref[...], wo_ref[:wa, :], preferred_element_type=F32)
    acc = acc + jnp.dot(ob.astype(BF16), wo_ref[wa:, :], preferred_element_type=F32)
    x1 = x_ref[...] + acc
    x1_ref[...] = x1
    hn = _rms(x1, g_ref[...]).astype(BF16)
    hn_ref[...] = hn
    f_ref[...] = lax.dot_general(wf_ref[...], hn, (((1,), (1,)), ((), ())),
                                 preferred_element_type=F32)


def _even_out(x2, oa, y, proj, w_glu, b_glu, w_out, g_next, wf_t):
    m, d = x2.shape
    nh = wf_t.shape[0]
    wa = oa.shape[1]
    nblk, _, lw = y.shape
    wb = nblk * lw
    tm = min(TM_OUT, m)
    zb_col = (proj.shape[1] - wb) // wb
    const = lambda i: (0, 0)
    return pl.pallas_call(
        _even_out_kernel,
        name="even_out",
        grid=(m // tm,),
        in_specs=[pl.BlockSpec((tm, d), lambda i: (i, 0)),
                  pl.BlockSpec((tm, wa), lambda i: (i, 0)),
                  pl.BlockSpec((nblk, tm, lw), lambda i: (0, i, 0)),
                  pl.BlockSpec((tm, wb), lambda i: (i, zb_col)),
                  pl.BlockSpec((wb, wb), const),
                  pl.BlockSpec((1, wb), const),
                  pl.BlockSpec((wa + wb, d), const),
                  pl.BlockSpec((1, d), const),
                  pl.BlockSpec((nh, d), const)],
        out_specs=[pl.BlockSpec((tm, d), lambda i: (i, 0)),
                   pl.BlockSpec((tm, d), lambda i: (i, 0)),
                   pl.BlockSpec((nh, tm), lambda i: (0, i))],
        out_shape=[jax.ShapeDtypeStruct((m, d), F32),
                   jax.ShapeDtypeStruct((m, d), BF16),
                   jax.ShapeDtypeStruct((nh, m), F32)],
        compiler_params=_cparams(("arbitrary",)),
    )(x2, oa, y, proj, w_glu, b_glu.reshape(1, wb), w_out, g_next.reshape(1, d), wf_t)


def _cum_forget_kernel(f_ref, b_ref, c_ref):
    x = f_ref[...] + b_ref[...]
    v = -(jnp.maximum(-x, 0.0) + jnp.log1p(jnp.exp(-jnp.abs(x))))
    n = v.shape[-1]
    pos = lax.broadcasted_iota(jnp.int32, v.shape, 1)
    d = 1
    while d < n:
        v = v + jnp.where(pos >= d, pltpu.roll(v, d, 1), 0.0)
        d *= 2
    c_ref[...] = v


def _cum_forget(f_t, b_forget, seq):
    nh, m = f_t.shape
    return pl.pallas_call(
        _cum_forget_kernel,
        name="cum_forget",
        grid=(m // seq,),
        in_specs=[pl.BlockSpec((nh, seq), lambda i: (0, i)),
                  pl.BlockSpec((nh, 1), lambda i: (0, 0))],
        out_specs=pl.BlockSpec((nh, seq), lambda i: (0, i)),
        out_shape=jax.ShapeDtypeStruct((nh, m), F32),
        compiler_params=_cparams(("arbitrary",)),
    )(f_t, b_forget.astype(F32).reshape(nh, 1))


def _split3(x):
    hi = x.astype(BF16).astype(F32)
    r = x - hi
    mid = r.astype(BF16).astype(F32)
    lo = (r - mid).astype(BF16).astype(F32)
    return hi, mid, lo


def _fox_kernel(q0_ref, qn_ref, k_ref, v_ref, z_ref, c_ref, o_ref,
                ke_ref, vt_ref, qa_ref, s_ref, mb_ref, acc_ref):
    qi = pl.program_id(2)
    nq = pl.num_programs(2)
    tq = qn_ref.shape[1]
    tk = tq
    dh = C_DH
    nh = c_ref.shape[1]
    nblk = k_ref.shape[1] // tk
    nt = (((1,), (1,)), ((), ()))

    def cols(h):
        return slice(h * dh, (h + 1) * dh)

    def prep_queries(q_ref, qb):
        qs = pl.multiple_of(qb * tq, tq)
        row = lax.broadcasted_iota(jnp.int32, (dh, tq), 0)
        for h in range(nh):
            c_base = jnp.max(c_ref[0, h, :, pl.ds(qs, tq)], axis=1, keepdims=True) * LOG2E
            bh, bm, bl = _split3(c_base)
            g = jnp.where(row < 3, 1.0, jnp.where(row == 3, bh, jnp.where(row == 4, bm,
                          jnp.where(row == 5, bl, 0.0))))
            qt = (q_ref[0, :, cols(h)].astype(F32) * (LOG2E / math.sqrt(dh))).T
            qa_ref[h, :dh, :] = qt.astype(BF16)
            qa_ref[h, dh:, :] = g.astype(BF16)

    half = tk // 2

    def produce(h, j, diagonal):
        ks = pl.multiple_of(j * tk, tk)

        def keys(start, n):
            return jnp.concatenate([k_ref[0, pl.ds(start, n), cols(h)],
                                    ke_ref[h, pl.ds(start, n), :]], axis=1)

        if not diagonal:
            s = jnp.dot(keys(ks, tk), qa_ref[h], preferred_element_type=F32)
            s_ref[h] = s
            return jnp.max(s, axis=0, keepdims=True)
        top = jnp.dot(keys(ks, half), qa_ref[h], preferred_element_type=F32)
        bot = jnp.dot(keys(ks + half, half), qa_ref[h, :, half:],
                      preferred_element_type=F32)
        top = jnp.where(lax.broadcasted_iota(jnp.int32, top.shape, 0)
                        <= lax.broadcasted_iota(jnp.int32, top.shape, 1), top, NEG)
        bot = jnp.where(lax.broadcasted_iota(jnp.int32, bot.shape, 0)
                        <= lax.broadcasted_iota(jnp.int32, bot.shape, 1), bot, NEG)
        s_ref[h, :half, :] = top
        s_ref[h, half:, half:] = bot
        bot_wide = jnp.concatenate([jnp.full((half, half), NEG, F32), bot], axis=1)
        return jnp.max(jnp.maximum(top, bot_wide), axis=0, keepdims=True)

    def consume(h, j, stats, diagonal=False):
        m, mb = stats
        ks = pl.multiple_of(j * tk, tk)
        m2 = jnp.maximum(m, mb)
        if not diagonal:
            p = jnp.exp2(s_ref[h] - m2).astype(BF16)
            pv = jnp.dot(vt_ref[h, :, pl.ds(ks, tk)], p, preferred_element_type=F32)
            acc_ref[h] = jnp.exp2(m - m2) * acc_ref[h] + pv
            return m2
        p_top = jnp.exp2(s_ref[h, :half, :] - m2).astype(BF16)
        mb_ref[h] = m2
        p_bot = jnp.exp2(s_ref[h, half:, half:] - mb_ref[h, :, half:]).astype(BF16)
        acc = jnp.exp2(m - m2) * acc_ref[h] + jnp.dot(vt_ref[h, :, pl.ds(ks, half)], p_top,
                                                      preferred_element_type=F32)
        acc_ref[h, :, :half] = acc[:, :half]
        acc_ref[h, :, half:] = acc[:, half:] + jnp.dot(vt_ref[h, :, pl.ds(ks + half, half)], p_bot,
                                                       preferred_element_type=F32)
        return m2

    def step(j, carry, diagonal_next):
        return tuple((consume(h, j, stats), produce(h, j + 1, diagonal_next))
                     for h, stats in enumerate(carry))

    @pl.when(qi == 0)
    def _():
        r = lax.broadcasted_iota(jnp.int32, (dh, dh), 0)
        c = lax.broadcasted_iota(jnp.int32, (dh, dh), 1)
        eye = (r == c).astype(BF16)
        row = lax.broadcasted_iota(jnp.int32, (dh, tk), 0)
        ones_row = (lax.broadcasted_iota(jnp.int32, (FOX_PAD, tk), 0) == 0).astype(BF16)

        def blk(j, carry):
            ks = pl.multiple_of(j * tk, tk)
            for h in range(nh):
                hi, mid, lo = _split3(c_ref[0, h, :, pl.ds(ks, tk)] * (-LOG2E))
                a = jnp.where(row == 0, hi, jnp.where(row == 1, mid, jnp.where(row == 2, lo,
                              jnp.where(row < 6, 1.0, 0.0))))
                ke_ref[h, pl.ds(ks, tk), :] = a.T.astype(BF16)
                vt = lax.dot_general(eye, v_ref[0, pl.ds(ks, tk), cols(h)], nt,
                                     preferred_element_type=F32)
                vt_ref[h, :dh, pl.ds(ks, tk)] = vt.astype(BF16)
                vt_ref[h, dh:, pl.ds(ks, tk)] = ones_row
            return carry

        lax.fori_loop(0, nblk, blk, 0)
        prep_queries(q0_ref, 0)
        for h in range(nh):
            mb_ref[h] = produce(h, 0, True)

    carry = []
    for h in range(nh):
        acc_ref[h] = jnp.zeros(acc_ref.shape[1:], F32)
        carry.append((jnp.full((1, tq), NEG, F32), mb_ref[h]))
    plain = functools.partial(step, diagonal_next=False)
    n_plain = jnp.maximum(qi - 1, 0)
    n_pairs = n_plain // 2
    carry = lax.fori_loop(0, n_pairs, lambda i, c: plain(2 * i + 1, plain(2 * i, c)),
                          tuple(carry))
    carry = lax.fori_loop(2 * n_pairs, n_plain, plain, carry)
    carry = lax.fori_loop(n_plain, qi, functools.partial(step, diagonal_next=True), carry)
    for h, stats in enumerate(carry):
        consume(h, qi, stats, diagonal=True)
    prep_queries(qn_ref, jnp.minimum(qi + 1, nq - 1))
    for h in range(nh):
        mb_ref[h] = produce(h, 0, False)
    for h in range(nh):
        acc = acc_ref[h]
        o = (acc[:dh] / acc[dh:dh + 1]).T
        o_ref[0, :, cols(h)] = (o * _silu(z_ref[0, :, cols(h)].astype(F32))).astype(o_ref.dtype)


def _fox(proj3, cum4):
    b, l, _ = proj3.shape
    ng = C_HEADS // FOX_HEADS
    w = FOX_HEADS * C_DH
    tq = min(FOX_TQ, l)
    nq = l // tq
    return pl.pallas_call(
        _fox_kernel,
        name="fox",
        grid=(b, ng, nq),
        in_specs=[pl.BlockSpec((1, tq, w), lambda bi, gi, qi: (bi, 0, gi)),
                  pl.BlockSpec((1, tq, w), lambda bi, gi, qi: (bi, jnp.minimum(qi + 1, nq - 1), gi)),
                  pl.BlockSpec((1, l, w), lambda bi, gi, qi: (bi, 0, ng + gi)),
                  pl.BlockSpec((1, l, w), lambda bi, gi, qi: (bi, 0, 2 * ng + gi)),
                  pl.BlockSpec((1, tq, w), lambda bi, gi, qi: (bi, qi, 3 * ng + gi)),
                  pl.BlockSpec((1, FOX_HEADS, 1, l), lambda bi, gi, qi: (bi, gi, 0, 0))],
        out_specs=pl.BlockSpec((1, tq, w), lambda bi, gi, qi: (bi, qi, gi)),
        out_shape=jax.ShapeDtypeStruct((b, l, C_HEADS * C_DH), BF16),
        scratch_shapes=[pltpu.VMEM((FOX_HEADS, l, C_DH), BF16),
                        pltpu.VMEM((FOX_HEADS, C_DH + FOX_PAD, l), BF16),
                        pltpu.VMEM((FOX_HEADS, 2 * C_DH, tq), BF16),
                        pltpu.VMEM((FOX_HEADS, tq, tq), F32),
                        pltpu.VMEM((FOX_HEADS, 1, tq), F32),
                        pltpu.VMEM((FOX_HEADS, C_DH + FOX_PAD, tq), F32)],
        compiler_params=_cparams(("arbitrary", "arbitrary", "arbitrary")),
    )(proj3, proj3, proj3, proj3, proj3, cum4)


def _odd_out_kernel(x_ref, o_ref, w_ref, g_ref, out_ref):
    x2 = x_ref[...] + jnp.dot(o_ref[...], w_ref[...], preferred_element_type=F32)
    out_ref[...] = _rms(x2, g_ref[...])


def _odd_out(x1, o, w_out, g):
    m, d = x1.shape
    k = o.shape[1]
    tm = min(TM_OUT, m)
    return pl.pallas_call(
        _odd_out_kernel,
        name="odd_out",
        grid=(m // tm,),
        in_specs=[pl.BlockSpec((tm, d), lambda i: (i, 0)),
                  pl.BlockSpec((tm, k), lambda i: (i, 0)),
                  pl.BlockSpec((k, d), lambda i: (0, 0)),
                  pl.BlockSpec((1, d), lambda i: (0, 0))],
        out_specs=pl.BlockSpec((tm, d), lambda i: (i, 0)),
        out_shape=jax.ShapeDtypeStruct((m, d), F32),
        compiler_params=_cparams(("arbitrary",)),
    )(x1, o, w_out, g.reshape(1, d))


def kernel(x, norm_even_g, w_in_even, rel_bias, s5_lambda_re, s5_lambda_im, s5_log_dt,
           s5_b_re, s5_b_im, s5_c_re, s5_c_im, s5_d, w_glu, b_glu, w_out_even,
           norm_odd_g, w_in_odd, b_forget, w_out_odd, final_norm_g):
    b, l, d = x.shape
    assert norm_even_g.shape[0] == 1 and norm_odd_g.shape[0] == 1, "one even + one odd layer"
    assert l % FOX_TQ == 0 and l % VT_BLK == 0 and l % A_QB == 0 and l >= A_KB
    m = b * l
    a_w = A_HEADS * A_DH
    s5_w = s5_d.shape[1] * S5_GROUP
    c_w = C_HEADS * C_DH

    x2 = x.reshape(m, d)
    assert s5_w == TN_PROJ and l % S5_TC == 0
    proj, u3 = _norm_proj(x2, norm_even_g[0], w_in_even[0].astype(BF16), 3 * a_w)
    proj3 = proj.reshape(b, l, proj.shape[1])
    o_a = _attn_a(proj3, rel_bias[0])
    y = _s5_mixer(u3, b, s5_lambda_re[0], s5_lambda_im[0], s5_log_dt[0], s5_b_re[0],
                  s5_b_im[0], s5_c_re[0], s5_c_im[0], s5_d[0])
    w_odd_t = w_in_odd[0].T
    wf_t = w_odd_t[4 * c_w:].astype(BF16)
    x1, hn1, f_t = _even_out(x2, o_a.reshape(m, a_w), y, proj,
                             w_glu[0].astype(BF16), b_glu[0].astype(F32),
                             w_out_even[0].astype(BF16), norm_odd_g[0], wf_t)
    proj1 = _proj_odd(hn1, w_odd_t, 4 * c_w)
    cum = _cum_forget(f_t, b_forget[0], l)
    cum4 = cum.reshape(C_HEADS, b, 1, l).transpose(1, 0, 2, 3)
    o_c = _fox(proj1.reshape(b, l, 4 * c_w), cum4)
    out = _odd_out(x1, o_c.reshape(m, c_w), w_out_odd[0].astype(BF16), final_norm_g)
    return out.reshape(b, l, d)
```

```python
import functools
import math

import jax
import jax.numpy as jnp
from jax import lax
from jax.experimental import pallas as pl
from jax.experimental.pallas import tpu as pltpu

F32 = jnp.float32
BF16 = jnp.bfloat16

EPS = 1e-6
NEG = -1e30
LOG2E = math.log2(math.e)

CHUNK = 64
N_LEFT = 8
A_HEADS = 16
A_DH = 64
MAX_REL = 128
S5_GROUP = 16
S5_STATE = 64
C_HEADS = 16
C_DH = 128

LANES = 128
VMEM_LIMIT = 56 * 1024 * 1024

TM_PROJ = 1024
TN_PROJ = 1024
PROJ_CHUNKS = 4
TM_OUT = 512
TM_PROJ_ODD = 2048
A_QB = 2 * CHUNK
A_KB = A_QB + N_LEFT * CHUNK
A_TW = A_KB + N_LEFT * CHUNK
A_PAD = 16
A_PAIRS = 2
VT_BLK = 512
S5_TC = 16
S5_GPB = LANES // S5_GROUP
FOX_TQ = 512
FOX_HEADS = 4
FOX_PAD = 16


def _cparams(sem):
    return pltpu.CompilerParams(dimension_semantics=sem, vmem_limit_bytes=VMEM_LIMIT)


def _rms(x, g):
    ms = jnp.mean(x * x, axis=-1, keepdims=True)
    return x * lax.rsqrt(ms + EPS) * g


def _silu(x):
    return x * jax.nn.sigmoid(x)


def _gelu_tanh(x):
    c = math.sqrt(2.0 / math.pi)
    return 0.5 * x * (1.0 + jnp.tanh(c * (x + 0.044715 * (x * x * x))))


def _norm_proj_kernel(x_ref, g_ref, w_ref, o_ref, u_ref, hn_ref, us_ref, *, u_tile):
    j = pl.program_id(1)
    tm = x_ref.shape[0]

    @pl.when(j == 0)
    def _():
        for c in range(PROJ_CHUNKS):
            rows = slice(c * tm // PROJ_CHUNKS, (c + 1) * tm // PROJ_CHUNKS)
            h = _rms(x_ref[rows, :], g_ref[...]).astype(BF16)
            hn_ref[rows, :] = h
            o_ref[rows, :] = jnp.dot(h, w_ref[...], preferred_element_type=F32).astype(o_ref.dtype)

    @pl.when(j != 0)
    def _():
        acc = jnp.dot(hn_ref[...], w_ref[...], preferred_element_type=F32)
        o_ref[...] = acc.astype(o_ref.dtype)

        @pl.when(j == u_tile)
        def _():
            nchunk = us_ref.shape[1] // S5_TC
            for blk in range(u_ref.shape[0]):
                us_ref[blk] = acc[:, blk * LANES:(blk + 1) * LANES]
                for t in range(S5_TC):
                    rows = us_ref[blk, pl.ds(t, nchunk, stride=S5_TC), :]
                    u_ref[blk, :, t * LANES:(t + 1) * LANES] = rows.astype(u_ref.dtype)


def _norm_proj(x2, g, w, u_start):
    m, d = x2.shape
    n = w.shape[1]
    tm, tn = min(TM_PROJ, m), TN_PROJ
    assert u_start % tn == 0 and u_start > 0 and tm % (16 * S5_TC) == 0
    nblk = tn // LANES
    return pl.pallas_call(
        functools.partial(_norm_proj_kernel, u_tile=u_start // tn),
        name="norm_proj",
        grid=(m // tm, n // tn),
        in_specs=[pl.BlockSpec((tm, d), lambda i, j: (i, 0)),
                  pl.BlockSpec((1, d), lambda i, j: (0, 0)),
                  pl.BlockSpec((d, tn), lambda i, j: (0, j))],
        out_specs=[pl.BlockSpec((tm, tn), lambda i, j: (i, j)),
                   pl.BlockSpec((nblk, tm // S5_TC, S5_TC * LANES), lambda i, j: (0, i, 0))],
        out_shape=[jax.ShapeDtypeStruct((m, n), BF16),
                   jax.ShapeDtypeStruct((nblk, m // S5_TC, S5_TC * LANES), BF16)],
        scratch_shapes=[pltpu.VMEM((tm, d), BF16), pltpu.VMEM((nblk, tm, LANES), F32)],
        compiler_params=_cparams(("arbitrary", "arbitrary")),
    )(x2, g.reshape(1, d), w)


def _proj_odd_kernel(h_ref, wt_ref, o_ref, wb_ref):
    @pl.when(pl.program_id(1) == 0)
    def _():
        wb_ref[...] = wt_ref[...].T.astype(BF16)

    o_ref[...] = jnp.dot(h_ref[...], wb_ref[...],
                         preferred_element_type=F32).astype(o_ref.dtype)


def _proj_odd(hn, wt, n):
    m, d = hn.shape
    tm, tn = min(TM_PROJ_ODD, m), TN_PROJ
    assert n % tn == 0 and m % tm == 0
    return pl.pallas_call(
        _proj_odd_kernel,
        name="proj_odd",
        grid=(n // tn, m // tm),
        in_specs=[pl.BlockSpec((tm, d), lambda j, i: (i, 0)),
                  pl.BlockSpec((tn, d), lambda j, i: (j, 0))],
        out_specs=pl.BlockSpec((tm, tn), lambda j, i: (i, j)),
        out_shape=jax.ShapeDtypeStruct((m, n), BF16),
        scratch_shapes=[pltpu.VMEM((d, tn), BF16)],
        compiler_params=_cparams(("arbitrary", "arbitrary")),
    )(hn, wt)


def _attn_a_kernel(q_ref, k_ref, v_ref, z_ref, e_ref, o_ref,
                   tt_ref, vt_ref, qt_ref, s_ref, pv_ref):
    l_all = k_ref.shape[1]
    w = q_ref.shape[2]
    pw = 2 * A_DH
    npair = w // pw
    nt = (((1,), (1,)), ((), ()))
    band = N_LEFT * CHUNK

    @pl.when(pl.program_id(1) == 0)
    def _():
        i = lax.broadcasted_iota(jnp.int32, (A_QB, A_TW), 0)
        m = lax.broadcasted_iota(jnp.int32, (A_QB, A_TW), 1)
        dchunk = m // CHUNK - i // CHUNK
        ok = (dchunk >= 0) & (dchunk <= N_LEFT)
        for h in range(2 * npair):
            rows = jnp.broadcast_to(e_ref[0, h:h + 1, :], (A_QB, A_QB + A_TW))
            toep = pltpu.roll(rows, 0, 1, stride=1, stride_axis=0)[:, A_QB:]
            t = jnp.where(ok, toep * LOG2E, NEG)
            tt_ref[:, h * A_QB:(h + 1) * A_QB] = t.T

    r = lax.broadcasted_iota(jnp.int32, (w, w), 0)
    c = lax.broadcasted_iota(jnp.int32, (w, w), 1)
    eye = (r == c).astype(BF16)
    ones_row = (lax.broadcasted_iota(jnp.int32, (A_PAD, VT_BLK), 0) == 0).astype(BF16)

    qscale = LOG2E / math.sqrt(A_DH)

    def vt_blk(j, carry):
        ks = pl.multiple_of(j * VT_BLK, VT_BLK)
        vt = lax.dot_general(eye, v_ref[0, pl.ds(ks, VT_BLK), :], nt, preferred_element_type=F32)
        for h in range(2 * npair):
            vt_ref[h, :A_DH, pl.ds(ks, VT_BLK)] = vt[h * A_DH:(h + 1) * A_DH].astype(BF16)
            vt_ref[h, A_DH:, pl.ds(ks, VT_BLK)] = ones_row
        qs = (q_ref[0, pl.ds(ks, VT_BLK), :].astype(F32) * qscale).astype(BF16)
        qt_ref[:, pl.ds(ks, VT_BLK)] = lax.dot_general(
            eye, qs, nt, preferred_element_type=F32).astype(BF16)
        return carry

    lax.fori_loop(0, l_all // VT_BLK, vt_blk, 0)

    head_row = lax.broadcasted_iota(jnp.int32, (pw, A_QB), 0)

    def window(qb):
        start = qb * A_QB - band
        start_c = jnp.maximum(start, 0)
        return pl.multiple_of(start_c, LANES), pl.multiple_of(start_c - start, LANES)

    def produce(qb, slot):
        qs = pl.multiple_of(qb * A_QB, A_QB)
        start_c, off = window(qb)
        out = []
        for p in range(npair):
            qt = qt_ref[p * pw:(p + 1) * pw, pl.ds(qs, A_QB)]
            zero = jnp.zeros_like(qt)
            qbd = jnp.concatenate([jnp.where(head_row < A_DH, qt, zero),
                                   jnp.where(head_row >= A_DH, qt, zero)], axis=1)
            kb = k_ref[0, pl.ds(start_c, A_KB), p * pw:(p + 1) * pw]
            s = jnp.dot(kb, qbd, preferred_element_type=F32)
            s = s + tt_ref[pl.ds(off, A_KB), 2 * p * A_QB:2 * (p + 1) * A_QB]
            s_ref[slot, p] = s
            out.append(jnp.max(s, axis=0, keepdims=True))
        return tuple(out)

    def consume(qb, slot, ms):
        start_c, _ = window(qb)
        for p in range(npair):
            pb = jnp.exp2(s_ref[slot, p] - ms[p]).astype(BF16)
            for h in range(2):
                vt = vt_ref[2 * p + h, :, pl.ds(start_c, A_KB)]
                pv_ref[slot, 2 * p + h] = jnp.dot(vt, pb[:, h * A_QB:(h + 1) * A_QB],
                                                  preferred_element_type=F32)

    def finish(qb, slot):
        qs = pl.multiple_of(qb * A_QB, A_QB)
        outs = []
        for h in range(2 * npair):
            oh = pv_ref[slot, h]
            outs.append(oh[:A_DH] * (1.0 / oh[A_DH:A_DH + 1]))
        o = jnp.concatenate(outs, axis=0).T
        gate = _silu(z_ref[0, pl.ds(qs, A_QB), :].astype(F32))
        o_ref[0, pl.ds(qs, A_QB), :] = (o * gate).astype(o_ref.dtype)

    def pair(i, m, lead=True):
        j = 2 * i
        if lead:
            finish(j - 1, 1)
        m1 = produce(j + 1, 1)
        consume(j, 0, m)
        m2 = produce(j + 2, 0)
        finish(j, 0)
        consume(j + 1, 1, m1)
        return m2

    nb = l_all // A_QB
    npairs = (nb - 1) // 2
    m = pair(0, produce(0, 0), lead=False)
    m = lax.fori_loop(1, npairs, pair, m)
    last = 2 * npairs
    finish(last - 1, 1)
    if (nb - 1) % 2:
        m1 = produce(last + 1, 1)
        consume(last, 0, m)
        finish(last, 0)
        consume(last + 1, 1, m1)
        finish(last + 1, 1)
    else:
        consume(last, 0, m)
        finish(last, 0)


def _attn_a(proj3, rel_bias):
    b, l, _ = proj3.shape
    nh = 2 * A_PAIRS
    ng = A_HEADS // nh
    width = nh * A_DH
    ncol = A_HEADS * A_DH // width
    zcol = (3 * A_HEADS * A_DH + (A_HEADS * A_DH)) // width
    left = N_LEFT * CHUNK + A_QB - MAX_REL
    right = A_QB + A_TW - left - (2 * MAX_REL + 1)
    ext = jnp.pad(rel_bias.astype(F32)[:, ::-1], ((0, 0), (left, right)), mode="edge")
    ext = ext.reshape(ng, nh, A_QB + A_TW)
    seq = lambda h, bi: (bi, 0, h)
    return pl.pallas_call(
        _attn_a_kernel,
        name="attn_a",
        grid=(ng, b),
        in_specs=[pl.BlockSpec((1, l, width), seq),
                  pl.BlockSpec((1, l, width), lambda h, bi: (bi, 0, ncol + h)),
                  pl.BlockSpec((1, l, width), lambda h, bi: (bi, 0, 2 * ncol + h)),
                  pl.BlockSpec((1, l, width), lambda h, bi: (bi, 0, zcol + h)),
                  pl.BlockSpec((1, nh, A_QB + A_TW), lambda h, bi: (h, 0, 0))],
        out_specs=pl.BlockSpec((1, l, width), seq),
        out_shape=jax.ShapeDtypeStruct((b, l, A_HEADS * A_DH), BF16),
        scratch_shapes=[pltpu.VMEM((A_TW, nh * A_QB), F32),
                        pltpu.VMEM((nh, A_DH + A_PAD, l), BF16),
                        pltpu.VMEM((width, l), BF16),
                        pltpu.VMEM((2, A_PAIRS, A_KB, 2 * A_QB), F32),
                        pltpu.VMEM((2, nh, A_DH + A_PAD, A_QB), F32)],
        compiler_params=_cparams(("arbitrary", "arbitrary")),
    )(proj3, proj3, proj3, proj3, ext)


def _cmul(ar, ai, br, bi):
    return ar * br - ai * bi, ar * bi + ai * br


def _s5_discretise(lr, li, log_dt):
    dt = jnp.exp(log_dt)
    mag = jnp.exp(lr * dt)
    ang = li * dt
    a_re = mag * jnp.cos(ang)
    a_im = mag * jnp.sin(ang)
    den = lr * lr + li * li
    nr = a_re - 1.0
    coef_re = (nr * lr + a_im * li) / den
    coef_im = (a_im * lr - nr * li) / den
    return (a_re, a_im), (coef_re, coef_im)


def _s5_weights_kernel(lam_pg_ref, ldt_pg_ref, c_pg_ref, lam_gp_ref, ldt_gp_ref, b_gp_ref,
                       lam_row_ref, ldt_row_ref, b_tl_ref, wm_ref, wr_ref, wo_ref, a_ref):
    tc, ch, ns, gpb = S5_TC, S5_GROUP, S5_STATE, S5_GPB
    lw = gpb * ch
    hi = lax.Precision.HIGHEST

    a_pg, _ = _s5_discretise(lam_pg_ref[0], lam_pg_ref[1], ldt_pg_ref[...])
    pw = (jnp.ones_like(a_pg[0]), jnp.zeros_like(a_pg[0]))
    ca = []
    for n in range(tc + 1):
        ca.append(_cmul(c_pg_ref[0], c_pg_ref[1], *pw))
        pw = _cmul(*pw, *a_pg)

    _, coef_gp = _s5_discretise(lam_gp_ref[0], lam_gp_ref[1], ldt_gp_ref[...])
    bb_gp = _cmul(*coef_gp, b_gp_ref[0], b_gp_ref[1])
    ca_re = jnp.concatenate([ca[n][0] for n in range(tc)], axis=1)
    ca_im = jnp.concatenate([ca[n][1] for n in range(tc)], axis=1)
    z = (jnp.dot(bb_gp[0], ca_re, precision=hi, preferred_element_type=F32)
         - jnp.dot(bb_gp[1], ca_im, precision=hi, preferred_element_type=F32))
    row_g = lax.broadcasted_iota(jnp.int32, z.shape, 0) // ch
    col_g = (lax.broadcasted_iota(jnp.int32, z.shape, 1) % lw) // ch
    z = jnp.where(row_g == col_g, z, 0.0).astype(BF16)
    for t in range(tc):
        parts = [z[:, :(tc - t) * lw]]
        if t:
            parts.insert(0, jnp.zeros((lw, t * lw), BF16))
        wm_ref[0, t * lw:(t + 1) * lw, :] = jnp.concatenate(parts, axis=1)

    rg = lax.broadcasted_iota(jnp.int32, (gpb * ns, lw), 0) // ns
    cg = lax.broadcasted_iota(jnp.int32, (gpb * ns, lw), 1) // ch
    keep = rg == cg
    for t2 in range(tc):
        cr, ci = ca[t2 + 1]
        cr = jnp.where(keep, jnp.concatenate([cr] * gpb, axis=0), 0.0)
        ci = jnp.where(keep, jnp.concatenate([ci] * gpb, axis=0), 0.0)
        wo_ref[0, :gpb * ns, t2 * lw:(t2 + 1) * lw] = cr.astype(BF16)
        wo_ref[0, gpb * ns:, t2 * lw:(t2 + 1) * lw] = (-ci).astype(BF16)

    a_row, coef_row = _s5_discretise(lam_row_ref[0], lam_row_ref[1], ldt_row_ref[...])
    bb_tl = _cmul(*coef_row, b_tl_ref[0], b_tl_ref[1])
    rg = lax.broadcasted_iota(jnp.int32, bb_tl[0].shape, 0) // ch
    cg = lax.broadcasted_iota(jnp.int32, bb_tl[0].shape, 1) // ns
    bb_tl = (jnp.where(rg == cg, bb_tl[0], 0.0), jnp.where(rg == cg, bb_tl[1], 0.0))
    pw = (jnp.ones_like(a_row[0]), jnp.zeros_like(a_row[0]))
    for t in range(tc - 1, -1, -1):
        r_re, r_im = _cmul(*pw, *bb_tl)
        wr_ref[0, t * lw:(t + 1) * lw, :] = jnp.concatenate([r_re, r_im], axis=1).astype(BF16)
        pw = _cmul(*pw, *a_row)
    a_ref[0, 0] = pw[0]
    a_ref[0, 1] = pw[1]


def _s5_param_views(lam_re, lam_im, log_dt, b_re, b_im, c_re, c_im):
    g, ns = lam_re.shape
    ch, gpb = S5_GROUP, S5_GPB
    lw = gpb * ch
    lam = jnp.stack([lam_re, lam_im]).astype(F32)
    ldt = log_dt.astype(F32)
    bst = jnp.stack([b_re, b_im]).astype(F32)
    cst = jnp.stack([c_re, c_im]).astype(F32)
    lam_pg = jnp.repeat(lam.transpose(0, 2, 1), ch, axis=2)
    ldt_pg = jnp.repeat(ldt, ch)[None]
    c_pg = cst.transpose(0, 3, 1, 2).reshape(2, ns, g * ch)
    lam_gp = jnp.repeat(lam, ch, axis=1)
    ldt_gp = jnp.broadcast_to(jnp.repeat(ldt, ch)[:, None], (g * ch, ns))
    b_gp = bst.transpose(0, 1, 3, 2).reshape(2, g * ch, ns)
    lam_row = lam.reshape(2, 1, g * ns)
    ldt_row = jnp.repeat(ldt, ns)[None]
    b_tl = jnp.tile(b_gp, (1, 1, gpb))
    last = lambda i, b: (0, 0, i)
    mid = lambda i, b: (0, i, 0)
    specs = [pl.BlockSpec((2, ns, lw), last),
             pl.BlockSpec((1, lw), lambda i, b: (0, i)),
             pl.BlockSpec((2, ns, lw), last),
             pl.BlockSpec((2, lw, ns), mid),
             pl.BlockSpec((lw, ns), lambda i, b: (i, 0)),
             pl.BlockSpec((2, lw, ns), mid),
             pl.BlockSpec((2, 1, gpb * ns), last),
             pl.BlockSpec((1, gpb * ns), lambda i, b: (0, i)),
             pl.BlockSpec((2, lw, gpb * ns), mid)]
    return [lam_pg, ldt_pg, c_pg, lam_gp, ldt_gp, b_gp, lam_row, ldt_row, b_tl], specs


def _s5_mix_kernel(u_ref, *refs):
    params, (d_ref, y_ref, wm_ref, wr_ref, wo_ref, a_ref, s_ref, x_ref, ys_ref) = refs[:9], refs[9:]

    @pl.when(pl.program_id(1) == 0)
    def _():
        _s5_weights_kernel(*params, wm_ref, wr_ref, wo_ref, a_ref)

    nk = u_ref.shape[1]
    half = s_ref.shape[1] // 2
    sub = 8
    u = u_ref[0]
    s_ref[...] = jnp.dot(u, wr_ref[0], preferred_element_type=F32)

    a1 = (a_ref[0, 0], a_ref[0, 1])
    a2 = _cmul(*a1, *a1)
    a4 = _cmul(*a2, *a2)
    a8 = _cmul(*a4, *a4)
    a3 = _cmul(*a2, *a1)
    pows = [(jnp.ones_like(a1[0]), jnp.zeros_like(a1[0])), a1, a2, a3, a4,
            _cmul(*a4, *a1), _cmul(*a4, *a2), _cmul(*a4, *a3)]
    p_re = jnp.concatenate([p[0] for p in pows], axis=0)
    p_im = jnp.concatenate([p[1] for p in pows], axis=0)
    rowid = lax.broadcasted_iota(jnp.int32, (sub, half), 0)
    levels = [(d, jnp.broadcast_to(a[0], (sub, half)), jnp.broadcast_to(a[1], (sub, half)))
              for d, a in ((1, a1), (2, a2), (4, a4))]

    def shift_rows(v, d):
        return jnp.where(rowid >= d, pltpu.roll(v, d, 0), 0.0)

    def tile(j, carry):
        cr, ci = carry
        rows = slice(j * sub, (j + 1) * sub)
        zr = s_ref[rows, :half]
        zi = s_ref[rows, half:]
        for d, br, bi in levels:
            sr, si = shift_rows(zr, d), shift_rows(zi, d)
            zr, zi = zr + br * sr - bi * si, zi + br * si + bi * sr
        x_ref[rows, :half] = p_re * cr - p_im * ci + shift_rows(zr, 1)
        x_ref[rows, half:] = p_re * ci + p_im * cr + shift_rows(zi, 1)
        return (a8[0] * cr - a8[1] * ci + zr[sub - 1:], a8[0] * ci + a8[1] * cr + zi[sub - 1:])

    carry = (jnp.zeros((1, half), F32), jnp.zeros((1, half), F32))
    for j in range(nk // sub):
        carry = tile(j, carry)
    xb = x_ref[...].astype(BF16)

    cw = 2 * LANES
    for n in range(u.shape[1] // cw):
        cs = slice(n * cw, (n + 1) * cw)
        y = jnp.dot(u[:, :(n + 1) * cw], wm_ref[0, :(n + 1) * cw, cs], preferred_element_type=F32)
        y = y + jnp.dot(xb, wo_ref[0, :, cs], preferred_element_type=F32)
        y = _gelu_tanh(y + d_ref[0, :, cs] * u[:, cs].astype(F32))
        for t in range(2 * n, 2 * n + 2):
            ys_ref[pl.ds(t, nk, stride=S5_TC), :] = y[:, (t - 2 * n) * LANES:(t - 2 * n + 1) * LANES]
    y_ref[0] = ys_ref[...].astype(y_ref.dtype)


def _s5_mixer(u3, nb, lam_re, lam_im, log_dt, b_re, b_im, c_re, c_im, d_skip):
    nblk, rows, w = u3.shape
    tc = S5_TC
    nk = rows // nb
    assert nk % 8 == 0
    views, view_specs = _s5_param_views(lam_re, lam_im, log_dt, b_re, b_im, c_re, c_im)
    ns = 2 * S5_GPB * S5_STATE
    d_rows = jnp.tile(d_skip.astype(F32).reshape(nblk, 1, LANES), (1, 1, tc))
    return pl.pallas_call(
        _s5_mix_kernel,
        name="s5_mix",
        grid=(nblk, nb),
        in_specs=[pl.BlockSpec((1, nk, w), lambda i, b: (i, b, 0))] + view_specs
                 + [pl.BlockSpec((1, 1, w), lambda i, b: (i, 0, 0))],
        out_specs=pl.BlockSpec((1, nk * tc, LANES), lambda i, b: (i, b, 0)),
        out_shape=jax.ShapeDtypeStruct((nblk, rows * tc, LANES), BF16),
        scratch_shapes=[pltpu.VMEM((1, w, w), BF16),
                        pltpu.VMEM((1, w, ns), BF16),
                        pltpu.VMEM((1, ns, w), BF16),
                        pltpu.VMEM((1, 2, 1, ns // 2), F32),
                        pltpu.VMEM((nk, ns), F32), pltpu.VMEM((nk, ns), F32),
                        pltpu.VMEM((nk * tc, LANES), F32)],
        compiler_params=_cparams(("arbitrary", "arbitrary")),
    )(u3, *views, d_rows)


def _even_out_kernel(x_ref, oa_ref, y_ref, zb_ref, wg_ref, bg_ref, wo_ref, g_ref, wf_ref,
                     x1_ref, hn_ref, f_ref):
    y = jnp.concatenate([y_ref[i] for i in range(y_ref.shape[0])], axis=1)
    t = jnp.dot(y, wg_ref[...], preferred_element_type=F32) + bg_ref[...]
    ob = y.astype(F32) * jax.nn.sigmoid(t) * _silu(zb_ref[...].astype(F32))
    wa = oa_ref.shape[1]
    acc = jnp.dot(oa_ref[...], wo_ref[:wa, :], preferred_element_type=F32)
    acc = acc + jnp.dot(ob.astype(BF16), wo_ref[wa:, :], preferred_element_type=F32)
    x1 = x_ref[...] + acc
    x1_ref[...] = x1
    hn = _rms(x1, g_ref[...]).astype(BF16)
    hn_ref[...] = hn
    f_ref[...] = lax.dot_general(wf_ref[...], hn, (((1,), (1,)), ((), ())),
                                 preferred_element_type=F32)


def _even_out(x2, oa, y, proj, w_glu, b_glu, w_out, g_next, wf_t):
    m, d = x2.shape
    nh = wf_t.shape[0]
    wa = oa.shape[1]
    nblk, _, lw = y.shape
    wb = nblk * lw
    tm = min(TM_OUT, m)
    zb_col = (proj.shape[1] - wb) // wb
    const = lambda i: (0, 0)
    return pl.pallas_call(
        _even_out_kernel,
        name="even_out",
        grid=(m // tm,),
        in_specs=[pl.BlockSpec((tm, d), lambda i: (i, 0)),
                  pl.BlockSpec((tm, wa), lambda i: (i, 0)),
                  pl.BlockSpec((nblk, tm, lw), lambda i: (0, i, 0)),
                  pl.BlockSpec((tm, wb), lambda i: (i, zb_col)),
                  pl.BlockSpec((wb, wb), const),
                  pl.BlockSpec((1, wb), const),
                  pl.BlockSpec((wa + wb, d), const),
                  pl.BlockSpec((1, d), const),
                  pl.BlockSpec((nh, d), const)],
        out_specs=[pl.BlockSpec((tm, d), lambda i: (i, 0)),
                   pl.BlockSpec((tm, d), lambda i: (i, 0)),
                   pl.BlockSpec((nh, tm), lambda i: (0, i))],
        out_shape=[jax.ShapeDtypeStruct((m, d), F32),
                   jax.ShapeDtypeStruct((m, d), BF16),
                   jax.ShapeDtypeStruct((nh, m), F32)],
        compiler_params=_cparams(("arbitrary",)),
    )(x2, oa, y, proj, w_glu, b_glu.reshape(1, wb), w_out, g_next.reshape(1, d), wf_t)


def _cum_forget_kernel(f_ref, b_ref, c_ref):
    x = f_ref[...] + b_ref[...]
    v = -(jnp.maximum(-x, 0.0) + jnp.log1p(jnp.exp(-jnp.abs(x))))
    n = v.shape[-1]
    pos = lax.broadcasted_iota(jnp.int32, v.shape, 1)
    d = 1
    while d < n:
        v = v + jnp.where(pos >= d, pltpu.roll(v, d, 1), 0.0)
        d *= 2
    c_ref[...] = v


def _cum_forget(f_t, b_forget, seq):
    nh, m = f_t.shape
    return pl.pallas_call(
        _cum_forget_kernel,
        name="cum_forget",
        grid=(m // seq,),
        in_specs=[pl.BlockSpec((nh, seq), lambda i: (0, i)),
                  pl.BlockSpec((nh, 1), lambda i: (0, 0))],
        out_specs=pl.BlockSpec((nh, seq), lambda i: (0, i)),
        out_shape=jax.ShapeDtypeStruct((nh, m), F32),
        compiler_params=_cparams(("arbitrary",)),
    )(f_t, b_forget.astype(F32).reshape(nh, 1))


def _split3(x):
    hi = x.astype(BF16).astype(F32)
    r = x - hi
    mid = r.astype(BF16).astype(F32)
    lo = (r - mid).astype(BF16).astype(F32)
    return hi, mid, lo


def _fox_kernel(q0_ref, qn_ref, k_ref, v_ref, z_ref, c_ref, o_ref,
                ke_ref, vt_ref, qa_ref, s_ref, mb_ref, acc_ref):
    qi = pl.program_id(2)
    nq = pl.num_programs(2)
    tq = qn_ref.shape[1]
    tk = tq
    dh = C_DH
    nh = c_ref.shape[1]
    nblk = k_ref.shape[1] // tk
    nt = (((1,), (1,)), ((), ()))

    def cols(h):
        return slice(h * dh, (h + 1) * dh)

    def prep_queries(q_ref, qb):
        qs = pl.multiple_of(qb * tq, tq)
        row = lax.broadcasted_iota(jnp.int32, (dh, tq), 0)
        for h in range(nh):
            c_base = jnp.max(c_ref[0, h, :, pl.ds(qs, tq)], axis=1, keepdims=True) * LOG2E
            bh, bm, bl = _split3(c_base)
            g = jnp.where(row < 3, 1.0, jnp.where(row == 3, bh, jnp.where(row == 4, bm,
                          jnp.where(row == 5, bl, 0.0))))
            qt = (q_ref[0, :, cols(h)].astype(F32) * (LOG2E / math.sqrt(dh))).T
            qa_ref[h, :dh, :] = qt.astype(BF16)
            qa_ref[h, dh:, :] = g.astype(BF16)

    half = tk // 2

    def produce(h, j, diagonal):
        ks = pl.multiple_of(j * tk, tk)

        def keys(start, n):
            return jnp.concatenate([k_ref[0, pl.ds(start, n), cols(h)],
                                    ke_ref[h, pl.ds(start, n), :]], axis=1)

        if not diagonal:
            s = jnp.dot(keys(ks, tk), qa_ref[h], preferred_element_type=F32)
            s_ref[h] = s
            return jnp.max(s, axis=0, keepdims=True)
        top = jnp.dot(keys(ks, half), qa_ref[h], preferred_element_type=F32)
        bot = jnp.dot(keys(ks + half, half), qa_ref[h, :, half:],
                      preferred_element_type=F32)
        top = jnp.where(lax.broadcasted_iota(jnp.int32, top.shape, 0)
                        <= lax.broadcasted_iota(jnp.int32, top.shape, 1), top, NEG)
        bot = jnp.where(lax.broadcasted_iota(jnp.int32, bot.shape, 0)
                        <= lax.broadcasted_iota(jnp.int32, bot.shape, 1), bot, NEG)
        s_ref[h, :half, :] = top
        s_ref[h, half:, half:] = bot
        bot_wide = jnp.concatenate([jnp.full((half, half), NEG, F32), bot], axis=1)
        return jnp.max(jnp.maximum(top, bot_wide), axis=0, keepdims=True)

    def consume(h, j, stats, diagonal=False):
        m, mb = stats
        ks = pl.multiple_of(j * tk, tk)
        m2 = jnp.maximum(m, mb)
        if not diagonal:
            p = jnp.exp2(s_ref[h] - m2).astype(BF16)
            pv = jnp.dot(vt_ref[h, :, pl.ds(ks, tk)], p, preferred_element_type=F32)
            acc_ref[h] = jnp.exp2(m - m2) * acc_ref[h] + pv
            return m2
        p_top = jnp.exp2(s_ref[h, :half, :] - m2).astype(BF16)
        mb_ref[h] = m2
        p_bot = jnp.exp2(s_ref[h, half:, half:] - mb_ref[h, :, half:]).astype(BF16)
        acc = jnp.exp2(m - m2) * acc_ref[h] + jnp.dot(vt_ref[h, :, pl.ds(ks, half)], p_top,
                                                      preferred_element_type=F32)
        acc_ref[h, :, :half] = acc[:, :half]
        acc_ref[h, :, half:] = acc[:, half:] + jnp.dot(vt_ref[h, :, pl.ds(ks + half, half)], p_bot,
                                                       preferred_element_type=F32)
        return m2

    def step(j, carry, diagonal_next):
        return tuple((consume(h, j, stats), produce(h, j + 1, diagonal_next))
                     for h, stats in enumerate(carry))

    @pl.when(qi == 0)
    def _():
        r = lax.broadcasted_iota(jnp.int32, (dh, dh), 0)
        c = lax.broadcasted_iota(jnp.int32, (dh, dh), 1)
        eye = (r == c).astype(BF16)
        row = lax.broadcasted_iota(jnp.int32, (dh, tk), 0)
        ones_row = (lax.broadcasted_iota(jnp.int32, (FOX_PAD, tk), 0) == 0).astype(BF16)

        def blk(j, carry):
            ks = pl.multiple_of(j * tk, tk)
            for h in range(nh):
                hi, mid, lo = _split3(c_ref[0, h, :, pl.ds(ks, tk)] * (-LOG2E))
                a = jnp.where(row == 0, hi, jnp.where(row == 1, mid, jnp.where(row == 2, lo,
                              jnp.where(row < 6, 1.0, 0.0))))
                ke_ref[h, pl.ds(ks, tk), :] = a.T.astype(BF16)
                vt = lax.dot_general(eye, v_ref[0, pl.ds(ks, tk), cols(h)], nt,
                                     preferred_element_type=F32)
                vt_ref[h, :dh, pl.ds(ks, tk)] = vt.astype(BF16)
                vt_ref[h, dh:, pl.ds(ks, tk)] = ones_row
            return carry

        lax.fori_loop(0, nblk, blk, 0)
        prep_queries(q0_ref, 0)
        for h in range(nh):
            mb_ref[h] = produce(h, 0, True)

    carry = []
    for h in range(nh):
        acc_ref[h] = jnp.zeros(acc_ref.shape[1:], F32)
        carry.append((jnp.full((1, tq), NEG, F32), mb_ref[h]))
    plain = functools.partial(step, diagonal_next=False)
    n_plain = jnp.maximum(qi - 1, 0)
    n_pairs = n_plain // 2
    carry = lax.fori_loop(0, n_pairs, lambda i, c: plain(2 * i + 1, plain(2 * i, c)),
                          tuple(carry))
    carry = lax.fori_loop(2 * n_pairs, n_plain, plain, carry)

    def finish_block(carry):
        for h, stats in enumerate(carry):
            consume(h, qi, stats, diagonal=True)
        prep_queries(qn_ref, jnp.minimum(qi + 1, nq - 1))
        for h in range(nh):
            mb_ref[h] = produce(h, 0, False)
        for h in range(nh):
            acc = acc_ref[h]
            o = (acc[:dh] / acc[dh:dh + 1]).T
            o_ref[0, :, cols(h)] = (o * _silu(z_ref[0, :, cols(h)].astype(F32))
                                    ).astype(o_ref.dtype)

    @pl.when(qi > 0)
    def _():
        finish_block(step(qi - 1, carry, True))

    @pl.when(qi == 0)
    def _():
        finish_block(carry)


def _fox(proj3, cum4):
    b, l, _ = proj3.shape
    ng = C_HEADS // FOX_HEADS
    w = FOX_HEADS * C_DH
    tq = min(FOX_TQ, l)
    nq = l // tq
    return pl.pallas_call(
        _fox_kernel,
        name="fox",
        grid=(b, ng, nq),
        in_specs=[pl.BlockSpec((1, tq, w), lambda bi, gi, qi: (bi, 0, gi)),
                  pl.BlockSpec((1, tq, w), lambda bi, gi, qi: (bi, jnp.minimum(qi + 1, nq - 1), gi)),
                  pl.BlockSpec((1, l, w), lambda bi, gi, qi: (bi, 0, ng + gi)),
                  pl.BlockSpec((1, l, w), lambda bi, gi, qi: (bi, 0, 2 * ng + gi)),
                  pl.BlockSpec((1, tq, w), lambda bi, gi, qi: (bi, qi, 3 * ng + gi)),
                  pl.BlockSpec((1, FOX_HEADS, 1, l), lambda bi, gi, qi: (bi, gi, 0, 0))],
        out_specs=pl.BlockSpec((1, tq, w), lambda bi, gi, qi: (bi, qi, gi)),
        out_shape=jax.ShapeDtypeStruct((b, l, C_HEADS * C_DH), BF16),
        scratch_shapes=[pltpu.VMEM((FOX_HEADS, l, C_DH), BF16),
                        pltpu.VMEM((FOX_HEADS, C_DH + FOX_PAD, l), BF16),
                        pltpu.VMEM((FOX_HEADS, 2 * C_DH, tq), BF16),
                        pltpu.VMEM((FOX_HEADS, tq, tq), F32),
                        pltpu.VMEM((FOX_HEADS, 1, tq), F32),
                        pltpu.VMEM((FOX_HEADS, C_DH + FOX_PAD, tq), F32)],
        compiler_params=_cparams(("arbitrary", "arbitrary", "arbitrary")),
    )(proj3, proj3, proj3, proj3, proj3, cum4)


def _odd_out_kernel(x_ref, o_ref, w_ref, g_ref, out_ref):
    x2 = x_ref[...] + jnp.dot(o_ref[...], w_ref[...], preferred_element_type=F32)
    out_ref[...] = _rms(x2, g_ref[...])


def _odd_out(x1, o, w_out, g):
    m, d = x1.shape
    k = o.shape[1]
    tm = min(TM_OUT, m)
    return pl.pallas_call(
        _odd_out_kernel,
        name="odd_out",
        grid=(m // tm,),
        in_specs=[pl.BlockSpec((tm, d), lambda i: (i, 0)),
                  pl.BlockSpec((tm, k), lambda i: (i, 0)),
                  pl.BlockSpec((k, d), lambda i: (0, 0)),
                  pl.BlockSpec((1, d), lambda i: (0, 0))],
        out_specs=pl.BlockSpec((tm, d), lambda i: (i, 0)),
        out_shape=jax.ShapeDtypeStruct((m, d), F32),
        compiler_params=_cparams(("arbitrary",)),
    )(x1, o, w_out, g.reshape(1, d))


def kernel(x, norm_even_g, w_in_even, rel_bias, s5_lambda_re, s5_lambda_im, s5_log_dt,
           s5_b_re, s5_b_im, s5_c_re, s5_c_im, s5_d, w_glu, b_glu, w_out_even,
           norm_odd_g, w_in_odd, b_forget, w_out_odd, final_norm_g):
    b, l, d = x.shape
    assert norm_even_g.shape[0] == 1 and norm_odd_g.shape[0] == 1, "one even + one odd layer"
    assert l % FOX_TQ == 0 and l % VT_BLK == 0 and l % A_QB == 0 and l >= A_KB
    m = b * l
    a_w = A_HEADS * A_DH
    s5_w = s5_d.shape[1] * S5_GROUP
    c_w = C_HEADS * C_DH

    x2 = x.reshape(m, d)
    assert s5_w == TN_PROJ and l % S5_TC == 0
    proj, u3 = _norm_proj(x2, norm_even_g[0], w_in_even[0].astype(BF16), 3 * a_w)
    proj3 = proj.reshape(b, l, proj.shape[1])
    o_a = _attn_a(proj3, rel_bias[0])
    y = _s5_mixer(u3, b, s5_lambda_re[0], s5_lambda_im[0], s5_log_dt[0], s5_b_re[0],
                  s5_b_im[0], s5_c_re[0], s5_c_im[0], s5_d[0])
    w_odd_t = w_in_odd[0].T
    wf_t = w_odd_t[4 * c_w:].astype(BF16)
    x1, hn1, f_t = _even_out(x2, o_a.reshape(m, a_w), y, proj,
                             w_glu[0].astype(BF16), b_glu[0].astype(F32),
                             w_out_even[0].astype(BF16), norm_odd_g[0], wf_t)
    proj1 = _proj_odd(hn1, w_odd_t, 4 * c_w)
    cum = _cum_forget(f_t, b_forget[0], l)
    cum4 = cum.reshape(C_HEADS, b, 1, l).transpose(1, 0, 2, 3)
    o_c = _fox(proj1.reshape(b, l, 4 * c_w), cum4)
    out = _odd_out(x1, o_c.reshape(m, c_w), w_out_odd[0].astype(BF16), final_norm_g)
    return out.reshape(b, l, d)
```

```python
import functools
import math

import jax
import jax.numpy as jnp
from jax import lax
from jax.experimental import pallas as pl
from jax.experimental.pallas import tpu as pltpu

F32 = jnp.float32
BF16 = jnp.bfloat16

EPS = 1e-6
NEG = -1e30
LOG2E = math.log2(math.e)

CHUNK = 64
N_LEFT = 8
A_HEADS = 16
A_DH = 64
MAX_REL = 128
S5_GROUP = 16
S5_STATE = 64
C_HEADS = 16
C_DH = 128

LANES = 128
VMEM_LIMIT = 56 * 1024 * 1024

TM_PROJ = 1024
TN_PROJ = 1024
PROJ_CHUNKS = 4
TM_OUT = 512
TM_PROJ_ODD = 2048
A_QB = 2 * CHUNK
A_KB = A_QB + N_LEFT * CHUNK
A_TW = A_KB + N_LEFT * CHUNK
A_PAD = 16
A_PAIRS = 2
VT_BLK = 512
S5_TC = 16
S5_GPB = LANES // S5_GROUP
FOX_TQ = 512
FOX_HEADS = 4
FOX_PAD = 16


def _cparams(sem):
    return pltpu.CompilerParams(dimension_semantics=sem, vmem_limit_bytes=VMEM_LIMIT)


def _rms(x, g):
    ms = jnp.mean(x * x, axis=-1, keepdims=True)
    return x * lax.rsqrt(ms + EPS) * g


def _silu(x):
    return x * jax.nn.sigmoid(x)


def _gelu_tanh(x):
    c = math.sqrt(2.0 / math.pi)
    return 0.5 * x * (1.0 + jnp.tanh(c * (x + 0.044715 * (x * x * x))))


def _norm_proj_kernel(x_ref, g_ref, w_ref, o_ref, u_ref, hn_ref, us_ref, *, u_tile):
    j = pl.program_id(1)
    tm = x_ref.shape[0]

    @pl.when(j == 0)
    def _():
        for c in range(PROJ_CHUNKS):
            rows = slice(c * tm // PROJ_CHUNKS, (c + 1) * tm // PROJ_CHUNKS)
            h = _rms(x_ref[rows, :], g_ref[...]).astype(BF16)
            hn_ref[rows, :] = h
            o_ref[rows, :] = jnp.dot(h, w_ref[...], preferred_element_type=F32).astype(o_ref.dtype)

    @pl.when(j != 0)
    def _():
        acc = jnp.dot(hn_ref[...], w_ref[...], preferred_element_type=F32)
        o_ref[...] = acc.astype(o_ref.dtype)

        @pl.when(j == u_tile)
        def _():
            nchunk = us_ref.shape[1] // S5_TC
            for blk in range(u_ref.shape[0]):
                us_ref[blk] = acc[:, blk * LANES:(blk + 1) * LANES]
                for t in range(S5_TC):
                    rows = us_ref[blk, pl.ds(t, nchunk, stride=S5_TC), :]
                    u_ref[blk, :, t * LANES:(t + 1) * LANES] = rows.astype(u_ref.dtype)


def _norm_proj(x2, g, w, u_start):
    m, d = x2.shape
    n = w.shape[1]
    tm, tn = min(TM_PROJ, m), TN_PROJ
    assert u_start % tn == 0 and u_start > 0 and tm % (16 * S5_TC) == 0
    nblk = tn // LANES
    return pl.pallas_call(
        functools.partial(_norm_proj_kernel, u_tile=u_start // tn),
        name="norm_proj",
        grid=(m // tm, n // tn),
        in_specs=[pl.BlockSpec((tm, d), lambda i, j: (i, 0)),
                  pl.BlockSpec((1, d), lambda i, j: (0, 0)),
                  pl.BlockSpec((d, tn), lambda i, j: (0, j))],
        out_specs=[pl.BlockSpec((tm, tn), lambda i, j: (i, j)),
                   pl.BlockSpec((nblk, tm // S5_TC, S5_TC * LANES), lambda i, j: (0, i, 0))],
        out_shape=[jax.ShapeDtypeStruct((m, n), BF16),
                   jax.ShapeDtypeStruct((nblk, m // S5_TC, S5_TC * LANES), BF16)],
        scratch_shapes=[pltpu.VMEM((tm, d), BF16), pltpu.VMEM((nblk, tm, LANES), F32)],
        compiler_params=_cparams(("arbitrary", "arbitrary")),
    )(x2, g.reshape(1, d), w)


def _proj_odd_kernel(h_ref, wt_ref, o_ref, wb_ref):
    @pl.when(pl.program_id(1) == 0)
    def _():
        wb_ref[...] = wt_ref[...].T.astype(BF16)

    o_ref[...] = jnp.dot(h_ref[...], wb_ref[...],
                         preferred_element_type=F32).astype(o_ref.dtype)


def _proj_odd(hn, wt, n):
    m, d = hn.shape
    tm, tn = min(TM_PROJ_ODD, m), TN_PROJ
    assert n % tn == 0 and m % tm == 0
    return pl.pallas_call(
        _proj_odd_kernel,
        name="proj_odd",
        grid=(n // tn, m // tm),
        in_specs=[pl.BlockSpec((tm, d), lambda j, i: (i, 0)),
                  pl.BlockSpec((tn, d), lambda j, i: (j, 0))],
        out_specs=pl.BlockSpec((tm, tn), lambda j, i: (i, j)),
        out_shape=jax.ShapeDtypeStruct((m, n), BF16),
        scratch_shapes=[pltpu.VMEM((d, tn), BF16)],
        compiler_params=_cparams(("arbitrary", "arbitrary")),
    )(hn, wt)


def _attn_a_kernel(q_ref, k_ref, v_ref, z_ref, e_ref, o_ref,
                   tt_ref, vt_ref, qt_ref, s_ref, pv_ref):
    l_all = k_ref.shape[1]
    w = q_ref.shape[2]
    pw = 2 * A_DH
    npair = w // pw
    nt = (((1,), (1,)), ((), ()))
    band = N_LEFT * CHUNK

    @pl.when(pl.program_id(1) == 0)
    def _():
        i = lax.broadcasted_iota(jnp.int32, (A_QB, A_TW), 0)
        m = lax.broadcasted_iota(jnp.int32, (A_QB, A_TW), 1)
        dchunk = m // CHUNK - i // CHUNK
        ok = (dchunk >= 0) & (dchunk <= N_LEFT)
        for h in range(2 * npair):
            rows = jnp.broadcast_to(e_ref[0, h:h + 1, :], (A_QB, A_QB + A_TW))
            toep = pltpu.roll(rows, 0, 1, stride=1, stride_axis=0)[:, A_QB:]
            t = jnp.where(ok, toep * LOG2E, NEG)
            tt_ref[:, h * A_QB:(h + 1) * A_QB] = t.T

    r = lax.broadcasted_iota(jnp.int32, (w, w), 0)
    c = lax.broadcasted_iota(jnp.int32, (w, w), 1)
    eye = (r == c).astype(BF16)
    ones_row = (lax.broadcasted_iota(jnp.int32, (A_PAD, VT_BLK), 0) == 0).astype(BF16)

    qscale = LOG2E / math.sqrt(A_DH)

    def vt_blk(j, carry):
        ks = pl.multiple_of(j * VT_BLK, VT_BLK)
        vt = lax.dot_general(eye, v_ref[0, pl.ds(ks, VT_BLK), :], nt, preferred_element_type=F32)
        for h in range(2 * npair):
            vt_ref[h, :A_DH, pl.ds(ks, VT_BLK)] = vt[h * A_DH:(h + 1) * A_DH].astype(BF16)
            vt_ref[h, A_DH:, pl.ds(ks, VT_BLK)] = ones_row
        qs = (q_ref[0, pl.ds(ks, VT_BLK), :].astype(F32) * qscale).astype(BF16)
        qt_ref[:, pl.ds(ks, VT_BLK)] = lax.dot_general(
            eye, qs, nt, preferred_element_type=F32).astype(BF16)
        return carry

    lax.fori_loop(0, l_all // VT_BLK, vt_blk, 0)

    head_row = lax.broadcasted_iota(jnp.int32, (pw, A_QB), 0)

    def window(qb):
        start = qb * A_QB - band
        start_c = jnp.maximum(start, 0)
        return pl.multiple_of(start_c, LANES), pl.multiple_of(start_c - start, LANES)

    def produce(qb, slot):
        qs = pl.multiple_of(qb * A_QB, A_QB)
        start_c, off = window(qb)
        out = []
        for p in range(npair):
            qt = qt_ref[p * pw:(p + 1) * pw, pl.ds(qs, A_QB)]
            zero = jnp.zeros_like(qt)
            qbd = jnp.concatenate([jnp.where(head_row < A_DH, qt, zero),
                                   jnp.where(head_row >= A_DH, qt, zero)], axis=1)
            kb = k_ref[0, pl.ds(start_c, A_KB), p * pw:(p + 1) * pw]
            s = jnp.dot(kb, qbd, preferred_element_type=F32)
            s = s + tt_ref[pl.ds(off, A_KB), 2 * p * A_QB:2 * (p + 1) * A_QB]
            s_ref[slot, p] = s
            out.append(jnp.max(s, axis=0, keepdims=True))
        return tuple(out)

    def consume(qb, slot, ms):
        start_c, _ = window(qb)
        for p in range(npair):
            pb = jnp.exp2(s_ref[slot, p] - ms[p]).astype(BF16)
            for h in range(2):
                vt = vt_ref[2 * p + h, :, pl.ds(start_c, A_KB)]
                pv_ref[slot, 2 * p + h] = jnp.dot(vt, pb[:, h * A_QB:(h + 1) * A_QB],
                                                  preferred_element_type=F32)

    def finish(qb, slot):
        qs = pl.multiple_of(qb * A_QB, A_QB)
        outs = []
        for h in range(2 * npair):
            oh = pv_ref[slot, h]
            outs.append(oh[:A_DH] * (1.0 / oh[A_DH:A_DH + 1]))
        o = jnp.concatenate(outs, axis=0).T
        gate = _silu(z_ref[0, pl.ds(qs, A_QB), :].astype(F32))
        o_ref[0, pl.ds(qs, A_QB), :] = (o * gate).astype(o_ref.dtype)

    def pair(i, m, lead=True):
        j = 2 * i
        if lead:
            finish(j - 1, 1)
        m1 = produce(j + 1, 1)
        consume(j, 0, m)
        m2 = produce(j + 2, 0)
        finish(j, 0)
        consume(j + 1, 1, m1)
        return m2

    nb = l_all // A_QB
    npairs = (nb - 1) // 2
    m = pair(0, produce(0, 0), lead=False)
    nquads = (npairs - 1) // 2
    m = lax.fori_loop(0, nquads, lambda t, mm: pair(2 * t + 2, pair(2 * t + 1, mm)), m)
    if (npairs - 1) % 2:
        m = pair(npairs - 1, m)
    last = 2 * npairs
    finish(last - 1, 1)
    if (nb - 1) % 2:
        m1 = produce(last + 1, 1)
        consume(last, 0, m)
        finish(last, 0)
        consume(last + 1, 1, m1)
        finish(last + 1, 1)
    else:
        consume(last, 0, m)
        finish(last, 0)


def _attn_a(proj3, rel_bias):
    b, l, _ = proj3.shape
    nh = 2 * A_PAIRS
    ng = A_HEADS // nh
    width = nh * A_DH
    ncol = A_HEADS * A_DH // width
    zcol = (3 * A_HEADS * A_DH + (A_HEADS * A_DH)) // width
    left = N_LEFT * CHUNK + A_QB - MAX_REL
    right = A_QB + A_TW - left - (2 * MAX_REL + 1)
    ext = jnp.pad(rel_bias.astype(F32)[:, ::-1], ((0, 0), (left, right)), mode="edge")
    ext = ext.reshape(ng, nh, A_QB + A_TW)
    seq = lambda h, bi: (bi, 0, h)
    return pl.pallas_call(
        _attn_a_kernel,
        name="attn_a",
        grid=(ng, b),
        in_specs=[pl.BlockSpec((1, l, width), seq),
                  pl.BlockSpec((1, l, width), lambda h, bi: (bi, 0, ncol + h)),
                  pl.BlockSpec((1, l, width), lambda h, bi: (bi, 0, 2 * ncol + h)),
                  pl.BlockSpec((1, l, width), lambda h, bi: (bi, 0, zcol + h)),
                  pl.BlockSpec((1, nh, A_QB + A_TW), lambda h, bi: (h, 0, 0))],
        out_specs=pl.BlockSpec((1, l, width), seq),
        out_shape=jax.ShapeDtypeStruct((b, l, A_HEADS * A_DH), BF16),
        scratch_shapes=[pltpu.VMEM((A_TW, nh * A_QB), F32),
                        pltpu.VMEM((nh, A_DH + A_PAD, l), BF16),
                        pltpu.VMEM((width, l), BF16),
                        pltpu.VMEM((2, A_PAIRS, A_KB, 2 * A_QB), F32),
                        pltpu.VMEM((2, nh, A_DH + A_PAD, A_QB), F32)],
        compiler_params=_cparams(("arbitrary", "arbitrary")),
    )(proj3, proj3, proj3, proj3, ext)


def _cmul(ar, ai, br, bi):
    return ar * br - ai * bi, ar * bi + ai * br


def _s5_discretise(lr, li, log_dt):
    dt = jnp.exp(log_dt)
    mag = jnp.exp(lr * dt)
    ang = li * dt
    a_re = mag * jnp.cos(ang)
    a_im = mag * jnp.sin(ang)
    den = lr * lr + li * li
    nr = a_re - 1.0
    coef_re = (nr * lr + a_im * li) / den
    coef_im = (a_im * lr - nr * li) / den
    return (a_re, a_im), (coef_re, coef_im)


def _s5_weights_kernel(lam_pg_ref, ldt_pg_ref, c_pg_ref, lam_gp_ref, ldt_gp_ref, b_gp_ref,
                       lam_row_ref, ldt_row_ref, b_tl_ref, wm_ref, wr_ref, wo_ref, a_ref):
    tc, ch, ns, gpb = S5_TC, S5_GROUP, S5_STATE, S5_GPB
    lw = gpb * ch
    hi = lax.Precision.HIGHEST

    a_pg, _ = _s5_discretise(lam_pg_ref[0], lam_pg_ref[1], ldt_pg_ref[...])
    pw = (jnp.ones_like(a_pg[0]), jnp.zeros_like(a_pg[0]))
    ca = []
    for n in range(tc + 1):
        ca.append(_cmul(c_pg_ref[0], c_pg_ref[1], *pw))
        pw = _cmul(*pw, *a_pg)

    _, coef_gp = _s5_discretise(lam_gp_ref[0], lam_gp_ref[1], ldt_gp_ref[...])
    bb_gp = _cmul(*coef_gp, b_gp_ref[0], b_gp_ref[1])
    ca_re = jnp.concatenate([ca[n][0] for n in range(tc)], axis=1)
    ca_im = jnp.concatenate([ca[n][1] for n in range(tc)], axis=1)
    z = (jnp.dot(bb_gp[0], ca_re, precision=hi, preferred_element_type=F32)
         - jnp.dot(bb_gp[1], ca_im, precision=hi, preferred_element_type=F32))
    row_g = lax.broadcasted_iota(jnp.int32, z.shape, 0) // ch
    col_g = (lax.broadcasted_iota(jnp.int32, z.shape, 1) % lw) // ch
    z = jnp.where(row_g == col_g, z, 0.0).astype(BF16)
    for t in range(tc):
        parts = [z[:, :(tc - t) * lw]]
        if t:
            parts.insert(0, jnp.zeros((lw, t * lw), BF16))
        wm_ref[0, t * lw:(t + 1) * lw, :] = jnp.concatenate(parts, axis=1)

    rg = lax.broadcasted_iota(jnp.int32, (gpb * ns, lw), 0) // ns
    cg = lax.broadcasted_iota(jnp.int32, (gpb * ns, lw), 1) // ch
    keep = rg == cg
    for t2 in range(tc):
        cr, ci = ca[t2 + 1]
        cr = jnp.where(keep, jnp.concatenate([cr] * gpb, axis=0), 0.0)
        ci = jnp.where(keep, jnp.concatenate([ci] * gpb, axis=0), 0.0)
        wo_ref[0, :gpb * ns, t2 * lw:(t2 + 1) * lw] = cr.astype(BF16)
        wo_ref[0, gpb * ns:, t2 * lw:(t2 + 1) * lw] = (-ci).astype(BF16)

    a_row, coef_row = _s5_discretise(lam_row_ref[0], lam_row_ref[1], ldt_row_ref[...])
    bb_tl = _cmul(*coef_row, b_tl_ref[0], b_tl_ref[1])
    rg = lax.broadcasted_iota(jnp.int32, bb_tl[0].shape, 0) // ch
    cg = lax.broadcasted_iota(jnp.int32, bb_tl[0].shape, 1) // ns
    bb_tl = (jnp.where(rg == cg, bb_tl[0], 0.0), jnp.where(rg == cg, bb_tl[1], 0.0))
    pw = (jnp.ones_like(a_row[0]), jnp.zeros_like(a_row[0]))
    for t in range(tc - 1, -1, -1):
        r_re, r_im = _cmul(*pw, *bb_tl)
        wr_ref[0, t * lw:(t + 1) * lw, :] = jnp.concatenate([r_re, r_im], axis=1).astype(BF16)
        pw = _cmul(*pw, *a_row)
    a_ref[0, 0] = pw[0]
    a_ref[0, 1] = pw[1]


def _s5_param_views(lam_re, lam_im, log_dt, b_re, b_im, c_re, c_im):
    g, ns = lam_re.shape
    ch, gpb = S5_GROUP, S5_GPB
    lw = gpb * ch
    lam = jnp.stack([lam_re, lam_im]).astype(F32)
    ldt = log_dt.astype(F32)
    bst = jnp.stack([b_re, b_im]).astype(F32)
    cst = jnp.stack([c_re, c_im]).astype(F32)
    lam_pg = jnp.repeat(lam.transpose(0, 2, 1), ch, axis=2)
    ldt_pg = jnp.repeat(ldt, ch)[None]
    c_pg = cst.transpose(0, 3, 1, 2).reshape(2, ns, g * ch)
    lam_gp = jnp.repeat(lam, ch, axis=1)
    ldt_gp = jnp.broadcast_to(jnp.repeat(ldt, ch)[:, None], (g * ch, ns))
    b_gp = bst.transpose(0, 1, 3, 2).reshape(2, g * ch, ns)
    lam_row = lam.reshape(2, 1, g * ns)
    ldt_row = jnp.repeat(ldt, ns)[None]
    b_tl = jnp.tile(b_gp, (1, 1, gpb))
    last = lambda i, b: (0, 0, i)
    mid = lambda i, b: (0, i, 0)
    specs = [pl.BlockSpec((2, ns, lw), last),
             pl.BlockSpec((1, lw), lambda i, b: (0, i)),
             pl.BlockSpec((2, ns, lw), last),
             pl.BlockSpec((2, lw, ns), mid),
             pl.BlockSpec((lw, ns), lambda i, b: (i, 0)),
             pl.BlockSpec((2, lw, ns), mid),
             pl.BlockSpec((2, 1, gpb * ns), last),
             pl.BlockSpec((1, gpb * ns), lambda i, b: (0, i)),
             pl.BlockSpec((2, lw, gpb * ns), mid)]
    return [lam_pg, ldt_pg, c_pg, lam_gp, ldt_gp, b_gp, lam_row, ldt_row, b_tl], specs


def _s5_mix_kernel(u_ref, *refs):
    params, (d_ref, y_ref, wm_ref, wr_ref, wo_ref, a_ref, s_ref, x_ref, ys_ref) = refs[:9], refs[9:]

    @pl.when(pl.program_id(1) == 0)
    def _():
        _s5_weights_kernel(*params, wm_ref, wr_ref, wo_ref, a_ref)

    nk = u_ref.shape[1]
    half = s_ref.shape[1] // 2
    sub = 8
    u = u_ref[0]
    s_ref[...] = jnp.dot(u, wr_ref[0], preferred_element_type=F32)

    a1 = (a_ref[0, 0], a_ref[0, 1])
    a2 = _cmul(*a1, *a1)
    a4 = _cmul(*a2, *a2)
    a8 = _cmul(*a4, *a4)
    a3 = _cmul(*a2, *a1)
    pows = [(jnp.ones_like(a1[0]), jnp.zeros_like(a1[0])), a1, a2, a3, a4,
            _cmul(*a4, *a1), _cmul(*a4, *a2), _cmul(*a4, *a3)]
    p_re = jnp.concatenate([p[0] for p in pows], axis=0)
    p_im = jnp.concatenate([p[1] for p in pows], axis=0)
    rowid = lax.broadcasted_iota(jnp.int32, (sub, half), 0)
    levels = [(d, jnp.broadcast_to(a[0], (sub, half)), jnp.broadcast_to(a[1], (sub, half)))
              for d, a in ((1, a1), (2, a2), (4, a4))]

    def shift_rows(v, d):
        return jnp.where(rowid >= d, pltpu.roll(v, d, 0), 0.0)

    def tile(j, carry):
        cr, ci = carry
        rows = slice(j * sub, (j + 1) * sub)
        zr = s_ref[rows, :half]
        zi = s_ref[rows, half:]
        for d, br, bi in levels:
            sr, si = shift_rows(zr, d), shift_rows(zi, d)
            zr, zi = zr + br * sr - bi * si, zi + br * si + bi * sr
        x_ref[rows, :half] = p_re * cr - p_im * ci + shift_rows(zr, 1)
        x_ref[rows, half:] = p_re * ci + p_im * cr + shift_rows(zi, 1)
        return (a8[0] * cr - a8[1] * ci + zr[sub - 1:], a8[0] * ci + a8[1] * cr + zi[sub - 1:])

    carry = (jnp.zeros((1, half), F32), jnp.zeros((1, half), F32))
    for j in range(nk // sub):
        carry = tile(j, carry)
    xb = x_ref[...].astype(BF16)

    cw = 2 * LANES
    for n in range(u.shape[1] // cw):
        cs = slice(n * cw, (n + 1) * cw)
        y = jnp.dot(u[:, :(n + 1) * cw], wm_ref[0, :(n + 1) * cw, cs], preferred_element_type=F32)
        y = y + jnp.dot(xb, wo_ref[0, :, cs], preferred_element_type=F32)
        y = _gelu_tanh(y + d_ref[0, :, cs] * u[:, cs].astype(F32))
        for t in range(2 * n, 2 * n + 2):
            ys_ref[pl.ds(t, nk, stride=S5_TC), :] = y[:, (t - 2 * n) * LANES:(t - 2 * n + 1) * LANES]
    y_ref[0] = ys_ref[...].astype(y_ref.dtype)


def _s5_mixer(u3, nb, lam_re, lam_im, log_dt, b_re, b_im, c_re, c_im, d_skip):
    nblk, rows, w = u3.shape
    tc = S5_TC
    nk = rows // nb
    assert nk % 8 == 0
    views, view_specs = _s5_param_views(lam_re, lam_im, log_dt, b_re, b_im, c_re, c_im)
    ns = 2 * S5_GPB * S5_STATE
    d_rows = jnp.tile(d_skip.astype(F32).reshape(nblk, 1, LANES), (1, 1, tc))
    return pl.pallas_call(
        _s5_mix_kernel,
        name="s5_mix",
        grid=(nblk, nb),
        in_specs=[pl.BlockSpec((1, nk, w), lambda i, b: (i, b, 0))] + view_specs
                 + [pl.BlockSpec((1, 1, w), lambda i, b: (i, 0, 0))],
        out_specs=pl.BlockSpec((1, nk * tc, LANES), lambda i, b: (i, b, 0)),
        out_shape=jax.ShapeDtypeStruct((nblk, rows * tc, LANES), BF16),
        scratch_shapes=[pltpu.VMEM((1, w, w), BF16),
                        pltpu.VMEM((1, w, ns), BF16),
                        pltpu.VMEM((1, ns, w), BF16),
                        pltpu.VMEM((1, 2, 1, ns // 2), F32),
                        pltpu.VMEM((nk, ns), F32), pltpu.VMEM((nk, ns), F32),
                        pltpu.VMEM((nk * tc, LANES), F32)],
        compiler_params=_cparams(("arbitrary", "arbitrary")),
    )(u3, *views, d_rows)


def _even_out_kernel(x_ref, oa_ref, y_ref, zb_ref, wg_ref, bg_ref, wo_ref, g_ref, wf_ref,
                     x1_ref, hn_ref, f_ref):
    y = jnp.concatenate([y_ref[i] for i in range(y_ref.shape[0])], axis=1)
    t = jnp.dot(y, wg_ref[...], preferred_element_type=F32) + bg_ref[...]
    ob = y.astype(F32) * jax.nn.sigmoid(t) * _silu(zb_ref[...].astype(F32))
    wa = oa_ref.shape[1]
    acc = jnp.dot(oa_ref[...], wo_ref[:wa, :], preferred_element_type=F32)
    acc = acc + jnp.dot(ob.astype(BF16), wo_ref[wa:, :], preferred_element_type=F32)
    x1 = x_ref[...] + acc
    x1_ref[...] = x1
    hn = _rms(x1, g_ref[...]).astype(BF16)
    hn_ref[...] = hn
    f_ref[...] = lax.dot_general(wf_ref[...], hn, (((1,), (1,)), ((), ())),
                                 preferred_element_type=F32)


def _even_out(x2, oa, y, proj, w_glu, b_glu, w_out, g_next, wf_t):
    m, d = x2.shape
    nh = wf_t.shape[0]
    wa = oa.shape[1]
    nblk, _, lw = y.shape
    wb = nblk * lw
    tm = min(TM_OUT, m)
    zb_col = (proj.shape[1] - wb) // wb
    const = lambda i: (0, 0)
    return pl.pallas_call(
        _even_out_kernel,
        name="even_out",
        grid=(m // tm,),
        in_specs=[pl.BlockSpec((tm, d), lambda i: (i, 0)),
                  pl.BlockSpec((tm, wa), lambda i: (i, 0)),
                  pl.BlockSpec((nblk, tm, lw), lambda i: (0, i, 0)),
                  pl.BlockSpec((tm, wb), lambda i: (i, zb_col)),
                  pl.BlockSpec((wb, wb), const),
                  pl.BlockSpec((1, wb), const),
                  pl.BlockSpec((wa + wb, d), const),
                  pl.BlockSpec((1, d), const),
                  pl.BlockSpec((nh, d), const)],
        out_specs=[pl.BlockSpec((tm, d), lambda i: (i, 0)),
                   pl.BlockSpec((tm, d), lambda i: (i, 0)),
                   pl.BlockSpec((nh, tm), lambda i: (0, i))],
        out_shape=[jax.ShapeDtypeStruct((m, d), F32),
                   jax.ShapeDtypeStruct((m, d), BF16),
                   jax.ShapeDtypeStruct((nh, m), F32)],
        compiler_params=_cparams(("arbitrary",)),
    )(x2, oa, y, proj, w_glu, b_glu.reshape(1, wb), w_out, g_next.reshape(1, d), wf_t)


def _cum_forget_kernel(f_ref, b_ref, c_ref):
    x = f_ref[...] + b_ref[...]
    v = -(jnp.maximum(-x, 0.0) + jnp.log1p(jnp.exp(-jnp.abs(x))))
    n = v.shape[-1]
    pos = lax.broadcasted_iota(jnp.int32, v.shape, 1)
    d = 1
    while d < n:
        v = v + jnp.where(pos >= d, pltpu.roll(v, d, 1), 0.0)
        d *= 2
    c_ref[...] = v


def _cum_forget(f_t, b_forget, seq):
    nh, m = f_t.shape
    return pl.pallas_call(
        _cum_forget_kernel,
        name="cum_forget",
        grid=(m // seq,),
        in_specs=[pl.BlockSpec((nh, seq), lambda i: (0, i)),
                  pl.BlockSpec((nh, 1), lambda i: (0, 0))],
        out_specs=pl.BlockSpec((nh, seq), lambda i: (0, i)),
        out_shape=jax.ShapeDtypeStruct((nh, m), F32),
        compiler_params=_cparams(("arbitrary",)),
    )(f_t, b_forget.astype(F32).reshape(nh, 1))


def _split3(x):
    hi = x.astype(BF16).astype(F32)
    r = x - hi
    mid = r.astype(BF16).astype(F32)
    lo = (r - mid).astype(BF16).astype(F32)
    return hi, mid, lo


def _fox_kernel(q0_ref, qn_ref, k_ref, v_ref, z_ref, c_ref, o_ref,
                ke_ref, vt_ref, qa_ref, s_ref, mb_ref, acc_ref):
    qi = pl.program_id(2)
    nq = pl.num_programs(2)
    tq = qn_ref.shape[1]
    tk = tq
    dh = C_DH
    nh = c_ref.shape[1]
    nblk = k_ref.shape[1] // tk
    nt = (((1,), (1,)), ((), ()))

    def cols(h):
        return slice(h * dh, (h + 1) * dh)

    def prep_queries(q_ref, qb):
        qs = pl.multiple_of(qb * tq, tq)
        row = lax.broadcasted_iota(jnp.int32, (dh, tq), 0)
        for h in range(nh):
            c_base = jnp.max(c_ref[0, h, :, pl.ds(qs, tq)], axis=1, keepdims=True) * LOG2E
            bh, bm, bl = _split3(c_base)
            g = jnp.where(row < 3, 1.0, jnp.where(row == 3, bh, jnp.where(row == 4, bm,
                          jnp.where(row == 5, bl, 0.0))))
            qt = (q_ref[0, :, cols(h)].astype(F32) * (LOG2E / math.sqrt(dh))).T
            qa_ref[h, :dh, :] = qt.astype(BF16)
            qa_ref[h, dh:, :] = g.astype(BF16)

    half = tk // 2

    def produce(h, j, diagonal):
        ks = pl.multiple_of(j * tk, tk)

        def keys(start, n):
            return jnp.concatenate([k_ref[0, pl.ds(start, n), cols(h)],
                                    ke_ref[h, pl.ds(start, n), :]], axis=1)

        if not diagonal:
            s = jnp.dot(keys(ks, tk), qa_ref[h], preferred_element_type=F32)
            s_ref[h] = s
            return jnp.max(s, axis=0, keepdims=True)
        top = jnp.dot(keys(ks, half), qa_ref[h], preferred_element_type=F32)
        bot = jnp.dot(keys(ks + half, half), qa_ref[h, :, half:],
                      preferred_element_type=F32)
        top = jnp.where(lax.broadcasted_iota(jnp.int32, top.shape, 0)
                        <= lax.broadcasted_iota(jnp.int32, top.shape, 1), top, NEG)
        bot = jnp.where(lax.broadcasted_iota(jnp.int32, bot.shape, 0)
                        <= lax.broadcasted_iota(jnp.int32, bot.shape, 1), bot, NEG)
        s_ref[h, :half, :] = top
        s_ref[h, half:, half:] = bot
        bot_wide = jnp.concatenate([jnp.full((half, half), NEG, F32), bot], axis=1)
        return jnp.max(jnp.maximum(top, bot_wide), axis=0, keepdims=True)

    def consume(h, j, stats, diagonal=False):
        m, mb = stats
        ks = pl.multiple_of(j * tk, tk)
        m2 = jnp.maximum(m, mb)
        if not diagonal:
            p = jnp.exp2(s_ref[h] - m2).astype(BF16)
            pv = jnp.dot(vt_ref[h, :, pl.ds(ks, tk)], p, preferred_element_type=F32)
            acc_ref[h] = jnp.exp2(m - m2) * acc_ref[h] + pv
            return m2
        p_top = jnp.exp2(s_ref[h, :half, :] - m2).astype(BF16)
        mb_ref[h] = m2
        p_bot = jnp.exp2(s_ref[h, half:, half:] - mb_ref[h, :, half:]).astype(BF16)
        acc = jnp.exp2(m - m2) * acc_ref[h] + jnp.dot(vt_ref[h, :, pl.ds(ks, half)], p_top,
                                                      preferred_element_type=F32)
        acc_ref[h, :, :half] = acc[:, :half]
        acc_ref[h, :, half:] = acc[:, half:] + jnp.dot(vt_ref[h, :, pl.ds(ks + half, half)], p_bot,
                                                       preferred_element_type=F32)
        return m2

    def step(j, carry, diagonal_next):
        return tuple((consume(h, j, stats), produce(h, j + 1, diagonal_next))
                     for h, stats in enumerate(carry))

    @pl.when(qi == 0)
    def _():
        r = lax.broadcasted_iota(jnp.int32, (dh, dh), 0)
        c = lax.broadcasted_iota(jnp.int32, (dh, dh), 1)
        eye = (r == c).astype(BF16)
        row = lax.broadcasted_iota(jnp.int32, (dh, tk), 0)
        ones_row = (lax.broadcasted_iota(jnp.int32, (FOX_PAD, tk), 0) == 0).astype(BF16)

        def blk(j, carry):
            ks = pl.multiple_of(j * tk, tk)
            for h in range(nh):
                hi, mid, lo = _split3(c_ref[0, h, :, pl.ds(ks, tk)] * (-LOG2E))
                a = jnp.where(row == 0, hi, jnp.where(row == 1, mid, jnp.where(row == 2, lo,
                              jnp.where(row < 6, 1.0, 0.0))))
                ke_ref[h, pl.ds(ks, tk), :] = a.T.astype(BF16)
                vt = lax.dot_general(eye, v_ref[0, pl.ds(ks, tk), cols(h)], nt,
                                     preferred_element_type=F32)
                vt_ref[h, :dh, pl.ds(ks, tk)] = vt.astype(BF16)
                vt_ref[h, dh:, pl.ds(ks, tk)] = ones_row
            return carry

        lax.fori_loop(0, nblk, blk, 0)
        prep_queries(q0_ref, 0)
        for h in range(nh):
            mb_ref[h] = produce(h, 0, True)

    carry = []
    for h in range(nh):
        acc_ref[h] = jnp.zeros(acc_ref.shape[1:], F32)
        carry.append((jnp.full((1, tq), NEG, F32), mb_ref[h]))
    plain = functools.partial(step, diagonal_next=False)
    n_plain = jnp.maximum(qi - 1, 0)
    n_pairs = n_plain // 2
    carry = lax.fori_loop(0, n_pairs, lambda i, c: plain(2 * i + 1, plain(2 * i, c)),
                          tuple(carry))
    carry = lax.fori_loop(2 * n_pairs, n_plain, plain, carry)

    def finish_block(carry):
        for h, stats in enumerate(carry):
            consume(h, qi, stats, diagonal=True)
        prep_queries(qn_ref, jnp.minimum(qi + 1, nq - 1))
        for h in range(nh):
            mb_ref[h] = produce(h, 0, False)
        for h in range(nh):
            acc = acc_ref[h]
            o = (acc[:dh] / acc[dh:dh + 1]).T
            o_ref[0, :, cols(h)] = (o * _silu(z_ref[0, :, cols(h)].astype(F32))
                                    ).astype(o_ref.dtype)

    @pl.when(qi > 0)
    def _():
        finish_block(step(qi - 1, carry, True))

    @pl.when(qi == 0)
    def _():
        finish_block(carry)


def _fox(proj3, cum4):
    b, l, _ = proj3.shape
    ng = C_HEADS // FOX_HEADS
    w = FOX_HEADS * C_DH
    tq = min(FOX_TQ, l)
    nq = l // tq
    return pl.pallas_call(
        _fox_kernel,
        name="fox",
        grid=(b, ng, nq),
        in_specs=[pl.BlockSpec((1, tq, w), lambda bi, gi, qi: (bi, 0, gi)),
                  pl.BlockSpec((1, tq, w), lambda bi, gi, qi: (bi, jnp.minimum(qi + 1, nq - 1), gi)),
                  pl.BlockSpec((1, l, w), lambda bi, gi, qi: (bi, 0, ng + gi)),
                  pl.BlockSpec((1, l, w), lambda bi, gi, qi: (bi, 0, 2 * ng + gi)),
                  pl.BlockSpec((1, tq, w), lambda bi, gi, qi: (bi, qi, 3 * ng + gi)),
                  pl.BlockSpec((1, FOX_HEADS, 1, l), lambda bi, gi, qi: (bi, gi, 0, 0))],
        out_specs=pl.BlockSpec((1, tq, w), lambda bi, gi, qi: (bi, qi, gi)),
        out_shape=jax.ShapeDtypeStruct((b, l, C_HEADS * C_DH), BF16),
        scratch_shapes=[pltpu.VMEM((FOX_HEADS, l, C_DH), BF16),
                        pltpu.VMEM((FOX_HEADS, C_DH + FOX_PAD, l), BF16),
                        pltpu.VMEM((FOX_HEADS, 2 * C_DH, tq), BF16),
                        pltpu.VMEM((FOX_HEADS, tq, tq), F32),
                        pltpu.VMEM((FOX_HEADS, 1, tq), F32),
                        pltpu.VMEM((FOX_HEADS, C_DH + FOX_PAD, tq), F32)],
        compiler_params=_cparams(("arbitrary", "arbitrary", "arbitrary")),
    )(proj3, proj3, proj3, proj3, proj3, cum4)


def _odd_out_kernel(x_ref, o_ref, w_ref, g_ref, out_ref):
    x2 = x_ref[...] + jnp.dot(o_ref[...], w_ref[...], preferred_element_type=F32)
    out_ref[...] = _rms(x2, g_ref[...])


def _odd_out(x1, o, w_out, g):
    m, d = x1.shape
    k = o.shape[1]
    tm = min(TM_OUT, m)
    return pl.pallas_call(
        _odd_out_kernel,
        name="odd_out",
        grid=(m // tm,),
        in_specs=[pl.BlockSpec((tm, d), lambda i: (i, 0)),
                  pl.BlockSpec((tm, k), lambda i: (i, 0)),
                  pl.BlockSpec((k, d), lambda i: (0, 0)),
                  pl.BlockSpec((1, d), lambda i: (0, 0))],
        out_specs=pl.BlockSpec((tm, d), lambda i: (i, 0)),
        out_shape=jax.ShapeDtypeStruct((m, d), F32),
        compiler_params=_cparams(("arbitrary",)),
    )(x1, o, w_out, g.reshape(1, d))


def kernel(x, norm_even_g, w_in_even, rel_bias, s5_lambda_re, s5_lambda_im, s5_log_dt,
           s5_b_re, s5_b_im, s5_c_re, s5_c_im, s5_d, w_glu, b_glu, w_out_even,
           norm_odd_g, w_in_odd, b_forget, w_out_odd, final_norm_g):
    b, l, d = x.shape
    assert norm_even_g.shape[0] == 1 and norm_odd_g.shape[0] == 1, "one even + one odd layer"
    assert l % FOX_TQ == 0 and l % VT_BLK == 0 and l % A_QB == 0 and l >= A_KB
    m = b * l
    a_w = A_HEADS * A_DH
    s5_w = s5_d.shape[1] * S5_GROUP
    c_w = C_HEADS * C_DH

    x2 = x.reshape(m, d)
    assert s5_w == TN_PROJ and l % S5_TC == 0
    proj, u3 = _norm_proj(x2, norm_even_g[0], w_in_even[0].astype(BF16), 3 * a_w)
    proj3 = proj.reshape(b, l, proj.shape[1])
    o_a = _attn_a(proj3, rel_bias[0])
    y = _s5_mixer(u3, b, s5_lambda_re[0], s5_lambda_im[0], s5_log_dt[0], s5_b_re[0],
                  s5_b_im[0], s5_c_re[0], s5_c_im[0], s5_d[0])
    w_odd_t = w_in_odd[0].T
    wf_t = w_odd_t[4 * c_w:].astype(BF16)
    x1, hn1, f_t = _even_out(x2, o_a.reshape(m, a_w), y, proj,
                             w_glu[0].astype(BF16), b_glu[0].astype(F32),
                             w_out_even[0].astype(BF16), norm_odd_g[0], wf_t)
    proj1 = _proj_odd(hn1, w_odd_t, 4 * c_w)
    cum = _cum_forget(f_t, b_forget[0], l)
    cum4 = cum.reshape(C_HEADS, b, 1, l).transpose(1, 0, 2, 3)
    o_c = _fox(proj1.reshape(b, l, 4 * c_w), cum4)
    out = _odd_out(x1, o_c.reshape(m, c_w), w_out_odd[0].astype(BF16), final_norm_g)
    return out.reshape(b, l, d)
```

```python
import functools
import math

import jax
import jax.numpy as jnp
from jax import lax
from jax.experimental import pallas as pl
from jax.experimental.pallas import tpu as pltpu

F32 = jnp.float32
BF16 = jnp.bfloat16

EPS = 1e-6
NEG = -1e30
LOG2E = math.log2(math.e)

CHUNK = 64
N_LEFT = 8
A_HEADS = 16
A_DH = 64
MAX_REL = 128
S5_GROUP = 16
S5_STATE = 64
C_HEADS = 16
C_DH = 128

LANES = 128
VMEM_LIMIT = 56 * 1024 * 1024

TM_PROJ = 1024
TN_PROJ = 1024
PROJ_CHUNKS = 4
TM_OUT = 512
TM_PROJ_ODD = 2048
A_QB = 2 * CHUNK
A_KB = A_QB + N_LEFT * CHUNK
A_TW = A_KB + N_LEFT * CHUNK
A_PAD = 16
A_PAIRS = 2
A_PAIRS_PER_TRIP = 4
VT_BLK = 512
S5_TC = 16
S5_GPB = LANES // S5_GROUP
FOX_TQ = 512
FOX_HEADS = 4
FOX_PAD = 16


def _cparams(sem):
    return pltpu.CompilerParams(dimension_semantics=sem, vmem_limit_bytes=VMEM_LIMIT)


def _rms(x, g):
    ms = jnp.mean(x * x, axis=-1, keepdims=True)
    return x * lax.rsqrt(ms + EPS) * g


def _silu(x):
    return x * jax.nn.sigmoid(x)


def _gelu_tanh(x):
    c = math.sqrt(2.0 / math.pi)
    return 0.5 * x * (1.0 + jnp.tanh(c * (x + 0.044715 * (x * x * x))))


def _norm_proj_kernel(x_ref, g_ref, w_ref, o_ref, u_ref, hn_ref, us_ref, *, u_tile):
    j = pl.program_id(1)
    tm = x_ref.shape[0]

    @pl.when(j == 0)
    def _():
        for c in range(PROJ_CHUNKS):
            rows = slice(c * tm // PROJ_CHUNKS, (c + 1) * tm // PROJ_CHUNKS)
            h = _rms(x_ref[rows, :], g_ref[...]).astype(BF16)
            hn_ref[rows, :] = h
            o_ref[rows, :] = jnp.dot(h, w_ref[...], preferred_element_type=F32).astype(o_ref.dtype)

    @pl.when(j != 0)
    def _():
        acc = jnp.dot(hn_ref[...], w_ref[...], preferred_element_type=F32)
        o_ref[...] = acc.astype(o_ref.dtype)

        @pl.when(j == u_tile)
        def _():
            nchunk = us_ref.shape[1] // S5_TC
            for blk in range(u_ref.shape[0]):
                us_ref[blk] = acc[:, blk * LANES:(blk + 1) * LANES]
                for t in range(S5_TC):
                    rows = us_ref[blk, pl.ds(t, nchunk, stride=S5_TC), :]
                    u_ref[blk, :, t * LANES:(t + 1) * LANES] = rows.astype(u_ref.dtype)


def _norm_proj(x2, g, w, u_start):
    m, d = x2.shape
    n = w.shape[1]
    tm, tn = min(TM_PROJ, m), TN_PROJ
    assert u_start % tn == 0 and u_start > 0 and tm % (16 * S5_TC) == 0
    nblk = tn // LANES
    return pl.pallas_call(
        functools.partial(_norm_proj_kernel, u_tile=u_start // tn),
        name="norm_proj",
        grid=(m // tm, n // tn),
        in_specs=[pl.BlockSpec((tm, d), lambda i, j: (i, 0)),
                  pl.BlockSpec((1, d), lambda i, j: (0, 0)),
                  pl.BlockSpec((d, tn), lambda i, j: (0, j))],
        out_specs=[pl.BlockSpec((tm, tn), lambda i, j: (i, j)),
                   pl.BlockSpec((nblk, tm // S5_TC, S5_TC * LANES), lambda i, j: (0, i, 0))],
        out_shape=[jax.ShapeDtypeStruct((m, n), BF16),
                   jax.ShapeDtypeStruct((nblk, m // S5_TC, S5_TC * LANES), BF16)],
        scratch_shapes=[pltpu.VMEM((tm, d), BF16), pltpu.VMEM((nblk, tm, LANES), F32)],
        compiler_params=_cparams(("arbitrary", "arbitrary")),
    )(x2, g.reshape(1, d), w)


def _proj_odd_kernel(h_ref, wt_ref, o_ref, wb_ref):
    @pl.when(pl.program_id(1) == 0)
    def _():
        wb_ref[...] = wt_ref[...].T.astype(BF16)

    o_ref[...] = jnp.dot(h_ref[...], wb_ref[...],
                         preferred_element_type=F32).astype(o_ref.dtype)


def _proj_odd(hn, wt, n):
    m, d = hn.shape
    tm, tn = min(TM_PROJ_ODD, m), TN_PROJ
    assert n % tn == 0 and m % tm == 0
    return pl.pallas_call(
        _proj_odd_kernel,
        name="proj_odd",
        grid=(n // tn, m // tm),
        in_specs=[pl.BlockSpec((tm, d), lambda j, i: (i, 0)),
                  pl.BlockSpec((tn, d), lambda j, i: (j, 0))],
        out_specs=pl.BlockSpec((tm, tn), lambda j, i: (i, j)),
        out_shape=jax.ShapeDtypeStruct((m, n), BF16),
        scratch_shapes=[pltpu.VMEM((d, tn), BF16)],
        compiler_params=_cparams(("arbitrary", "arbitrary")),
    )(hn, wt)


def _attn_a_kernel(q_ref, k_ref, v_ref, z_ref, e_ref, o_ref,
                   tt_ref, vt_ref, qt_ref, s_ref, pv_ref):
    l_all = k_ref.shape[1]
    w = q_ref.shape[2]
    pw = 2 * A_DH
    npair = w // pw
    nt = (((1,), (1,)), ((), ()))
    band = N_LEFT * CHUNK

    @pl.when(pl.program_id(1) == 0)
    def _():
        i = lax.broadcasted_iota(jnp.int32, (A_QB, A_TW), 0)
        m = lax.broadcasted_iota(jnp.int32, (A_QB, A_TW), 1)
        dchunk = m // CHUNK - i // CHUNK
        ok = (dchunk >= 0) & (dchunk <= N_LEFT)
        for h in range(2 * npair):
            rows = jnp.broadcast_to(e_ref[0, h:h + 1, :], (A_QB, A_QB + A_TW))
            toep = pltpu.roll(rows, 0, 1, stride=1, stride_axis=0)[:, A_QB:]
            t = jnp.where(ok, toep * LOG2E, NEG)
            tt_ref[:, h * A_QB:(h + 1) * A_QB] = t.T

    r = lax.broadcasted_iota(jnp.int32, (w, w), 0)
    c = lax.broadcasted_iota(jnp.int32, (w, w), 1)
    eye = (r == c).astype(BF16)
    ones_row = (lax.broadcasted_iota(jnp.int32, (A_PAD, VT_BLK), 0) == 0).astype(BF16)

    qscale = LOG2E / math.sqrt(A_DH)

    def vt_blk(j, carry):
        ks = pl.multiple_of(j * VT_BLK, VT_BLK)
        vt = lax.dot_general(eye, v_ref[0, pl.ds(ks, VT_BLK), :], nt, preferred_element_type=F32)
        for h in range(2 * npair):
            vt_ref[h, :A_DH, pl.ds(ks, VT_BLK)] = vt[h * A_DH:(h + 1) * A_DH].astype(BF16)
            vt_ref[h, A_DH:, pl.ds(ks, VT_BLK)] = ones_row
        qs = (q_ref[0, pl.ds(ks, VT_BLK), :].astype(F32) * qscale).astype(BF16)
        qt_ref[:, pl.ds(ks, VT_BLK)] = lax.dot_general(
            eye, qs, nt, preferred_element_type=F32).astype(BF16)
        return carry

    lax.fori_loop(0, l_all // VT_BLK, vt_blk, 0)

    head_row = lax.broadcasted_iota(jnp.int32, (pw, A_QB), 0)

    def window(qb):
        start = qb * A_QB - band
        start_c = jnp.maximum(start, 0)
        return pl.multiple_of(start_c, LANES), pl.multiple_of(start_c - start, LANES)

    def produce(qb, slot):
        qs = pl.multiple_of(qb * A_QB, A_QB)
        start_c, off = window(qb)
        out = []
        for p in range(npair):
            qt = qt_ref[p * pw:(p + 1) * pw, pl.ds(qs, A_QB)]
            zero = jnp.zeros_like(qt)
            qbd = jnp.concatenate([jnp.where(head_row < A_DH, qt, zero),
                                   jnp.where(head_row >= A_DH, qt, zero)], axis=1)
            kb = k_ref[0, pl.ds(start_c, A_KB), p * pw:(p + 1) * pw]
            s = jnp.dot(kb, qbd, preferred_element_type=F32)
            s = s + tt_ref[pl.ds(off, A_KB), 2 * p * A_QB:2 * (p + 1) * A_QB]
            s_ref[slot, p] = s
            out.append(jnp.max(s, axis=0, keepdims=True))
        return tuple(out)

    def consume(qb, slot, ms):
        start_c, _ = window(qb)
        for p in range(npair):
            pb = jnp.exp2(s_ref[slot, p] - ms[p]).astype(BF16)
            for h in range(2):
                vt = vt_ref[2 * p + h, :, pl.ds(start_c, A_KB)]
                pv_ref[slot, 2 * p + h] = jnp.dot(vt, pb[:, h * A_QB:(h + 1) * A_QB],
                                                  preferred_element_type=F32)

    def finish(qb, slot):
        qs = pl.multiple_of(qb * A_QB, A_QB)
        outs = []
        for h in range(2 * npair):
            oh = pv_ref[slot, h]
            outs.append(oh[:A_DH] * (1.0 / oh[A_DH:A_DH + 1]))
        o = jnp.concatenate(outs, axis=0).T
        gate = _silu(z_ref[0, pl.ds(qs, A_QB), :].astype(F32))
        o_ref[0, pl.ds(qs, A_QB), :] = (o * gate).astype(o_ref.dtype)

    def pair(i, m, lead=True):
        j = 2 * i
        if lead:
            finish(j - 1, 1)
        m1 = produce(j + 1, 1)
        consume(j, 0, m)
        m2 = produce(j + 2, 0)
        finish(j, 0)
        consume(j + 1, 1, m1)
        return m2

    nb = l_all // A_QB
    npairs = (nb - 1) // 2
    m = pair(0, produce(0, 0), lead=False)
    def trip(t, mm):
        for r in range(A_PAIRS_PER_TRIP):
            mm = pair(A_PAIRS_PER_TRIP * t + 1 + r, mm)
        return mm

    ntrips = (npairs - 1) // A_PAIRS_PER_TRIP
    m = lax.fori_loop(0, ntrips, trip, m)
    for i in range(1 + A_PAIRS_PER_TRIP * ntrips, npairs):
        m = pair(i, m)
    last = 2 * npairs
    finish(last - 1, 1)
    if (nb - 1) % 2:
        m1 = produce(last + 1, 1)
        consume(last, 0, m)
        finish(last, 0)
        consume(last + 1, 1, m1)
        finish(last + 1, 1)
    else:
        consume(last, 0, m)
        finish(last, 0)


def _attn_a(proj3, rel_bias):
    b, l, _ = proj3.shape
    nh = 2 * A_PAIRS
    ng = A_HEADS // nh
    width = nh * A_DH
    ncol = A_HEADS * A_DH // width
    zcol = (3 * A_HEADS * A_DH + (A_HEADS * A_DH)) // width
    left = N_LEFT * CHUNK + A_QB - MAX_REL
    right = A_QB + A_TW - left - (2 * MAX_REL + 1)
    ext = jnp.pad(rel_bias.astype(F32)[:, ::-1], ((0, 0), (left, right)), mode="edge")
    ext = ext.reshape(ng, nh, A_QB + A_TW)
    seq = lambda h, bi: (bi, 0, h)
    return pl.pallas_call(
        _attn_a_kernel,
        name="attn_a",
        grid=(ng, b),
        in_specs=[pl.BlockSpec((1, l, width), seq),
                  pl.BlockSpec((1, l, width), lambda h, bi: (bi, 0, ncol + h)),
                  pl.BlockSpec((1, l, width), lambda h, bi: (bi, 0, 2 * ncol + h)),
                  pl.BlockSpec((1, l, width), lambda h, bi: (bi, 0, zcol + h)),
                  pl.BlockSpec((1, nh, A_QB + A_TW), lambda h, bi: (h, 0, 0))],
        out_specs=pl.BlockSpec((1, l, width), seq),
        out_shape=jax.ShapeDtypeStruct((b, l, A_HEADS * A_DH), BF16),
        scratch_shapes=[pltpu.VMEM((A_TW, nh * A_QB), F32),
                        pltpu.VMEM((nh, A_DH + A_PAD, l), BF16),
                        pltpu.VMEM((width, l), BF16),
                        pltpu.VMEM((2, A_PAIRS, A_KB, 2 * A_QB), F32),
                        pltpu.VMEM((2, nh, A_DH + A_PAD, A_QB), F32)],
        compiler_params=_cparams(("arbitrary", "arbitrary")),
    )(proj3, proj3, proj3, proj3, ext)


def _cmul(ar, ai, br, bi):
    return ar * br - ai * bi, ar * bi + ai * br


def _s5_discretise(lr, li, log_dt):
    dt = jnp.exp(log_dt)
    mag = jnp.exp(lr * dt)
    ang = li * dt
    a_re = mag * jnp.cos(ang)
    a_im = mag * jnp.sin(ang)
    den = lr * lr + li * li
    nr = a_re - 1.0
    coef_re = (nr * lr + a_im * li) / den
    coef_im = (a_im * lr - nr * li) / den
    return (a_re, a_im), (coef_re, coef_im)


def _s5_weights_kernel(lam_pg_ref, ldt_pg_ref, c_pg_ref, lam_gp_ref, ldt_gp_ref, b_gp_ref,
                       lam_row_ref, ldt_row_ref, b_tl_ref, wm_ref, wr_ref, wo_ref, a_ref):
    tc, ch, ns, gpb = S5_TC, S5_GROUP, S5_STATE, S5_GPB
    lw = gpb * ch
    hi = lax.Precision.HIGHEST

    a_pg, _ = _s5_discretise(lam_pg_ref[0], lam_pg_ref[1], ldt_pg_ref[...])
    pw = (jnp.ones_like(a_pg[0]), jnp.zeros_like(a_pg[0]))
    ca = []
    for n in range(tc + 1):
        ca.append(_cmul(c_pg_ref[0], c_pg_ref[1], *pw))
        pw = _cmul(*pw, *a_pg)

    _, coef_gp = _s5_discretise(lam_gp_ref[0], lam_gp_ref[1], ldt_gp_ref[...])
    bb_gp = _cmul(*coef_gp, b_gp_ref[0], b_gp_ref[1])
    ca_re = jnp.concatenate([ca[n][0] for n in range(tc)], axis=1)
    ca_im = jnp.concatenate([ca[n][1] for n in range(tc)], axis=1)
    z = (jnp.dot(bb_gp[0], ca_re, precision=hi, preferred_element_type=F32)
         - jnp.dot(bb_gp[1], ca_im, precision=hi, preferred_element_type=F32))
    row_g = lax.broadcasted_iota(jnp.int32, z.shape, 0) // ch
    col_g = (lax.broadcasted_iota(jnp.int32, z.shape, 1) % lw) // ch
    z = jnp.where(row_g == col_g, z, 0.0).astype(BF16)
    for t in range(tc):
        parts = [z[:, :(tc - t) * lw]]
        if t:
            parts.insert(0, jnp.zeros((lw, t * lw), BF16))
        wm_ref[0, t * lw:(t + 1) * lw, :] = jnp.concatenate(parts, axis=1)

    rg = lax.broadcasted_iota(jnp.int32, (gpb * ns, lw), 0) // ns
    cg = lax.broadcasted_iota(jnp.int32, (gpb * ns, lw), 1) // ch
    keep = rg == cg
    for t2 in range(tc):
        cr, ci = ca[t2 + 1]
        cr = jnp.where(keep, jnp.concatenate([cr] * gpb, axis=0), 0.0)
        ci = jnp.where(keep, jnp.concatenate([ci] * gpb, axis=0), 0.0)
        wo_ref[0, :gpb * ns, t2 * lw:(t2 + 1) * lw] = cr.astype(BF16)
        wo_ref[0, gpb * ns:, t2 * lw:(t2 + 1) * lw] = (-ci).astype(BF16)

    a_row, coef_row = _s5_discretise(lam_row_ref[0], lam_row_ref[1], ldt_row_ref[...])
    bb_tl = _cmul(*coef_row, b_tl_ref[0], b_tl_ref[1])
    rg = lax.broadcasted_iota(jnp.int32, bb_tl[0].shape, 0) // ch
    cg = lax.broadcasted_iota(jnp.int32, bb_tl[0].shape, 1) // ns
    bb_tl = (jnp.where(rg == cg, bb_tl[0], 0.0), jnp.where(rg == cg, bb_tl[1], 0.0))
    pw = (jnp.ones_like(a_row[0]), jnp.zeros_like(a_row[0]))
    for t in range(tc - 1, -1, -1):
        r_re, r_im = _cmul(*pw, *bb_tl)
        wr_ref[0, t * lw:(t + 1) * lw, :] = jnp.concatenate([r_re, r_im], axis=1).astype(BF16)
        pw = _cmul(*pw, *a_row)
    a_ref[0, 0] = pw[0]
    a_ref[0, 1] = pw[1]


def _s5_param_views(lam_re, lam_im, log_dt, b_re, b_im, c_re, c_im):
    g, ns = lam_re.shape
    ch, gpb = S5_GROUP, S5_GPB
    lw = gpb * ch
    lam = jnp.stack([lam_re, lam_im]).astype(F32)
    ldt = log_dt.astype(F32)
    bst = jnp.stack([b_re, b_im]).astype(F32)
    cst = jnp.stack([c_re, c_im]).astype(F32)
    lam_pg = jnp.repeat(lam.transpose(0, 2, 1), ch, axis=2)
    ldt_pg = jnp.repeat(ldt, ch)[None]
    c_pg = cst.transpose(0, 3, 1, 2).reshape(2, ns, g * ch)
    lam_gp = jnp.repeat(lam, ch, axis=1)
    ldt_gp = jnp.broadcast_to(jnp.repeat(ldt, ch)[:, None], (g * ch, ns))
    b_gp = bst.transpose(0, 1, 3, 2).reshape(2, g * ch, ns)
    lam_row = lam.reshape(2, 1, g * ns)
    ldt_row = jnp.repeat(ldt, ns)[None]
    b_tl = jnp.tile(b_gp, (1, 1, gpb))
    last = lambda i, b: (0, 0, i)
    mid = lambda i, b: (0, i, 0)
    specs = [pl.BlockSpec((2, ns, lw), last),
             pl.BlockSpec((1, lw), lambda i, b: (0, i)),
             pl.BlockSpec((2, ns, lw), last),
             pl.BlockSpec((2, lw, ns), mid),
             pl.BlockSpec((lw, ns), lambda i, b: (i, 0)),
             pl.BlockSpec((2, lw, ns), mid),
             pl.BlockSpec((2, 1, gpb * ns), last),
             pl.BlockSpec((1, gpb * ns), lambda i, b: (0, i)),
             pl.BlockSpec((2, lw, gpb * ns), mid)]
    return [lam_pg, ldt_pg, c_pg, lam_gp, ldt_gp, b_gp, lam_row, ldt_row, b_tl], specs


def _s5_mix_kernel(u_ref, *refs):
    params, (d_ref, y_ref, wm_ref, wr_ref, wo_ref, a_ref, s_ref, x_ref, ys_ref) = refs[:9], refs[9:]

    @pl.when(pl.program_id(1) == 0)
    def _():
        _s5_weights_kernel(*params, wm_ref, wr_ref, wo_ref, a_ref)

    nk = u_ref.shape[1]
    half = s_ref.shape[1] // 2
    sub = 8
    u = u_ref[0]
    s_ref[...] = jnp.dot(u, wr_ref[0], preferred_element_type=F32)

    a1 = (a_ref[0, 0], a_ref[0, 1])
    a2 = _cmul(*a1, *a1)
    a4 = _cmul(*a2, *a2)
    a8 = _cmul(*a4, *a4)
    a3 = _cmul(*a2, *a1)
    pows = [(jnp.ones_like(a1[0]), jnp.zeros_like(a1[0])), a1, a2, a3, a4,
            _cmul(*a4, *a1), _cmul(*a4, *a2), _cmul(*a4, *a3)]
    p_re = jnp.concatenate([p[0] for p in pows], axis=0)
    p_im = jnp.concatenate([p[1] for p in pows], axis=0)
    rowid = lax.broadcasted_iota(jnp.int32, (sub, half), 0)
    levels = [(d, jnp.broadcast_to(a[0], (sub, half)), jnp.broadcast_to(a[1], (sub, half)))
              for d, a in ((1, a1), (2, a2), (4, a4))]

    def shift_rows(v, d):
        return jnp.where(rowid >= d, pltpu.roll(v, d, 0), 0.0)

    def tile(j, carry):
        cr, ci = carry
        rows = slice(j * sub, (j + 1) * sub)
        zr = s_ref[rows, :half]
        zi = s_ref[rows, half:]
        for d, br, bi in levels:
            sr, si = shift_rows(zr, d), shift_rows(zi, d)
            zr, zi = zr + br * sr - bi * si, zi + br * si + bi * sr
        x_ref[rows, :half] = p_re * cr - p_im * ci + shift_rows(zr, 1)
        x_ref[rows, half:] = p_re * ci + p_im * cr + shift_rows(zi, 1)
        return (a8[0] * cr - a8[1] * ci + zr[sub - 1:], a8[0] * ci + a8[1] * cr + zi[sub - 1:])

    carry = (jnp.zeros((1, half), F32), jnp.zeros((1, half), F32))
    for j in range(nk // sub):
        carry = tile(j, carry)
    xb = x_ref[...].astype(BF16)

    cw = 2 * LANES
    for n in range(u.shape[1] // cw):
        cs = slice(n * cw, (n + 1) * cw)
        y = jnp.dot(u[:, :(n + 1) * cw], wm_ref[0, :(n + 1) * cw, cs], preferred_element_type=F32)
        y = y + jnp.dot(xb, wo_ref[0, :, cs], preferred_element_type=F32)
        y = _gelu_tanh(y + d_ref[0, :, cs] * u[:, cs].astype(F32))
        for t in range(2 * n, 2 * n + 2):
            ys_ref[pl.ds(t, nk, stride=S5_TC), :] = y[:, (t - 2 * n) * LANES:(t - 2 * n + 1) * LANES]
    y_ref[0] = ys_ref[...].astype(y_ref.dtype)


def _s5_mixer(u3, nb, lam_re, lam_im, log_dt, b_re, b_im, c_re, c_im, d_skip):
    nblk, rows, w = u3.shape
    tc = S5_TC
    nk = rows // nb
    assert nk % 8 == 0
    views, view_specs = _s5_param_views(lam_re, lam_im, log_dt, b_re, b_im, c_re, c_im)
    ns = 2 * S5_GPB * S5_STATE
    d_rows = jnp.tile(d_skip.astype(F32).reshape(nblk, 1, LANES), (1, 1, tc))
    return pl.pallas_call(
        _s5_mix_kernel,
        name="s5_mix",
        grid=(nblk, nb),
        in_specs=[pl.BlockSpec((1, nk, w), lambda i, b: (i, b, 0))] + view_specs
                 + [pl.BlockSpec((1, 1, w), lambda i, b: (i, 0, 0))],
        out_specs=pl.BlockSpec((1, nk * tc, LANES), lambda i, b: (i, b, 0)),
        out_shape=jax.ShapeDtypeStruct((nblk, rows * tc, LANES), BF16),
        scratch_shapes=[pltpu.VMEM((1, w, w), BF16),
                        pltpu.VMEM((1, w, ns), BF16),
                        pltpu.VMEM((1, ns, w), BF16),
                        pltpu.VMEM((1, 2, 1, ns // 2), F32),
                        pltpu.VMEM((nk, ns), F32), pltpu.VMEM((nk, ns), F32),
                        pltpu.VMEM((nk * tc, LANES), F32)],
        compiler_params=_cparams(("arbitrary", "arbitrary")),
    )(u3, *views, d_rows)


def _even_out_kernel(x_ref, oa_ref, y_ref, zb_ref, wg_ref, bg_ref, wo_ref, g_ref, wf_ref,
                     x1_ref, hn_ref, f_ref):
    y = jnp.concatenate([y_ref[i] for i in range(y_ref.shape[0])], axis=1)
    t = jnp.dot(y, wg_ref[...], preferred_element_type=F32) + bg_ref[...]
    ob = y.astype(F32) * jax.nn.sigmoid(t) * _silu(zb_ref[...].astype(F32))
    wa = oa_ref.shape[1]
    acc = jnp.dot(oa_ref[...], wo_ref[:wa, :], preferred_element_type=F32)
    acc = acc + jnp.dot(ob.astype(BF16), wo_ref[wa:, :], preferred_element_type=F32)
    x1 = x_ref[...] + acc
    x1_ref[...] = x1
    hn = _rms(x1, g_ref[...]).astype(BF16)
    hn_ref[...] = hn
    f_ref[...] = lax.dot_general(wf_ref[...], hn, (((1,), (1,)), ((), ())),
                                 preferred_element_type=F32)


def _even_out(x2, oa, y, proj, w_glu, b_glu, w_out, g_next, wf_t):
    m, d = x2.shape
    nh = wf_t.shape[0]
    wa = oa.shape[1]
    nblk, _, lw = y.shape
    wb = nblk * lw
    tm = min(TM_OUT, m)
    zb_col = (proj.shape[1] - wb) // wb
    const = lambda i: (0, 0)
    return pl.pallas_call(
        _even_out_kernel,
        name="even_out",
        grid=(m // tm,),
        in_specs=[pl.BlockSpec((tm, d), lambda i: (i, 0)),
                  pl.BlockSpec((tm, wa), lambda i: (i, 0)),
                  pl.BlockSpec((nblk, tm, lw), lambda i: (0, i, 0)),
                  pl.BlockSpec((tm, wb), lambda i: (i, zb_col)),
                  pl.BlockSpec((wb, wb), const),
                  pl.BlockSpec((1, wb), const),
                  pl.BlockSpec((wa + wb, d), const),
                  pl.BlockSpec((1, d), const),
                  pl.BlockSpec((nh, d), const)],
        out_specs=[pl.BlockSpec((tm, d), lambda i: (i, 0)),
                   pl.BlockSpec((tm, d), lambda i: (i, 0)),
                   pl.BlockSpec((nh, tm), lambda i: (0, i))],
        out_shape=[jax.ShapeDtypeStruct((m, d), F32),
                   jax.ShapeDtypeStruct((m, d), BF16),
                   jax.ShapeDtypeStruct((nh, m), F32)],
        compiler_params=_cparams(("arbitrary",)),
    )(x2, oa, y, proj, w_glu, b_glu.reshape(1, wb), w_out, g_next.reshape(1, d), wf_t)


def _cum_forget_kernel(f_ref, b_ref, c_ref):
    x = f_ref[...] + b_ref[...]
    v = -(jnp.maximum(-x, 0.0) + jnp.log1p(jnp.exp(-jnp.abs(x))))
    n = v.shape[-1]
    pos = lax.broadcasted_iota(jnp.int32, v.shape, 1)
    d = 1
    while d < n:
        v = v + jnp.where(pos >= d, pltpu.roll(v, d, 1), 0.0)
        d *= 2
    c_ref[...] = v


def _cum_forget(f_t, b_forget, seq):
    nh, m = f_t.shape
    return pl.pallas_call(
        _cum_forget_kernel,
        name="cum_forget",
        grid=(m // seq,),
        in_specs=[pl.BlockSpec((nh, seq), lambda i: (0, i)),
                  pl.BlockSpec((nh, 1), lambda i: (0, 0))],
        out_specs=pl.BlockSpec((nh, seq), lambda i: (0, i)),
        out_shape=jax.ShapeDtypeStruct((nh, m), F32),
        compiler_params=_cparams(("arbitrary",)),
    )(f_t, b_forget.astype(F32).reshape(nh, 1))


def _split3(x):
    hi = x.astype(BF16).astype(F32)
    r = x - hi
    mid = r.astype(BF16).astype(F32)
    lo = (r - mid).astype(BF16).astype(F32)
    return hi, mid, lo


def _fox_kernel(q0_ref, qn_ref, k_ref, v_ref, z_ref, c_ref, o_ref,
                ke_ref, vt_ref, qa_ref, s_ref, mb_ref, acc_ref):
    qi = pl.program_id(2)
    nq = pl.num_programs(2)
    tq = qn_ref.shape[1]
    tk = tq
    dh = C_DH
    nh = c_ref.shape[1]
    nblk = k_ref.shape[1] // tk
    nt = (((1,), (1,)), ((), ()))

    def cols(h):
        return slice(h * dh, (h + 1) * dh)

    def prep_queries(q_ref, qb):
        qs = pl.multiple_of(qb * tq, tq)
        row = lax.broadcasted_iota(jnp.int32, (dh, tq), 0)
        for h in range(nh):
            c_base = jnp.max(c_ref[0, h, :, pl.ds(qs, tq)], axis=1, keepdims=True) * LOG2E
            bh, bm, bl = _split3(c_base)
            g = jnp.where(row < 3, 1.0, jnp.where(row == 3, bh, jnp.where(row == 4, bm,
                          jnp.where(row == 5, bl, 0.0))))
            qt = (q_ref[0, :, cols(h)].astype(F32) * (LOG2E / math.sqrt(dh))).T
            qa_ref[h, :dh, :] = qt.astype(BF16)
            qa_ref[h, dh:, :] = g.astype(BF16)

    half = tk // 2

    def produce(h, j, diagonal):
        ks = pl.multiple_of(j * tk, tk)

        def keys(start, n):
            return jnp.concatenate([k_ref[0, pl.ds(start, n), cols(h)],
                                    ke_ref[h, pl.ds(start, n), :]], axis=1)

        if not diagonal:
            s = jnp.dot(keys(ks, tk), qa_ref[h], preferred_element_type=F32)
            s_ref[h] = s
            return jnp.max(s, axis=0, keepdims=True)
        top = jnp.dot(keys(ks, half), qa_ref[h], preferred_element_type=F32)
        bot = jnp.dot(keys(ks + half, half), qa_ref[h, :, half:],
                      preferred_element_type=F32)
        top = jnp.where(lax.broadcasted_iota(jnp.int32, top.shape, 0)
                        <= lax.broadcasted_iota(jnp.int32, top.shape, 1), top, NEG)
        bot = jnp.where(lax.broadcasted_iota(jnp.int32, bot.shape, 0)
                        <= lax.broadcasted_iota(jnp.int32, bot.shape, 1), bot, NEG)
        s_ref[h, :half, :] = top
        s_ref[h, half:, half:] = bot
        bot_wide = jnp.concatenate([jnp.full((half, half), NEG, F32), bot], axis=1)
        return jnp.max(jnp.maximum(top, bot_wide), axis=0, keepdims=True)

    def consume(h, j, stats, diagonal=False):
        m, mb = stats
        ks = pl.multiple_of(j * tk, tk)
        m2 = jnp.maximum(m, mb)
        if not diagonal:
            p = jnp.exp2(s_ref[h] - m2).astype(BF16)
            pv = jnp.dot(vt_ref[h, :, pl.ds(ks, tk)], p, preferred_element_type=F32)
            acc_ref[h] = jnp.exp2(m - m2) * acc_ref[h] + pv
            return m2
        p_top = jnp.exp2(s_ref[h, :half, :] - m2).astype(BF16)
        mb_ref[h] = m2
        p_bot = jnp.exp2(s_ref[h, half:, half:] - mb_ref[h, :, half:]).astype(BF16)
        acc = jnp.exp2(m - m2) * acc_ref[h] + jnp.dot(vt_ref[h, :, pl.ds(ks, half)], p_top,
                                                      preferred_element_type=F32)
        acc_ref[h, :, :half] = acc[:, :half]
        acc_ref[h, :, half:] = acc[:, half:] + jnp.dot(vt_ref[h, :, pl.ds(ks + half, half)], p_bot,
                                                       preferred_element_type=F32)
        return m2

    def step(j, carry, diagonal_next):
        return tuple((consume(h, j, stats), produce(h, j + 1, diagonal_next))
                     for h, stats in enumerate(carry))

    @pl.when(qi == 0)
    def _():
        r = lax.broadcasted_iota(jnp.int32, (dh, dh), 0)
        c = lax.broadcasted_iota(jnp.int32, (dh, dh), 1)
        eye = (r == c).astype(BF16)
        row = lax.broadcasted_iota(jnp.int32, (dh, tk), 0)
        ones_row = (lax.broadcasted_iota(jnp.int32, (FOX_PAD, tk), 0) == 0).astype(BF16)

        def blk(j, carry):
            ks = pl.multiple_of(j * tk, tk)
            for h in range(nh):
                hi, mid, lo = _split3(c_ref[0, h, :, pl.ds(ks, tk)] * (-LOG2E))
                a = jnp.where(row == 0, hi, jnp.where(row == 1, mid, jnp.where(row == 2, lo,
                              jnp.where(row < 6, 1.0, 0.0))))
                ke_ref[h, pl.ds(ks, tk), :] = a.T.astype(BF16)
                vt = lax.dot_general(eye, v_ref[0, pl.ds(ks, tk), cols(h)], nt,
                                     preferred_element_type=F32)
                vt_ref[h, :dh, pl.ds(ks, tk)] = vt.astype(BF16)
                vt_ref[h, dh:, pl.ds(ks, tk)] = ones_row
            return carry

        lax.fori_loop(0, nblk, blk, 0)
        prep_queries(q0_ref, 0)
        for h in range(nh):
            mb_ref[h] = produce(h, 0, True)

    carry = []
    for h in range(nh):
        acc_ref[h] = jnp.zeros(acc_ref.shape[1:], F32)
        carry.append((jnp.full((1, tq), NEG, F32), mb_ref[h]))
    plain = functools.partial(step, diagonal_next=False)
    n_plain = jnp.maximum(qi - 1, 0)
    n_pairs = n_plain // 2
    carry = lax.fori_loop(0, n_pairs, lambda i, c: plain(2 * i + 1, plain(2 * i, c)),
                          tuple(carry))
    carry = lax.fori_loop(2 * n_pairs, n_plain, plain, carry)

    def finish_block(carry):
        for h, stats in enumerate(carry):
            consume(h, qi, stats, diagonal=True)
        prep_queries(qn_ref, jnp.minimum(qi + 1, nq - 1))
        for h in range(nh):
            mb_ref[h] = produce(h, 0, False)
        for h in range(nh):
            acc = acc_ref[h]
            o = (acc[:dh] / acc[dh:dh + 1]).T
            o_ref[0, :, cols(h)] = (o * _silu(z_ref[0, :, cols(h)].astype(F32))
                                    ).astype(o_ref.dtype)

    @pl.when(qi > 0)
    def _():
        finish_block(step(qi - 1, carry, True))

    @pl.when(qi == 0)
    def _():
        finish_block(carry)


def _fox(proj3, cum4):
    b, l, _ = proj3.shape
    ng = C_HEADS // FOX_HEADS
    w = FOX_HEADS * C_DH
    tq = min(FOX_TQ, l)
    nq = l // tq
    return pl.pallas_call(
        _fox_kernel,
        name="fox",
        grid=(b, ng, nq),
        in_specs=[pl.BlockSpec((1, tq, w), lambda bi, gi, qi: (bi, 0, gi)),
                  pl.BlockSpec((1, tq, w), lambda bi, gi, qi: (bi, jnp.minimum(qi + 1, nq - 1), gi)),
                  pl.BlockSpec((1, l, w), lambda bi, gi, qi: (bi, 0, ng + gi)),
                  pl.BlockSpec((1, l, w), lambda bi, gi, qi: (bi, 0, 2 * ng + gi)),
                  pl.BlockSpec((1, tq, w), lambda bi, gi, qi: (bi, qi, 3 * ng + gi)),
                  pl.BlockSpec((1, FOX_HEADS, 1, l), lambda bi, gi, qi: (bi, gi, 0, 0))],
        out_specs=pl.BlockSpec((1, tq, w), lambda bi, gi, qi: (bi, qi, gi)),
        out_shape=jax.ShapeDtypeStruct((b, l, C_HEADS * C_DH), BF16),
        scratch_shapes=[pltpu.VMEM((FOX_HEADS, l, C_DH), BF16),
                        pltpu.VMEM((FOX_HEADS, C_DH + FOX_PAD, l), BF16),
                        pltpu.VMEM((FOX_HEADS, 2 * C_DH, tq), BF16),
                        pltpu.VMEM((FOX_HEADS, tq, tq), F32),
                        pltpu.VMEM((FOX_HEADS, 1, tq), F32),
                        pltpu.VMEM((FOX_HEADS, C_DH + FOX_PAD, tq), F32)],
        compiler_params=_cparams(("arbitrary", "arbitrary", "arbitrary")),
    )(proj3, proj3, proj3, proj3, proj3, cum4)


def _odd_out_kernel(x_ref, o_ref, w_ref, g_ref, out_ref):
    x2 = x_ref[...] + jnp.dot(o_ref[...], w_ref[...], preferred_element_type=F32)
    out_ref[...] = _rms(x2, g_ref[...])


def _odd_out(x1, o, w_out, g):
    m, d = x1.shape
    k = o.shape[1]
    tm = min(TM_OUT, m)
    return pl.pallas_call(
        _odd_out_kernel,
        name="odd_out",
        grid=(m // tm,),
        in_specs=[pl.BlockSpec((tm, d), lambda i: (i, 0)),
                  pl.BlockSpec((tm, k), lambda i: (i, 0)),
                  pl.BlockSpec((k, d), lambda i: (0, 0)),
                  pl.BlockSpec((1, d), lambda i: (0, 0))],
        out_specs=pl.BlockSpec((tm, d), lambda i: (i, 0)),
        out_shape=jax.ShapeDtypeStruct((m, d), F32),
        compiler_params=_cparams(("arbitrary",)),
    )(x1, o, w_out, g.reshape(1, d))


def kernel(x, norm_even_g, w_in_even, rel_bias, s5_lambda_re, s5_lambda_im, s5_log_dt,
           s5_b_re, s5_b_im, s5_c_re, s5_c_im, s5_d, w_glu, b_glu, w_out_even,
           norm_odd_g, w_in_odd, b_forget, w_out_odd, final_norm_g):
    b, l, d = x.shape
    assert norm_even_g.shape[0] == 1 and norm_odd_g.shape[0] == 1, "one even + one odd layer"
    assert l % FOX_TQ == 0 and l % VT_BLK == 0 and l % A_QB == 0 and l >= A_KB
    m = b * l
    a_w = A_HEADS * A_DH
    s5_w = s5_d.shape[1] * S5_GROUP
    c_w = C_HEADS * C_DH

    x2 = x.reshape(m, d)
    assert s5_w == TN_PROJ and l % S5_TC == 0
    proj, u3 = _norm_proj(x2, norm_even_g[0], w_in_even[0].astype(BF16), 3 * a_w)
    proj3 = proj.reshape(b, l, proj.shape[1])
    o_a = _attn_a(proj3, rel_bias[0])
    y = _s5_mixer(u3, b, s5_lambda_re[0], s5_lambda_im[0], s5_log_dt[0], s5_b_re[0],
                  s5_b_im[0], s5_c_re[0], s5_c_im[0], s5_d[0])
    w_odd_t = w_in_odd[0].T
    wf_t = w_odd_t[4 * c_w:].astype(BF16)
    x1, hn1, f_t = _even_out(x2, o_a.reshape(m, a_w), y, proj,
                             w_glu[0].astype(BF16), b_glu[0].astype(F32),
                             w_out_even[0].astype(BF16), norm_odd_g[0], wf_t)
    proj1 = _proj_odd(hn1, w_odd_t, 4 * c_w)
    cum = _cum_forget(f_t, b_forget[0], l)
    cum4 = cum.reshape(C_HEADS, b, 1, l).transpose(1, 0, 2, 3)
    o_c = _fox(proj1.reshape(b, l, 4 * c_w), cum4)
    out = _odd_out(x1, o_c.reshape(m, c_w), w_out_odd[0].astype(BF16), final_norm_g)
    return out.reshape(b, l, d)
```

```python
import functools
import math

import jax
import jax.numpy as jnp
from jax import lax
from jax.experimental import pallas as pl
from jax.experimental.pallas import tpu as pltpu

F32 = jnp.float32
BF16 = jnp.bfloat16

EPS = 1e-6
NEG = -1e30
LOG2E = math.log2(math.e)

CHUNK = 64
N_LEFT = 8
A_HEADS = 16
A_DH = 64
MAX_REL = 128
S5_GROUP = 16
S5_STATE = 64
C_HEADS = 16
C_DH = 128

LANES = 128
VMEM_LIMIT = 56 * 1024 * 1024

TM_PROJ = 1024
TN_PROJ = 1024
PROJ_CHUNKS = 4
TM_OUT = 512
TM_PROJ_ODD = 2048
A_QB = 2 * CHUNK
A_KB = A_QB + N_LEFT * CHUNK
A_TW = A_KB + N_LEFT * CHUNK
A_PAD = 16
A_PAIRS = 2
A_PAIRS_PER_TRIP = 4
VT_BLK = 512
S5_TC = 16
S5_GPB = LANES // S5_GROUP
FOX_TQ = 512
FOX_HEADS = 4
FOX_PAD = 16


def _cparams(sem):
    return pltpu.CompilerParams(dimension_semantics=sem, vmem_limit_bytes=VMEM_LIMIT)


def _rms(x, g):
    ms = jnp.mean(x * x, axis=-1, keepdims=True)
    return x * lax.rsqrt(ms + EPS) * g


def _silu(x):
    return x * jax.nn.sigmoid(x)


def _gelu_tanh(x):
    c = math.sqrt(2.0 / math.pi)
    return 0.5 * x * (1.0 + jnp.tanh(c * (x + 0.044715 * (x * x * x))))


def _norm_proj_kernel(x_ref, g_ref, w_ref, o_ref, u_ref, hn_ref, us_ref, *, u_tile):
    j = pl.program_id(1)
    tm = x_ref.shape[0]

    @pl.when(j == 0)
    def _():
        for c in range(PROJ_CHUNKS):
            rows = slice(c * tm // PROJ_CHUNKS, (c + 1) * tm // PROJ_CHUNKS)
            h = _rms(x_ref[rows, :], g_ref[...]).astype(BF16)
            hn_ref[rows, :] = h
            o_ref[rows, :] = jnp.dot(h, w_ref[...], preferred_element_type=F32).astype(o_ref.dtype)

    @pl.when(j != 0)
    def _():
        acc = jnp.dot(hn_ref[...], w_ref[...], preferred_element_type=F32)
        o_ref[...] = acc.astype(o_ref.dtype)

        @pl.when(j == u_tile)
        def _():
            nchunk = us_ref.shape[1] // S5_TC
            for blk in range(u_ref.shape[0]):
                us_ref[blk] = acc[:, blk * LANES:(blk + 1) * LANES]
                for t in range(S5_TC):
                    rows = us_ref[blk, pl.ds(t, nchunk, stride=S5_TC), :]
                    u_ref[blk, :, t * LANES:(t + 1) * LANES] = rows.astype(u_ref.dtype)


def _norm_proj(x2, g, w, u_start):
    m, d = x2.shape
    n = w.shape[1]
    tm, tn = min(TM_PROJ, m), TN_PROJ
    assert u_start % tn == 0 and u_start > 0 and tm % (16 * S5_TC) == 0
    nblk = tn // LANES
    return pl.pallas_call(
        functools.partial(_norm_proj_kernel, u_tile=u_start // tn),
        name="norm_proj",
        grid=(m // tm, n // tn),
        in_specs=[pl.BlockSpec((tm, d), lambda i, j: (i, 0)),
                  pl.BlockSpec((1, d), lambda i, j: (0, 0)),
                  pl.BlockSpec((d, tn), lambda i, j: (0, j))],
        out_specs=[pl.BlockSpec((tm, tn), lambda i, j: (i, j)),
                   pl.BlockSpec((nblk, tm // S5_TC, S5_TC * LANES), lambda i, j: (0, i, 0))],
        out_shape=[jax.ShapeDtypeStruct((m, n), BF16),
                   jax.ShapeDtypeStruct((nblk, m // S5_TC, S5_TC * LANES), BF16)],
        scratch_shapes=[pltpu.VMEM((tm, d), BF16), pltpu.VMEM((nblk, tm, LANES), F32)],
        compiler_params=_cparams(("arbitrary", "arbitrary")),
    )(x2, g.reshape(1, d), w)


def _proj_odd_kernel(h_ref, wt_ref, o_ref, wb_ref):
    @pl.when(pl.program_id(1) == 0)
    def _():
        wb_ref[...] = wt_ref[...].T.astype(BF16)

    o_ref[...] = jnp.dot(h_ref[...], wb_ref[...],
                         preferred_element_type=F32).astype(o_ref.dtype)


def _proj_odd(hn, wt, n):
    m, d = hn.shape
    tm, tn = min(TM_PROJ_ODD, m), TN_PROJ
    assert n % tn == 0 and m % tm == 0
    return pl.pallas_call(
        _proj_odd_kernel,
        name="proj_odd",
        grid=(n // tn, m // tm),
        in_specs=[pl.BlockSpec((tm, d), lambda j, i: (i, 0)),
                  pl.BlockSpec((tn, d), lambda j, i: (j, 0))],
        out_specs=pl.BlockSpec((tm, tn), lambda j, i: (i, j)),
        out_shape=jax.ShapeDtypeStruct((m, n), BF16),
        scratch_shapes=[pltpu.VMEM((d, tn), BF16)],
        compiler_params=_cparams(("arbitrary", "arbitrary")),
    )(hn, wt)


def _attn_a_kernel(q_ref, k_ref, v_ref, z_ref, e_ref, o_ref,
                   tt_ref, vt_ref, qt_ref, s_ref, pv_ref):
    l_all = k_ref.shape[1]
    w = q_ref.shape[2]
    pw = 2 * A_DH
    npair = w // pw
    nt = (((1,), (1,)), ((), ()))
    band = N_LEFT * CHUNK

    @pl.when(pl.program_id(1) == 0)
    def _():
        i = lax.broadcasted_iota(jnp.int32, (A_QB, A_TW), 0)
        m = lax.broadcasted_iota(jnp.int32, (A_QB, A_TW), 1)
        dchunk = m // CHUNK - i // CHUNK
        ok = (dchunk >= 0) & (dchunk <= N_LEFT)
        for h in range(2 * npair):
            rows = jnp.broadcast_to(e_ref[0, h:h + 1, :], (A_QB, A_QB + A_TW))
            toep = pltpu.roll(rows, 0, 1, stride=1, stride_axis=0)[:, A_QB:]
            t = jnp.where(ok, toep * LOG2E, NEG)
            tt_ref[:, h * A_QB:(h + 1) * A_QB] = t.T

    r = lax.broadcasted_iota(jnp.int32, (w, w), 0)
    c = lax.broadcasted_iota(jnp.int32, (w, w), 1)
    eye = (r == c).astype(BF16)
    ones_row = (lax.broadcasted_iota(jnp.int32, (A_PAD, VT_BLK), 0) == 0).astype(BF16)

    qscale = LOG2E / math.sqrt(A_DH)

    def vt_blk(j, carry):
        ks = pl.multiple_of(j * VT_BLK, VT_BLK)
        vt = lax.dot_general(eye, v_ref[0, pl.ds(ks, VT_BLK), :], nt, preferred_element_type=F32)
        for h in range(2 * npair):
            vt_ref[h, :A_DH, pl.ds(ks, VT_BLK)] = vt[h * A_DH:(h + 1) * A_DH].astype(BF16)
            vt_ref[h, A_DH:, pl.ds(ks, VT_BLK)] = ones_row
        qs = (q_ref[0, pl.ds(ks, VT_BLK), :].astype(F32) * qscale).astype(BF16)
        qt_ref[:, pl.ds(ks, VT_BLK)] = lax.dot_general(
            eye, qs, nt, preferred_element_type=F32).astype(BF16)
        return carry

    lax.fori_loop(0, l_all // VT_BLK, vt_blk, 0, unroll=True)

    head_row = lax.broadcasted_iota(jnp.int32, (pw, A_QB), 0)

    def window(qb):
        start = qb * A_QB - band
        start_c = jnp.maximum(start, 0)
        return pl.multiple_of(start_c, LANES), pl.multiple_of(start_c - start, LANES)

    def produce(qb, slot):
        qs = pl.multiple_of(qb * A_QB, A_QB)
        start_c, off = window(qb)
        out = []
        for p in range(npair):
            qt = qt_ref[p * pw:(p + 1) * pw, pl.ds(qs, A_QB)]
            zero = jnp.zeros_like(qt)
            qbd = jnp.concatenate([jnp.where(head_row < A_DH, qt, zero),
                                   jnp.where(head_row >= A_DH, qt, zero)], axis=1)
            kb = k_ref[0, pl.ds(start_c, A_KB), p * pw:(p + 1) * pw]
            s = jnp.dot(kb, qbd, preferred_element_type=F32)
            s = s + tt_ref[pl.ds(off, A_KB), 2 * p * A_QB:2 * (p + 1) * A_QB]
            s_ref[slot, p] = s
            out.append(jnp.max(s, axis=0, keepdims=True))
        return tuple(out)

    def consume(qb, slot, ms):
        start_c, _ = window(qb)
        for p in range(npair):
            pb = jnp.exp2(s_ref[slot, p] - ms[p]).astype(BF16)
            for h in range(2):
                vt = vt_ref[2 * p + h, :, pl.ds(start_c, A_KB)]
                pv_ref[slot, 2 * p + h] = jnp.dot(vt, pb[:, h * A_QB:(h + 1) * A_QB],
                                                  preferred_element_type=F32)

    def finish(qb, slot):
        qs = pl.multiple_of(qb * A_QB, A_QB)
        outs = []
        for h in range(2 * npair):
            oh = pv_ref[slot, h]
            outs.append(oh[:A_DH] * (1.0 / oh[A_DH:A_DH + 1]))
        o = jnp.concatenate(outs, axis=0).T
        gate = _silu(z_ref[0, pl.ds(qs, A_QB), :].astype(F32))
        o_ref[0, pl.ds(qs, A_QB), :] = (o * gate).astype(o_ref.dtype)

    def pair(i, m, lead=True):
        j = 2 * i
        if lead:
            finish(j - 1, 1)
        m1 = produce(j + 1, 1)
        consume(j, 0, m)
        m2 = produce(j + 2, 0)
        finish(j, 0)
        consume(j + 1, 1, m1)
        return m2

    nb = l_all // A_QB
    npairs = (nb - 1) // 2
    m = pair(0, produce(0, 0), lead=False)
    def trip(t, mm):
        for r in range(A_PAIRS_PER_TRIP):
            mm = pair(A_PAIRS_PER_TRIP * t + 1 + r, mm)
        return mm

    ntrips = (npairs - 1) // A_PAIRS_PER_TRIP
    m = lax.fori_loop(0, ntrips, trip, m)
    for i in range(1 + A_PAIRS_PER_TRIP * ntrips, npairs):
        m = pair(i, m)
    last = 2 * npairs
    finish(last - 1, 1)
    if (nb - 1) % 2:
        m1 = produce(last + 1, 1)
        consume(last, 0, m)
        finish(last, 0)
        consume(last + 1, 1, m1)
        finish(last + 1, 1)
    else:
        consume(last, 0, m)
        finish(last, 0)


def _attn_a(proj3, rel_bias):
    b, l, _ = proj3.shape
    nh = 2 * A_PAIRS
    ng = A_HEADS // nh
    width = nh * A_DH
    ncol = A_HEADS * A_DH // width
    zcol = (3 * A_HEADS * A_DH + (A_HEADS * A_DH)) // width
    left = N_LEFT * CHUNK + A_QB - MAX_REL
    right = A_QB + A_TW - left - (2 * MAX_REL + 1)
    ext = jnp.pad(rel_bias.astype(F32)[:, ::-1], ((0, 0), (left, right)), mode="edge")
    ext = ext.reshape(ng, nh, A_QB + A_TW)
    seq = lambda h, bi: (bi, 0, h)
    return pl.pallas_call(
        _attn_a_kernel,
        name="attn_a",
        grid=(ng, b),
        in_specs=[pl.BlockSpec((1, l, width), seq),
                  pl.BlockSpec((1, l, width), lambda h, bi: (bi, 0, ncol + h)),
                  pl.BlockSpec((1, l, width), lambda h, bi: (bi, 0, 2 * ncol + h)),
                  pl.BlockSpec((1, l, width), lambda h, bi: (bi, 0, zcol + h)),
                  pl.BlockSpec((1, nh, A_QB + A_TW), lambda h, bi: (h, 0, 0))],
        out_specs=pl.BlockSpec((1, l, width), seq),
        out_shape=jax.ShapeDtypeStruct((b, l, A_HEADS * A_DH), BF16),
        scratch_shapes=[pltpu.VMEM((A_TW, nh * A_QB), F32),
                        pltpu.VMEM((nh, A_DH + A_PAD, l), BF16),
                        pltpu.VMEM((width, l), BF16),
                        pltpu.VMEM((2, A_PAIRS, A_KB, 2 * A_QB), F32),
                        pltpu.VMEM((2, nh, A_DH + A_PAD, A_QB), F32)],
        compiler_params=_cparams(("arbitrary", "arbitrary")),
    )(proj3, proj3, proj3, proj3, ext)


def _cmul(ar, ai, br, bi):
    return ar * br - ai * bi, ar * bi + ai * br


def _s5_discretise(lr, li, log_dt):
    dt = jnp.exp(log_dt)
    mag = jnp.exp(lr * dt)
    ang = li * dt
    a_re = mag * jnp.cos(ang)
    a_im = mag * jnp.sin(ang)
    den = lr * lr + li * li
    nr = a_re - 1.0
    coef_re = (nr * lr + a_im * li) / den
    coef_im = (a_im * lr - nr * li) / den
    return (a_re, a_im), (coef_re, coef_im)


def _s5_weights_kernel(lam_pg_ref, ldt_pg_ref, c_pg_ref, lam_gp_ref, ldt_gp_ref, b_gp_ref,
                       lam_row_ref, ldt_row_ref, b_tl_ref, wm_ref, wr_ref, wo_ref, a_ref):
    tc, ch, ns, gpb = S5_TC, S5_GROUP, S5_STATE, S5_GPB
    lw = gpb * ch
    hi = lax.Precision.HIGHEST

    a_pg, _ = _s5_discretise(lam_pg_ref[0], lam_pg_ref[1], ldt_pg_ref[...])
    pw = (jnp.ones_like(a_pg[0]), jnp.zeros_like(a_pg[0]))
    ca = []
    for n in range(tc + 1):
        ca.append(_cmul(c_pg_ref[0], c_pg_ref[1], *pw))
        pw = _cmul(*pw, *a_pg)

    _, coef_gp = _s5_discretise(lam_gp_ref[0], lam_gp_ref[1], ldt_gp_ref[...])
    bb_gp = _cmul(*coef_gp, b_gp_ref[0], b_gp_ref[1])
    ca_re = jnp.concatenate([ca[n][0] for n in range(tc)], axis=1)
    ca_im = jnp.concatenate([ca[n][1] for n in range(tc)], axis=1)
    z = (jnp.dot(bb_gp[0], ca_re, precision=hi, preferred_element_type=F32)
         - jnp.dot(bb_gp[1], ca_im, precision=hi, preferred_element_type=F32))
    row_g = lax.broadcasted_iota(jnp.int32, z.shape, 0) // ch
    col_g = (lax.broadcasted_iota(jnp.int32, z.shape, 1) % lw) // ch
    z = jnp.where(row_g == col_g, z, 0.0).astype(BF16)
    for t in range(tc):
        parts = [z[:, :(tc - t) * lw]]
        if t:
            parts.insert(0, jnp.zeros((lw, t * lw), BF16))
        wm_ref[0, t * lw:(t + 1) * lw, :] = jnp.concatenate(parts, axis=1)

    rg = lax.broadcasted_iota(jnp.int32, (gpb * ns, lw), 0) // ns
    cg = lax.broadcasted_iota(jnp.int32, (gpb * ns, lw), 1) // ch
    keep = rg == cg
    for t2 in range(tc):
        cr, ci = ca[t2 + 1]
        cr = jnp.where(keep, jnp.concatenate([cr] * gpb, axis=0), 0.0)
        ci = jnp.where(keep, jnp.concatenate([ci] * gpb, axis=0), 0.0)
        wo_ref[0, :gpb * ns, t2 * lw:(t2 + 1) * lw] = cr.astype(BF16)
        wo_ref[0, gpb * ns:, t2 * lw:(t2 + 1) * lw] = (-ci).astype(BF16)

    a_row, coef_row = _s5_discretise(lam_row_ref[0], lam_row_ref[1], ldt_row_ref[...])
    bb_tl = _cmul(*coef_row, b_tl_ref[0], b_tl_ref[1])
    rg = lax.broadcasted_iota(jnp.int32, bb_tl[0].shape, 0) // ch
    cg = lax.broadcasted_iota(jnp.int32, bb_tl[0].shape, 1) // ns
    bb_tl = (jnp.where(rg == cg, bb_tl[0], 0.0), jnp.where(rg == cg, bb_tl[1], 0.0))
    pw = (jnp.ones_like(a_row[0]), jnp.zeros_like(a_row[0]))
    for t in range(tc - 1, -1, -1):
        r_re, r_im = _cmul(*pw, *bb_tl)
        wr_ref[0, t * lw:(t + 1) * lw, :] = jnp.concatenate([r_re, r_im], axis=1).astype(BF16)
        pw = _cmul(*pw, *a_row)
    a_ref[0, 0] = pw[0]
    a_ref[0, 1] = pw[1]


def _s5_param_views(lam_re, lam_im, log_dt, b_re, b_im, c_re, c_im):
    g, ns = lam_re.shape
    ch, gpb = S5_GROUP, S5_GPB
    lw = gpb * ch
    lam = jnp.stack([lam_re, lam_im]).astype(F32)
    ldt = log_dt.astype(F32)
    bst = jnp.stack([b_re, b_im]).astype(F32)
    cst = jnp.stack([c_re, c_im]).astype(F32)
    lam_pg = jnp.repeat(lam.transpose(0, 2, 1), ch, axis=2)
    ldt_pg = jnp.repeat(ldt, ch)[None]
    c_pg = cst.transpose(0, 3, 1, 2).reshape(2, ns, g * ch)
    lam_gp = jnp.repeat(lam, ch, axis=1)
    ldt_gp = jnp.broadcast_to(jnp.repeat(ldt, ch)[:, None], (g * ch, ns))
    b_gp = bst.transpose(0, 1, 3, 2).reshape(2, g * ch, ns)
    lam_row = lam.reshape(2, 1, g * ns)
    ldt_row = jnp.repeat(ldt, ns)[None]
    b_tl = jnp.tile(b_gp, (1, 1, gpb))
    last = lambda i, b: (0, 0, i)
    mid = lambda i, b: (0, i, 0)
    specs = [pl.BlockSpec((2, ns, lw), last),
             pl.BlockSpec((1, lw), lambda i, b: (0, i)),
             pl.BlockSpec((2, ns, lw), last),
             pl.BlockSpec((2, lw, ns), mid),
             pl.BlockSpec((lw, ns), lambda i, b: (i, 0)),
             pl.BlockSpec((2, lw, ns), mid),
             pl.BlockSpec((2, 1, gpb * ns), last),
             pl.BlockSpec((1, gpb * ns), lambda i, b: (0, i)),
             pl.BlockSpec((2, lw, gpb * ns), mid)]
    return [lam_pg, ldt_pg, c_pg, lam_gp, ldt_gp, b_gp, lam_row, ldt_row, b_tl], specs


def _s5_mix_kernel(u_ref, *refs):
    params, (d_ref, y_ref, wm_ref, wr_ref, wo_ref, a_ref, s_ref, x_ref, ys_ref) = refs[:9], refs[9:]

    @pl.when(pl.program_id(1) == 0)
    def _():
        _s5_weights_kernel(*params, wm_ref, wr_ref, wo_ref, a_ref)

    nk = u_ref.shape[1]
    half = s_ref.shape[1] // 2
    sub = 8
    u = u_ref[0]
    s_ref[...] = jnp.dot(u, wr_ref[0], preferred_element_type=F32)

    a1 = (a_ref[0, 0], a_ref[0, 1])
    a2 = _cmul(*a1, *a1)
    a4 = _cmul(*a2, *a2)
    a8 = _cmul(*a4, *a4)
    a3 = _cmul(*a2, *a1)
    pows = [(jnp.ones_like(a1[0]), jnp.zeros_like(a1[0])), a1, a2, a3, a4,
            _cmul(*a4, *a1), _cmul(*a4, *a2), _cmul(*a4, *a3)]
    p_re = jnp.concatenate([p[0] for p in pows], axis=0)
    p_im = jnp.concatenate([p[1] for p in pows], axis=0)
    rowid = lax.broadcasted_iota(jnp.int32, (sub, half), 0)
    levels = [(d, jnp.broadcast_to(a[0], (sub, half)), jnp.broadcast_to(a[1], (sub, half)))
              for d, a in ((1, a1), (2, a2), (4, a4))]

    def shift_rows(v, d):
        return jnp.where(rowid >= d, pltpu.roll(v, d, 0), 0.0)

    def tile(j, carry):
        cr, ci = carry
        rows = slice(j * sub, (j + 1) * sub)
        zr = s_ref[rows, :half]
        zi = s_ref[rows, half:]
        for d, br, bi in levels:
            sr, si = shift_rows(zr, d), shift_rows(zi, d)
            zr, zi = zr + br * sr - bi * si, zi + br * si + bi * sr
        x_ref[rows, :half] = p_re * cr - p_im * ci + shift_rows(zr, 1)
        x_ref[rows, half:] = p_re * ci + p_im * cr + shift_rows(zi, 1)
        return (a8[0] * cr - a8[1] * ci + zr[sub - 1:], a8[0] * ci + a8[1] * cr + zi[sub - 1:])

    carry = (jnp.zeros((1, half), F32), jnp.zeros((1, half), F32))
    for j in range(nk // sub):
        carry = tile(j, carry)
    xb = x_ref[...].astype(BF16)

    cw = 2 * LANES
    for n in range(u.shape[1] // cw):
        cs = slice(n * cw, (n + 1) * cw)
        y = jnp.dot(u[:, :(n + 1) * cw], wm_ref[0, :(n + 1) * cw, cs], preferred_element_type=F32)
        y = y + jnp.dot(xb, wo_ref[0, :, cs], preferred_element_type=F32)
        y = _gelu_tanh(y + d_ref[0, :, cs] * u[:, cs].astype(F32))
        for t in range(2 * n, 2 * n + 2):
            ys_ref[pl.ds(t, nk, stride=S5_TC), :] = y[:, (t - 2 * n) * LANES:(t - 2 * n + 1) * LANES]
    y_ref[0] = ys_ref[...].astype(y_ref.dtype)


def _s5_mixer(u3, nb, lam_re, lam_im, log_dt, b_re, b_im, c_re, c_im, d_skip):
    nblk, rows, w = u3.shape
    tc = S5_TC
    nk = rows // nb
    assert nk % 8 == 0
    views, view_specs = _s5_param_views(lam_re, lam_im, log_dt, b_re, b_im, c_re, c_im)
    ns = 2 * S5_GPB * S5_STATE
    d_rows = jnp.tile(d_skip.astype(F32).reshape(nblk, 1, LANES), (1, 1, tc))
    return pl.pallas_call(
        _s5_mix_kernel,
        name="s5_mix",
        grid=(nblk, nb),
        in_specs=[pl.BlockSpec((1, nk, w), lambda i, b: (i, b, 0))] + view_specs
                 + [pl.BlockSpec((1, 1, w), lambda i, b: (i, 0, 0))],
        out_specs=pl.BlockSpec((1, nk * tc, LANES), lambda i, b: (i, b, 0)),
        out_shape=jax.ShapeDtypeStruct((nblk, rows * tc, LANES), BF16),
        scratch_shapes=[pltpu.VMEM((1, w, w), BF16),
                        pltpu.VMEM((1, w, ns), BF16),
                        pltpu.VMEM((1, ns, w), BF16),
                        pltpu.VMEM((1, 2, 1, ns // 2), F32),
                        pltpu.VMEM((nk, ns), F32), pltpu.VMEM((nk, ns), F32),
                        pltpu.VMEM((nk * tc, LANES), F32)],
        compiler_params=_cparams(("arbitrary", "arbitrary")),
    )(u3, *views, d_rows)


def _even_out_kernel(x_ref, oa_ref, y_ref, zb_ref, wg_ref, bg_ref, wo_ref, g_ref, wf_ref,
                     x1_ref, hn_ref, f_ref):
    y = jnp.concatenate([y_ref[i] for i in range(y_ref.shape[0])], axis=1)
    t = jnp.dot(y, wg_ref[...], preferred_element_type=F32) + bg_ref[...]
    ob = y.astype(F32) * jax.nn.sigmoid(t) * _silu(zb_ref[...].astype(F32))
    wa = oa_ref.shape[1]
    acc = jnp.dot(oa_ref[...], wo_ref[:wa, :], preferred_element_type=F32)
    acc = acc + jnp.dot(ob.astype(BF16), wo_ref[wa:, :], preferred_element_type=F32)
    x1 = x_ref[...] + acc
    x1_ref[...] = x1
    hn = _rms(x1, g_ref[...]).astype(BF16)
    hn_ref[...] = hn
    f_ref[...] = lax.dot_general(wf_ref[...], hn, (((1,), (1,)), ((), ())),
                                 preferred_element_type=F32)


def _even_out(x2, oa, y, proj, w_glu, b_glu, w_out, g_next, wf_t):
    m, d = x2.shape
    nh = wf_t.shape[0]
    wa = oa.shape[1]
    nblk, _, lw = y.shape
    wb = nblk * lw
    tm = min(TM_OUT, m)
    zb_col = (proj.shape[1] - wb) // wb
    const = lambda i: (0, 0)
    return pl.pallas_call(
        _even_out_kernel,
        name="even_out",
        grid=(m // tm,),
        in_specs=[pl.BlockSpec((tm, d), lambda i: (i, 0)),
                  pl.BlockSpec((tm, wa), lambda i: (i, 0)),
                  pl.BlockSpec((nblk, tm, lw), lambda i: (0, i, 0)),
                  pl.BlockSpec((tm, wb), lambda i: (i, zb_col)),
                  pl.BlockSpec((wb, wb), const),
                  pl.BlockSpec((1, wb), const),
                  pl.BlockSpec((wa + wb, d), const),
                  pl.BlockSpec((1, d), const),
                  pl.BlockSpec((nh, d), const)],
        out_specs=[pl.BlockSpec((tm, d), lambda i: (i, 0)),
                   pl.BlockSpec((tm, d), lambda i: (i, 0)),
                   pl.BlockSpec((nh, tm), lambda i: (0, i))],
        out_shape=[jax.ShapeDtypeStruct((m, d), F32),
                   jax.ShapeDtypeStruct((m, d), BF16),
                   jax.ShapeDtypeStruct((nh, m), F32)],
        compiler_params=_cparams(("arbitrary",)),
    )(x2, oa, y, proj, w_glu, b_glu.reshape(1, wb), w_out, g_next.reshape(1, d), wf_t)


def _cum_forget_kernel(f_ref, b_ref, c_ref):
    x = f_ref[...] + b_ref[...]
    v = -(jnp.maximum(-x, 0.0) + jnp.log1p(jnp.exp(-jnp.abs(x))))
    n = v.shape[-1]
    pos = lax.broadcasted_iota(jnp.int32, v.shape, 1)
    d = 1
    while d < n:
        v = v + jnp.where(pos >= d, pltpu.roll(v, d, 1), 0.0)
        d *= 2
    c_ref[...] = v


def _cum_forget(f_t, b_forget, seq):
    nh, m = f_t.shape
    return pl.pallas_call(
        _cum_forget_kernel,
        name="cum_forget",
        grid=(m // seq,),
        in_specs=[pl.BlockSpec((nh, seq), lambda i: (0, i)),
                  pl.BlockSpec((nh, 1), lambda i: (0, 0))],
        out_specs=pl.BlockSpec((nh, seq), lambda i: (0, i)),
        out_shape=jax.ShapeDtypeStruct((nh, m), F32),
        compiler_params=_cparams(("arbitrary",)),
    )(f_t, b_forget.astype(F32).reshape(nh, 1))


def _split3(x):
    hi = x.astype(BF16).astype(F32)
    r = x - hi
    mid = r.astype(BF16).astype(F32)
    lo = (r - mid).astype(BF16).astype(F32)
    return hi, mid, lo


def _fox_kernel(q0_ref, qn_ref, k_ref, v_ref, z_ref, c_ref, o_ref,
                ke_ref, vt_ref, qa_ref, s_ref, mb_ref, acc_ref):
    qi = pl.program_id(2)
    nq = pl.num_programs(2)
    tq = qn_ref.shape[1]
    tk = tq
    dh = C_DH
    nh = c_ref.shape[1]
    nblk = k_ref.shape[1] // tk
    nt = (((1,), (1,)), ((), ()))

    def cols(h):
        return slice(h * dh, (h + 1) * dh)

    def prep_queries(q_ref, qb):
        qs = pl.multiple_of(qb * tq, tq)
        row = lax.broadcasted_iota(jnp.int32, (dh, tq), 0)
        for h in range(nh):
            c_base = jnp.max(c_ref[0, h, :, pl.ds(qs, tq)], axis=1, keepdims=True) * LOG2E
            bh, bm, bl = _split3(c_base)
            g = jnp.where(row < 3, 1.0, jnp.where(row == 3, bh, jnp.where(row == 4, bm,
                          jnp.where(row == 5, bl, 0.0))))
            qt = (q_ref[0, :, cols(h)].astype(F32) * (LOG2E / math.sqrt(dh))).T
            qa_ref[h, :dh, :] = qt.astype(BF16)
            qa_ref[h, dh:, :] = g.astype(BF16)

    half = tk // 2

    def produce(h, j, diagonal):
        ks = pl.multiple_of(j * tk, tk)

        def keys(start, n):
            return jnp.concatenate([k_ref[0, pl.ds(start, n), cols(h)],
                                    ke_ref[h, pl.ds(start, n), :]], axis=1)

        if not diagonal:
            s = jnp.dot(keys(ks, tk), qa_ref[h], preferred_element_type=F32)
            s_ref[h] = s
            return jnp.max(s, axis=0, keepdims=True)
        top = jnp.dot(keys(ks, half), qa_ref[h], preferred_element_type=F32)
        bot = jnp.dot(keys(ks + half, half), qa_ref[h, :, half:],
                      preferred_element_type=F32)
        top = jnp.where(lax.broadcasted_iota(jnp.int32, top.shape, 0)
                        <= lax.broadcasted_iota(jnp.int32, top.shape, 1), top, NEG)
        bot = jnp.where(lax.broadcasted_iota(jnp.int32, bot.shape, 0)
                        <= lax.broadcasted_iota(jnp.int32, bot.shape, 1), bot, NEG)
        s_ref[h, :half, :] = top
        s_ref[h, half:, half:] = bot
        bot_wide = jnp.concatenate([jnp.full((half, half), NEG, F32), bot], axis=1)
        return jnp.max(jnp.maximum(top, bot_wide), axis=0, keepdims=True)

    def consume(h, j, stats, diagonal=False):
        m, mb = stats
        ks = pl.multiple_of(j * tk, tk)
        m2 = jnp.maximum(m, mb)
        if not diagonal:
            p = jnp.exp2(s_ref[h] - m2).astype(BF16)
            pv = jnp.dot(vt_ref[h, :, pl.ds(ks, tk)], p, preferred_element_type=F32)
            acc_ref[h] = jnp.exp2(m - m2) * acc_ref[h] + pv
            return m2
        p_top = jnp.exp2(s_ref[h, :half, :] - m2).astype(BF16)
        mb_ref[h] = m2
        p_bot = jnp.exp2(s_ref[h, half:, half:] - mb_ref[h, :, half:]).astype(BF16)
        acc = jnp.exp2(m - m2) * acc_ref[h] + jnp.dot(vt_ref[h, :, pl.ds(ks, half)], p_top,
                                                      preferred_element_type=F32)
        acc_ref[h, :, :half] = acc[:, :half]
        acc_ref[h, :, half:] = acc[:, half:] + jnp.dot(vt_ref[h, :, pl.ds(ks + half, half)], p_bot,
                                                       preferred_element_type=F32)
        return m2

    def step(j, carry, diagonal_next):
        return tuple((consume(h, j, stats), produce(h, j + 1, diagonal_next))
                     for h, stats in enumerate(carry))

    @pl.when(qi == 0)
    def _():
        r = lax.broadcasted_iota(jnp.int32, (dh, dh), 0)
        c = lax.broadcasted_iota(jnp.int32, (dh, dh), 1)
        eye = (r == c).astype(BF16)
        row = lax.broadcasted_iota(jnp.int32, (dh, tk), 0)
        ones_row = (lax.broadcasted_iota(jnp.int32, (FOX_PAD, tk), 0) == 0).astype(BF16)

        def blk(j, carry):
            ks = pl.multiple_of(j * tk, tk)
            for h in range(nh):
                hi, mid, lo = _split3(c_ref[0, h, :, pl.ds(ks, tk)] * (-LOG2E))
                a = jnp.where(row == 0, hi, jnp.where(row == 1, mid, jnp.where(row == 2, lo,
                              jnp.where(row < 6, 1.0, 0.0))))
                ke_ref[h, pl.ds(ks, tk), :] = a.T.astype(BF16)
                vt = lax.dot_general(eye, v_ref[0, pl.ds(ks, tk), cols(h)], nt,
                                     preferred_element_type=F32)
                vt_ref[h, :dh, pl.ds(ks, tk)] = vt.astype(BF16)
                vt_ref[h, dh:, pl.ds(ks, tk)] = ones_row
            return carry

        lax.fori_loop(0, nblk, blk, 0, unroll=True)
        prep_queries(q0_ref, 0)
        for h in range(nh):
            mb_ref[h] = produce(h, 0, True)

    carry = []
    for h in range(nh):
        acc_ref[h] = jnp.zeros(acc_ref.shape[1:], F32)
        carry.append((jnp.full((1, tq), NEG, F32), mb_ref[h]))
    plain = functools.partial(step, diagonal_next=False)
    n_plain = jnp.maximum(qi - 1, 0)
    n_pairs = n_plain // 2
    carry = lax.fori_loop(0, n_pairs, lambda i, c: plain(2 * i + 1, plain(2 * i, c)),
                          tuple(carry))
    carry = lax.fori_loop(2 * n_pairs, n_plain, plain, carry)

    def finish_block(carry):
        for h, stats in enumerate(carry):
            consume(h, qi, stats, diagonal=True)
        prep_queries(qn_ref, jnp.minimum(qi + 1, nq - 1))
        for h in range(nh):
            mb_ref[h] = produce(h, 0, False)
        for h in range(nh):
            acc = acc_ref[h]
            o = (acc[:dh] / acc[dh:dh + 1]).T
            o_ref[0, :, cols(h)] = (o * _silu(z_ref[0, :, cols(h)].astype(F32))
                                    ).astype(o_ref.dtype)

    @pl.when(qi > 0)
    def _():
        finish_block(step(qi - 1, carry, True))

    @pl.when(qi == 0)
    def _():
        finish_block(carry)


def _fox(proj3, cum4):
    b, l, _ = proj3.shape
    ng = C_HEADS // FOX_HEADS
    w = FOX_HEADS * C_DH
    tq = min(FOX_TQ, l)
    nq = l // tq
    return pl.pallas_call(
        _fox_kernel,
        name="fox",
        grid=(b, ng, nq),
        in_specs=[pl.BlockSpec((1, tq, w), lambda bi, gi, qi: (bi, 0, gi)),
                  pl.BlockSpec((1, tq, w), lambda bi, gi, qi: (bi, jnp.minimum(qi + 1, nq - 1), gi)),
                  pl.BlockSpec((1, l, w), lambda bi, gi, qi: (bi, 0, ng + gi)),
                  pl.BlockSpec((1, l, w), lambda bi, gi, qi: (bi, 0, 2 * ng + gi)),
                  pl.BlockSpec((1, tq, w), lambda bi, gi, qi: (bi, qi, 3 * ng + gi)),
                  pl.BlockSpec((1, FOX_HEADS, 1, l), lambda bi, gi, qi: (bi, gi, 0, 0))],
        out_specs=pl.BlockSpec((1, tq, w), lambda bi, gi, qi: (bi, qi, gi)),
        out_shape=jax.ShapeDtypeStruct((b, l, C_HEADS * C_DH), BF16),
        scratch_shapes=[pltpu.VMEM((FOX_HEADS, l, C_DH), BF16),
                        pltpu.VMEM((FOX_HEADS, C_DH + FOX_PAD, l), BF16),
                        pltpu.VMEM((FOX_HEADS, 2 * C_DH, tq), BF16),
                        pltpu.VMEM((FOX_HEADS, tq, tq), F32),
                        pltpu.VMEM((FOX_HEADS, 1, tq), F32),
                        pltpu.VMEM((FOX_HEADS, C_DH + FOX_PAD, tq), F32)],
        compiler_params=_cparams(("arbitrary", "arbitrary", "arbitrary")),
    )(proj3, proj3, proj3, proj3, proj3, cum4)


def _odd_out_kernel(x_ref, o_ref, w_ref, g_ref, out_ref):
    x2 = x_ref[...] + jnp.dot(o_ref[...], w_ref[...], preferred_element_type=F32)
    out_ref[...] = _rms(x2, g_ref[...])


def _odd_out(x1, o, w_out, g):
    m, d = x1.shape
    k = o.shape[1]
    tm = min(TM_OUT, m)
    return pl.pallas_call(
        _odd_out_kernel,
        name="odd_out",
        grid=(m // tm,),
        in_specs=[pl.BlockSpec((tm, d), lambda i: (i, 0)),
                  pl.BlockSpec((tm, k), lambda i: (i, 0)),
                  pl.BlockSpec((k, d), lambda i: (0, 0)),
                  pl.BlockSpec((1, d), lambda i: (0, 0))],
        out_specs=pl.BlockSpec((tm, d), lambda i: (i, 0)),
        out_shape=jax.ShapeDtypeStruct((m, d), F32),
        compiler_params=_cparams(("arbitrary",)),
    )(x1, o, w_out, g.reshape(1, d))


def kernel(x, norm_even_g, w_in_even, rel_bias, s5_lambda_re, s5_lambda_im, s5_log_dt,
           s5_b_re, s5_b_im, s5_c_re, s5_c_im, s5_d, w_glu, b_glu, w_out_even,
           norm_odd_g, w_in_odd, b_forget, w_out_odd, final_norm_g):
    b, l, d = x.shape
    assert norm_even_g.shape[0] == 1 and norm_odd_g.shape[0] == 1, "one even + one odd layer"
    assert l % FOX_TQ == 0 and l % VT_BLK == 0 and l % A_QB == 0 and l >= A_KB
    m = b * l
    a_w = A_HEADS * A_DH
    s5_w = s5_d.shape[1] * S5_GROUP
    c_w = C_HEADS * C_DH

    x2 = x.reshape(m, d)
    assert s5_w == TN_PROJ and l % S5_TC == 0
    proj, u3 = _norm_proj(x2, norm_even_g[0], w_in_even[0].astype(BF16), 3 * a_w)
    proj3 = proj.reshape(b, l, proj.shape[1])
    o_a = _attn_a(proj3, rel_bias[0])
    y = _s5_mixer(u3, b, s5_lambda_re[0], s5_lambda_im[0], s5_log_dt[0], s5_b_re[0],
                  s5_b_im[0], s5_c_re[0], s5_c_im[0], s5_d[0])
    w_odd_t = w_in_odd[0].T
    wf_t = w_odd_t[4 * c_w:].astype(BF16)
    x1, hn1, f_t = _even_out(x2, o_a.reshape(m, a_w), y, proj,
                             w_glu[0].astype(BF16), b_glu[0].astype(F32),
                             w_out_even[0].astype(BF16), norm_odd_g[0], wf_t)
    proj1 = _proj_odd(hn1, w_odd_t, 4 * c_w)
    cum = _cum_forget(f_t, b_forget[0], l)
    cum4 = cum.reshape(C_HEADS, b, 1, l).transpose(1, 0, 2, 3)
    o_c = _fox(proj1.reshape(b, l, 4 * c_w), cum4)
    out = _odd_out(x1, o_c.reshape(m, c_w), w_out_odd[0].astype(BF16), final_norm_g)
    return out.reshape(b, l, d)
```

```python
import functools
import math

import jax
import jax.numpy as jnp
from jax import lax
from jax.experimental import pallas as pl
from jax.experimental.pallas import tpu as pltpu

F32 = jnp.float32
BF16 = jnp.bfloat16

EPS = 1e-6
NEG = -1e30
LOG2E = math.log2(math.e)

CHUNK = 64
N_LEFT = 8
A_HEADS = 16
A_DH = 64
MAX_REL = 128
S5_GROUP = 16
S5_STATE = 64
C_HEADS = 16
C_DH = 128

LANES = 128
VMEM_LIMIT = 56 * 1024 * 1024

TM_PROJ = 1024
TN_PROJ = 1024
PROJ_CHUNKS = 4
TM_OUT = 512
TM_PROJ_ODD = 2048
A_QB = 2 * CHUNK
A_KB = A_QB + N_LEFT * CHUNK
A_TW = A_KB + N_LEFT * CHUNK
A_PAD = 16
A_PAIRS = 2
A_PAIRS_PER_TRIP = 4
VT_BLK = 512
S5_TC = 16
S5_GPB = LANES // S5_GROUP
FOX_TQ = 512
FOX_HEADS = 4
FOX_PAD = 16


def _cparams(sem):
    return pltpu.CompilerParams(dimension_semantics=sem, vmem_limit_bytes=VMEM_LIMIT)


def _rms(x, g):
    ms = jnp.mean(x * x, axis=-1, keepdims=True)
    return x * lax.rsqrt(ms + EPS) * g


def _silu(x):
    return x * jax.nn.sigmoid(x)


def _gelu_tanh(x):
    c = math.sqrt(2.0 / math.pi)
    return 0.5 * x * (1.0 + jnp.tanh(c * (x + 0.044715 * (x * x * x))))


def _norm_proj_kernel(x_ref, g_ref, w_ref, o_ref, u_ref, hn_ref, us_ref, *, u_tile):
    j = pl.program_id(1)
    tm = x_ref.shape[0]

    @pl.when(j == 0)
    def _():
        for c in range(PROJ_CHUNKS):
            rows = slice(c * tm // PROJ_CHUNKS, (c + 1) * tm // PROJ_CHUNKS)
            h = _rms(x_ref[rows, :], g_ref[...]).astype(BF16)
            hn_ref[rows, :] = h
            o_ref[rows, :] = jnp.dot(h, w_ref[...], preferred_element_type=F32).astype(o_ref.dtype)

    @pl.when(j != 0)
    def _():
        acc = jnp.dot(hn_ref[...], w_ref[...], preferred_element_type=F32)
        o_ref[...] = acc.astype(o_ref.dtype)

        @pl.when(j == u_tile)
        def _():
            nchunk = us_ref.shape[1] // S5_TC
            for blk in range(u_ref.shape[0]):
                us_ref[blk] = acc[:, blk * LANES:(blk + 1) * LANES]
                for t in range(S5_TC):
                    rows = us_ref[blk, pl.ds(t, nchunk, stride=S5_TC), :]
                    u_ref[blk, :, t * LANES:(t + 1) * LANES] = rows.astype(u_ref.dtype)


def _norm_proj(x2, g, w, u_start):
    m, d = x2.shape
    n = w.shape[1]
    tm, tn = min(TM_PROJ, m), TN_PROJ
    assert u_start % tn == 0 and u_start > 0 and tm % (16 * S5_TC) == 0
    nblk = tn // LANES
    return pl.pallas_call(
        functools.partial(_norm_proj_kernel, u_tile=u_start // tn),
        name="norm_proj",
        grid=(m // tm, n // tn),
        in_specs=[pl.BlockSpec((tm, d), lambda i, j: (i, 0)),
                  pl.BlockSpec((1, d), lambda i, j: (0, 0)),
                  pl.BlockSpec((d, tn), lambda i, j: (0, j))],
        out_specs=[pl.BlockSpec((tm, tn), lambda i, j: (i, j)),
                   pl.BlockSpec((nblk, tm // S5_TC, S5_TC * LANES), lambda i, j: (0, i, 0))],
        out_shape=[jax.ShapeDtypeStruct((m, n), BF16),
                   jax.ShapeDtypeStruct((nblk, m // S5_TC, S5_TC * LANES), BF16)],
        scratch_shapes=[pltpu.VMEM((tm, d), BF16), pltpu.VMEM((nblk, tm, LANES), F32)],
        compiler_params=_cparams(("arbitrary", "arbitrary")),
    )(x2, g.reshape(1, d), w)


def _proj_odd_kernel(h_ref, wt_ref, o_ref, wb_ref):
    @pl.when(pl.program_id(1) == 0)
    def _():
        wb_ref[...] = wt_ref[...].T.astype(BF16)

    o_ref[...] = jnp.dot(h_ref[...], wb_ref[...],
                         preferred_element_type=F32).astype(o_ref.dtype)


def _proj_odd(hn, wt, n):
    m, d = hn.shape
    tm, tn = min(TM_PROJ_ODD, m), TN_PROJ
    assert n % tn == 0 and m % tm == 0
    return pl.pallas_call(
        _proj_odd_kernel,
        name="proj_odd",
        grid=(n // tn, m // tm),
        in_specs=[pl.BlockSpec((tm, d), lambda j, i: (i, 0)),
                  pl.BlockSpec((tn, d), lambda j, i: (j, 0))],
        out_specs=pl.BlockSpec((tm, tn), lambda j, i: (i, j)),
        out_shape=jax.ShapeDtypeStruct((m, n), BF16),
        scratch_shapes=[pltpu.VMEM((d, tn), BF16)],
        compiler_params=_cparams(("arbitrary", "arbitrary")),
    )(hn, wt)


def _attn_a_kernel(q_ref, k_ref, v_ref, z_ref, e_ref, o_ref,
                   tt_ref, vt_ref, qt_ref, s_ref, pv_ref):
    l_all = k_ref.shape[1]
    w = q_ref.shape[2]
    pw = 2 * A_DH
    npair = w // pw
    nt = (((1,), (1,)), ((), ()))
    band = N_LEFT * CHUNK

    @pl.when(pl.program_id(1) == 0)
    def _():
        i = lax.broadcasted_iota(jnp.int32, (A_QB, A_TW), 0)
        m = lax.broadcasted_iota(jnp.int32, (A_QB, A_TW), 1)
        dchunk = m // CHUNK - i // CHUNK
        ok = (dchunk >= 0) & (dchunk <= N_LEFT)
        for h in range(2 * npair):
            rows = jnp.broadcast_to(e_ref[0, h:h + 1, :], (A_QB, A_QB + A_TW))
            toep = pltpu.roll(rows, 0, 1, stride=1, stride_axis=0)[:, A_QB:]
            t = jnp.where(ok, toep * LOG2E, NEG)
            tt_ref[:, h * A_QB:(h + 1) * A_QB] = t.T

    r = lax.broadcasted_iota(jnp.int32, (w, w), 0)
    c = lax.broadcasted_iota(jnp.int32, (w, w), 1)
    eye = (r == c).astype(BF16)
    ones_row = (lax.broadcasted_iota(jnp.int32, (A_PAD, VT_BLK), 0) == 0).astype(BF16)

    qscale = LOG2E / math.sqrt(A_DH)

    def vt_blk(j, carry):
        ks = pl.multiple_of(j * VT_BLK, VT_BLK)
        vt = lax.dot_general(eye, v_ref[0, pl.ds(ks, VT_BLK), :], nt, preferred_element_type=F32)
        for h in range(2 * npair):
            vt_ref[h, :A_DH, pl.ds(ks, VT_BLK)] = vt[h * A_DH:(h + 1) * A_DH].astype(BF16)
            vt_ref[h, A_DH:, pl.ds(ks, VT_BLK)] = ones_row
        qs = (q_ref[0, pl.ds(ks, VT_BLK), :].astype(F32) * qscale).astype(BF16)
        qt_ref[:, pl.ds(ks, VT_BLK)] = lax.dot_general(
            eye, qs, nt, preferred_element_type=F32).astype(BF16)
        return carry

    lax.fori_loop(0, l_all // VT_BLK, vt_blk, 0, unroll=True)

    head_row = lax.broadcasted_iota(jnp.int32, (pw, A_QB), 0)

    def window(qb):
        start = qb * A_QB - band
        start_c = jnp.maximum(start, 0)
        return pl.multiple_of(start_c, LANES), pl.multiple_of(start_c - start, LANES)

    def produce(qb, slot):
        qs = pl.multiple_of(qb * A_QB, A_QB)
        start_c, off = window(qb)
        out = []
        for p in range(npair):
            qt = qt_ref[p * pw:(p + 1) * pw, pl.ds(qs, A_QB)]
            zero = jnp.zeros_like(qt)
            qbd = jnp.concatenate([jnp.where(head_row < A_DH, qt, zero),
                                   jnp.where(head_row >= A_DH, qt, zero)], axis=1)
            kb = k_ref[0, pl.ds(start_c, A_KB), p * pw:(p + 1) * pw]
            s = jnp.dot(kb, qbd, preferred_element_type=F32)
            s = s + tt_ref[pl.ds(off, A_KB), 2 * p * A_QB:2 * (p + 1) * A_QB]
            s_ref[slot, p] = s
            out.append(jnp.max(s, axis=0, keepdims=True))
        return tuple(out)

    def consume(qb, slot, ms):
        start_c, _ = window(qb)
        for p in range(npair):
            pb = jnp.exp2(s_ref[slot, p] - ms[p]).astype(BF16)
            for h in range(2):
                vt = vt_ref[2 * p + h, :, pl.ds(start_c, A_KB)]
                pv_ref[slot, 2 * p + h] = jnp.dot(vt, pb[:, h * A_QB:(h + 1) * A_QB],
                                                  preferred_element_type=F32)

    def finish(qb, slot):
        qs = pl.multiple_of(qb * A_QB, A_QB)
        outs = []
        for h in range(2 * npair):
            oh = pv_ref[slot, h]
            outs.append(oh[:A_DH] * (1.0 / oh[A_DH:A_DH + 1]))
        o = jnp.concatenate(outs, axis=0).T
        gate = _silu(z_ref[0, pl.ds(qs, A_QB), :].astype(F32))
        o_ref[0, pl.ds(qs, A_QB), :] = (o * gate).astype(o_ref.dtype)

    def pair(i, m, lead=True):
        j = 2 * i
        if lead:
            finish(j - 1, 1)
        m1 = produce(j + 1, 1)
        consume(j, 0, m)
        m2 = produce(j + 2, 0)
        finish(j, 0)
        consume(j + 1, 1, m1)
        return m2

    nb = l_all // A_QB
    npairs = (nb - 1) // 2
    m = pair(0, produce(0, 0), lead=False)
    def trip(t, mm):
        for r in range(A_PAIRS_PER_TRIP):
            mm = pair(A_PAIRS_PER_TRIP * t + 1 + r, mm)
        return mm

    ntrips = (npairs - 1) // A_PAIRS_PER_TRIP
    m = lax.fori_loop(0, ntrips, trip, m)
    for i in range(1 + A_PAIRS_PER_TRIP * ntrips, npairs):
        m = pair(i, m)
    last = 2 * npairs
    finish(last - 1, 1)
    if (nb - 1) % 2:
        m1 = produce(last + 1, 1)
        consume(last, 0, m)
        finish(last, 0)
        consume(last + 1, 1, m1)
        finish(last + 1, 1)
    else:
        consume(last, 0, m)
        finish(last, 0)


def _attn_a(proj3, rel_bias):
    b, l, _ = proj3.shape
    nh = 2 * A_PAIRS
    ng = A_HEADS // nh
    width = nh * A_DH
    ncol = A_HEADS * A_DH // width
    zcol = (3 * A_HEADS * A_DH + (A_HEADS * A_DH)) // width
    left = N_LEFT * CHUNK + A_QB - MAX_REL
    right = A_QB + A_TW - left - (2 * MAX_REL + 1)
    ext = jnp.pad(rel_bias.astype(F32)[:, ::-1], ((0, 0), (left, right)), mode="edge")
    ext = ext.reshape(ng, nh, A_QB + A_TW)
    seq = lambda h, bi: (bi, 0, h)
    return pl.pallas_call(
        _attn_a_kernel,
        name="attn_a",
        grid=(ng, b),
        in_specs=[pl.BlockSpec((1, l, width), seq),
                  pl.BlockSpec((1, l, width), lambda h, bi: (bi, 0, ncol + h)),
                  pl.BlockSpec((1, l, width), lambda h, bi: (bi, 0, 2 * ncol + h)),
                  pl.BlockSpec((1, l, width), lambda h, bi: (bi, 0, zcol + h)),
                  pl.BlockSpec((1, nh, A_QB + A_TW), lambda h, bi: (h, 0, 0))],
        out_specs=pl.BlockSpec((1, l, width), seq),
        out_shape=jax.ShapeDtypeStruct((b, l, A_HEADS * A_DH), BF16),
        scratch_shapes=[pltpu.VMEM((A_TW, nh * A_QB), F32),
                        pltpu.VMEM((nh, A_DH + A_PAD, l), BF16),
                        pltpu.VMEM((width, l), BF16),
                        pltpu.VMEM((2, A_PAIRS, A_KB, 2 * A_QB), F32),
                        pltpu.VMEM((2, nh, A_DH + A_PAD, A_QB), F32)],
        compiler_params=_cparams(("arbitrary", "arbitrary")),
    )(proj3, proj3, proj3, proj3, ext)


def _cmul(ar, ai, br, bi):
    return ar * br - ai * bi, ar * bi + ai * br


def _s5_discretise(lr, li, log_dt):
    dt = jnp.exp(log_dt)
    mag = jnp.exp(lr * dt)
    ang = li * dt
    a_re = mag * jnp.cos(ang)
    a_im = mag * jnp.sin(ang)
    den = lr * lr + li * li
    nr = a_re - 1.0
    coef_re = (nr * lr + a_im * li) / den
    coef_im = (a_im * lr - nr * li) / den
    return (a_re, a_im), (coef_re, coef_im)


def _s5_weights_kernel(lam_pg_ref, ldt_pg_ref, c_pg_ref, lam_gp_ref, ldt_gp_ref, b_gp_ref,
                       lam_row_ref, ldt_row_ref, b_tl_ref, wm_ref, wr_ref, wo_ref, a_ref):
    tc, ch, ns, gpb = S5_TC, S5_GROUP, S5_STATE, S5_GPB
    lw = gpb * ch
    hi = lax.Precision.HIGHEST

    a_pg, _ = _s5_discretise(lam_pg_ref[0], lam_pg_ref[1], ldt_pg_ref[...])
    pw = (jnp.ones_like(a_pg[0]), jnp.zeros_like(a_pg[0]))
    ca = []
    for n in range(tc + 1):
        ca.append(_cmul(c_pg_ref[0], c_pg_ref[1], *pw))
        pw = _cmul(*pw, *a_pg)

    _, coef_gp = _s5_discretise(lam_gp_ref[0], lam_gp_ref[1], ldt_gp_ref[...])
    bb_gp = _cmul(*coef_gp, b_gp_ref[0], b_gp_ref[1])
    ca_re = jnp.concatenate([ca[n][0] for n in range(tc)], axis=1)
    ca_im = jnp.concatenate([ca[n][1] for n in range(tc)], axis=1)
    z = (jnp.dot(bb_gp[0], ca_re, precision=hi, preferred_element_type=F32)
         - jnp.dot(bb_gp[1], ca_im, precision=hi, preferred_element_type=F32))
    row_g = lax.broadcasted_iota(jnp.int32, z.shape, 0) // ch
    col_g = (lax.broadcasted_iota(jnp.int32, z.shape, 1) % lw) // ch
    z = jnp.where(row_g == col_g, z, 0.0).astype(BF16)
    for t in range(tc):
        parts = [z[:, :(tc - t) * lw]]
        if t:
            parts.insert(0, jnp.zeros((lw, t * lw), BF16))
        wm_ref[0, t * lw:(t + 1) * lw, :] = jnp.concatenate(parts, axis=1)

    rg = lax.broadcasted_iota(jnp.int32, (gpb * ns, lw), 0) // ns
    cg = lax.broadcasted_iota(jnp.int32, (gpb * ns, lw), 1) // ch
    keep = rg == cg
    for t2 in range(tc):
        cr, ci = ca[t2 + 1]
        cr = jnp.where(keep, jnp.concatenate([cr] * gpb, axis=0), 0.0)
        ci = jnp.where(keep, jnp.concatenate([ci] * gpb, axis=0), 0.0)
        wo_ref[0, :gpb * ns, t2 * lw:(t2 + 1) * lw] = cr.astype(BF16)
        wo_ref[0, gpb * ns:, t2 * lw:(t2 + 1) * lw] = (-ci).astype(BF16)

    a_row, coef_row = _s5_discretise(lam_row_ref[0], lam_row_ref[1], ldt_row_ref[...])
    bb_tl = _cmul(*coef_row, b_tl_ref[0], b_tl_ref[1])
    rg = lax.broadcasted_iota(jnp.int32, bb_tl[0].shape, 0) // ch
    cg = lax.broadcasted_iota(jnp.int32, bb_tl[0].shape, 1) // ns
    bb_tl = (jnp.where(rg == cg, bb_tl[0], 0.0), jnp.where(rg == cg, bb_tl[1], 0.0))
    pw = (jnp.ones_like(a_row[0]), jnp.zeros_like(a_row[0]))
    for t in range(tc - 1, -1, -1):
        r_re, r_im = _cmul(*pw, *bb_tl)
        wr_ref[0, t * lw:(t + 1) * lw, :] = jnp.concatenate([r_re, r_im], axis=1).astype(BF16)
        pw = _cmul(*pw, *a_row)
    a_ref[0, 0] = pw[0]
    a_ref[0, 1] = pw[1]


def _s5_param_views(lam_re, lam_im, log_dt, b_re, b_im, c_re, c_im):
    g, ns = lam_re.shape
    ch, gpb = S5_GROUP, S5_GPB
    lw = gpb * ch
    lam = jnp.stack([lam_re, lam_im]).astype(F32)
    ldt = log_dt.astype(F32)
    bst = jnp.stack([b_re, b_im]).astype(F32)
    cst = jnp.stack([c_re, c_im]).astype(F32)
    lam_pg = jnp.repeat(lam.transpose(0, 2, 1), ch, axis=2)
    ldt_pg = jnp.repeat(ldt, ch)[None]
    c_pg = cst.transpose(0, 3, 1, 2).reshape(2, ns, g * ch)
    lam_gp = jnp.repeat(lam, ch, axis=1)
    ldt_gp = jnp.broadcast_to(jnp.repeat(ldt, ch)[:, None], (g * ch, ns))
    b_gp = bst.transpose(0, 1, 3, 2).reshape(2, g * ch, ns)
    lam_row = lam.reshape(2, 1, g * ns)
    ldt_row = jnp.repeat(ldt, ns)[None]
    b_tl = jnp.tile(b_gp, (1, 1, gpb))
    last = lambda i, b: (0, 0, i)
    mid = lambda i, b: (0, i, 0)
    specs = [pl.BlockSpec((2, ns, lw), last),
             pl.BlockSpec((1, lw), lambda i, b: (0, i)),
             pl.BlockSpec((2, ns, lw), last),
             pl.BlockSpec((2, lw, ns), mid),
             pl.BlockSpec((lw, ns), lambda i, b: (i, 0)),
             pl.BlockSpec((2, lw, ns), mid),
             pl.BlockSpec((2, 1, gpb * ns), last),
             pl.BlockSpec((1, gpb * ns), lambda i, b: (0, i)),
             pl.BlockSpec((2, lw, gpb * ns), mid)]
    return [lam_pg, ldt_pg, c_pg, lam_gp, ldt_gp, b_gp, lam_row, ldt_row, b_tl], specs


def _s5_mix_kernel(u_ref, *refs):
    params, (d_ref, y_ref, wm_ref, wr_ref, wo_ref, a_ref, s_ref, x_ref, ys_ref) = refs[:9], refs[9:]

    @pl.when(pl.program_id(1) == 0)
    def _():
        _s5_weights_kernel(*params, wm_ref, wr_ref, wo_ref, a_ref)

    nk = u_ref.shape[1]
    half = s_ref.shape[1] // 2
    sub = 8
    u = u_ref[0]
    s_ref[...] = jnp.dot(u, wr_ref[0], preferred_element_type=F32)

    a1 = (a_ref[0, 0], a_ref[0, 1])
    a2 = _cmul(*a1, *a1)
    a4 = _cmul(*a2, *a2)
    a8 = _cmul(*a4, *a4)
    a3 = _cmul(*a2, *a1)
    pows = [(jnp.ones_like(a1[0]), jnp.zeros_like(a1[0])), a1, a2, a3, a4,
            _cmul(*a4, *a1), _cmul(*a4, *a2), _cmul(*a4, *a3)]
    p_re = jnp.concatenate([p[0] for p in pows], axis=0)
    p_im = jnp.concatenate([p[1] for p in pows], axis=0)
    rowid = lax.broadcasted_iota(jnp.int32, (sub, half), 0)
    levels = [(d, jnp.broadcast_to(a[0], (sub, half)), jnp.broadcast_to(a[1], (sub, half)))
              for d, a in ((1, a1), (2, a2), (4, a4))]

    def shift_rows(v, d):
        return jnp.where(rowid >= d, pltpu.roll(v, d, 0), 0.0)

    def tile(j, carry):
        cr, ci = carry
        rows = slice(j * sub, (j + 1) * sub)
        zr = s_ref[rows, :half]
        zi = s_ref[rows, half:]
        for d, br, bi in levels:
            sr, si = shift_rows(zr, d), shift_rows(zi, d)
            zr, zi = zr + br * sr - bi * si, zi + br * si + bi * sr
        x_ref[rows, :half] = p_re * cr - p_im * ci + shift_rows(zr, 1)
        x_ref[rows, half:] = p_re * ci + p_im * cr + shift_rows(zi, 1)
        return (a8[0] * cr - a8[1] * ci + zr[sub - 1:], a8[0] * ci + a8[1] * cr + zi[sub - 1:])

    carry = (jnp.zeros((1, half), F32), jnp.zeros((1, half), F32))
    for j in range(nk // sub):
        carry = tile(j, carry)
    xb = x_ref[...].astype(BF16)

    cw = 2 * LANES
    for n in range(u.shape[1] // cw):
        cs = slice(n * cw, (n + 1) * cw)
        y = jnp.dot(u[:, :(n + 1) * cw], wm_ref[0, :(n + 1) * cw, cs], preferred_element_type=F32)
        y = y + jnp.dot(xb, wo_ref[0, :, cs], preferred_element_type=F32)
        y = _gelu_tanh(y + d_ref[0, :, cs] * u[:, cs].astype(F32))
        for t in range(2 * n, 2 * n + 2):
            ys_ref[pl.ds(t, nk, stride=S5_TC), :] = y[:, (t - 2 * n) * LANES:(t - 2 * n + 1) * LANES]
    y_ref[0] = ys_ref[...].astype(y_ref.dtype)


def _s5_mixer(u3, nb, lam_re, lam_im, log_dt, b_re, b_im, c_re, c_im, d_skip):
    nblk, rows, w = u3.shape
    tc = S5_TC
    nk = rows // nb
    assert nk % 8 == 0
    views, view_specs = _s5_param_views(lam_re, lam_im, log_dt, b_re, b_im, c_re, c_im)
    ns = 2 * S5_GPB * S5_STATE
    d_rows = jnp.tile(d_skip.astype(F32).reshape(nblk, 1, LANES), (1, 1, tc))
    return pl.pallas_call(
        _s5_mix_kernel,
        name="s5_mix",
        grid=(nblk, nb),
        in_specs=[pl.BlockSpec((1, nk, w), lambda i, b: (i, b, 0))] + view_specs
                 + [pl.BlockSpec((1, 1, w), lambda i, b: (i, 0, 0))],
        out_specs=pl.BlockSpec((1, nk * tc, LANES), lambda i, b: (i, b, 0)),
        out_shape=jax.ShapeDtypeStruct((nblk, rows * tc, LANES), BF16),
        scratch_shapes=[pltpu.VMEM((1, w, w), BF16),
                        pltpu.VMEM((1, w, ns), BF16),
                        pltpu.VMEM((1, ns, w), BF16),
                        pltpu.VMEM((1, 2, 1, ns // 2), F32),
                        pltpu.VMEM((nk, ns), F32), pltpu.VMEM((nk, ns), F32),
                        pltpu.VMEM((nk * tc, LANES), F32)],
        compiler_params=_cparams(("arbitrary", "arbitrary")),
    )(u3, *views, d_rows)


def _even_out_kernel(x_ref, oa_ref, y_ref, zb_ref, wg_ref, bg_ref, wo_ref, g_ref, wf_ref,
                     x1_ref, hn_ref, f_ref):
    y = jnp.concatenate([y_ref[i] for i in range(y_ref.shape[0])], axis=1)
    t = jnp.dot(y, wg_ref[...], preferred_element_type=F32) + bg_ref[...]
    ob = y.astype(F32) * jax.nn.sigmoid(t) * _silu(zb_ref[...].astype(F32))
    wa = oa_ref.shape[1]
    acc = jnp.dot(oa_ref[...], wo_ref[:wa, :], preferred_element_type=F32)
    acc = acc + jnp.dot(ob.astype(BF16), wo_ref[wa:, :], preferred_element_type=F32)
    x1 = x_ref[...] + acc
    x1_ref[...] = x1
    hn = _rms(x1, g_ref[...]).astype(BF16)
    hn_ref[...] = hn
    f_ref[...] = lax.dot_general(wf_ref[...], hn, (((1,), (1,)), ((), ())),
                                 preferred_element_type=F32)


def _even_out(x2, oa, y, proj, w_glu, b_glu, w_out, g_next, wf_t):
    m, d = x2.shape
    nh = wf_t.shape[0]
    wa = oa.shape[1]
    nblk, _, lw = y.shape
    wb = nblk * lw
    tm = min(TM_OUT, m)
    zb_col = (proj.shape[1] - wb) // wb
    const = lambda i: (0, 0)
    return pl.pallas_call(
        _even_out_kernel,
        name="even_out",
        grid=(m // tm,),
        in_specs=[pl.BlockSpec((tm, d), lambda i: (i, 0)),
                  pl.BlockSpec((tm, wa), lambda i: (i, 0)),
                  pl.BlockSpec((nblk, tm, lw), lambda i: (0, i, 0)),
                  pl.BlockSpec((tm, wb), lambda i: (i, zb_col)),
                  pl.BlockSpec((wb, wb), const),
                  pl.BlockSpec((1, wb), const),
                  pl.BlockSpec((wa + wb, d), const),
                  pl.BlockSpec((1, d), const),
                  pl.BlockSpec((nh, d), const)],
        out_specs=[pl.BlockSpec((tm, d), lambda i: (i, 0)),
                   pl.BlockSpec((tm, d), lambda i: (i, 0)),
                   pl.BlockSpec((nh, tm), lambda i: (0, i))],
        out_shape=[jax.ShapeDtypeStruct((m, d), F32),
                   jax.ShapeDtypeStruct((m, d), BF16),
                   jax.ShapeDtypeStruct((nh, m), F32)],
        compiler_params=_cparams(("arbitrary",)),
    )(x2, oa, y, proj, w_glu, b_glu.reshape(1, wb), w_out, g_next.reshape(1, d), wf_t)


def _cum_forget_kernel(f_ref, b_ref, c_ref):
    x = f_ref[...] + b_ref[...]
    v = -(jnp.maximum(-x, 0.0) + jnp.log1p(jnp.exp(-jnp.abs(x))))
    n = v.shape[-1]
    pos = lax.broadcasted_iota(jnp.int32, v.shape, 1)
    d = 1
    while d < n:
        v = v + jnp.where(pos >= d, pltpu.roll(v, d, 1), 0.0)
        d *= 2
    c_ref[...] = v


def _cum_forget(f_t, b_forget, seq):
    nh, m = f_t.shape
    return pl.pallas_call(
        _cum_forget_kernel,
        name="cum_forget",
        grid=(m // seq,),
        in_specs=[pl.BlockSpec((nh, seq), lambda i: (0, i)),
                  pl.BlockSpec((nh, 1), lambda i: (0, 0))],
        out_specs=pl.BlockSpec((nh, seq), lambda i: (0, i)),
        out_shape=jax.ShapeDtypeStruct((nh, m), F32),
        compiler_params=_cparams(("arbitrary",)),
    )(f_t, b_forget.astype(F32).reshape(nh, 1))


def _split3(x):
    hi = x.astype(BF16).astype(F32)
    r = x - hi
    mid = r.astype(BF16).astype(F32)
    lo = (r - mid).astype(BF16).astype(F32)
    return hi, mid, lo


def _fox_kernel(q0_ref, qn_ref, k_ref, v_ref, z_ref, c_ref, o_ref,
                ke_ref, vt_ref, qa_ref, s_ref, mb_ref, acc_ref):
    qi = pl.program_id(2)
    nq = pl.num_programs(2)
    tq = qn_ref.shape[1]
    tk = tq
    dh = C_DH
    nh = c_ref.shape[1]
    nblk = k_ref.shape[1] // tk
    nt = (((1,), (1,)), ((), ()))

    def cols(h):
        return slice(h * dh, (h + 1) * dh)

    def prep_queries(q_ref, qb):
        qs = pl.multiple_of(qb * tq, tq)
        row = lax.broadcasted_iota(jnp.int32, (dh, tq), 0)
        for h in range(nh):
            c_base = jnp.max(c_ref[0, h, :, pl.ds(qs, tq)], axis=1, keepdims=True) * LOG2E
            bh, bm, bl = _split3(c_base)
            g = jnp.where(row < 3, 1.0, jnp.where(row == 3, bh, jnp.where(row == 4, bm,
                          jnp.where(row == 5, bl, 0.0))))
            qt = (q_ref[0, :, cols(h)].astype(F32) * (LOG2E / math.sqrt(dh))).T
            qa_ref[h, :dh, :] = qt.astype(BF16)
            qa_ref[h, dh:, :] = g.astype(BF16)

    half = tk // 2

    def produce(h, j, diagonal):
        ks = pl.multiple_of(j * tk, tk)

        def keys(start, n):
            return jnp.concatenate([k_ref[0, pl.ds(start, n), cols(h)],
                                    ke_ref[h, pl.ds(start, n), :]], axis=1)

        if not diagonal:
            s = jnp.dot(keys(ks, tk), qa_ref[h], preferred_element_type=F32)
            s_ref[h] = s
            return jnp.max(s, axis=0, keepdims=True)
        top = jnp.dot(keys(ks, half), qa_ref[h], preferred_element_type=F32)
        bot = jnp.dot(keys(ks + half, half), qa_ref[h, :, half:],
                      preferred_element_type=F32)
        top = jnp.where(lax.broadcasted_iota(jnp.int32, top.shape, 0)
                        <= lax.broadcasted_iota(jnp.int32, top.shape, 1), top, NEG)
        bot = jnp.where(lax.broadcasted_iota(jnp.int32, bot.shape, 0)
                        <= lax.broadcasted_iota(jnp.int32, bot.shape, 1), bot, NEG)
        s_ref[h, :half, :] = top
        s_ref[h, half:, half:] = bot
        bot_wide = jnp.concatenate([jnp.full((half, half), NEG, F32), bot], axis=1)
        return jnp.max(jnp.maximum(top, bot_wide), axis=0, keepdims=True)

    def consume(h, j, stats, diagonal=False):
        m, mb = stats
        ks = pl.multiple_of(j * tk, tk)
        m2 = jnp.maximum(m, mb)
        if not diagonal:
            p = jnp.exp2(s_ref[h] - m2).astype(BF16)
            pv = jnp.dot(vt_ref[h, :, pl.ds(ks, tk)], p, preferred_element_type=F32)
            acc_ref[h] = jnp.exp2(m - m2) * acc_ref[h] + pv
            return m2
        p_top = jnp.exp2(s_ref[h, :half, :] - m2).astype(BF16)
        mb_ref[h] = m2
        p_bot = jnp.exp2(s_ref[h, half:, half:] - mb_ref[h, :, half:]).astype(BF16)
        acc = jnp.exp2(m - m2) * acc_ref[h] + jnp.dot(vt_ref[h, :, pl.ds(ks, half)], p_top,
                                                      preferred_element_type=F32)
        acc_ref[h, :, :half] = acc[:, :half]
        acc_ref[h, :, half:] = acc[:, half:] + jnp.dot(vt_ref[h, :, pl.ds(ks + half, half)], p_bot,
                                                       preferred_element_type=F32)
        return m2

    def step(j, carry, diagonal_next):
        return tuple((consume(h, j, stats), produce(h, j + 1, diagonal_next))
                     for h, stats in enumerate(carry))

    @pl.when(qi == 0)
    def _():
        r = lax.broadcasted_iota(jnp.int32, (dh, dh), 0)
        c = lax.broadcasted_iota(jnp.int32, (dh, dh), 1)
        eye = (r == c).astype(BF16)
        row = lax.broadcasted_iota(jnp.int32, (dh, tk), 0)
        ones_row = (lax.broadcasted_iota(jnp.int32, (FOX_PAD, tk), 0) == 0).astype(BF16)

        def blk(j, carry):
            ks = pl.multiple_of(j * tk, tk)
            for h in range(nh):
                hi, mid, lo = _split3(c_ref[0, h, :, pl.ds(ks, tk)] * (-LOG2E))
                a = jnp.where(row == 0, hi, jnp.where(row == 1, mid, jnp.where(row == 2, lo,
                              jnp.where(row < 6, 1.0, 0.0))))
                ke_ref[h, pl.ds(ks, tk), :] = a.T.astype(BF16)
                vt = lax.dot_general(eye, v_ref[0, pl.ds(ks, tk), cols(h)], nt,
                                     preferred_element_type=F32)
                vt_ref[h, :dh, pl.ds(ks, tk)] = vt.astype(BF16)
                vt_ref[h, dh:, pl.ds(ks, tk)] = ones_row
            return carry

        lax.fori_loop(0, nblk, blk, 0, unroll=True)
        prep_queries(q0_ref, 0)
        for h in range(nh):
            mb_ref[h] = produce(h, 0, True)

    carry = []
    for h in range(nh):
        acc_ref[h] = jnp.zeros(acc_ref.shape[1:], F32)
        carry.append((jnp.full((1, tq), NEG, F32), mb_ref[h]))
    plain = functools.partial(step, diagonal_next=False)
    n_plain = jnp.maximum(qi - 1, 0)
    n_pairs = n_plain // 2
    n_quads = n_plain // 4
    carry = lax.fori_loop(
        0, n_quads,
        lambda i, c: plain(4 * i + 3, plain(4 * i + 2, plain(4 * i + 1, plain(4 * i, c)))),
        tuple(carry))
    carry = lax.fori_loop(2 * n_quads, n_pairs, lambda i, c: plain(2 * i + 1, plain(2 * i, c)),
                          carry)
    carry = lax.fori_loop(2 * n_pairs, n_plain, plain, carry)

    def finish_block(carry):
        for h, stats in enumerate(carry):
            consume(h, qi, stats, diagonal=True)
        prep_queries(qn_ref, jnp.minimum(qi + 1, nq - 1))
        for h in range(nh):
            mb_ref[h] = produce(h, 0, False)
        for h in range(nh):
            acc = acc_ref[h]
            o = (acc[:dh] / acc[dh:dh + 1]).T
            o_ref[0, :, cols(h)] = (o * _silu(z_ref[0, :, cols(h)].astype(F32))
                                    ).astype(o_ref.dtype)

    @pl.when(qi > 0)
    def _():
        finish_block(step(qi - 1, carry, True))

    @pl.when(qi == 0)
    def _():
        finish_block(carry)


def _fox(proj3, cum4):
    b, l, _ = proj3.shape
    ng = C_HEADS // FOX_HEADS
    w = FOX_HEADS * C_DH
    tq = min(FOX_TQ, l)
    nq = l // tq
    return pl.pallas_call(
        _fox_kernel,
        name="fox",
        grid=(b, ng, nq),
        in_specs=[pl.BlockSpec((1, tq, w), lambda bi, gi, qi: (bi, 0, gi)),
                  pl.BlockSpec((1, tq, w), lambda bi, gi, qi: (bi, jnp.minimum(qi + 1, nq - 1), gi)),
                  pl.BlockSpec((1, l, w), lambda bi, gi, qi: (bi, 0, ng + gi)),
                  pl.BlockSpec((1, l, w), lambda bi, gi, qi: (bi, 0, 2 * ng + gi)),
                  pl.BlockSpec((1, tq, w), lambda bi, gi, qi: (bi, qi, 3 * ng + gi)),
                  pl.BlockSpec((1, FOX_HEADS, 1, l), lambda bi, gi, qi: (bi, gi, 0, 0))],
        out_specs=pl.BlockSpec((1, tq, w), lambda bi, gi, qi: (bi, qi, gi)),
        out_shape=jax.ShapeDtypeStruct((b, l, C_HEADS * C_DH), BF16),
        scratch_shapes=[pltpu.VMEM((FOX_HEADS, l, C_DH), BF16),
                        pltpu.VMEM((FOX_HEADS, C_DH + FOX_PAD, l), BF16),
                        pltpu.VMEM((FOX_HEADS, 2 * C_DH, tq), BF16),
                        pltpu.VMEM((FOX_HEADS, tq, tq), F32),
                        pltpu.VMEM((FOX_HEADS, 1, tq), F32),
                        pltpu.VMEM((FOX_HEADS, C_DH + FOX_PAD, tq), F32)],
        compiler_params=_cparams(("arbitrary", "arbitrary", "arbitrary")),
    )(proj3, proj3, proj3, proj3, proj3, cum4)


def _odd_out_kernel(x_ref, o_ref, w_ref, g_ref, out_ref):
    x2 = x_ref[...] + jnp.dot(o_ref[...], w_ref[...], preferred_element_type=F32)
    out_ref[...] = _rms(x2, g_ref[...])


def _odd_out(x1, o, w_out, g):
    m, d = x1.shape
    k = o.shape[1]
    tm = min(TM_OUT, m)
    return pl.pallas_call(
        _odd_out_kernel,
        name="odd_out",
        grid=(m // tm,),
        in_specs=[pl.BlockSpec((tm, d), lambda i: (i, 0)),
                  pl.BlockSpec((tm, k), lambda i: (i, 0)),
                  pl.BlockSpec((k, d), lambda i: (0, 0)),
                  pl.BlockSpec((1, d), lambda i: (0, 0))],
        out_specs=pl.BlockSpec((tm, d), lambda i: (i, 0)),
        out_shape=jax.ShapeDtypeStruct((m, d), F32),
        compiler_params=_cparams(("arbitrary",)),
    )(x1, o, w_out, g.reshape(1, d))


def kernel(x, norm_even_g, w_in_even, rel_bias, s5_lambda_re, s5_lambda_im, s5_log_dt,
           s5_b_re, s5_b_im, s5_c_re, s5_c_im, s5_d, w_glu, b_glu, w_out_even,
           norm_odd_g, w_in_odd, b_forget, w_out_odd, final_norm_g):
    b, l, d = x.shape
    assert norm_even_g.shape[0] == 1 and norm_odd_g.shape[0] == 1, "one even + one odd layer"
    assert l % FOX_TQ == 0 and l % VT_BLK == 0 and l % A_QB == 0 and l >= A_KB
    m = b * l
    a_w = A_HEADS * A_DH
    s5_w = s5_d.shape[1] * S5_GROUP
    c_w = C_HEADS * C_DH

    x2 = x.reshape(m, d)
    assert s5_w == TN_PROJ and l % S5_TC == 0
    proj, u3 = _norm_proj(x2, norm_even_g[0], w_in_even[0].astype(BF16), 3 * a_w)
    proj3 = proj.reshape(b, l, proj.shape[1])
    o_a = _attn_a(proj3, rel_bias[0])
    y = _s5_mixer(u3, b, s5_lambda_re[0], s5_lambda_im[0], s5_log_dt[0], s5_b_re[0],
                  s5_b_im[0], s5_c_re[0], s5_c_im[0], s5_d[0])
    w_odd_t = w_in_odd[0].T
    wf_t = w_odd_t[4 * c_w:].astype(BF16)
    x1, hn1, f_t = _even_out(x2, o_a.reshape(m, a_w), y, proj,
                             w_glu[0].astype(BF16), b_glu[0].astype(F32),
                             w_out_even[0].astype(BF16), norm_odd_g[0], wf_t)
    proj1 = _proj_odd(hn1, w_odd_t, 4 * c_w)
    cum = _cum_forget(f_t, b_forget[0], l)
    cum4 = cum.reshape(C_HEADS, b, 1, l).transpose(1, 0, 2, 3)
    o_c = _fox(proj1.reshape(b, l, 4 * c_w), cum4)
    out = _odd_out(x1, o_c.reshape(m, c_w), w_out_odd[0].astype(BF16), final_norm_g)
    return out.reshape(b, l, d)
```
